```python
import math
import jax, jax.numpy as jnp
from jax import lax
import numpy as np

D_MODEL = 1024
BATCH = 16
SEQ = 4096
DEPTH = 1

CHUNK = 64

CONV_HEAD_DIM = 64
CONV_HEADS = 12
D_CONV = CONV_HEADS * CONV_HEAD_DIM
CONV_WIDTH = 3
SSM_GROUP_CH = 16
SSM_GROUPS = 16
D_SSM = SSM_GROUPS * SSM_GROUP_CH
SSM_STATE = 64
D_IN = 3 * D_CONV + D_SSM
DT_MIN = 0.001
DT_MAX = 0.1

N_EXPERTS = 64
TOP_K = 6
N_EXPERT_GROUPS = 8
TOPK_GROUPS = 4
EXPERTS_PER_GROUP = N_EXPERTS // N_EXPERT_GROUPS
D_EXPERT = 256
D_SHARED = 256
ROUTED_SCALE = 2.5
EXPERT_BLOCK = 256

RMS_EPS = 1e-6

kernel_name = 'hybrid_conv_s5_moe_adaln_block'


def rms_norm(x, g):
    xf = x.astype(jnp.float32)
    y = xf * lax.rsqrt(jnp.mean(xf * xf, axis=-1, keepdims=True) + RMS_EPS)
    return (y * g.astype(jnp.float32)).astype(x.dtype)


def modulate(h, shift, scale):
    return h * (1.0 + scale[:, None, :]) + shift[:, None, :]


def short_conv_mixer(b_gate, c_gate, v, conv_w, conv_b):
    z = c_gate * v
    z = lax.conv_general_dilated(
        z, conv_w[:, None, :].astype(z.dtype), window_strides=(1,),
        padding=[(CONV_WIDTH - 1, 0)], dimension_numbers=('NWC', 'WIO', 'NWC'),
        feature_group_count=z.shape[-1])
    z = z + conv_b.astype(z.dtype)
    return b_gate * z


def _complex_scan_combine(e1, e2):
    a1r, a1i, b1r, b1i = e1
    a2r, a2i, b2r, b2i = e2
    ar = a1r * a2r - a1i * a2i
    ai = a1r * a2i + a1i * a2r
    br = a2r * b1r - a2i * b1i + b2r
    bi = a2r * b1i + a2i * b1r + b2i
    return (ar, ai, br, bi)


def s5_mixer(u, lam_re, lam_im, log_dt, b_re, b_im, c_re, c_im, d_skip, w_glu, b_glu):
    bsz, seq, _ = u.shape
    uf = u.astype(jnp.float32).reshape(bsz, seq, SSM_GROUPS, SSM_GROUP_CH)
    lr = lam_re.astype(jnp.float32)
    li = lam_im.astype(jnp.float32)
    dt = jnp.exp(log_dt.astype(jnp.float32))[:, None]
    mag = jnp.exp(lr * dt)
    ang = li * dt
    ab_re = mag * jnp.cos(ang)
    ab_im = mag * jnp.sin(ang)
    den = lr * lr + li * li
    nr = ab_re - 1.0
    ni = ab_im
    q_re = (nr * lr + ni * li) / den
    q_im = (ni * lr - nr * li) / den
    br = b_re.astype(jnp.float32)
    bi = b_im.astype(jnp.float32)
    bb_re = q_re[..., None] * br - q_im[..., None] * bi
    bb_im = q_re[..., None] * bi + q_im[..., None] * br
    bu_re = jnp.einsum('bsgh,gph->bsgp', uf, bb_re)
    bu_im = jnp.einsum('bsgh,gph->bsgp', uf, bb_im)
    a_re = jnp.broadcast_to(ab_re, bu_re.shape)
    a_im = jnp.broadcast_to(ab_im, bu_im.shape)
    _, _, xs_re, xs_im = lax.associative_scan(
        _complex_scan_combine, (a_re, a_im, bu_re, bu_im), axis=1)
    y = (jnp.einsum('bsgp,ghp->bsgh', xs_re, c_re.astype(jnp.float32))
         - jnp.einsum('bsgp,ghp->bsgh', xs_im, c_im.astype(jnp.float32)))
    y = y + d_skip.astype(jnp.float32).reshape(SSM_GROUPS, SSM_GROUP_CH) * uf
    y = y.reshape(bsz, seq, D_SSM)
    y = jax.nn.gelu(y)
    y = y * jax.nn.sigmoid(y @ w_glu.astype(jnp.float32) + b_glu.astype(jnp.float32))
    return y.astype(u.dtype)


def swiglu(h, w1, w3, w2):
    return (jax.nn.silu(h @ w1) * (h @ w3)) @ w2


def routed_experts(h, w_router, router_bias, w1, w3, w2):
    n_tok, d = h.shape
    scores = jax.nn.sigmoid(h.astype(jnp.float32) @ w_router.astype(jnp.float32))
    biased = scores + router_bias.astype(jnp.float32)
    grp = biased.reshape(n_tok, N_EXPERT_GROUPS, EXPERTS_PER_GROUP)
    grp_score = lax.top_k(grp, 2)[0].sum(-1)
    _, grp_idx = lax.top_k(grp_score, TOPK_GROUPS)
    grp_sel = jnp.any(grp_idx[..., None] == jnp.arange(N_EXPERT_GROUPS), axis=1)
    expert_mask = jnp.repeat(grp_sel, EXPERTS_PER_GROUP, axis=-1)
    masked = jnp.where(expert_mask, biased, -jnp.inf)
    _, e_idx = lax.top_k(masked, TOP_K)
    w_sel = jnp.take_along_axis(scores, e_idx, axis=-1)
    w_sel = w_sel / jnp.sum(w_sel, axis=-1, keepdims=True) * ROUTED_SCALE

    n_assign = n_tok * TOP_K
    m = EXPERT_BLOCK
    n_blocks = -(-(n_assign + N_EXPERTS * (m - 1)) // m)
    n_rows = n_blocks * m
    e_flat = e_idx.reshape(n_assign)
    tok_flat = jnp.arange(n_assign, dtype=jnp.int32) // TOP_K
    w_flat = w_sel.reshape(n_assign)
    order = jnp.argsort(e_flat)
    e_sorted = e_flat[order]
    counts = jnp.bincount(e_flat, length=N_EXPERTS)
    padded = (counts + m - 1) // m * m
    start = jnp.cumsum(counts) - counts
    pend = jnp.cumsum(padded)
    pstart = pend - padded
    rank = jnp.arange(n_assign, dtype=jnp.int32) - start[e_sorted]
    dest = pstart[e_sorted] + rank
    tok_buf = jnp.zeros((n_rows,), jnp.int32).at[dest].set(tok_flat[order])
    w_buf = jnp.zeros((n_rows,), jnp.float32).at[dest].set(w_flat[order])
    block_e = jnp.minimum(
        jnp.searchsorted(pend, jnp.arange(n_blocks) * m, side='right'), N_EXPERTS - 1)

    def expert_block(args):
        tok, wt, e = args
        xb = h[tok]
        yb = swiglu(xb, w1[e], w3[e], w2[e])
        return yb * wt[:, None].astype(yb.dtype)

    ys = lax.map(expert_block, (tok_buf.reshape(n_blocks, m),
                                w_buf.reshape(n_blocks, m), block_e))
    return jnp.zeros_like(h).at[tok_buf].add(ys.reshape(n_rows, d).astype(h.dtype))


def setup_inputs(seed: int = 0) -> dict:
    key = jax.random.key(seed)
    ks = jax.random.split(key, 32)
    f32 = jnp.float32
    nrm = lambda k, shape, s: jax.random.normal(k, shape, f32) * s
    d = D_MODEL
    n_idx = jnp.arange(SSM_STATE, dtype=f32)
    return {
        'x': nrm(ks[0], (BATCH, SEQ, d), 1.0),
        'c': nrm(ks[1], (BATCH, d), 1.0),
        'w_ada': nrm(ks[2], (d, 6 * d), 0.02),
        'b_ada': nrm(ks[3], (6 * d,), 0.01),
        'g_mix': 1.0 + nrm(ks[4], (d,), 0.01),
        'w_in': nrm(ks[5], (d, D_IN), d ** -0.5),
        'conv_w': nrm(ks[6], (CONV_WIDTH, D_CONV), CONV_WIDTH ** -0.5),
        'conv_b': nrm(ks[7], (D_CONV,), 0.01),
        'g_conv': 1.0 + nrm(ks[8], (D_CONV,), 0.01),
        'lam_re': -0.5 + nrm(ks[9], (SSM_GROUPS, SSM_STATE), 0.01),
        'lam_im': math.pi * n_idx[None, :] + nrm(ks[10], (SSM_GROUPS, SSM_STATE), 0.01),
        'log_dt': jax.random.uniform(ks[11], (SSM_GROUPS,), f32,
                                     math.log(DT_MIN), math.log(DT_MAX)),
        'b_re': nrm(ks[12], (SSM_GROUPS, SSM_STATE, SSM_GROUP_CH), (2 * SSM_GROUP_CH) ** -0.5),
        'b_im': nrm(ks[13], (SSM_GROUPS, SSM_STATE, SSM_GROUP_CH), (2 * SSM_GROUP_CH) ** -0.5),
        'c_re': nrm(ks[14], (SSM_GROUPS, SSM_GROUP_CH, SSM_STATE), SSM_STATE ** -0.5),
        'c_im': nrm(ks[15], (SSM_GROUPS, SSM_GROUP_CH, SSM_STATE), SSM_STATE ** -0.5),
        'd_skip': nrm(ks[16], (D_SSM,), 1.0),
        'w_glu': nrm(ks[17], (D_SSM, D_SSM), D_SSM ** -0.5),
        'b_glu': nrm(ks[18], (D_SSM,), 0.01),
        'g_ssm': 1.0 + nrm(ks[19], (D_SSM,), 0.01),
        'w_out': nrm(ks[20], (d, d), d ** -0.5),
        'g_ffn': 1.0 + nrm(ks[21], (d,), 0.01),
        'w_router': nrm(ks[22], (d, N_EXPERTS), d ** -0.5),
        'router_bias': nrm(ks[23], (N_EXPERTS,), 0.01),
        'w1': nrm(ks[24], (N_EXPERTS, d, D_EXPERT), d ** -0.5),
        'w3': nrm(ks[25], (N_EXPERTS, d, D_EXPERT), d ** -0.5),
        'w2': nrm(ks[26], (N_EXPERTS, D_EXPERT, d), D_EXPERT ** -0.5),
        'ws1': nrm(ks[27], (d, D_SHARED), d ** -0.5),
        'ws3': nrm(ks[28], (d, D_SHARED), d ** -0.5),
        'ws2': nrm(ks[29], (D_SHARED, d), D_SHARED ** -0.5),
        'g_final': 1.0 + nrm(ks[30], (d,), 0.01),
    }


def reference(x, c, w_ada, b_ada, g_mix, w_in, conv_w, conv_b, g_conv,
              lam_re, lam_im, log_dt, b_re, b_im, c_re, c_im, d_skip, w_glu, b_glu,
              g_ssm, w_out, g_ffn, w_router, router_bias, w1, w3, w2,
              ws1, ws3, ws2, g_final):
    bsz, seq, d = x.shape
    mod = jax.nn.silu(c) @ w_ada + b_ada
    shift1, scale1, gate1, shift2, scale2, gate2 = jnp.split(mod, 6, axis=-1)
    for _ in range(DEPTH):
        h = modulate(rms_norm(x, g_mix), shift1, scale1)
        proj = h @ w_in
        b_gate, c_gate, v, u = jnp.split(
            proj, [D_CONV, 2 * D_CONV, 3 * D_CONV], axis=-1)
        y_conv = rms_norm(short_conv_mixer(b_gate, c_gate, v, conv_w, conv_b), g_conv)
        y_ssm = rms_norm(s5_mixer(u, lam_re, lam_im, log_dt, b_re, b_im, c_re, c_im,
                                  d_skip, w_glu, b_glu), g_ssm)
        y = jnp.concatenate([y_conv, y_ssm], axis=-1) @ w_out
        x = x + gate1[:, None, :] * y
        h2 = modulate(rms_norm(x, g_ffn), shift2, scale2).reshape(bsz * seq, d)
        ffn = swiglu(h2, ws1, ws3, ws2) + routed_experts(h2, w_router, router_bias, w1, w3, w2)
        x = x + gate2[:, None, :] * ffn.reshape(bsz, seq, d)
    return rms_norm(x, g_final)
```

```python
import functools
import math

import jax
import jax.numpy as jnp
from jax import lax
from jax.experimental import pallas as pl
from jax.experimental.pallas import tpu as pltpu

F32 = jnp.float32
BF16 = jnp.bfloat16

D_CONV = 768
CONV_HEADS = 12
D_SSM = 256
SSM_GROUPS = 16
SSM_GROUP_CH = 16
SSM_STATE = 64
N_EXPERTS = 64
TOP_K = 6
N_EXPERT_GROUPS = 8
TOPK_GROUPS = 4
EXPERTS_PER_GROUP = 8
ROUTED_SCALE = 2.5
RMS_EPS = 1e-6

S5_CHUNK = 128
V7X_VMEM_LIMIT = 56 * 1024 * 1024
NEG_INF = float("-inf")


def _rms(x, g):
    return x * lax.rsqrt(jnp.mean(x * x, axis=-1, keepdims=True) + RMS_EPS) * g


def _gelu_tanh(x):
    return 0.5 * x * (1.0 + jnp.tanh(math.sqrt(2.0 / math.pi) * (x + 0.044715 * (x * x * x))))


def _params(*sem):
    return pltpu.CompilerParams(dimension_semantics=sem, vmem_limit_bytes=V7X_VMEM_LIMIT)


def _adaln_kernel(c_ref, w_ref, b_ref, o_ref):
    c = c_ref[...]
    sc = c * jax.nn.sigmoid(c)
    o_ref[...] = jnp.dot(sc, w_ref[...], preferred_element_type=F32,
                         precision=lax.Precision.HIGHEST) + b_ref[...]


def _adaln(c, w_ada, b_ada):
    bsz, d = c.shape
    n = w_ada.shape[1]
    bn = 1024
    return pl.pallas_call(
        _adaln_kernel,
        grid=(n // bn,),
        in_specs=[pl.BlockSpec((bsz, d), lambda j: (0, 0)),
                  pl.BlockSpec((d, bn), lambda j: (0, j)),
                  pl.BlockSpec((1, bn), lambda j: (0, j))],
        out_specs=pl.BlockSpec((bsz, bn), lambda j: (0, j)),
        out_shape=jax.ShapeDtypeStruct((bsz, n), F32),
        compiler_params=_params("arbitrary"),
        name="adaln",
    )(c, w_ada, b_ada.reshape(1, n))


def _mix_front_kernel(x_ref, shift_ref, scale_ref, gmix_ref, win_ref, cw_ref, cb_ref, gconv_ref,
                      yconv_ref, ut_ref, zprev_ref):
    s = pl.program_id(1)

    @pl.when(s == 0)
    def _():
        zprev_ref[...] = jnp.zeros_like(zprev_ref)

    x = x_ref[0]
    h = _rms(x, gmix_ref[...]) * (1.0 + scale_ref[0]) + shift_ref[0]
    proj = jnp.dot(h.astype(BF16), win_ref[...], preferred_element_type=F32)
    b_gate = proj[:, :D_CONV]
    c_gate = proj[:, D_CONV:2 * D_CONV]
    v = proj[:, 2 * D_CONV:3 * D_CONV]
    u = proj[:, 3 * D_CONV:]

    z = c_gate * v
    ts = z.shape[0]
    prev = zprev_ref[...]
    rid = lax.broadcasted_iota(jnp.int32, z.shape, 0)
    z1 = jnp.where(rid == 0, prev[7:8, :], pltpu.roll(z, 1, axis=0))
    z2 = jnp.where(rid == 0, prev[6:7, :], jnp.where(rid == 1, prev[7:8, :], pltpu.roll(z, 2, axis=0)))
    zprev_ref[...] = z[ts - 8:, :]
    cw = cw_ref[...]
    conv = cw[0:1, :] * z2 + cw[1:2, :] * z1 + cw[2:3, :] * z + cb_ref[...]
    yconv_ref[0] = _rms(b_gate * conv, gconv_ref[...]).astype(BF16)
    ut_ref[0] = u.T.astype(BF16)


def _mix_front(x, shift1, scale1, g_mix, w_in_bf, conv_w, conv_b, g_conv, ts):
    bsz, seq, d = x.shape
    d_in = w_in_bf.shape[1]
    row = lambda b, s: (b, 0, 0)
    const2 = lambda b, s: (0, 0)
    return pl.pallas_call(
        _mix_front_kernel,
        grid=(bsz, seq // ts),
        in_specs=[pl.BlockSpec((1, ts, d), lambda b, s: (b, s, 0)),
                  pl.BlockSpec((1, 1, d), row),
                  pl.BlockSpec((1, 1, d), row),
                  pl.BlockSpec((1, d), const2),
                  pl.BlockSpec((d, d_in), const2),
                  pl.BlockSpec((8, D_CONV), const2),
                  pl.BlockSpec((1, D_CONV), const2),
                  pl.BlockSpec((1, D_CONV), const2)],
        out_specs=[pl.BlockSpec((1, ts, D_CONV), lambda b, s: (b, s, 0)),
                   pl.BlockSpec((1, D_SSM, ts), lambda b, s: (b, 0, s))],
        out_shape=[jax.ShapeDtypeStruct((bsz, seq, D_CONV), BF16),
                   jax.ShapeDtypeStruct((bsz, D_SSM, seq), BF16)],
        scratch_shapes=[pltpu.VMEM((8, D_CONV), F32)],
        compiler_params=_params("arbitrary", "arbitrary"),
        name="mix_front",
    )(x, shift1, scale1, g_mix.reshape(1, d), w_in_bf,
      jnp.pad(conv_w, ((0, 8 - conv_w.shape[0]), (0, 0))), conv_b.reshape(1, D_CONV),
      g_conv.reshape(1, D_CONV))


def _s5_tables(lam_re, lam_im, log_dt, b_re, b_im, c_re, c_im, d_skip, chunk, n_chunks):
    hp = lax.Precision.HIGHEST
    lr = lam_re.astype(F32)
    li = lam_im.astype(F32)
    dt = jnp.exp(log_dt.astype(F32))[:, None]
    mag = jnp.exp(lr * dt)
    ang = li * dt
    ab_re = mag * jnp.cos(ang)
    ab_im = mag * jnp.sin(ang)
    den = lr * lr + li * li
    nr = ab_re - 1.0
    ni = ab_im
    q_re = (nr * lr + ni * li) / den
    q_im = (ni * lr - nr * li) / den
    br = b_re.astype(F32)
    bi = b_im.astype(F32)
    bb_re = q_re[..., None] * br - q_im[..., None] * bi
    bb_im = q_re[..., None] * bi + q_im[..., None] * br

    def a_power(tau):
        t = tau.astype(F32)[None, ..., None]
        m = jnp.exp(t * (lr * dt)[:, None, :])
        th = t * ang[:, None, :]
        return m * jnp.cos(th), m * jnp.sin(th)

    L = chunk
    pw_re, pw_im = a_power(jnp.arange(L + 1))
    cr = c_re.astype(F32)[:, None]
    ci = c_im.astype(F32)[:, None]
    cp_re = cr * pw_re[:, :, None, :] - ci * pw_im[:, :, None, :]
    cp_im = cr * pw_im[:, :, None, :] + ci * pw_re[:, :, None, :]
    kern = (jnp.einsum('gthp,gpk->gthk', cp_re[:, :L], bb_re, precision=hp)
            - jnp.einsum('gthp,gpk->gthk', cp_im[:, :L], bb_im, precision=hp))
    eye = jnp.eye(SSM_GROUP_CH, dtype=F32)
    kern = kern.at[:, 0].add(d_skip.astype(F32).reshape(SSM_GROUPS, SSM_GROUP_CH)[:, :, None] * eye)
    kt = kern.transpose(0, 3, 2, 1)
    lag = jnp.arange(L)[None, :] - jnp.arange(L)[:, None]
    toep = jnp.where(lag >= 0, kt[..., jnp.clip(lag, 0, L - 1)], 0.0)
    toep = toep.transpose(0, 1, 3, 2, 4).reshape(SSM_GROUPS, SSM_GROUP_CH * L, SSM_GROUP_CH * L)

    rev_re = pw_re[:, L - 1::-1][:, :L]
    rev_im = pw_im[:, L - 1::-1][:, :L]
    e_re = rev_re[:, None, :, :] * bb_re.transpose(0, 2, 1)[:, :, None, :] \
        - rev_im[:, None, :, :] * bb_im.transpose(0, 2, 1)[:, :, None, :]
    e_im = rev_re[:, None, :, :] * bb_im.transpose(0, 2, 1)[:, :, None, :] \
        + rev_im[:, None, :, :] * bb_re.transpose(0, 2, 1)[:, :, None, :]
    e_mat = jnp.concatenate([e_re, e_im], axis=-1).reshape(SSM_GROUPS, SSM_GROUP_CH * L, 2 * SSM_STATE)

    f_re = cp_re[:, 1:].transpose(0, 3, 2, 1)
    f_im = -cp_im[:, 1:].transpose(0, 3, 2, 1)
    f_mat = jnp.concatenate([f_re, f_im], axis=1).reshape(SSM_GROUPS, 2 * SSM_STATE, SSM_GROUP_CH * L)

    n_steps = max(1, (n_chunks - 1).bit_length())
    sr, si = a_power(L * (2 ** jnp.arange(n_steps)))
    a1 = jnp.concatenate([sr, sr], axis=-1)
    a2 = jnp.concatenate([-si, si], axis=-1)
    a_pow = jnp.stack([a1, a2], axis=2)
    return toep.astype(BF16), e_mat.astype(BF16), f_mat.astype(BF16), a_pow


def _s5_kernel(u_ref, t_ref, e_ref, f_ref, a_ref, y_ref, *, n_chunks, n_steps):
    u = u_ref[0]
    st = jnp.dot(u, e_ref[0], preferred_element_type=F32)
    cidx = lax.broadcasted_iota(jnp.int32, st.shape, 0) % n_chunks
    p2 = st.shape[1]

    def shifted(xv, d):
        return jnp.where(cidx >= d, pltpu.roll(xv, d, axis=0), 0.0)

    for k in range(n_steps):
        d = 1 << k
        if d >= n_chunks:
            break
        z = shifted(st, d)
        a1 = a_ref[0, k, 0:1, :]
        a2 = a_ref[0, k, 1:2, :]
        st = st + a1 * z + a2 * pltpu.roll(z, p2 // 2, axis=1)
    s_in = shifted(st, 1)
    y = jnp.dot(u, t_ref[0], preferred_element_type=F32)
    y = y + jnp.dot(s_in.astype(BF16), f_ref[0], preferred_element_type=F32)
    y_ref[0] = y.astype(BF16)


def _s5_core(u_g, toep, e_mat, f_mat, a_pow, n_chunks):
    g, rows, hl = u_g.shape
    n_steps = a_pow.shape[1]
    p2 = e_mat.shape[2]
    blk = lambda i: (i, 0, 0)
    return pl.pallas_call(
        functools.partial(_s5_kernel, n_chunks=n_chunks, n_steps=n_steps),
        grid=(g,),
        in_specs=[pl.BlockSpec((1, rows, hl), blk),
                  pl.BlockSpec((1, hl, hl), blk),
                  pl.BlockSpec((1, hl, p2), blk),
                  pl.BlockSpec((1, p2, hl), blk),
                  pl.BlockSpec((1, n_steps, 2, p2), lambda i: (i, 0, 0, 0))],
        out_specs=pl.BlockSpec((1, rows, hl), blk),
        out_shape=jax.ShapeDtypeStruct((g, rows, hl), BF16),
        compiler_params=_params("arbitrary"),
        name="s5_core",
    )(u_g, toep, e_mat, f_mat, a_pow)


def _route(scores, biased):
    n_tok = scores.shape[1]
    shape3 = (N_EXPERT_GROUPS, EXPERTS_PER_GROUP, n_tok)
    sc3 = scores.reshape(shape3)
    b3 = biased.reshape(shape3)
    j_iota = lax.broadcasted_iota(jnp.int32, shape3, 1)
    g_iota3 = lax.broadcasted_iota(jnp.int32, shape3, 0)
    e_iota = g_iota3 * EXPERTS_PER_GROUP + j_iota

    m1 = jnp.max(b3, axis=1, keepdims=True)
    i1 = jnp.min(jnp.where(b3 == m1, j_iota, EXPERTS_PER_GROUP), axis=1, keepdims=True)
    m2 = jnp.max(jnp.where(j_iota == i1, NEG_INF, b3), axis=1, keepdims=True)
    gs = m1 + m2

    g_iota = lax.broadcasted_iota(jnp.int32, gs.shape, 0)
    gsel = jnp.zeros(gs.shape, F32)
    cur = gs
    for _ in range(TOPK_GROUPS):
        m = jnp.max(cur, axis=0, keepdims=True)
        ig = jnp.min(jnp.where(cur == m, g_iota, N_EXPERT_GROUPS), axis=0, keepdims=True)
        pick = g_iota == ig
        gsel = jnp.where(pick, 1.0, gsel)
        cur = jnp.where(pick, NEG_INF, cur)

    cur = jnp.where(gsel > 0.0, b3, NEG_INF)
    sel = jnp.zeros(shape3, F32)
    for _ in range(TOP_K):
        m = jnp.max(jnp.max(cur, axis=1, keepdims=True), axis=0, keepdims=True)
        ie = jnp.where(cur == m, e_iota, N_EXPERTS)
        ie = jnp.min(jnp.min(ie, axis=1, keepdims=True), axis=0, keepdims=True)
        pick = e_iota == ie
        sel = jnp.where(pick, 1.0, sel)
        cur = jnp.where(pick, NEG_INF, cur)
    picked = sel * sc3
    tot = jnp.sum(jnp.sum(picked, axis=1, keepdims=True), axis=0, keepdims=True)
    cw = picked / tot * ROUTED_SCALE
    return cw.reshape(N_EXPERTS, n_tok)


def _mix_back_kernel(x_ref, yconv_ref, yt_ref, gate1_ref, shift2_ref, scale2_ref, gate2_ref,
                     wglu_ref, bglu_ref, gssm_ref, woc_ref, wos_ref, gffn_ref,
                     wrh_ref, wrl_ref, rbias_ref, ws13_ref, ws2_ref,
                     x1_ref, h2_ref, cw_ref):
    y = yt_ref[0].astype(F32).T
    y = _gelu_tanh(y)
    y = y * jax.nn.sigmoid(jnp.dot(y.astype(BF16), wglu_ref[...], preferred_element_type=F32)
                           + bglu_ref[...])
    y_ssm = _rms(y, gssm_ref[...])
    mix = jnp.dot(yconv_ref[0], woc_ref[...], preferred_element_type=F32)
    mix = mix + jnp.dot(y_ssm.astype(BF16), wos_ref[...], preferred_element_type=F32)
    x1 = x_ref[0] + gate1_ref[0] * mix

    h2 = _rms(x1, gffn_ref[...]) * (1.0 + scale2_ref[0]) + shift2_ref[0]
    h2b = h2.astype(BF16)
    h2_ref[0] = h2b

    a = jnp.dot(h2b, ws13_ref[...], preferred_element_type=F32)
    f = a.shape[1] // 2
    act = (a[:, :f] * jax.nn.sigmoid(a[:, :f])) * a[:, f:]
    shared = jnp.dot(act.astype(BF16), ws2_ref[...], preferred_element_type=F32)
    x1_ref[0] = x1 + gate2_ref[0] * shared

    h2l = (h2 - h2b.astype(F32)).astype(BF16)
    logits = (jnp.dot(h2b, wrh_ref[...], preferred_element_type=F32)
              + jnp.dot(h2b, wrl_ref[...], preferred_element_type=F32)
              + jnp.dot(h2l, wrh_ref[...], preferred_element_type=F32))
    scores = jax.nn.sigmoid(logits.T)
    cw = _route(scores, scores + rbias_ref[...])
    cw_ref[0] = cw.T


def _mix_back(x, yconv, yt, gate1, shift2, scale2, gate2, w_glu_bf, b_glu, g_ssm, wo_conv, wo_ssm,
              g_ffn, wr_hi, wr_lo, router_bias, ws13, ws2_bf, ts):
    bsz, seq, d = x.shape
    row = lambda b, s: (b, 0, 0)
    const2 = lambda b, s: (0, 0)
    tile = lambda b, s: (b, s, 0)
    full = lambda a: pl.BlockSpec(a.shape, const2)
    args = (w_glu_bf, b_glu.reshape(1, D_SSM), g_ssm.reshape(1, D_SSM), wo_conv, wo_ssm,
            g_ffn.reshape(1, d), wr_hi, wr_lo, router_bias.reshape(N_EXPERTS, 1), ws13, ws2_bf)
    return pl.pallas_call(
        _mix_back_kernel,
        grid=(bsz, seq // ts),
        in_specs=[pl.BlockSpec((1, ts, d), tile),
                  pl.BlockSpec((1, ts, D_CONV), tile),
                  pl.BlockSpec((1, D_SSM, ts), lambda b, s: (b, 0, s)),
                  pl.BlockSpec((1, 1, d), row), pl.BlockSpec((1, 1, d), row),
                  pl.BlockSpec((1, 1, d), row), pl.BlockSpec((1, 1, d), row)]
                 + [full(a) for a in args],
        out_specs=[pl.BlockSpec((1, ts, d), tile),
                   pl.BlockSpec((1, ts, d), tile),
                   pl.BlockSpec((1, ts, N_EXPERTS), tile)],
        out_shape=[jax.ShapeDtypeStruct((bsz, seq, d), F32),
                   jax.ShapeDtypeStruct((bsz, seq, d), BF16),
                   jax.ShapeDtypeStruct((bsz, seq, N_EXPERTS), F32)],
        compiler_params=_params("arbitrary", "arbitrary"),
        name="mix_back",
    )(x, yconv, yt, gate1, shift2, scale2, gate2, *args)


def _moe_dense_kernel(h2_ref, cw_ref, x1_ref, gate2_ref, w13_ref, w2_ref, gfin_ref, o_ref, acc_ref):
    e = pl.program_id(1)

    @pl.when(e == 0)
    def _():
        acc_ref[...] = jnp.zeros_like(acc_ref)

    a = jnp.dot(h2_ref[...], w13_ref[0], preferred_element_type=F32)
    f = a.shape[1] // 2
    act = (a[:, :f] * jax.nn.sigmoid(a[:, :f])) * a[:, f:]
    cw = cw_ref[...]
    lane = lax.broadcasted_iota(jnp.int32, cw.shape, 1)
    col = jnp.sum(jnp.where(lane == e, cw, 0.0), axis=1, keepdims=True)
    acc_ref[...] += jnp.dot((act * col).astype(BF16), w2_ref[0], preferred_element_type=F32)

    @pl.when(e == pl.num_programs(1) - 1)
    def _():
        x2 = x1_ref[...] + gate2_ref[0] * acc_ref[...]
        o_ref[...] = _rms(x2, gfin_ref[...])


def _moe_dense(h2, cw, x1, gate2, w13, w2_bf, g_final, tm, tiles_per_seq):
    n_tok, d = h2.shape
    n_e = w13.shape[0]
    f2 = w13.shape[2]
    tok = lambda i, e: (i, 0)
    return pl.pallas_call(
        _moe_dense_kernel,
        grid=(n_tok // tm, n_e),
        in_specs=[pl.BlockSpec((tm, d), tok),
                  pl.BlockSpec((tm, n_e), tok),
                  pl.BlockSpec((tm, d), tok),
                  pl.BlockSpec((1, 1, d), lambda i, e: (i // tiles_per_seq, 0, 0)),
                  pl.BlockSpec((1, d, f2), lambda i, e: (e, 0, 0)),
                  pl.BlockSpec((1, f2 // 2, d), lambda i, e: (e, 0, 0)),
                  pl.BlockSpec((1, d), lambda i, e: (0, 0))],
        out_specs=pl.BlockSpec((tm, d), tok),
        out_shape=jax.ShapeDtypeStruct((n_tok, d), F32),
        scratch_shapes=[pltpu.VMEM((tm, d), F32)],
        compiler_params=_params("arbitrary", "arbitrary"),
        name="moe_dense",
    )(h2, cw, x1, gate2, w13, w2_bf, g_final.reshape(1, d))


def kernel(x, c, w_ada, b_ada, g_mix, w_in, conv_w, conv_b, g_conv, lam_re, lam_im, log_dt, b_re, b_im,
           c_re, c_im, d_skip, w_glu, b_glu, g_ssm, w_out, g_ffn, w_router, router_bias, w1, w3, w2,
           ws1, ws3, ws2, g_final):
    bsz, seq, d = x.shape
    ts = min(512, seq)
    L = min(S5_CHUNK, seq)
    n_chunks = seq // L
    G, H = SSM_GROUPS, SSM_GROUP_CH

    mod = _adaln(c, w_ada, b_ada).reshape(bsz, 6, 1, d)
    shift1, scale1, gate1, shift2, scale2, gate2 = (mod[:, i] for i in range(6))

    yconv, ut = _mix_front(x, shift1, scale1, g_mix, w_in.astype(BF16), conv_w, conv_b, g_conv, ts)

    u_g = ut.reshape(bsz, G, H, n_chunks, L).transpose(1, 0, 3, 2, 4).reshape(G, bsz * n_chunks, H * L)
    toep, e_mat, f_mat, a_pow = _s5_tables(lam_re, lam_im, log_dt, b_re, b_im, c_re, c_im, d_skip,
                                           L, n_chunks)
    y_g = _s5_core(u_g, toep, e_mat, f_mat, a_pow, n_chunks)
    yt = y_g.reshape(G, bsz, n_chunks, H, L).transpose(1, 0, 3, 2, 4).reshape(bsz, G * H, seq)

    wr_hi = w_router.astype(BF16)
    wr_lo = (w_router - wr_hi.astype(F32)).astype(BF16)
    w_out_bf = w_out.astype(BF16)
    x1, h2, cw = _mix_back(
        x, yconv, yt, gate1, shift2, scale2, gate2, w_glu.astype(BF16), b_glu, g_ssm,
        w_out_bf[:D_CONV], w_out_bf[D_CONV:], g_ffn, wr_hi, wr_lo, router_bias,
        jnp.concatenate([ws1, ws3], axis=1).astype(BF16), ws2.astype(BF16), ts)

    n_tok = bsz * seq
    tm = min(1024, seq)
    w13 = jnp.concatenate([w1, w3], axis=2).astype(BF16)
    out = _moe_dense(h2.reshape(n_tok, d), cw.reshape(n_tok, N_EXPERTS), x1.reshape(n_tok, d), gate2,
                     w13, w2.astype(BF16), g_final, tm, seq // tm)
    return out.reshape(bsz, seq, d)
```

```python
import functools
import math

import jax
import jax.numpy as jnp
from jax import lax
from jax.experimental import pallas as pl
from jax.experimental.pallas import tpu as pltpu

F32 = jnp.float32
BF16 = jnp.bfloat16
U32 = jnp.uint32
I32 = jnp.int32

D_CONV = 768
D_SSM = 256
SSM_GROUPS = 16
SSM_GROUP_CH = 16
SSM_STATE = 64
N_EXPERTS = 64
TOP_K = 6
N_EXPERT_GROUPS = 8
TOPK_GROUPS = 4
EXPERTS_PER_GROUP = 8
ROUTED_SCALE = 2.5
RMS_EPS = 1e-6

S5_CHUNK = 128
EXPERT_ROWS = 256
COMBINE_TOKENS = 256
DMA_UNROLL = 8
V7X_VMEM_LIMIT = 56 * 1024 * 1024
NEG_INF = float("-inf")


def _rms(x, g):
    return x * lax.rsqrt(jnp.mean(x * x, axis=-1, keepdims=True) + RMS_EPS) * g


def _gelu_tanh(x):
    return 0.5 * x * (1.0 + jnp.tanh(math.sqrt(2.0 / math.pi) * (x + 0.044715 * (x * x * x))))


def _silu(x):
    return x * jax.nn.sigmoid(x)


def _params(*sem):
    return pltpu.CompilerParams(dimension_semantics=sem, vmem_limit_bytes=V7X_VMEM_LIMIT)


def _pack_halves(y):
    m = y.shape[1] // 2
    bits = lax.bitcast_convert_type(y.astype(BF16).astype(F32), U32)
    return ((bits[:, m:] >> 16) << 16) | (bits[:, :m] >> 16)


def _unpack_halves(w):
    lo = lax.bitcast_convert_type(w << 16, F32)
    hi = lax.bitcast_convert_type((w >> 16) << 16, F32)
    return lo, hi


def _adaln_kernel(c_ref, w_ref, b_ref, o_ref):
    o_ref[...] = jnp.dot(_silu(c_ref[...]), w_ref[...], preferred_element_type=F32,
                         precision=lax.Precision.HIGHEST) + b_ref[...]


def _adaln(c, w_ada, b_ada):
    bsz, d = c.shape
    n = w_ada.shape[1]
    bn = 1024
    return pl.pallas_call(
        _adaln_kernel,
        grid=(n // bn,),
        in_specs=[pl.BlockSpec((bsz, d), lambda j: (0, 0)),
                  pl.BlockSpec((d, bn), lambda j: (0, j)),
                  pl.BlockSpec((1, bn), lambda j: (0, j))],
        out_specs=pl.BlockSpec((bsz, bn), lambda j: (0, j)),
        out_shape=jax.ShapeDtypeStruct((bsz, n), F32),
        compiler_params=_params("arbitrary"),
        name="adaln",
    )(c, w_ada, b_ada.reshape(1, n))


def _mix_front_kernel(x_ref, shift_ref, scale_ref, gmix_ref, win_ref, cw_ref, cb_ref, gconv_ref,
                      yconv_ref, ut_ref, zprev_ref):
    s = pl.program_id(1)

    @pl.when(s == 0)
    def _():
        zprev_ref[...] = jnp.zeros_like(zprev_ref)

    x = x_ref[0]
    h = _rms(x, gmix_ref[...]) * (1.0 + scale_ref[0]) + shift_ref[0]
    proj = jnp.dot(h.astype(BF16), win_ref[...], preferred_element_type=F32)
    b_gate = proj[:, :D_CONV]
    c_gate = proj[:, D_CONV:2 * D_CONV]
    v = proj[:, 2 * D_CONV:3 * D_CONV]
    u = proj[:, 3 * D_CONV:]

    z = c_gate * v
    ts = z.shape[0]
    prev = zprev_ref[...]
    rid = lax.broadcasted_iota(I32, z.shape, 0)
    z1 = jnp.where(rid == 0, prev[7:8, :], pltpu.roll(z, 1, axis=0))
    z2 = jnp.where(rid == 0, prev[6:7, :], jnp.where(rid == 1, prev[7:8, :], pltpu.roll(z, 2, axis=0)))
    zprev_ref[...] = z[ts - 8:, :]
    cw = cw_ref[...]
    conv = cw[0:1, :] * z2 + cw[1:2, :] * z1 + cw[2:3, :] * z + cb_ref[...]
    yconv_ref[0] = _rms(b_gate * conv, gconv_ref[...]).astype(BF16)
    ut_ref[0] = u.T.astype(BF16)


def _mix_front(x, shift1, scale1, g_mix, w_in_bf, conv_w, conv_b, g_conv, ts):
    bsz, seq, d = x.shape
    d_in = w_in_bf.shape[1]
    row = lambda b, s: (b, 0, 0)
    const2 = lambda b, s: (0, 0)
    return pl.pallas_call(
        _mix_front_kernel,
        grid=(bsz, seq // ts),
        in_specs=[pl.BlockSpec((1, ts, d), lambda b, s: (b, s, 0)),
                  pl.BlockSpec((1, 1, d), row),
                  pl.BlockSpec((1, 1, d), row),
                  pl.BlockSpec((1, d), const2),
                  pl.BlockSpec((d, d_in), const2),
                  pl.BlockSpec((8, D_CONV), const2),
                  pl.BlockSpec((1, D_CONV), const2),
                  pl.BlockSpec((1, D_CONV), const2)],
        out_specs=[pl.BlockSpec((1, ts, D_CONV), lambda b, s: (b, s, 0)),
                   pl.BlockSpec((1, D_SSM, ts), lambda b, s: (b, 0, s))],
        out_shape=[jax.ShapeDtypeStruct((bsz, seq, D_CONV), BF16),
                   jax.ShapeDtypeStruct((bsz, D_SSM, seq), BF16)],
        scratch_shapes=[pltpu.VMEM((8, D_CONV), F32)],
        compiler_params=_params("arbitrary", "arbitrary"),
        name="mix_front",
    )(x, shift1, scale1, g_mix.reshape(1, d), w_in_bf,
      jnp.pad(conv_w, ((0, 8 - conv_w.shape[0]), (0, 0))), conv_b.reshape(1, D_CONV),
      g_conv.reshape(1, D_CONV))


def _s5_tables(lam_re, lam_im, log_dt, b_re, b_im, c_re, c_im, d_skip, chunk, n_chunks):
    hp = lax.Precision.HIGHEST
    G, H, P, L = SSM_GROUPS, SSM_GROUP_CH, SSM_STATE, chunk
    lr = lam_re.astype(F32)
    li = lam_im.astype(F32)
    dt = jnp.exp(log_dt.astype(F32))[:, None]
    mag = jnp.exp(lr * dt)
    ang = li * dt
    ab_re = mag * jnp.cos(ang)
    ab_im = mag * jnp.sin(ang)
    den = lr * lr + li * li
    nr = ab_re - 1.0
    ni = ab_im
    q_re = (nr * lr + ni * li) / den
    q_im = (ni * lr - nr * li) / den
    br = b_re.astype(F32)
    bi = b_im.astype(F32)
    bb_re = q_re[..., None] * br - q_im[..., None] * bi
    bb_im = q_re[..., None] * bi + q_im[..., None] * br

    def a_power(tau):
        t = tau.astype(F32)[None, :, None]
        m = jnp.exp(t * (lr * dt)[:, None, :])
        th = t * ang[:, None, :]
        return m * jnp.cos(th), m * jnp.sin(th)

    pw_re, pw_im = a_power(jnp.arange(L + 1))
    cr = c_re.astype(F32)[:, None]
    ci = c_im.astype(F32)[:, None]
    cp_re = cr * pw_re[:, :, None, :] - ci * pw_im[:, :, None, :]
    cp_im = cr * pw_im[:, :, None, :] + ci * pw_re[:, :, None, :]
    kern = (jnp.einsum('gthp,gpk->gthk', cp_re[:, :L], bb_re, precision=hp)
            - jnp.einsum('gthp,gpk->gthk', cp_im[:, :L], bb_im, precision=hp))
    kern = kern.at[:, 0].add(d_skip.astype(F32).reshape(G, H)[:, :, None] * jnp.eye(H, dtype=F32))
    kern = kern.transpose(0, 3, 2, 1).reshape(G, H * H, L)

    rev_re = pw_re[:, L - 1::-1]
    rev_im = pw_im[:, L - 1::-1]
    bt_re = bb_re.transpose(0, 2, 1)[:, :, None, :]
    bt_im = bb_im.transpose(0, 2, 1)[:, :, None, :]
    e_re = rev_re[:, None] * bt_re - rev_im[:, None] * bt_im
    e_im = rev_re[:, None] * bt_im + rev_im[:, None] * bt_re
    e_mat = jnp.concatenate([e_re, e_im], axis=-1).reshape(G, H * L, 2 * P)

    f_re = cp_re[:, 1:].transpose(0, 3, 2, 1)
    f_im = -cp_im[:, 1:].transpose(0, 3, 2, 1)
    f_mat = jnp.concatenate([f_re, f_im], axis=1).reshape(G, 2 * P, H * L)

    n_steps = max(1, (n_chunks - 1).bit_length())
    sr, si = a_power(L * (2 ** jnp.arange(n_steps)))
    a_pow = jnp.stack([jnp.concatenate([sr, sr], axis=-1),
                       jnp.concatenate([-si, si], axis=-1)], axis=2)
    return kern, e_mat.astype(BF16), f_mat.astype(BF16), a_pow


def _s5_kernel(u_ref, k_ref, e_ref, f_ref, a_ref, y_ref, toep_ref, *, n_chunks, n_steps, chunk):
    L, H = chunk, SSM_GROUP_CH

    causal = lax.broadcasted_iota(I32, (L, L), 1) >= lax.broadcasted_iota(I32, (L, L), 0)

    def build(hin, carry):
        r0 = pl.multiple_of(hin * L, L)
        for hout in range(H):
            krow = k_ref[0, pl.ds(hin * H + hout, 1), :]
            blk = pltpu.roll(jnp.broadcast_to(krow, (L, L)), 0, axis=1, stride=1, stride_axis=0)
            toep_ref[pl.ds(r0, L), hout * L:(hout + 1) * L] = jnp.where(causal, blk, 0.0).astype(BF16)
        return carry

    lax.fori_loop(0, H, build, 0)

    u = u_ref[0]
    st = jnp.dot(u, e_ref[0], preferred_element_type=F32)
    cidx = lax.broadcasted_iota(I32, st.shape, 0) % n_chunks
    p2 = st.shape[1]

    def shifted(xv, d):
        return jnp.where(cidx >= d, pltpu.roll(xv, d, axis=0), 0.0)

    for k in range(n_steps):
        d = 1 << k
        if d >= n_chunks:
            break
        z = shifted(st, d)
        st = st + a_ref[0, k, 0:1, :] * z + a_ref[0, k, 1:2, :] * pltpu.roll(z, p2 // 2, axis=1)
    s_in = shifted(st, 1)
    y = jnp.dot(u, toep_ref[...], preferred_element_type=F32)
    y = y + jnp.dot(s_in.astype(BF16), f_ref[0], preferred_element_type=F32)
    y_ref[0] = y.astype(BF16)


def _s5_core(u_g, kern, e_mat, f_mat, a_pow, n_chunks, chunk):
    g, rows, hl = u_g.shape
    n_steps = a_pow.shape[1]
    p2 = e_mat.shape[2]
    blk = lambda i: (i, 0, 0)
    return pl.pallas_call(
        functools.partial(_s5_kernel, n_chunks=n_chunks, n_steps=n_steps, chunk=chunk),
        grid=(g,),
        in_specs=[pl.BlockSpec((1, rows, hl), blk),
                  pl.BlockSpec((1,) + kern.shape[1:], blk),
                  pl.BlockSpec((1, hl, p2), blk),
                  pl.BlockSpec((1, p2, hl), blk),
                  pl.BlockSpec((1, n_steps, 2, p2), lambda i: (i, 0, 0, 0))],
        out_specs=pl.BlockSpec((1, rows, hl), blk),
        out_shape=jax.ShapeDtypeStruct((g, rows, hl), BF16),
        scratch_shapes=[pltpu.VMEM((hl, hl), BF16)],
        compiler_params=_params("arbitrary"),
        name="s5_core",
    )(u_g, kern, e_mat, f_mat, a_pow)


def _route(scores, biased, tri, base):
    n_tok = scores.shape[1]
    shape3 = (N_EXPERT_GROUPS, EXPERTS_PER_GROUP, n_tok)
    sc3 = scores.reshape(shape3)
    b3 = biased.reshape(shape3)
    j_iota = lax.broadcasted_iota(I32, shape3, 1)
    e_iota = lax.broadcasted_iota(I32, shape3, 0) * EXPERTS_PER_GROUP + j_iota

    def red(fn, x):
        return fn(fn(x, axis=1, keepdims=True), axis=0, keepdims=True)

    m1 = jnp.max(b3, axis=1, keepdims=True)
    i1 = jnp.min(jnp.where(b3 == m1, j_iota, EXPERTS_PER_GROUP), axis=1, keepdims=True)
    m2 = jnp.max(jnp.where(j_iota == i1, NEG_INF, b3), axis=1, keepdims=True)
    gs = m1 + m2

    g_iota = lax.broadcasted_iota(I32, gs.shape, 0)
    gsel = jnp.zeros(gs.shape, F32)
    cur = gs
    for _ in range(TOPK_GROUPS):
        m = jnp.max(cur, axis=0, keepdims=True)
        ig = jnp.min(jnp.where(cur == m, g_iota, N_EXPERT_GROUPS), axis=0, keepdims=True)
        pick = g_iota == ig
        gsel = jnp.where(pick, 1.0, gsel)
        cur = jnp.where(pick, NEG_INF, cur)

    cur = jnp.where(gsel > 0.0, b3, NEG_INF)
    sel = jnp.zeros(shape3, F32)
    ids, vals = [], []
    for _ in range(TOP_K):
        m = red(jnp.max, cur)
        ie = red(jnp.min, jnp.where(cur == m, e_iota, N_EXPERTS))
        pick = e_iota == ie
        ids.append(ie)
        vals.append(red(jnp.sum, jnp.where(pick, sc3, 0.0)))
        sel = jnp.where(pick, 1.0, sel)
        cur = jnp.where(pick, NEG_INF, cur)
    tot = functools.reduce(lambda a, b: a + b, vals)

    sel2 = sel.reshape(N_EXPERTS, n_tok)
    before = jnp.dot(sel2.astype(BF16), tri, preferred_element_type=F32) + base
    before3 = before.reshape(shape3)
    ranks = [red(jnp.sum, jnp.where(e_iota == ie, before3, 0.0)) for ie in ids]

    def rows(parts, dtype):
        parts = [p.reshape(1, n_tok).astype(dtype) for p in parts]
        return jnp.concatenate(parts + [jnp.zeros((8 - len(parts), n_tok), dtype)], axis=0)

    e8 = rows(ids, I32)
    w8 = rows([v / tot * ROUTED_SCALE for v in vals], F32)
    r8 = rows(ranks, I32)
    return e8, w8, r8, jnp.sum(sel2, axis=1, keepdims=True)


def _mix_back_kernel(x_ref, yconv_ref, yt_ref, gate1_ref, shift2_ref, scale2_ref, gate2_ref,
                     wglu_ref, bglu_ref, gssm_ref, woc_ref, wos_ref, gffn_ref,
                     wrh_ref, wrl_ref, rbias_ref, ws13_ref, ws2_ref, tri_ref,
                     x1_ref, h2p_ref, e8_ref, w8_ref, r8_ref, cnt_ref):
    first = jnp.logical_and(pl.program_id(0) == 0, pl.program_id(1) == 0)

    @pl.when(first)
    def _():
        cnt_ref[...] = jnp.zeros_like(cnt_ref)

    y = yt_ref[0].astype(F32).T
    y = _gelu_tanh(y)
    y = y * jax.nn.sigmoid(jnp.dot(y.astype(BF16), wglu_ref[...], preferred_element_type=F32)
                           + bglu_ref[...])
    y_ssm = _rms(y, gssm_ref[...])
    mix = jnp.dot(yconv_ref[0], woc_ref[...], preferred_element_type=F32)
    mix = mix + jnp.dot(y_ssm.astype(BF16), wos_ref[...], preferred_element_type=F32)
    x1 = x_ref[0] + gate1_ref[0] * mix

    h2 = _rms(x1, gffn_ref[...]) * (1.0 + scale2_ref[0]) + shift2_ref[0]
    h2b = h2.astype(BF16)
    h2p_ref[0] = _pack_halves(h2)

    a = jnp.dot(h2b, ws13_ref[...], preferred_element_type=F32)
    f = a.shape[1] // 2
    act = _silu(a[:, :f]) * a[:, f:]
    shared = jnp.dot(act.astype(BF16), ws2_ref[...], preferred_element_type=F32)
    x1_ref[0] = x1 + gate2_ref[0] * shared

    h2l = (h2 - h2b.astype(F32)).astype(BF16)
    logits = (jnp.dot(h2b, wrh_ref[...], preferred_element_type=F32)
              + jnp.dot(h2b, wrl_ref[...], preferred_element_type=F32)
              + jnp.dot(h2l, wrh_ref[...], preferred_element_type=F32))
    scores = jax.nn.sigmoid(logits.T)
    base = cnt_ref[:, 0:1]
    e8, w8, r8, cnt = _route(scores, scores + rbias_ref[...], tri_ref[...], base)
    e8_ref[...] = e8
    w8_ref[0] = w8.T
    r8_ref[...] = r8
    cnt_ref[...] = cnt_ref[...] + cnt


def _mix_back(x, yconv, yt, gate1, shift2, scale2, gate2, w_glu_bf, b_glu, g_ssm, wo_conv, wo_ssm,
              g_ffn, wr_hi, wr_lo, router_bias, ws13, ws2_bf, ts):
    bsz, seq, d = x.shape
    n_tok = bsz * seq
    tiles = seq // ts
    row = lambda b, s: (b, 0, 0)
    const2 = lambda b, s: (0, 0)
    tile = lambda b, s: (b, s, 0)
    flat = lambda b, s: (0, b * tiles + s)
    full = lambda a: pl.BlockSpec(a.shape, const2)
    tri = jnp.triu(jnp.ones((ts, ts), BF16), k=1)
    args = (w_glu_bf, b_glu.reshape(1, D_SSM), g_ssm.reshape(1, D_SSM), wo_conv, wo_ssm,
            g_ffn.reshape(1, d), wr_hi, wr_lo, router_bias.reshape(N_EXPERTS, 1), ws13, ws2_bf, tri)
    return pl.pallas_call(
        _mix_back_kernel,
        grid=(bsz, tiles),
        in_specs=[pl.BlockSpec((1, ts, d), tile),
                  pl.BlockSpec((1, ts, D_CONV), tile),
                  pl.BlockSpec((1, D_SSM, ts), lambda b, s: (b, 0, s)),
                  pl.BlockSpec((1, 1, d), row), pl.BlockSpec((1, 1, d), row),
                  pl.BlockSpec((1, 1, d), row), pl.BlockSpec((1, 1, d), row)]
                 + [full(a) for a in args],
        out_specs=[pl.BlockSpec((1, ts, d), tile),
                   pl.BlockSpec((1, ts, d // 2), tile),
                   pl.BlockSpec((8, ts), flat),
                   pl.BlockSpec((1, ts, 8), tile),
                   pl.BlockSpec((8, ts), flat),
                   pl.BlockSpec((N_EXPERTS, 128), const2)],
        out_shape=[jax.ShapeDtypeStruct((bsz, seq, d), F32),
                   jax.ShapeDtypeStruct((bsz, seq, d // 2), U32),
                   jax.ShapeDtypeStruct((8, n_tok), I32),
                   jax.ShapeDtypeStruct((bsz, seq, 8), F32),
                   jax.ShapeDtypeStruct((8, n_tok), I32),
                   jax.ShapeDtypeStruct((N_EXPERTS, 128), F32)],
        compiler_params=_params("arbitrary", "arbitrary"),
        name="mix_back",
    )(x, yconv, yt, gate1, shift2, scale2, gate2, *args)


def _row_gather(idx_ref, col, src_hbm, dst_ref, sem, n_rows):
    def body(it, carry):
        for j in range(DMA_UNROLL):
            r = it * DMA_UNROLL + j
            pltpu.make_async_copy(src_hbm.at[pl.ds(idx_ref[col, r], 1)], dst_ref.at[pl.ds(r, 1)], sem).start()
        return carry
    lax.fori_loop(0, n_rows // DMA_UNROLL, body, 0)


def _experts_kernel(be_ref, tokc_ref, tokn_ref, h2p_hbm, w1_ref, w3_ref, w2_ref, y_ref,
                    xbuf, w13_s, w2_s, sem):
    i = pl.program_id(0)
    n = pl.num_programs(0)
    bm = xbuf.shape[1]
    slot = i % 2

    @pl.when(i == 0)
    def _():
        _row_gather(tokc_ref, 0, h2p_hbm, xbuf.at[0], sem.at[0], bm)

    @pl.when(i + 1 < n)
    def _():
        _row_gather(tokn_ref, 0, h2p_hbm, xbuf.at[1 - slot], sem.at[1 - slot], bm)

    changed = jnp.logical_or(i == 0, be_ref[i] != be_ref[jnp.maximum(i - 1, 0)])

    @pl.when(changed)
    def _():
        f = w1_ref.shape[2]
        w13_s[:, :f] = w1_ref[0].astype(BF16)
        w13_s[:, f:] = w3_ref[0].astype(BF16)
        w2_s[...] = w2_ref[0].astype(BF16)

    pltpu.make_async_copy(h2p_hbm.at[pl.ds(0, bm)], xbuf.at[slot], sem.at[slot]).wait()
    lo, hi = _unpack_halves(xbuf[slot])
    half = lo.shape[1]
    a = jnp.dot(lo.astype(BF16), w13_s[:half, :], preferred_element_type=F32)
    a = a + jnp.dot(hi.astype(BF16), w13_s[half:, :], preferred_element_type=F32)
    f = a.shape[1] // 2
    act = _silu(a[:, :f]) * a[:, f:]
    y_ref[...] = _pack_halves(jnp.dot(act.astype(BF16), w2_s[...], preferred_element_type=F32))


def _experts(block_e, tok_buf, h2p, w1, w3, w2, bm):
    n_rows = tok_buf.shape[0]
    n_blocks = n_rows // bm
    n_tok, half = h2p.shape
    d, f = w1.shape[1], w1.shape[2]
    tok3 = tok_buf.reshape(n_blocks, 1, bm)
    smem_blk = lambda fn: pl.BlockSpec((1, 1, bm), fn, memory_space=pltpu.SMEM)
    grid_spec = pltpu.PrefetchScalarGridSpec(
        num_scalar_prefetch=1,
        grid=(n_blocks,),
        in_specs=[smem_blk(lambda i, be: (i, 0, 0)),
                  smem_blk(lambda i, be: (jnp.minimum(i + 1, n_blocks - 1), 0, 0)),
                  pl.BlockSpec(memory_space=pl.ANY),
                  pl.BlockSpec((1, d, f), lambda i, be: (be[i], 0, 0)),
                  pl.BlockSpec((1, d, f), lambda i, be: (be[i], 0, 0)),
                  pl.BlockSpec((1, f, d), lambda i, be: (be[i], 0, 0))],
        out_specs=pl.BlockSpec((bm, half), lambda i, be: (i, 0)),
        scratch_shapes=[pltpu.VMEM((2, bm, half), U32),
                        pltpu.VMEM((d, 2 * f), BF16),
                        pltpu.VMEM((f, d), BF16),
                        pltpu.SemaphoreType.DMA((2,))],
    )

    def body(be_ref, tokc_ref, tokn_ref, *rest):
        _experts_kernel(be_ref, tokc_ref.at[0], tokn_ref.at[0], *rest)

    return pl.pallas_call(
        body,
        grid_spec=grid_spec,
        out_shape=jax.ShapeDtypeStruct((n_rows, half), U32),
        compiler_params=_params("arbitrary"),
        name="experts",
    )(block_e, tok3, tok3, h2p, w1, w3, w2)


def _combine_kernel(posc_ref, posn_ref, ys_hbm, w8_ref, x1_ref, gate2_ref, gfin_ref, o_ref, gbuf, sem):
    i = pl.program_id(0)
    n = pl.num_programs(0)
    tt = gbuf.shape[2]
    slot = i % 2

    def start_all(pos_ref, s):
        for k in range(TOP_K):
            _row_gather(pos_ref, k, ys_hbm, gbuf.at[s, k], sem.at[s], tt)

    @pl.when(i == 0)
    def _():
        start_all(posc_ref, 0)

    @pl.when(i + 1 < n)
    def _():
        start_all(posn_ref, 1 - slot)

    for k in range(TOP_K):
        pltpu.make_async_copy(ys_hbm.at[pl.ds(0, tt)], gbuf.at[slot, k], sem.at[slot]).wait()
    w8 = w8_ref[...]
    acc_lo = acc_hi = None
    for k in range(TOP_K):
        lo, hi = _unpack_halves(gbuf[slot, k])
        wk = w8[:, k:k + 1]
        acc_lo = wk * lo if acc_lo is None else acc_lo + wk * lo
        acc_hi = wk * hi if acc_hi is None else acc_hi + wk * hi
    routed = jnp.concatenate([acc_lo, acc_hi], axis=1)
    o_ref[...] = _rms(x1_ref[...] + gate2_ref[0] * routed, gfin_ref[...])


def _combine(pos8, ys, w8, x1, gate2, g_final, tt, tiles_per_seq):
    n_tok, d = x1.shape
    half = ys.shape[1]
    n_tiles = n_tok // tt
    tok = lambda i: (i, 0)
    smem_blk = lambda fn: pl.BlockSpec((8, tt), fn, memory_space=pltpu.SMEM)
    return pl.pallas_call(
        _combine_kernel,
        grid=(n_tiles,),
        in_specs=[smem_blk(lambda i: (0, i)),
                  smem_blk(lambda i: (0, jnp.minimum(i + 1, n_tiles - 1))),
                  pl.BlockSpec(memory_space=pl.ANY),
                  pl.BlockSpec((tt, 8), tok),
                  pl.BlockSpec((tt, d), tok),
                  pl.BlockSpec((1, 1, d), lambda i: (i // tiles_per_seq, 0, 0)),
                  pl.BlockSpec((1, d), lambda i: (0, 0))],
        out_specs=pl.BlockSpec((tt, d), tok),
        out_shape=jax.ShapeDtypeStruct((n_tok, d), F32),
        scratch_shapes=[pltpu.VMEM((2, TOP_K, tt, half), U32),
                        pltpu.SemaphoreType.DMA((2,))],
        compiler_params=_params("arbitrary"),
        name="combine",
    )(pos8, pos8, ys, w8, x1, gate2, g_final.reshape(1, d))


def _dispatch_plan(e8, r8, counts, bm):
    n_tok = e8.shape[1]
    n_assign = n_tok * TOP_K
    n_blocks = -(-(n_assign + N_EXPERTS * (bm - 1)) // bm)
    cnt = counts.astype(I32)
    padded = (cnt + bm - 1) // bm * bm
    pend = jnp.cumsum(padded)
    pstart = pend - padded
    pos8 = pstart[e8] + r8
    tok = jnp.broadcast_to(jnp.arange(n_tok, dtype=I32), (TOP_K, n_tok))
    tok_buf = jnp.zeros((n_blocks * bm,), I32).at[pos8[:TOP_K].reshape(-1)].set(tok.reshape(-1))
    block_e = jnp.minimum(jnp.searchsorted(pend, jnp.arange(n_blocks, dtype=I32) * bm, side='right'),
                          N_EXPERTS - 1).astype(I32)
    return pos8, tok_buf, block_e


def kernel(x, c, w_ada, b_ada, g_mix, w_in, conv_w, conv_b, g_conv, lam_re, lam_im, log_dt, b_re, b_im,
           c_re, c_im, d_skip, w_glu, b_glu, g_ssm, w_out, g_ffn, w_router, router_bias, w1, w3, w2,
           ws1, ws3, ws2, g_final):
    bsz, seq, d = x.shape
    n_tok = bsz * seq
    ts = min(512, seq)
    L = min(S5_CHUNK, seq)
    n_chunks = seq // L
    G, H = SSM_GROUPS, SSM_GROUP_CH

    mod = _adaln(c, w_ada, b_ada).reshape(bsz, 6, 1, d)
    shift1, scale1, gate1, shift2, scale2, gate2 = (mod[:, i] for i in range(6))

    yconv, ut = _mix_front(x, shift1, scale1, g_mix, w_in.astype(BF16), conv_w, conv_b, g_conv, ts)

    u_g = ut.reshape(bsz, G, H, n_chunks, L).transpose(1, 0, 3, 2, 4).reshape(G, bsz * n_chunks, H * L)
    kern, e_mat, f_mat, a_pow = _s5_tables(lam_re, lam_im, log_dt, b_re, b_im, c_re, c_im, d_skip,
                                           L, n_chunks)
    y_g = _s5_core(u_g, kern, e_mat, f_mat, a_pow, n_chunks, L)
    yt = y_g.reshape(G, bsz, n_chunks, H, L).transpose(1, 0, 3, 2, 4).reshape(bsz, G * H, seq)

    wr_hi = w_router.astype(BF16)
    wr_lo = (w_router - wr_hi.astype(F32)).astype(BF16)
    w_out_bf = w_out.astype(BF16)
    x1, h2p, e8, w8, r8, counts = _mix_back(
        x, yconv, yt, gate1, shift2, scale2, gate2, w_glu.astype(BF16), b_glu, g_ssm,
        w_out_bf[:D_CONV], w_out_bf[D_CONV:], g_ffn, wr_hi, wr_lo, router_bias,
        jnp.concatenate([ws1, ws3], axis=1).astype(BF16), ws2.astype(BF16), ts)

    pos8, tok_buf, block_e = _dispatch_plan(e8, r8, counts[:, 0], EXPERT_ROWS)
    ys = _experts(block_e, tok_buf, h2p.reshape(n_tok, d // 2), w1, w3, w2, EXPERT_ROWS)
    tt = min(COMBINE_TOKENS, seq)
    out = _combine(pos8, ys, w8.reshape(n_tok, 8), x1.reshape(n_tok, d), gate2, g_final, tt, seq // tt)
    return out.reshape(bsz, seq, d)
```

```python
import functools
import math

import jax
import jax.numpy as jnp
from jax import lax
from jax.experimental import pallas as pl
from jax.experimental.pallas import tpu as pltpu
from jax.experimental.pallas import tpu_sc as plsc

F32 = jnp.float32
BF16 = jnp.bfloat16
U32 = jnp.uint32
I32 = jnp.int32

D_CONV = 768
D_SSM = 256
SSM_GROUPS = 16
SSM_GROUP_CH = 16
SSM_STATE = 64
N_EXPERTS = 64
TOP_K = 6
N_EXPERT_GROUPS = 8
TOPK_GROUPS = 4
EXPERTS_PER_GROUP = 8
ROUTED_SCALE = 2.5
RMS_EPS = 1e-6

S5_CHUNK = 128
EXPERT_ROWS = 512
COMBINE_TOKENS = 512
V7X_SC_CORES = 2
V7X_SC_SUBCORES = 16
SC_WORKERS = V7X_SC_CORES * V7X_SC_SUBCORES
SC_WINDOW = 64
V7X_VMEM_LIMIT = 56 * 1024 * 1024
NEG_INF = float("-inf")


def _rms(x, g):
    return x * lax.rsqrt(jnp.mean(x * x, axis=-1, keepdims=True) + RMS_EPS) * g


def _gelu_tanh(x):
    return 0.5 * x * (1.0 + jnp.tanh(math.sqrt(2.0 / math.pi) * (x + 0.044715 * (x * x * x))))


def _silu(x):
    return x * jax.nn.sigmoid(x)


def _params(*sem):
    return pltpu.CompilerParams(dimension_semantics=sem, vmem_limit_bytes=V7X_VMEM_LIMIT)


def _pack_halves(y):
    m = y.shape[1] // 2
    bits = lax.bitcast_convert_type(y.astype(BF16).astype(F32), U32)
    return ((bits[:, m:] >> 16) << 16) | (bits[:, :m] >> 16)


def _unpack_halves(w):
    lo = lax.bitcast_convert_type(w << 16, F32)
    hi = lax.bitcast_convert_type((w >> 16) << 16, F32)
    return lo, hi


def _adaln_kernel(c_ref, w_ref, b_ref, o_ref):
    o_ref[...] = jnp.dot(_silu(c_ref[...]), w_ref[...], preferred_element_type=F32,
                         precision=lax.Precision.HIGHEST) + b_ref[...]


def _adaln(c, w_ada, b_ada):
    bsz, d = c.shape
    n = w_ada.shape[1]
    bn = 1024
    return pl.pallas_call(
        _adaln_kernel,
        grid=(n // bn,),
        in_specs=[pl.BlockSpec((bsz, d), lambda j: (0, 0)),
                  pl.BlockSpec((d, bn), lambda j: (0, j)),
                  pl.BlockSpec((1, bn), lambda j: (0, j))],
        out_specs=pl.BlockSpec((bsz, bn), lambda j: (0, j)),
        out_shape=jax.ShapeDtypeStruct((bsz, n), F32),
        compiler_params=_params("arbitrary"),
        name="adaln",
    )(c, w_ada, b_ada.reshape(1, n))


def _mix_front_kernel(x_ref, shift_ref, scale_ref, gmix_ref, win_ref, cw_ref, cb_ref, gconv_ref,
                      yconv_ref, ut_ref, zprev_ref):
    s = pl.program_id(1)

    @pl.when(s == 0)
    def _():
        zprev_ref[...] = jnp.zeros_like(zprev_ref)

    x = x_ref[0]
    h = _rms(x, gmix_ref[...]) * (1.0 + scale_ref[0]) + shift_ref[0]
    proj = jnp.dot(h.astype(BF16), win_ref[...], preferred_element_type=F32)
    b_gate = proj[:, :D_CONV]
    c_gate = proj[:, D_CONV:2 * D_CONV]
    v = proj[:, 2 * D_CONV:3 * D_CONV]
    u = proj[:, 3 * D_CONV:]

    z = c_gate * v
    ts = z.shape[0]
    prev = zprev_ref[...]
    rid = lax.broadcasted_iota(I32, z.shape, 0)
    z1 = jnp.where(rid == 0, prev[7:8, :], pltpu.roll(z, 1, axis=0))
    z2 = jnp.where(rid == 0, prev[6:7, :], jnp.where(rid == 1, prev[7:8, :], pltpu.roll(z, 2, axis=0)))
    zprev_ref[...] = z[ts - 8:, :]
    cw = cw_ref[...]
    conv = cw[0:1, :] * z2 + cw[1:2, :] * z1 + cw[2:3, :] * z + cb_ref[...]
    yconv_ref[0] = _rms(b_gate * conv, gconv_ref[...]).astype(BF16)
    ut_ref[0] = u.T.astype(BF16)


def _mix_front(x, shift1, scale1, g_mix, w_in_bf, conv_w, conv_b, g_conv, ts):
    bsz, seq, d = x.shape
    d_in = w_in_bf.shape[1]
    row = lambda b, s: (b, 0, 0)
    const2 = lambda b, s: (0, 0)
    return pl.pallas_call(
        _mix_front_kernel,
        grid=(bsz, seq // ts),
        in_specs=[pl.BlockSpec((1, ts, d), lambda b, s: (b, s, 0)),
                  pl.BlockSpec((1, 1, d), row),
                  pl.BlockSpec((1, 1, d), row),
                  pl.BlockSpec((1, d), const2),
                  pl.BlockSpec((d, d_in), const2),
                  pl.BlockSpec((8, D_CONV), const2),
                  pl.BlockSpec((1, D_CONV), const2),
                  pl.BlockSpec((1, D_CONV), const2)],
        out_specs=[pl.BlockSpec((1, ts, D_CONV), lambda b, s: (b, s, 0)),
                   pl.BlockSpec((1, D_SSM, ts), lambda b, s: (b, 0, s))],
        out_shape=[jax.ShapeDtypeStruct((bsz, seq, D_CONV), BF16),
                   jax.ShapeDtypeStruct((bsz, D_SSM, seq), BF16)],
        scratch_shapes=[pltpu.VMEM((8, D_CONV), F32)],
        compiler_params=_params("arbitrary", "arbitrary"),
        name="mix_front",
    )(x, shift1, scale1, g_mix.reshape(1, d), w_in_bf,
      jnp.pad(conv_w, ((0, 8 - conv_w.shape[0]), (0, 0))), conv_b.reshape(1, D_CONV),
      g_conv.reshape(1, D_CONV))


def _s5_tables(lam_re, lam_im, log_dt, b_re, b_im, c_re, c_im, d_skip, chunk, n_chunks):
    hp = lax.Precision.HIGHEST
    G, H, P, L = SSM_GROUPS, SSM_GROUP_CH, SSM_STATE, chunk
    lr = lam_re.astype(F32)
    li = lam_im.astype(F32)
    dt = jnp.exp(log_dt.astype(F32))[:, None]
    mag = jnp.exp(lr * dt)
    ang = li * dt
    ab_re = mag * jnp.cos(ang)
    ab_im = mag * jnp.sin(ang)
    den = lr * lr + li * li
    nr = ab_re - 1.0
    ni = ab_im
    q_re = (nr * lr + ni * li) / den
    q_im = (ni * lr - nr * li) / den
    br = b_re.astype(F32)
    bi = b_im.astype(F32)
    bb_re = q_re[..., None] * br - q_im[..., None] * bi
    bb_im = q_re[..., None] * bi + q_im[..., None] * br

    def a_power(tau):
        t = tau.astype(F32)[None, :, None]
        m = jnp.exp(t * (lr * dt)[:, None, :])
        th = t * ang[:, None, :]
        return m * jnp.cos(th), m * jnp.sin(th)

    pw_re, pw_im = a_power(jnp.arange(L + 1))
    cr = c_re.astype(F32)[:, None]
    ci = c_im.astype(F32)[:, None]
    cp_re = cr * pw_re[:, :, None, :] - ci * pw_im[:, :, None, :]
    cp_im = cr * pw_im[:, :, None, :] + ci * pw_re[:, :, None, :]
    kern = (jnp.einsum('gthp,gpk->gthk', cp_re[:, :L], bb_re, precision=hp)
            - jnp.einsum('gthp,gpk->gthk', cp_im[:, :L], bb_im, precision=hp))
    kern = kern.at[:, 0].add(d_skip.astype(F32).reshape(G, H)[:, :, None] * jnp.eye(H, dtype=F32))
    kern = kern.transpose(0, 3, 2, 1).reshape(G, H * H, L)

    rev_re = pw_re[:, L - 1::-1]
    rev_im = pw_im[:, L - 1::-1]
    bt_re = bb_re.transpose(0, 2, 1)[:, :, None, :]
    bt_im = bb_im.transpose(0, 2, 1)[:, :, None, :]
    e_re = rev_re[:, None] * bt_re - rev_im[:, None] * bt_im
    e_im = rev_re[:, None] * bt_im + rev_im[:, None] * bt_re
    e_mat = jnp.concatenate([e_re, e_im], axis=-1).reshape(G, H * L, 2 * P)

    f_re = cp_re[:, 1:].transpose(0, 3, 2, 1)
    f_im = -cp_im[:, 1:].transpose(0, 3, 2, 1)
    f_mat = jnp.concatenate([f_re, f_im], axis=1).reshape(G, 2 * P, H * L)

    n_steps = max(1, (n_chunks - 1).bit_length())
    sr, si = a_power(L * (2 ** jnp.arange(n_steps)))
    a_pow = jnp.stack([jnp.concatenate([sr, sr], axis=-1),
                       jnp.concatenate([-si, si], axis=-1)], axis=2)
    return kern, e_mat.astype(BF16), f_mat.astype(BF16), a_pow


def _s5_kernel(u_ref, k_ref, e_ref, f_ref, a_ref, y_ref, toep_ref, *, n_chunks, n_steps, chunk):
    L, H = chunk, SSM_GROUP_CH

    causal = lax.broadcasted_iota(I32, (L, L), 1) >= lax.broadcasted_iota(I32, (L, L), 0)

    def build(hin, carry):
        r0 = pl.multiple_of(hin * L, L)
        for hout in range(H):
            krow = k_ref[0, pl.ds(hin * H + hout, 1), :]
            blk = pltpu.roll(jnp.broadcast_to(krow, (L, L)), 0, axis=1, stride=1, stride_axis=0)
            toep_ref[pl.ds(r0, L), hout * L:(hout + 1) * L] = jnp.where(causal, blk, 0.0).astype(BF16)
        return carry

    lax.fori_loop(0, H, build, 0)

    u = u_ref[0]
    st = jnp.dot(u, e_ref[0], preferred_element_type=F32)
    cidx = lax.broadcasted_iota(I32, st.shape, 0) % n_chunks
    p2 = st.shape[1]

    def shifted(xv, d):
        return jnp.where(cidx >= d, pltpu.roll(xv, d, axis=0), 0.0)

    for k in range(n_steps):
        d = 1 << k
        if d >= n_chunks:
            break
        z = shifted(st, d)
        st = st + a_ref[0, k, 0:1, :] * z + a_ref[0, k, 1:2, :] * pltpu.roll(z, p2 // 2, axis=1)
    s_in = shifted(st, 1)
    y = jnp.dot(u, toep_ref[...], preferred_element_type=F32)
    y = y + jnp.dot(s_in.astype(BF16), f_ref[0], preferred_element_type=F32)
    y_ref[0] = y.astype(BF16)


def _s5_core(u_g, kern, e_mat, f_mat, a_pow, n_chunks, chunk):
    g, rows, hl = u_g.shape
    n_steps = a_pow.shape[1]
    p2 = e_mat.shape[2]
    blk = lambda i: (i, 0, 0)
    return pl.pallas_call(
        functools.partial(_s5_kernel, n_chunks=n_chunks, n_steps=n_steps, chunk=chunk),
        grid=(g,),
        in_specs=[pl.BlockSpec((1, rows, hl), blk),
                  pl.BlockSpec((1,) + kern.shape[1:], blk),
                  pl.BlockSpec((1, hl, p2), blk),
                  pl.BlockSpec((1, p2, hl), blk),
                  pl.BlockSpec((1, n_steps, 2, p2), lambda i: (i, 0, 0, 0))],
        out_specs=pl.BlockSpec((1, rows, hl), blk),
        out_shape=jax.ShapeDtypeStruct((g, rows, hl), BF16),
        scratch_shapes=[pltpu.VMEM((hl, hl), BF16)],
        compiler_params=_params("arbitrary"),
        name="s5_core",
    )(u_g, kern, e_mat, f_mat, a_pow)


def _route(scores, biased, tri, base):
    n_tok = scores.shape[1]
    shape3 = (N_EXPERT_GROUPS, EXPERTS_PER_GROUP, n_tok)
    sc3 = scores.reshape(shape3)
    b3 = biased.reshape(shape3)
    j_iota = lax.broadcasted_iota(I32, shape3, 1)
    e_iota = lax.broadcasted_iota(I32, shape3, 0) * EXPERTS_PER_GROUP + j_iota

    def red(fn, x):
        return fn(fn(x, axis=1, keepdims=True), axis=0, keepdims=True)

    m1 = jnp.max(b3, axis=1, keepdims=True)
    i1 = jnp.min(jnp.where(b3 == m1, j_iota, EXPERTS_PER_GROUP), axis=1, keepdims=True)
    m2 = jnp.max(jnp.where(j_iota == i1, NEG_INF, b3), axis=1, keepdims=True)
    gs = m1 + m2

    g_iota = lax.broadcasted_iota(I32, gs.shape, 0)
    gsel = jnp.zeros(gs.shape, F32)
    cur = gs
    for _ in range(TOPK_GROUPS):
        m = jnp.max(cur, axis=0, keepdims=True)
        ig = jnp.min(jnp.where(cur == m, g_iota, N_EXPERT_GROUPS), axis=0, keepdims=True)
        pick = g_iota == ig
        gsel = jnp.where(pick, 1.0, gsel)
        cur = jnp.where(pick, NEG_INF, cur)

    cur = jnp.where(gsel > 0.0, b3, NEG_INF)
    sel = jnp.zeros(shape3, F32)
    ids, vals = [], []
    for _ in range(TOP_K):
        m = red(jnp.max, cur)
        ie = red(jnp.min, jnp.where(cur == m, e_iota, N_EXPERTS))
        pick = e_iota == ie
        ids.append(ie)
        vals.append(red(jnp.sum, jnp.where(pick, sc3, 0.0)))
        sel = jnp.where(pick, 1.0, sel)
        cur = jnp.where(pick, NEG_INF, cur)
    tot = functools.reduce(lambda a, b: a + b, vals)

    sel2 = sel.reshape(N_EXPERTS, n_tok)
    before = jnp.dot(sel2.astype(BF16), tri, preferred_element_type=F32) + base
    before3 = before.reshape(shape3)
    ranks = [red(jnp.sum, jnp.where(e_iota == ie, before3, 0.0)) for ie in ids]

    def rows(parts, dtype):
        parts = [p.reshape(1, n_tok).astype(dtype) for p in parts]
        return jnp.concatenate(parts + [jnp.zeros((8 - len(parts), n_tok), dtype)], axis=0)

    e8 = rows(ids, I32)
    w8 = rows([v / tot * ROUTED_SCALE for v in vals], F32)
    r8 = rows(ranks, I32)
    return e8, w8, r8, jnp.sum(sel2, axis=1, keepdims=True)


def _mix_back_kernel(x_ref, yconv_ref, yt_ref, gate1_ref, shift2_ref, scale2_ref, gate2_ref,
                     wglu_ref, bglu_ref, gssm_ref, woc_ref, wos_ref, gffn_ref,
                     wrh_ref, wrl_ref, rbias_ref, ws13_ref, ws2_ref, tri_ref,
                     x1_ref, h2p_ref, e8_ref, w8_ref, r8_ref, cnt_ref):
    first = jnp.logical_and(pl.program_id(0) == 0, pl.program_id(1) == 0)

    @pl.when(first)
    def _():
        cnt_ref[...] = jnp.zeros_like(cnt_ref)

    y = yt_ref[0].astype(F32).T
    y = _gelu_tanh(y)
    y = y * jax.nn.sigmoid(jnp.dot(y.astype(BF16), wglu_ref[...], preferred_element_type=F32)
                           + bglu_ref[...])
    y_ssm = _rms(y, gssm_ref[...])
    mix = jnp.dot(yconv_ref[0], woc_ref[...], preferred_element_type=F32)
    mix = mix + jnp.dot(y_ssm.astype(BF16), wos_ref[...], preferred_element_type=F32)
    x1 = x_ref[0] + gate1_ref[0] * mix

    h2 = _rms(x1, gffn_ref[...]) * (1.0 + scale2_ref[0]) + shift2_ref[0]
    h2b = h2.astype(BF16)
    h2p_ref[0] = lax.bitcast_convert_type(_pack_halves(h2), I32)

    a = jnp.dot(h2b, ws13_ref[...], preferred_element_type=F32)
    f = a.shape[1] // 2
    act = _silu(a[:, :f]) * a[:, f:]
    shared = jnp.dot(act.astype(BF16), ws2_ref[...], preferred_element_type=F32)
    x1_ref[0] = x1 + gate2_ref[0] * shared

    h2l = (h2 - h2b.astype(F32)).astype(BF16)
    logits = (jnp.dot(h2b, wrh_ref[...], preferred_element_type=F32)
              + jnp.dot(h2b, wrl_ref[...], preferred_element_type=F32)
              + jnp.dot(h2l, wrh_ref[...], preferred_element_type=F32))
    scores = jax.nn.sigmoid(logits.T)
    base = cnt_ref[:, 0:1]
    e8, w8, r8, cnt = _route(scores, scores + rbias_ref[...], tri_ref[...], base)
    e8_ref[...] = e8
    w8_ref[0] = w8.T
    r8_ref[...] = r8
    cnt_ref[...] = cnt_ref[...] + cnt


def _mix_back(x, yconv, yt, gate1, shift2, scale2, gate2, w_glu_bf, b_glu, g_ssm, wo_conv, wo_ssm,
              g_ffn, wr_hi, wr_lo, router_bias, ws13, ws2_bf, ts):
    bsz, seq, d = x.shape
    n_tok = bsz * seq
    tiles = seq // ts
    row = lambda b, s: (b, 0, 0)
    const2 = lambda b, s: (0, 0)
    tile = lambda b, s: (b, s, 0)
    flat = lambda b, s: (0, b * tiles + s)
    full = lambda a: pl.BlockSpec(a.shape, const2)
    tri = jnp.triu(jnp.ones((ts, ts), BF16), k=1)
    args = (w_glu_bf, b_glu.reshape(1, D_SSM), g_ssm.reshape(1, D_SSM), wo_conv, wo_ssm,
            g_ffn.reshape(1, d), wr_hi, wr_lo, router_bias.reshape(N_EXPERTS, 1), ws13, ws2_bf, tri)
    return pl.pallas_call(
        _mix_back_kernel,
        grid=(bsz, tiles),
        in_specs=[pl.BlockSpec((1, ts, d), tile),
                  pl.BlockSpec((1, ts, D_CONV), tile),
                  pl.BlockSpec((1, D_SSM, ts), lambda b, s: (b, 0, s)),
                  pl.BlockSpec((1, 1, d), row), pl.BlockSpec((1, 1, d), row),
                  pl.BlockSpec((1, 1, d), row), pl.BlockSpec((1, 1, d), row)]
                 + [full(a) for a in args],
        out_specs=[pl.BlockSpec((1, ts, d), tile),
                   pl.BlockSpec((1, ts, d // 2), tile),
                   pl.BlockSpec((8, ts), flat),
                   pl.BlockSpec((1, ts, 8), tile),
                   pl.BlockSpec((8, ts), flat),
                   pl.BlockSpec((N_EXPERTS, 128), const2)],
        out_shape=[jax.ShapeDtypeStruct((bsz, seq, d), F32),
                   jax.ShapeDtypeStruct((bsz, seq, d // 2), I32),
                   jax.ShapeDtypeStruct((8, n_tok), I32),
                   jax.ShapeDtypeStruct((bsz, seq, 8), F32),
                   jax.ShapeDtypeStruct((8, n_tok), I32),
                   jax.ShapeDtypeStruct((N_EXPERTS, 128), F32)],
        compiler_params=_params("arbitrary", "arbitrary"),
        name="mix_back",
    )(x, yconv, yt, gate1, shift2, scale2, gate2, *args)


def _plan_kernel(cnt_ref, e8_ref, r8_ref, pos_ref, *, bm):
    cnt = cnt_ref[...].astype(I32)
    padded = (cnt + (bm - 1)) // bm * bm
    rid = lax.broadcasted_iota(I32, padded.shape, 0)
    incl = padded
    d = 1
    while d < N_EXPERTS:
        incl = incl + jnp.where(rid >= d, pltpu.roll(incl, d, axis=0), 0)
        d *= 2
    pstart = incl - padded
    lanes = pstart.shape[1]

    def chunk(ci, carry):
        c0 = pl.multiple_of(ci * lanes, lanes)
        e = e8_ref[:, pl.ds(c0, lanes)]
        acc = r8_ref[:, pl.ds(c0, lanes)]
        for ex in range(N_EXPERTS):
            acc = acc + jnp.where(e == ex, pstart[ex:ex + 1, :], 0)
        pos_ref[:, pl.ds(c0, lanes)] = acc
        return carry

    lax.fori_loop(0, e8_ref.shape[1] // lanes, chunk, 0)


def _plan(counts, e8, r8, bm):
    n_tok = e8.shape[1]
    tb = min(8192, n_tok)
    return pl.pallas_call(
        functools.partial(_plan_kernel, bm=bm),
        grid=(n_tok // tb,),
        in_specs=[pl.BlockSpec(counts.shape, lambda i: (0, 0)),
                  pl.BlockSpec((8, tb), lambda i: (0, i)),
                  pl.BlockSpec((8, tb), lambda i: (0, i))],
        out_specs=pl.BlockSpec((8, tb), lambda i: (0, i)),
        out_shape=jax.ShapeDtypeStruct((8, n_tok), I32),
        compiler_params=_params("arbitrary"),
        name="plan",
    )(counts, e8, r8)


def _sc_mesh():
    return plsc.VectorSubcoreMesh(core_axis_name="c", subcore_axis_name="s",
                                  num_cores=V7X_SC_CORES, num_subcores=V7X_SC_SUBCORES)


def _sc_worker_base(per_worker):
    return (lax.axis_index("s") * V7X_SC_CORES + lax.axis_index("c")) * per_worker


def _sc_dispatch(h2p, posflat, n_rows):
    n_tok, half = h2p.shape
    win = SC_WINDOW
    per_worker = n_tok // SC_WORKERS
    n_win = per_worker // win

    @functools.partial(
        pl.kernel, mesh=_sc_mesh(),
        out_type=jax.ShapeDtypeStruct((n_rows, half), I32),
        scratch_types=[pltpu.VMEM((TOP_K, win), I32), pltpu.VMEM((win, half), I32),
                       pltpu.SemaphoreType.DMA],
        name="sc_dispatch")
    def run(h2p_hbm, pos_hbm, xs_hbm, idx_v, rows_v, sem):
        base = _sc_worker_base(per_worker)

        @pl.loop(0, n_win)
        def _(j):
            t0 = pl.multiple_of(base + j * win, win)
            loads = [pltpu.async_copy(h2p_hbm.at[pl.ds(t0, win)], rows_v, sem)]
            for k in range(TOP_K):
                loads.append(pltpu.async_copy(pos_hbm.at[pl.ds(k * n_tok + t0, win)], idx_v.at[k], sem))
            for cp in loads:
                cp.wait()
            stores = [pltpu.async_copy(rows_v, xs_hbm.at[idx_v.at[k]], sem) for k in range(TOP_K)]
            for cp in stores:
                cp.wait()

    return run(h2p, posflat)


def _sc_gather(ys, posflat):
    n_idx = posflat.shape[0]
    half = ys.shape[1]
    win = SC_WINDOW
    per_worker = n_idx // SC_WORKERS
    n_win = per_worker // win

    @functools.partial(
        pl.kernel, mesh=_sc_mesh(),
        out_type=jax.ShapeDtypeStruct((n_idx, half), I32),
        scratch_types=[pltpu.VMEM((win,), I32), pltpu.VMEM((win, half), I32), pltpu.SemaphoreType.DMA],
        name="sc_gather")
    def run(ys_hbm, pos_hbm, out_hbm, idx_v, rows_v, sem):
        base = _sc_worker_base(per_worker)

        @pl.loop(0, n_win)
        def _(j):
            off = pl.multiple_of(base + j * win, win)
            pltpu.sync_copy(pos_hbm.at[pl.ds(off, win)], idx_v)
            pltpu.async_copy(ys_hbm.at[idx_v], rows_v, sem).wait()
            pltpu.sync_copy(rows_v, out_hbm.at[pl.ds(off, win)])

    return run(ys, posflat)


def _experts_kernel(be_ref, x_ref, w1_ref, w3_ref, w2_ref, y_ref, w13_s, w2_s):
    i = pl.program_id(0)
    changed = jnp.logical_or(i == 0, be_ref[i] != be_ref[jnp.maximum(i - 1, 0)])

    @pl.when(changed)
    def _():
        f = w1_ref.shape[2]
        w13_s[:, :f] = w1_ref[0].astype(BF16)
        w13_s[:, f:] = w3_ref[0].astype(BF16)
        w2_s[...] = w2_ref[0].astype(BF16)

    lo, hi = _unpack_halves(lax.bitcast_convert_type(x_ref[...], U32))
    half = lo.shape[1]
    a = jnp.dot(lo.astype(BF16), w13_s[:half, :], preferred_element_type=F32)
    a = a + jnp.dot(hi.astype(BF16), w13_s[half:, :], preferred_element_type=F32)
    f = a.shape[1] // 2
    act = _silu(a[:, :f]) * a[:, f:]
    y = jnp.dot(act.astype(BF16), w2_s[...], preferred_element_type=F32)
    y_ref[...] = lax.bitcast_convert_type(_pack_halves(y), I32)


def _experts(block_e, xs, w1, w3, w2, bm):
    n_rows, half = xs.shape
    d, f = w1.shape[1], w1.shape[2]
    grid_spec = pltpu.PrefetchScalarGridSpec(
        num_scalar_prefetch=1,
        grid=(n_rows // bm,),
        in_specs=[pl.BlockSpec((bm, half), lambda i, be: (i, 0)),
                  pl.BlockSpec((1, d, f), lambda i, be: (be[i], 0, 0)),
                  pl.BlockSpec((1, d, f), lambda i, be: (be[i], 0, 0)),
                  pl.BlockSpec((1, f, d), lambda i, be: (be[i], 0, 0))],
        out_specs=pl.BlockSpec((bm, half), lambda i, be: (i, 0)),
        scratch_shapes=[pltpu.VMEM((d, 2 * f), BF16), pltpu.VMEM((f, d), BF16)],
    )
    return pl.pallas_call(
        _experts_kernel,
        grid_spec=grid_spec,
        out_shape=jax.ShapeDtypeStruct((n_rows, half), I32),
        compiler_params=_params("arbitrary"),
        name="experts",
    )(block_e, xs, w1, w3, w2)


def _combine_kernel(g_ref, w8_ref, x1_ref, gate2_ref, gfin_ref, o_ref):
    w8 = w8_ref[...]
    acc_lo = acc_hi = None
    for k in range(TOP_K):
        lo, hi = _unpack_halves(lax.bitcast_convert_type(g_ref[k], U32))
        wk = w8[:, k:k + 1]
        acc_lo = wk * lo if acc_lo is None else acc_lo + wk * lo
        acc_hi = wk * hi if acc_hi is None else acc_hi + wk * hi
    routed = jnp.concatenate([acc_lo, acc_hi], axis=1)
    o_ref[...] = _rms(x1_ref[...] + gate2_ref[0] * routed, gfin_ref[...])


def _combine(g6, w8, x1, gate2, g_final, tt, tiles_per_seq):
    n_tok, d = x1.shape
    half = g6.shape[2]
    tok = lambda i: (i, 0)
    return pl.pallas_call(
        _combine_kernel,
        grid=(n_tok // tt,),
        in_specs=[pl.BlockSpec((TOP_K, tt, half), lambda i: (0, i, 0)),
                  pl.BlockSpec((tt, 8), tok),
                  pl.BlockSpec((tt, d), tok),
                  pl.BlockSpec((1, 1, d), lambda i: (i // tiles_per_seq, 0, 0)),
                  pl.BlockSpec((1, d), lambda i: (0, 0))],
        out_specs=pl.BlockSpec((tt, d), tok),
        out_shape=jax.ShapeDtypeStruct((n_tok, d), F32),
        compiler_params=_params("arbitrary"),
        name="combine",
    )(g6, w8, x1, gate2, g_final.reshape(1, d))


def _block_experts(counts, bm, n_blocks):
    padded = (counts.astype(I32) + (bm - 1)) // bm * bm
    pend = jnp.cumsum(padded)
    starts = jnp.arange(n_blocks, dtype=I32) * bm
    return jnp.minimum(jnp.sum((pend[None, :] <= starts[:, None]).astype(I32), axis=1), N_EXPERTS - 1)


def kernel(x, c, w_ada, b_ada, g_mix, w_in, conv_w, conv_b, g_conv, lam_re, lam_im, log_dt, b_re, b_im,
           c_re, c_im, d_skip, w_glu, b_glu, g_ssm, w_out, g_ffn, w_router, router_bias, w1, w3, w2,
           ws1, ws3, ws2, g_final):
    bsz, seq, d = x.shape
    n_tok = bsz * seq
    ts = min(512, seq)
    L = min(S5_CHUNK, seq)
    n_chunks = seq // L
    G, H = SSM_GROUPS, SSM_GROUP_CH

    mod = _adaln(c, w_ada, b_ada).reshape(bsz, 6, 1, d)
    shift1, scale1, gate1, shift2, scale2, gate2 = (mod[:, i] for i in range(6))

    yconv, ut = _mix_front(x, shift1, scale1, g_mix, w_in.astype(BF16), conv_w, conv_b, g_conv, ts)

    u_g = ut.reshape(bsz, G, H, n_chunks, L).transpose(1, 0, 3, 2, 4).reshape(G, bsz * n_chunks, H * L)
    kern, e_mat, f_mat, a_pow = _s5_tables(lam_re, lam_im, log_dt, b_re, b_im, c_re, c_im, d_skip,
                                           L, n_chunks)
    y_g = _s5_core(u_g, kern, e_mat, f_mat, a_pow, n_chunks, L)
    yt = y_g.reshape(G, bsz, n_chunks, H, L).transpose(1, 0, 3, 2, 4).reshape(bsz, G * H, seq)

    wr_hi = w_router.astype(BF16)
    wr_lo = (w_router - wr_hi.astype(F32)).astype(BF16)
    w_out_bf = w_out.astype(BF16)
    x1, h2p, e8, w8, r8, counts = _mix_back(
        x, yconv, yt, gate1, shift2, scale2, gate2, w_glu.astype(BF16), b_glu, g_ssm,
        w_out_bf[:D_CONV], w_out_bf[D_CONV:], g_ffn, wr_hi, wr_lo, router_bias,
        jnp.concatenate([ws1, ws3], axis=1).astype(BF16), ws2.astype(BF16), ts)

    bm = EXPERT_ROWS
    n_blocks = -(-(n_tok * TOP_K + N_EXPERTS * (bm - 1)) // bm)
    pos8 = _plan(counts, e8, r8, bm)
    posflat = pos8[:TOP_K].reshape(TOP_K * n_tok)
    block_e = _block_experts(counts[:, 0], bm, n_blocks)
    xs = _sc_dispatch(h2p.reshape(n_tok, d // 2), posflat, n_blocks * bm)
    ys = _experts(block_e, xs, w1, w3, w2, bm)
    g6 = _sc_gather(ys, posflat).reshape(TOP_K, n_tok, d // 2)
    tt = min(COMBINE_TOKENS, seq)
    out = _combine(g6, w8.reshape(n_tok, 8), x1.reshape(n_tok, d), gate2, g_final, tt, seq // tt)
    return out.reshape(bsz, seq, d)
```

```python
import functools
import math

import jax
import jax.numpy as jnp
import numpy as np
from jax import lax
from jax.experimental import pallas as pl
from jax.experimental.pallas import tpu as pltpu
from jax.experimental.pallas import tpu_sc as plsc

F32 = jnp.float32
BF16 = jnp.bfloat16
U32 = jnp.uint32
I32 = jnp.int32

D_CONV = 768
D_SSM = 256
SSM_GROUPS = 16
SSM_GROUP_CH = 16
SSM_STATE = 64
N_EXPERTS = 64
TOP_K = 6
N_EXPERT_GROUPS = 8
TOPK_GROUPS = 4
EXPERTS_PER_GROUP = 8
ROUTED_SCALE = 2.5
RMS_EPS = 1e-6

S5_CHUNK = 128
EXPERT_ROWS = 1024
EXPERT_SUB = 512
MIX_SUB = 256
COMBINE_TOKENS = 512
V7X_SC_CORES = 2
V7X_SC_SUBCORES = 16
SC_WORKERS = V7X_SC_CORES * V7X_SC_SUBCORES
SC_WINDOW = 64
V7X_VMEM_LIMIT = 56 * 1024 * 1024
NEG_INF = float("-inf")
HIGH_HALF = np.uint32(0xFFFF0000)


def _rms(x, g):
    return x * lax.rsqrt(jnp.mean(x * x, axis=-1, keepdims=True) + RMS_EPS) * g


def _gelu_tanh(x):
    return 0.5 * x * (1.0 + jnp.tanh(math.sqrt(2.0 / math.pi) * (x + 0.044715 * (x * x * x))))


def _silu(x):
    return x * jax.nn.sigmoid(x)


def _params(*sem):
    return pltpu.CompilerParams(dimension_semantics=sem, vmem_limit_bytes=V7X_VMEM_LIMIT)


def _pack_halves(y):
    m = y.shape[1] // 2
    bits = lax.bitcast_convert_type(y.astype(BF16).astype(F32), U32)
    return (bits[:, m:] & HIGH_HALF) | (bits[:, :m] >> 16)


def _unpack_halves(w):
    lo = lax.bitcast_convert_type(w << 16, F32)
    hi = lax.bitcast_convert_type(w & HIGH_HALF, F32)
    return lo, hi


def _adaln_kernel(c_ref, w_ref, b_ref, o_ref):
    o_ref[...] = jnp.dot(_silu(c_ref[...]), w_ref[...], preferred_element_type=F32,
                         precision=lax.Precision.HIGHEST) + b_ref[...]


def _adaln(c, w_ada, b_ada):
    bsz, d = c.shape
    n = w_ada.shape[1]
    bn = 1024
    return pl.pallas_call(
        _adaln_kernel,
        grid=(n // bn,),
        in_specs=[pl.BlockSpec((bsz, d), lambda j: (0, 0)),
                  pl.BlockSpec((d, bn), lambda j: (0, j)),
                  pl.BlockSpec((1, bn), lambda j: (0, j))],
        out_specs=pl.BlockSpec((bsz, bn), lambda j: (0, j)),
        out_shape=jax.ShapeDtypeStruct((bsz, n), F32),
        compiler_params=_params("arbitrary"),
        name="adaln",
    )(c, w_ada, b_ada.reshape(1, n))


def _mix_front_kernel(x_ref, shift_ref, scale_ref, gmix_ref, win_ref, cw_ref, cb_ref, gconv_ref,
                      yconv_ref, ut_ref, zprev_ref):
    s = pl.program_id(1)

    @pl.when(s == 0)
    def _():
        zprev_ref[...] = jnp.zeros_like(zprev_ref)

    x = x_ref[0]
    h = _rms(x, gmix_ref[...]) * (1.0 + scale_ref[0]) + shift_ref[0]
    proj = jnp.dot(h.astype(BF16), win_ref[...], preferred_element_type=F32)
    b_gate = proj[:, :D_CONV]
    c_gate = proj[:, D_CONV:2 * D_CONV]
    v = proj[:, 2 * D_CONV:3 * D_CONV]
    u = proj[:, 3 * D_CONV:]

    z = c_gate * v
    ts = z.shape[0]
    prev = zprev_ref[...]
    rid = lax.broadcasted_iota(I32, z.shape, 0)
    z1 = jnp.where(rid == 0, prev[7:8, :], pltpu.roll(z, 1, axis=0))
    z2 = jnp.where(rid == 0, prev[6:7, :], jnp.where(rid == 1, prev[7:8, :], pltpu.roll(z, 2, axis=0)))
    zprev_ref[...] = z[ts - 8:, :]
    cw = cw_ref[...]
    conv = cw[0:1, :] * z2 + cw[1:2, :] * z1 + cw[2:3, :] * z + cb_ref[...]
    yconv_ref[0] = _rms(b_gate * conv, gconv_ref[...]).astype(BF16)
    ut_ref[0] = u.T.astype(BF16)


def _mix_front(x, shift1, scale1, g_mix, w_in_bf, conv_w, conv_b, g_conv, ts):
    bsz, seq, d = x.shape
    d_in = w_in_bf.shape[1]
    row = lambda b, s: (b, 0, 0)
    const2 = lambda b, s: (0, 0)
    return pl.pallas_call(
        _mix_front_kernel,
        grid=(bsz, seq // ts),
        in_specs=[pl.BlockSpec((1, ts, d), lambda b, s: (b, s, 0)),
                  pl.BlockSpec((1, 1, d), row),
                  pl.BlockSpec((1, 1, d), row),
                  pl.BlockSpec((1, d), const2),
                  pl.BlockSpec((d, d_in), const2),
                  pl.BlockSpec((8, D_CONV), const2),
                  pl.BlockSpec((1, D_CONV), const2),
                  pl.BlockSpec((1, D_CONV), const2)],
        out_specs=[pl.BlockSpec((1, ts, D_CONV), lambda b, s: (b, s, 0)),
                   pl.BlockSpec((1, D_SSM, ts), lambda b, s: (b, 0, s))],
        out_shape=[jax.ShapeDtypeStruct((bsz, seq, D_CONV), BF16),
                   jax.ShapeDtypeStruct((bsz, D_SSM, seq), BF16)],
        scratch_shapes=[pltpu.VMEM((8, D_CONV), F32)],
        compiler_params=_params("arbitrary", "arbitrary"),
        name="mix_front",
    )(x, shift1, scale1, g_mix.reshape(1, d), w_in_bf,
      jnp.pad(conv_w, ((0, 8 - conv_w.shape[0]), (0, 0))), conv_b.reshape(1, D_CONV),
      g_conv.reshape(1, D_CONV))


def _s5_tables(lam_re, lam_im, log_dt, b_re, b_im, c_re, c_im, d_skip, chunk, n_chunks):
    hp = lax.Precision.HIGHEST
    G, H, P, L = SSM_GROUPS, SSM_GROUP_CH, SSM_STATE, chunk
    lr = lam_re.astype(F32)
    li = lam_im.astype(F32)
    dt = jnp.exp(log_dt.astype(F32))[:, None]
    mag = jnp.exp(lr * dt)
    ang = li * dt
    ab_re = mag * jnp.cos(ang)
    ab_im = mag * jnp.sin(ang)
    den = lr * lr + li * li
    nr = ab_re - 1.0
    ni = ab_im
    q_re = (nr * lr + ni * li) / den
    q_im = (ni * lr - nr * li) / den
    br = b_re.astype(F32)
    bi = b_im.astype(F32)
    bb_re = q_re[..., None] * br - q_im[..., None] * bi
    bb_im = q_re[..., None] * bi + q_im[..., None] * br

    def a_power(tau):
        t = tau.astype(F32)[None, :, None]
        m = jnp.exp(t * (lr * dt)[:, None, :])
        th = t * ang[:, None, :]
        return m * jnp.cos(th), m * jnp.sin(th)

    pw_re, pw_im = a_power(jnp.arange(L + 1))
    cr = c_re.astype(F32)[:, None]
    ci = c_im.astype(F32)[:, None]
    cp_re = cr * pw_re[:, :, None, :] - ci * pw_im[:, :, None, :]
    cp_im = cr * pw_im[:, :, None, :] + ci * pw_re[:, :, None, :]
    kern = (jnp.einsum('gthp,gpk->gthk', cp_re[:, :L], bb_re, precision=hp)
            - jnp.einsum('gthp,gpk->gthk', cp_im[:, :L], bb_im, precision=hp))
    kern = kern.at[:, 0].add(d_skip.astype(F32).reshape(G, H)[:, :, None] * jnp.eye(H, dtype=F32))
    kern = kern.transpose(0, 3, 2, 1).reshape(G, H * H, L)

    rev_re = pw_re[:, L - 1::-1]
    rev_im = pw_im[:, L - 1::-1]
    bt_re = bb_re.transpose(0, 2, 1)[:, :, None, :]
    bt_im = bb_im.transpose(0, 2, 1)[:, :, None, :]
    e_re = rev_re[:, None] * bt_re - rev_im[:, None] * bt_im
    e_im = rev_re[:, None] * bt_im + rev_im[:, None] * bt_re
    e_mat = jnp.concatenate([e_re, e_im], axis=-1).reshape(G, H * L, 2 * P)

    f_re = cp_re[:, 1:].transpose(0, 3, 2, 1)
    f_im = -cp_im[:, 1:].transpose(0, 3, 2, 1)
    f_mat = jnp.concatenate([f_re, f_im], axis=1).reshape(G, 2 * P, H * L)

    n_steps = max(1, (n_chunks - 1).bit_length())
    sr, si = a_power(L * (2 ** jnp.arange(n_steps)))
    a_pow = jnp.stack([jnp.concatenate([sr, sr], axis=-1),
                       jnp.concatenate([-si, si], axis=-1)], axis=2)
    return kern, e_mat.astype(BF16), f_mat.astype(BF16), a_pow


def _s5_kernel(u_ref, k_ref, e_ref, f_ref, a_ref, y_ref, toep_ref, *, n_chunks, n_steps, chunk):
    L, H = chunk, SSM_GROUP_CH

    causal = lax.broadcasted_iota(I32, (L, L), 1) >= lax.broadcasted_iota(I32, (L, L), 0)

    def build(hin, carry):
        r0 = pl.multiple_of(hin * L, L)
        for hout in range(H):
            krow = k_ref[0, pl.ds(hin * H + hout, 1), :]
            blk = pltpu.roll(jnp.broadcast_to(krow, (L, L)), 0, axis=1, stride=1, stride_axis=0)
            toep_ref[pl.ds(r0, L), hout * L:(hout + 1) * L] = jnp.where(causal, blk, 0.0).astype(BF16)
        return carry

    lax.fori_loop(0, H, build, 0)

    u = u_ref[0]
    st = jnp.dot(u, e_ref[0], preferred_element_type=F32)
    cidx = lax.broadcasted_iota(I32, st.shape, 0) % n_chunks
    p2 = st.shape[1]

    def shifted(xv, d):
        return jnp.where(cidx >= d, pltpu.roll(xv, d, axis=0), 0.0)

    for k in range(n_steps):
        d = 1 << k
        if d >= n_chunks:
            break
        z = shifted(st, d)
        st = st + a_ref[0, k, 0:1, :] * z + a_ref[0, k, 1:2, :] * pltpu.roll(z, p2 // 2, axis=1)
    s_in = shifted(st, 1)
    y = jnp.dot(u, toep_ref[...], preferred_element_type=F32)
    y = y + jnp.dot(s_in.astype(BF16), f_ref[0], preferred_element_type=F32)
    y_ref[0] = y.astype(BF16)


def _s5_core(u_g, kern, e_mat, f_mat, a_pow, n_chunks, chunk):
    g, rows, hl = u_g.shape
    n_steps = a_pow.shape[1]
    p2 = e_mat.shape[2]
    blk = lambda i: (i, 0, 0)
    return pl.pallas_call(
        functools.partial(_s5_kernel, n_chunks=n_chunks, n_steps=n_steps, chunk=chunk),
        grid=(g,),
        in_specs=[pl.BlockSpec((1, rows, hl), blk),
                  pl.BlockSpec((1,) + kern.shape[1:], blk),
                  pl.BlockSpec((1, hl, p2), blk),
                  pl.BlockSpec((1, p2, hl), blk),
                  pl.BlockSpec((1, n_steps, 2, p2), lambda i: (i, 0, 0, 0))],
        out_specs=pl.BlockSpec((1, rows, hl), blk),
        out_shape=jax.ShapeDtypeStruct((g, rows, hl), BF16),
        scratch_shapes=[pltpu.VMEM((hl, hl), BF16)],
        compiler_params=_params("arbitrary"),
        name="s5_core",
    )(u_g, kern, e_mat, f_mat, a_pow)


def _route(scores, biased, tri, base):
    n_tok = scores.shape[1]
    shape3 = (N_EXPERT_GROUPS, EXPERTS_PER_GROUP, n_tok)
    sc3 = scores.reshape(shape3)
    b3 = biased.reshape(shape3)
    j_iota = lax.broadcasted_iota(I32, shape3, 1)
    e_iota = lax.broadcasted_iota(I32, shape3, 0) * EXPERTS_PER_GROUP + j_iota

    def red(fn, x):
        return fn(fn(x, axis=1, keepdims=True), axis=0, keepdims=True)

    m1 = jnp.max(b3, axis=1, keepdims=True)
    i1 = jnp.min(jnp.where(b3 == m1, j_iota, EXPERTS_PER_GROUP), axis=1, keepdims=True)
    m2 = jnp.max(jnp.where(j_iota == i1, NEG_INF, b3), axis=1, keepdims=True)
    gs = m1 + m2

    g_iota = lax.broadcasted_iota(I32, gs.shape, 0)
    gsel = jnp.zeros(gs.shape, F32)
    cur = gs
    for _ in range(TOPK_GROUPS):
        m = jnp.max(cur, axis=0, keepdims=True)
        ig = jnp.min(jnp.where(cur == m, g_iota, N_EXPERT_GROUPS), axis=0, keepdims=True)
        pick = g_iota == ig
        gsel = jnp.where(pick, 1.0, gsel)
        cur = jnp.where(pick, NEG_INF, cur)

    cur = jnp.where(gsel > 0.0, b3, NEG_INF)
    sel = jnp.zeros(shape3, F32)
    ids, vals = [], []
    for _ in range(TOP_K):
        m = red(jnp.max, cur)
        ie = red(jnp.min, jnp.where(cur == m, e_iota, N_EXPERTS))
        pick = e_iota == ie
        ids.append(ie)
        vals.append(red(jnp.sum, jnp.where(pick, sc3, 0.0)))
        sel = jnp.where(pick, 1.0, sel)
        cur = jnp.where(pick, NEG_INF, cur)
    tot = functools.reduce(lambda a, b: a + b, vals)

    sel2 = sel.reshape(N_EXPERTS, n_tok)
    before = jnp.dot(sel2.astype(BF16), tri, preferred_element_type=F32) + base
    before3 = before.reshape(shape3)
    ranks = [red(jnp.sum, jnp.where(e_iota == ie, before3, 0.0)) for ie in ids]

    def rows(parts, dtype):
        parts = [p.reshape(1, n_tok).astype(dtype) for p in parts]
        return jnp.concatenate(parts + [jnp.zeros((8 - len(parts), n_tok), dtype)], axis=0)

    e8 = rows(ids, I32)
    w8 = rows([v / tot * ROUTED_SCALE for v in vals], F32)
    r8 = rows(ranks, I32)
    return e8, w8, r8, jnp.sum(sel2, axis=1, keepdims=True)


def _mix_back_kernel(x_ref, yconv_ref, yt_ref, gate1_ref, shift2_ref, scale2_ref, gate2_ref,
                     wglu_ref, bglu_ref, gssm_ref, woc_ref, wos_ref, gffn_ref,
                     wrh_ref, wrl_ref, rbias_ref, ws13_ref, ws2_ref, tri_ref,
                     x1_ref, h2p_ref, e8_ref, w8_ref, r8_ref, cnt_ref):
    first = jnp.logical_and(pl.program_id(0) == 0, pl.program_id(1) == 0)

    @pl.when(first)
    def _():
        cnt_ref[...] = jnp.zeros_like(cnt_ref)

    sub = tri_ref.shape[0]
    base = cnt_ref[:, 0:1]
    for r0 in range(0, x_ref.shape[1], sub):
        y = yt_ref[0, :, r0:r0 + sub].astype(F32).T
        y = _gelu_tanh(y)
        y = y * jax.nn.sigmoid(jnp.dot(y.astype(BF16), wglu_ref[...], preferred_element_type=F32)
                               + bglu_ref[...])
        y_ssm = _rms(y, gssm_ref[...])
        mix = jnp.dot(yconv_ref[0, r0:r0 + sub, :], woc_ref[...], preferred_element_type=F32)
        mix = mix + jnp.dot(y_ssm.astype(BF16), wos_ref[...], preferred_element_type=F32)
        x1 = x_ref[0, r0:r0 + sub, :] + gate1_ref[0] * mix

        h2 = _rms(x1, gffn_ref[...]) * (1.0 + scale2_ref[0]) + shift2_ref[0]
        h2b = h2.astype(BF16)
        h2p_ref[0, r0:r0 + sub, :] = lax.bitcast_convert_type(_pack_halves(h2), I32)

        a = jnp.dot(h2b, ws13_ref[...], preferred_element_type=F32)
        f = a.shape[1] // 2
        act = _silu(a[:, :f]) * a[:, f:]
        shared = jnp.dot(act.astype(BF16), ws2_ref[...], preferred_element_type=F32)
        x1_ref[0, r0:r0 + sub, :] = x1 + gate2_ref[0] * shared

        h2l = (h2 - h2b.astype(F32)).astype(BF16)
        logits = (jnp.dot(h2b, wrh_ref[...], preferred_element_type=F32)
                  + jnp.dot(h2b, wrl_ref[...], preferred_element_type=F32)
                  + jnp.dot(h2l, wrh_ref[...], preferred_element_type=F32))
        scores = jax.nn.sigmoid(logits.T)
        e8, w8, r8, cnt = _route(scores, scores + rbias_ref[...], tri_ref[...], base)
        e8_ref[:, r0:r0 + sub] = e8
        w8_ref[0, r0:r0 + sub, :] = w8.T
        r8_ref[:, r0:r0 + sub] = r8
        base = base + cnt
    cnt_ref[...] = jnp.broadcast_to(base, cnt_ref.shape)


def _mix_back(x, yconv, yt, gate1, shift2, scale2, gate2, w_glu_bf, b_glu, g_ssm, wo_conv, wo_ssm,
              g_ffn, wr_hi, wr_lo, router_bias, ws13, ws2_bf, ts):
    bsz, seq, d = x.shape
    n_tok = bsz * seq
    tiles = seq // ts
    row = lambda b, s: (b, 0, 0)
    const2 = lambda b, s: (0, 0)
    tile = lambda b, s: (b, s, 0)
    flat = lambda b, s: (0, b * tiles + s)
    full = lambda a: pl.BlockSpec(a.shape, const2)
    sub = min(MIX_SUB, ts)
    tri = jnp.triu(jnp.ones((sub, sub), BF16), k=1)
    args = (w_glu_bf, b_glu.reshape(1, D_SSM), g_ssm.reshape(1, D_SSM), wo_conv, wo_ssm,
            g_ffn.reshape(1, d), wr_hi, wr_lo, router_bias.reshape(N_EXPERTS, 1), ws13, ws2_bf, tri)
    return pl.pallas_call(
        _mix_back_kernel,
        grid=(bsz, tiles),
        in_specs=[pl.BlockSpec((1, ts, d), tile),
                  pl.BlockSpec((1, ts, D_CONV), tile),
                  pl.BlockSpec((1, D_SSM, ts), lambda b, s: (b, 0, s)),
                  pl.BlockSpec((1, 1, d), row), pl.BlockSpec((1, 1, d), row),
                  pl.BlockSpec((1, 1, d), row), pl.BlockSpec((1, 1, d), row)]
                 + [full(a) for a in args],
        out_specs=[pl.BlockSpec((1, ts, d), tile),
                   pl.BlockSpec((1, ts, d // 2), tile),
                   pl.BlockSpec((8, ts), flat),
                   pl.BlockSpec((1, ts, 8), tile),
                   pl.BlockSpec((8, ts), flat),
                   pl.BlockSpec((N_EXPERTS, 128), const2)],
        out_shape=[jax.ShapeDtypeStruct((bsz, seq, d), F32),
                   jax.ShapeDtypeStruct((bsz, seq, d // 2), I32),
                   jax.ShapeDtypeStruct((8, n_tok), I32),
                   jax.ShapeDtypeStruct((bsz, seq, 8), F32),
                   jax.ShapeDtypeStruct((8, n_tok), I32),
                   jax.ShapeDtypeStruct((N_EXPERTS, 128), F32)],
        compiler_params=_params("arbitrary", "arbitrary"),
        name="mix_back",
    )(x, yconv, yt, gate1, shift2, scale2, gate2, *args)


def _plan_kernel(cnt_ref, e8_ref, r8_ref, pos_ref, *, bm):
    cnt = cnt_ref[...].astype(I32)
    padded = (cnt + (bm - 1)) // bm * bm
    rid = lax.broadcasted_iota(I32, padded.shape, 0)
    incl = padded
    d = 1
    while d < N_EXPERTS:
        incl = incl + jnp.where(rid >= d, pltpu.roll(incl, d, axis=0), 0)
        d *= 2
    pstart = incl - padded
    lanes = pstart.shape[1]

    def chunk(ci, carry):
        c0 = pl.multiple_of(ci * lanes, lanes)
        e = e8_ref[:, pl.ds(c0, lanes)]
        acc = r8_ref[:, pl.ds(c0, lanes)]
        for ex in range(N_EXPERTS):
            acc = acc + jnp.where(e == ex, pstart[ex:ex + 1, :], 0)
        pos_ref[:, pl.ds(c0, lanes)] = acc
        return carry

    lax.fori_loop(0, e8_ref.shape[1] // lanes, chunk, 0)


def _plan(counts, e8, r8, bm):
    n_tok = e8.shape[1]
    tb = min(8192, n_tok)
    return pl.pallas_call(
        functools.partial(_plan_kernel, bm=bm),
        grid=(n_tok // tb,),
        in_specs=[pl.BlockSpec(counts.shape, lambda i: (0, 0)),
                  pl.BlockSpec((8, tb), lambda i: (0, i)),
                  pl.BlockSpec((8, tb), lambda i: (0, i))],
        out_specs=pl.BlockSpec((8, tb), lambda i: (0, i)),
        out_shape=jax.ShapeDtypeStruct((8, n_tok), I32),
        compiler_params=_params("arbitrary"),
        name="plan",
    )(counts, e8, r8)


def _sc_mesh():
    return plsc.VectorSubcoreMesh(core_axis_name="c", subcore_axis_name="s",
                                  num_cores=V7X_SC_CORES, num_subcores=V7X_SC_SUBCORES)


def _sc_worker_base(per_worker):
    return (lax.axis_index("s") * V7X_SC_CORES + lax.axis_index("c")) * per_worker


def _sc_dispatch(h2p, posflat, n_rows):
    n_tok, half = h2p.shape
    win = SC_WINDOW
    per_worker = n_tok // SC_WORKERS
    n_win = per_worker // win

    wins_per_k = n_tok // win

    @functools.partial(
        pl.kernel, mesh=_sc_mesh(),
        out_type=jax.ShapeDtypeStruct((n_rows, half), I32),
        scratch_types=[pltpu.VMEM((TOP_K, n_win, win), I32), pltpu.VMEM((2, win, half), I32),
                       pltpu.SemaphoreType.DMA((2,)), pltpu.SemaphoreType.DMA((2,))],
        name="sc_dispatch")
    def run(h2p_hbm, pos_hbm, xs_hbm, idx_v, rows_v, sem_in, sem_out):
        wid = _sc_worker_base(1)
        for k in range(TOP_K):
            pltpu.sync_copy(pos_hbm.at[pl.ds(k * wins_per_k + wid * n_win, n_win)], idx_v.at[k])

        @pl.loop(0, n_win, step=2)
        def _(j):
            loads = []
            for b in range(2):
                t0 = pl.multiple_of((wid * n_win + j + b) * win, win)
                loads.append(pltpu.async_copy(h2p_hbm.at[pl.ds(t0, win)], rows_v.at[b], sem_in.at[b]))
            stores = []
            for b in range(2):
                loads[b].wait()
                for k in range(TOP_K):
                    stores.append(pltpu.async_copy(rows_v.at[b], xs_hbm.at[idx_v.at[k, j + b]], sem_out.at[b]))
            for cp in stores:
                cp.wait()

    return run(h2p, posflat.reshape(TOP_K * wins_per_k, win))


def _sc_gather(ys, posflat):
    n_idx = posflat.shape[0]
    half = ys.shape[1]
    win = SC_WINDOW
    per_worker = n_idx // SC_WORKERS
    n_win = per_worker // win

    @functools.partial(
        pl.kernel, mesh=_sc_mesh(),
        out_type=jax.ShapeDtypeStruct((n_idx, half), I32),
        scratch_types=[pltpu.VMEM((n_win, win), I32), pltpu.VMEM((2, win, half), I32),
                       pltpu.SemaphoreType.DMA((2,)), pltpu.SemaphoreType.DMA((2,))],
        name="sc_gather")
    def run(ys_hbm, pos_hbm, out_hbm, idx_v, rows_v, sem_in, sem_out):
        wid = _sc_worker_base(1)
        pltpu.sync_copy(pos_hbm.at[pl.ds(wid * n_win, n_win)], idx_v)

        @pl.loop(0, n_win, step=2)
        def _(j):
            gathers = [pltpu.async_copy(ys_hbm.at[idx_v.at[j + b]], rows_v.at[b], sem_in.at[b])
                       for b in range(2)]
            writes = []
            for b in range(2):
                gathers[b].wait()
                off = pl.multiple_of((wid * n_win + j + b) * win, win)
                writes.append(pltpu.async_copy(rows_v.at[b], out_hbm.at[pl.ds(off, win)], sem_out.at[b]))
            for cp in writes:
                cp.wait()

    return run(ys, posflat.reshape(n_idx // win, win))


def _experts_kernel(be_ref, x_ref, w1_ref, w3_ref, w2_ref, y_ref, w13_s, w2_s):
    i = pl.program_id(0)
    changed = jnp.logical_or(i == 0, be_ref[i] != be_ref[jnp.maximum(i - 1, 0)])

    @pl.when(changed)
    def _():
        f = w1_ref.shape[2]
        w13_s[:, :f] = w1_ref[0].astype(BF16)
        w13_s[:, f:] = w3_ref[0].astype(BF16)
        w2_s[...] = w2_ref[0].astype(BF16)

    half = x_ref.shape[1]
    sub = min(EXPERT_SUB, x_ref.shape[0])
    for r0 in range(0, x_ref.shape[0], sub):
        lo, hi = _unpack_halves(lax.bitcast_convert_type(x_ref[r0:r0 + sub, :], U32))
        a = jnp.dot(lo.astype(BF16), w13_s[:half, :], preferred_element_type=F32)
        a = a + jnp.dot(hi.astype(BF16), w13_s[half:, :], preferred_element_type=F32)
        f = a.shape[1] // 2
        act = _silu(a[:, :f]) * a[:, f:]
        y = jnp.dot(act.astype(BF16), w2_s[...], preferred_element_type=F32)
        y_ref[r0:r0 + sub, :] = lax.bitcast_convert_type(_pack_halves(y), I32)


def _experts(block_e, xs, w1, w3, w2, bm):
    n_rows, half = xs.shape
    d, f = w1.shape[1], w1.shape[2]
    grid_spec = pltpu.PrefetchScalarGridSpec(
        num_scalar_prefetch=1,
        grid=(n_rows // bm,),
        in_specs=[pl.BlockSpec((bm, half), lambda i, be: (i, 0)),
                  pl.BlockSpec((1, d, f), lambda i, be: (be[i], 0, 0)),
                  pl.BlockSpec((1, d, f), lambda i, be: (be[i], 0, 0)),
                  pl.BlockSpec((1, f, d), lambda i, be: (be[i], 0, 0))],
        out_specs=pl.BlockSpec((bm, half), lambda i, be: (i, 0)),
        scratch_shapes=[pltpu.VMEM((d, 2 * f), BF16), pltpu.VMEM((f, d), BF16)],
    )
    return pl.pallas_call(
        _experts_kernel,
        grid_spec=grid_spec,
        out_shape=jax.ShapeDtypeStruct((n_rows, half), I32),
        compiler_params=_params("arbitrary"),
        name="experts",
    )(block_e, xs, w1, w3, w2)


def _combine_kernel(g_ref, w8_ref, x1_ref, gate2_ref, gfin_ref, o_ref):
    w8 = w8_ref[...]
    acc_lo = acc_hi = None
    for k in range(TOP_K):
        lo, hi = _unpack_halves(lax.bitcast_convert_type(g_ref[k], U32))
        wk = w8[:, k:k + 1]
        acc_lo = wk * lo if acc_lo is None else acc_lo + wk * lo
        acc_hi = wk * hi if acc_hi is None else acc_hi + wk * hi
    routed = jnp.concatenate([acc_lo, acc_hi], axis=1)
    o_ref[...] = _rms(x1_ref[...] + gate2_ref[0] * routed, gfin_ref[...])


def _combine(g6, w8, x1, gate2, g_final, tt, tiles_per_seq):
    n_tok, d = x1.shape
    half = g6.shape[2]
    tok = lambda i: (i, 0)
    return pl.pallas_call(
        _combine_kernel,
        grid=(n_tok // tt,),
        in_specs=[pl.BlockSpec((TOP_K, tt, half), lambda i: (0, i, 0)),
                  pl.BlockSpec((tt, 8), tok),
                  pl.BlockSpec((tt, d), tok),
                  pl.BlockSpec((1, 1, d), lambda i: (i // tiles_per_seq, 0, 0)),
                  pl.BlockSpec((1, d), lambda i: (0, 0))],
        out_specs=pl.BlockSpec((tt, d), tok),
        out_shape=jax.ShapeDtypeStruct((n_tok, d), F32),
        compiler_params=_params("arbitrary"),
        name="combine",
    )(g6, w8, x1, gate2, g_final.reshape(1, d))


def _block_experts(counts, bm, n_blocks):
    padded = (counts.astype(I32) + (bm - 1)) // bm * bm
    pend = jnp.cumsum(padded)
    starts = jnp.arange(n_blocks, dtype=I32) * bm
    return jnp.minimum(jnp.sum((pend[None, :] <= starts[:, None]).astype(I32), axis=1), N_EXPERTS - 1)


def kernel(x, c, w_ada, b_ada, g_mix, w_in, conv_w, conv_b, g_conv, lam_re, lam_im, log_dt, b_re, b_im,
           c_re, c_im, d_skip, w_glu, b_glu, g_ssm, w_out, g_ffn, w_router, router_bias, w1, w3, w2,
           ws1, ws3, ws2, g_final):
    bsz, seq, d = x.shape
    n_tok = bsz * seq
    ts = min(512, seq)
    L = min(S5_CHUNK, seq)
    n_chunks = seq // L
    G, H = SSM_GROUPS, SSM_GROUP_CH

    mod = _adaln(c, w_ada, b_ada).reshape(bsz, 6, 1, d)
    shift1, scale1, gate1, shift2, scale2, gate2 = (mod[:, i] for i in range(6))

    yconv, ut = _mix_front(x, shift1, scale1, g_mix, w_in.astype(BF16), conv_w, conv_b, g_conv, ts)

    u_g = ut.reshape(bsz, G, H, n_chunks, L).transpose(1, 0, 3, 2, 4).reshape(G, bsz * n_chunks, H * L)
    kern, e_mat, f_mat, a_pow = _s5_tables(lam_re, lam_im, log_dt, b_re, b_im, c_re, c_im, d_skip,
                                           L, n_chunks)
    y_g = _s5_core(u_g, kern, e_mat, f_mat, a_pow, n_chunks, L)
    yt = y_g.reshape(G, bsz, n_chunks, H, L).transpose(1, 0, 3, 2, 4).reshape(bsz, G * H, seq)

    wr_hi = w_router.astype(BF16)
    wr_lo = (w_router - wr_hi.astype(F32)).astype(BF16)
    w_out_bf = w_out.astype(BF16)
    x1, h2p, e8, w8, r8, counts = _mix_back(
        x, yconv, yt, gate1, shift2, scale2, gate2, w_glu.astype(BF16), b_glu, g_ssm,
        w_out_bf[:D_CONV], w_out_bf[D_CONV:], g_ffn, wr_hi, wr_lo, router_bias,
        jnp.concatenate([ws1, ws3], axis=1).astype(BF16), ws2.astype(BF16), ts)

    bm = EXPERT_ROWS
    n_blocks = -(-(n_tok * TOP_K + N_EXPERTS * (bm - 1)) // bm)
    pos8 = _plan(counts, e8, r8, bm)
    posflat = pos8[:TOP_K].reshape(TOP_K * n_tok)
    block_e = _block_experts(counts[:, 0], bm, n_blocks)
    xs = _sc_dispatch(h2p.reshape(n_tok, d // 2), posflat, n_blocks * bm)
    ys = _experts(block_e, xs, w1, w3, w2, bm)
    g6 = _sc_gather(ys, posflat).reshape(TOP_K, n_tok, d // 2)
    tt = min(COMBINE_TOKENS, seq)
    out = _combine(g6, w8.reshape(n_tok, 8), x1.reshape(n_tok, d), gate2, g_final, tt, seq // tt)
    return out.reshape(bsz, seq, d)
```

```python
import functools
import math

import jax
import jax.numpy as jnp
import numpy as np
from jax import lax
from jax.experimental import pallas as pl
from jax.experimental.pallas import tpu as pltpu
from jax.experimental.pallas import tpu_sc as plsc

F32 = jnp.float32
BF16 = jnp.bfloat16
U32 = jnp.uint32
I32 = jnp.int32

D_CONV = 768
D_SSM = 256
SSM_GROUPS = 16
SSM_GROUP_CH = 16
SSM_STATE = 64
N_EXPERTS = 64
TOP_K = 6
N_EXPERT_GROUPS = 8
TOPK_GROUPS = 4
EXPERTS_PER_GROUP = 8
ROUTED_SCALE = 2.5
RMS_EPS = 1e-6

S5_CHUNK = 128
EXPERT_ROWS = 1024
EXPERT_SUB = 512
MIX_SUB = 256
COMBINE_TOKENS = 512
MOE_PARTS = 2
V7X_SC_CORES = 2
V7X_SC_SUBCORES = 16
SC_WORKERS = V7X_SC_CORES * V7X_SC_SUBCORES
SC_WINDOW = 64
V7X_VMEM_LIMIT = 56 * 1024 * 1024
NEG_INF = float("-inf")
HIGH_HALF = np.uint32(0xFFFF0000)


def _rms(x, g):
    return x * lax.rsqrt(jnp.mean(x * x, axis=-1, keepdims=True) + RMS_EPS) * g


def _gelu_tanh(x):
    return 0.5 * x * (1.0 + jnp.tanh(math.sqrt(2.0 / math.pi) * (x + 0.044715 * (x * x * x))))


def _silu(x):
    return x * jax.nn.sigmoid(x)


def _params(*sem):
    return pltpu.CompilerParams(dimension_semantics=sem, vmem_limit_bytes=V7X_VMEM_LIMIT)


def _pack_halves(y):
    m = y.shape[1] // 2
    bits = lax.bitcast_convert_type(y.astype(BF16).astype(F32), U32)
    return (bits[:, m:] & HIGH_HALF) | (bits[:, :m] >> 16)


def _unpack_halves(w):
    lo = lax.bitcast_convert_type(w << 16, F32)
    hi = lax.bitcast_convert_type(w & HIGH_HALF, F32)
    return lo, hi


def _adaln_kernel(c_ref, w_ref, b_ref, o_ref):
    o_ref[...] = jnp.dot(_silu(c_ref[...]), w_ref[...], preferred_element_type=F32,
                         precision=lax.Precision.HIGHEST) + b_ref[...]


def _adaln(c, w_ada, b_ada):
    bsz, d = c.shape
    n = w_ada.shape[1]
    bn = 1024
    return pl.pallas_call(
        _adaln_kernel,
        grid=(n // bn,),
        in_specs=[pl.BlockSpec((bsz, d), lambda j: (0, 0)),
                  pl.BlockSpec((d, bn), lambda j: (0, j)),
                  pl.BlockSpec((1, bn), lambda j: (0, j))],
        out_specs=pl.BlockSpec((bsz, bn), lambda j: (0, j)),
        out_shape=jax.ShapeDtypeStruct((bsz, n), F32),
        compiler_params=_params("arbitrary"),
        name="adaln",
    )(c, w_ada, b_ada.reshape(1, n))


def _mix_front_kernel(x_ref, shift_ref, scale_ref, gmix_ref, win_ref, cw_ref, cb_ref, gconv_ref,
                      yconv_ref, ut_ref, zprev_ref):
    s = pl.program_id(1)

    @pl.when(s == 0)
    def _():
        zprev_ref[...] = jnp.zeros_like(zprev_ref)

    x = x_ref[0]
    h = _rms(x, gmix_ref[...]) * (1.0 + scale_ref[0]) + shift_ref[0]
    proj = jnp.dot(h.astype(BF16), win_ref[...], preferred_element_type=F32)
    b_gate = proj[:, :D_CONV]
    c_gate = proj[:, D_CONV:2 * D_CONV]
    v = proj[:, 2 * D_CONV:3 * D_CONV]
    u = proj[:, 3 * D_CONV:]

    z = c_gate * v
    ts = z.shape[0]
    prev = zprev_ref[...]
    rid = lax.broadcasted_iota(I32, z.shape, 0)
    z1 = jnp.where(rid == 0, prev[7:8, :], pltpu.roll(z, 1, axis=0))
    z2 = jnp.where(rid == 0, prev[6:7, :], jnp.where(rid == 1, prev[7:8, :], pltpu.roll(z, 2, axis=0)))
    zprev_ref[...] = z[ts - 8:, :]
    cw = cw_ref[...]
    conv = cw[0:1, :] * z2 + cw[1:2, :] * z1 + cw[2:3, :] * z + cb_ref[...]
    yconv_ref[0] = _rms(b_gate * conv, gconv_ref[...]).astype(BF16)
    ut_ref[0] = u.T.astype(BF16)


def _mix_front(x, shift1, scale1, g_mix, w_in_bf, conv_w, conv_b, g_conv, ts):
    bsz, seq, d = x.shape
    d_in = w_in_bf.shape[1]
    row = lambda b, s: (b, 0, 0)
    const2 = lambda b, s: (0, 0)
    return pl.pallas_call(
        _mix_front_kernel,
        grid=(bsz, seq // ts),
        in_specs=[pl.BlockSpec((1, ts, d), lambda b, s: (b, s, 0)),
                  pl.BlockSpec((1, 1, d), row),
                  pl.BlockSpec((1, 1, d), row),
                  pl.BlockSpec((1, d), const2),
                  pl.BlockSpec((d, d_in), const2),
                  pl.BlockSpec((8, D_CONV), const2),
                  pl.BlockSpec((1, D_CONV), const2),
                  pl.BlockSpec((1, D_CONV), const2)],
        out_specs=[pl.BlockSpec((1, ts, D_CONV), lambda b, s: (b, s, 0)),
                   pl.BlockSpec((1, D_SSM, ts), lambda b, s: (b, 0, s))],
        out_shape=[jax.ShapeDtypeStruct((bsz, seq, D_CONV), BF16),
                   jax.ShapeDtypeStruct((bsz, D_SSM, seq), BF16)],
        scratch_shapes=[pltpu.VMEM((8, D_CONV), F32)],
        compiler_params=_params("arbitrary", "arbitrary"),
        name="mix_front",
    )(x, shift1, scale1, g_mix.reshape(1, d), w_in_bf,
      jnp.pad(conv_w, ((0, 8 - conv_w.shape[0]), (0, 0))), conv_b.reshape(1, D_CONV),
      g_conv.reshape(1, D_CONV))


def _s5_tables(lam_re, lam_im, log_dt, b_re, b_im, c_re, c_im, d_skip, chunk, n_chunks):
    hp = lax.Precision.HIGHEST
    G, H, P, L = SSM_GROUPS, SSM_GROUP_CH, SSM_STATE, chunk
    lr = lam_re.astype(F32)
    li = lam_im.astype(F32)
    dt = jnp.exp(log_dt.astype(F32))[:, None]
    mag = jnp.exp(lr * dt)
    ang = li * dt
    ab_re = mag * jnp.cos(ang)
    ab_im = mag * jnp.sin(ang)
    den = lr * lr + li * li
    nr = ab_re - 1.0
    ni = ab_im
    q_re = (nr * lr + ni * li) / den
    q_im = (ni * lr - nr * li) / den
    br = b_re.astype(F32)
    bi = b_im.astype(F32)
    bb_re = q_re[..., None] * br - q_im[..., None] * bi
    bb_im = q_re[..., None] * bi + q_im[..., None] * br

    def a_power(tau):
        t = tau.astype(F32)[None, :, None]
        m = jnp.exp(t * (lr * dt)[:, None, :])
        th = t * ang[:, None, :]
        return m * jnp.cos(th), m * jnp.sin(th)

    pw_re, pw_im = a_power(jnp.arange(L + 1))
    cr = c_re.astype(F32)[:, None]
    ci = c_im.astype(F32)[:, None]
    cp_re = cr * pw_re[:, :, None, :] - ci * pw_im[:, :, None, :]
    cp_im = cr * pw_im[:, :, None, :] + ci * pw_re[:, :, None, :]
    kern = (jnp.einsum('gthp,gpk->gthk', cp_re[:, :L], bb_re, precision=hp)
            - jnp.einsum('gthp,gpk->gthk', cp_im[:, :L], bb_im, precision=hp))
    kern = kern.at[:, 0].add(d_skip.astype(F32).reshape(G, H)[:, :, None] * jnp.eye(H, dtype=F32))
    kern = kern.transpose(0, 3, 2, 1).reshape(G, H * H, L)

    rev_re = pw_re[:, L - 1::-1]
    rev_im = pw_im[:, L - 1::-1]
    bt_re = bb_re.transpose(0, 2, 1)[:, :, None, :]
    bt_im = bb_im.transpose(0, 2, 1)[:, :, None, :]
    e_re = rev_re[:, None] * bt_re - rev_im[:, None] * bt_im
    e_im = rev_re[:, None] * bt_im + rev_im[:, None] * bt_re
    e_mat = jnp.concatenate([e_re, e_im], axis=-1).reshape(G, H * L, 2 * P)

    f_re = cp_re[:, 1:].transpose(0, 3, 2, 1)
    f_im = -cp_im[:, 1:].transpose(0, 3, 2, 1)
    f_mat = jnp.concatenate([f_re, f_im], axis=1).reshape(G, 2 * P, H * L)

    n_steps = max(1, (n_chunks - 1).bit_length())
    sr, si = a_power(L * (2 ** jnp.arange(n_steps)))
    a_pow = jnp.stack([jnp.concatenate([sr, sr], axis=-1),
                       jnp.concatenate([-si, si], axis=-1)], axis=2)
    return kern, e_mat.astype(BF16), f_mat.astype(BF16), a_pow


def _s5_kernel(u_ref, k_ref, e_ref, f_ref, a_ref, y_ref, toep_ref, *, n_chunks, n_steps, chunk):
    L, H = chunk, SSM_GROUP_CH

    causal = lax.broadcasted_iota(I32, (L, L), 1) >= lax.broadcasted_iota(I32, (L, L), 0)

    def build(hin, carry):
        r0 = pl.multiple_of(hin * L, L)
        for hout in range(H):
            krow = k_ref[0, pl.ds(hin * H + hout, 1), :]
            blk = pltpu.roll(jnp.broadcast_to(krow, (L, L)), 0, axis=1, stride=1, stride_axis=0)
            toep_ref[pl.ds(r0, L), hout * L:(hout + 1) * L] = jnp.where(causal, blk, 0.0).astype(BF16)
        return carry

    lax.fori_loop(0, H, build, 0)

    u = u_ref[0]
    st = jnp.dot(u, e_ref[0], preferred_element_type=F32)
    cidx = lax.broadcasted_iota(I32, st.shape, 0) % n_chunks
    p2 = st.shape[1]

    def shifted(xv, d):
        return jnp.where(cidx >= d, pltpu.roll(xv, d, axis=0), 0.0)

    for k in range(n_steps):
        d = 1 << k
        if d >= n_chunks:
            break
        z = shifted(st, d)
        st = st + a_ref[0, k, 0:1, :] * z + a_ref[0, k, 1:2, :] * pltpu.roll(z, p2 // 2, axis=1)
    s_in = shifted(st, 1)
    y = jnp.dot(u, toep_ref[...], preferred_element_type=F32)
    y = y + jnp.dot(s_in.astype(BF16), f_ref[0], preferred_element_type=F32)
    y_ref[0] = y.astype(BF16)


def _s5_core(u_g, kern, e_mat, f_mat, a_pow, n_chunks, chunk):
    g, rows, hl = u_g.shape
    n_steps = a_pow.shape[1]
    p2 = e_mat.shape[2]
    blk = lambda i: (i, 0, 0)
    return pl.pallas_call(
        functools.partial(_s5_kernel, n_chunks=n_chunks, n_steps=n_steps, chunk=chunk),
        grid=(g,),
        in_specs=[pl.BlockSpec((1, rows, hl), blk),
                  pl.BlockSpec((1,) + kern.shape[1:], blk),
                  pl.BlockSpec((1, hl, p2), blk),
                  pl.BlockSpec((1, p2, hl), blk),
                  pl.BlockSpec((1, n_steps, 2, p2), lambda i: (i, 0, 0, 0))],
        out_specs=pl.BlockSpec((1, rows, hl), blk),
        out_shape=jax.ShapeDtypeStruct((g, rows, hl), BF16),
        scratch_shapes=[pltpu.VMEM((hl, hl), BF16)],
        compiler_params=_params("arbitrary"),
        name="s5_core",
    )(u_g, kern, e_mat, f_mat, a_pow)


def _route(scores, biased, tri, base):
    n_tok = scores.shape[1]
    shape3 = (N_EXPERT_GROUPS, EXPERTS_PER_GROUP, n_tok)
    sc3 = scores.reshape(shape3)
    b3 = biased.reshape(shape3)
    j_iota = lax.broadcasted_iota(I32, shape3, 1)
    e_iota = lax.broadcasted_iota(I32, shape3, 0) * EXPERTS_PER_GROUP + j_iota

    def red(fn, x):
        return fn(fn(x, axis=1, keepdims=True), axis=0, keepdims=True)

    m1 = jnp.max(b3, axis=1, keepdims=True)
    i1 = jnp.min(jnp.where(b3 == m1, j_iota, EXPERTS_PER_GROUP), axis=1, keepdims=True)
    m2 = jnp.max(jnp.where(j_iota == i1, NEG_INF, b3), axis=1, keepdims=True)
    gs = m1 + m2

    g_iota = lax.broadcasted_iota(I32, gs.shape, 0)
    gsel = jnp.zeros(gs.shape, F32)
    cur = gs
    for _ in range(TOPK_GROUPS):
        m = jnp.max(cur, axis=0, keepdims=True)
        ig = jnp.min(jnp.where(cur == m, g_iota, N_EXPERT_GROUPS), axis=0, keepdims=True)
        pick = g_iota == ig
        gsel = jnp.where(pick, 1.0, gsel)
        cur = jnp.where(pick, NEG_INF, cur)

    cur = jnp.where(gsel > 0.0, b3, NEG_INF)
    sel = jnp.zeros(shape3, F32)
    ids, vals = [], []
    for _ in range(TOP_K):
        m = red(jnp.max, cur)
        ie = red(jnp.min, jnp.where(cur == m, e_iota, N_EXPERTS))
        pick = e_iota == ie
        ids.append(ie)
        vals.append(red(jnp.sum, jnp.where(pick, sc3, 0.0)))
        sel = jnp.where(pick, 1.0, sel)
        cur = jnp.where(pick, NEG_INF, cur)
    tot = functools.reduce(lambda a, b: a + b, vals)

    sel2 = sel.reshape(N_EXPERTS, n_tok)
    before = jnp.dot(sel2.astype(BF16), tri, preferred_element_type=F32) + base
    before3 = before.reshape(shape3)
    ranks = [red(jnp.sum, jnp.where(e_iota == ie, before3, 0.0)) for ie in ids]

    def rows(parts, dtype):
        parts = [p.reshape(1, n_tok).astype(dtype) for p in parts]
        return jnp.concatenate(parts + [jnp.zeros((8 - len(parts), n_tok), dtype)], axis=0)

    e8 = rows(ids, I32)
    w8 = rows([v / tot * ROUTED_SCALE for v in vals], F32)
    r8 = rows(ranks, I32)
    return e8, w8, r8, jnp.sum(sel2, axis=1, keepdims=True)


def _mix_back_kernel(x_ref, yconv_ref, yt_ref, gate1_ref, shift2_ref, scale2_ref, gate2_ref,
                     wglu_ref, bglu_ref, gssm_ref, woc_ref, wos_ref, gffn_ref,
                     wrh_ref, wrl_ref, rbias_ref, ws13_ref, ws2_ref, tri_ref,
                     x1_ref, h2p_ref, e8_ref, w8_ref, r8_ref, cnt_ref):
    first = jnp.logical_and(pl.program_id(0) == 0, pl.program_id(1) == 0)

    @pl.when(first)
    def _():
        cnt_ref[...] = jnp.zeros_like(cnt_ref)

    sub = tri_ref.shape[0]
    base = cnt_ref[:, 0:1]
    for r0 in range(0, x_ref.shape[1], sub):
        y = yt_ref[0, :, r0:r0 + sub].astype(F32).T
        y = _gelu_tanh(y)
        y = y * jax.nn.sigmoid(jnp.dot(y.astype(BF16), wglu_ref[...], preferred_element_type=F32)
                               + bglu_ref[...])
        y_ssm = _rms(y, gssm_ref[...])
        mix = jnp.dot(yconv_ref[0, r0:r0 + sub, :], woc_ref[...], preferred_element_type=F32)
        mix = mix + jnp.dot(y_ssm.astype(BF16), wos_ref[...], preferred_element_type=F32)
        x1 = x_ref[0, r0:r0 + sub, :] + gate1_ref[0] * mix

        h2 = _rms(x1, gffn_ref[...]) * (1.0 + scale2_ref[0]) + shift2_ref[0]
        h2b = h2.astype(BF16)
        h2p_ref[0, r0:r0 + sub, :] = lax.bitcast_convert_type(_pack_halves(h2), I32)

        a = jnp.dot(h2b, ws13_ref[...], preferred_element_type=F32)
        f = a.shape[1] // 2
        act = _silu(a[:, :f]) * a[:, f:]
        shared = jnp.dot(act.astype(BF16), ws2_ref[...], preferred_element_type=F32)
        x1_ref[0, r0:r0 + sub, :] = x1 + gate2_ref[0] * shared

        h2l = (h2 - h2b.astype(F32)).astype(BF16)
        logits = (jnp.dot(h2b, wrh_ref[...], preferred_element_type=F32)
                  + jnp.dot(h2b, wrl_ref[...], preferred_element_type=F32)
                  + jnp.dot(h2l, wrh_ref[...], preferred_element_type=F32))
        scores = jax.nn.sigmoid(logits.T)
        e8, w8, r8, cnt = _route(scores, scores + rbias_ref[...], tri_ref[...], base)
        e8_ref[:, r0:r0 + sub] = e8
        w8_ref[0, r0:r0 + sub, :] = w8.T
        r8_ref[:, r0:r0 + sub] = r8
        base = base + cnt
    cnt_ref[...] = jnp.broadcast_to(base, cnt_ref.shape)


def _mix_back(x, yconv, yt, gate1, shift2, scale2, gate2, w_glu_bf, b_glu, g_ssm, wo_conv, wo_ssm,
              g_ffn, wr_hi, wr_lo, router_bias, ws13, ws2_bf, ts, b0, bsz):
    _, seq, d = x.shape
    n_tok = bsz * seq
    tiles = seq // ts
    row = lambda b, s: (b + b0, 0, 0)
    const2 = lambda b, s: (0, 0)
    tile_in = lambda b, s: (b + b0, s, 0)
    tile = lambda b, s: (b, s, 0)
    flat = lambda b, s: (0, b * tiles + s)
    full = lambda a: pl.BlockSpec(a.shape, const2)
    sub = min(MIX_SUB, ts)
    tri = jnp.triu(jnp.ones((sub, sub), BF16), k=1)
    args = (w_glu_bf, b_glu.reshape(1, D_SSM), g_ssm.reshape(1, D_SSM), wo_conv, wo_ssm,
            g_ffn.reshape(1, d), wr_hi, wr_lo, router_bias.reshape(N_EXPERTS, 1), ws13, ws2_bf, tri)
    return pl.pallas_call(
        _mix_back_kernel,
        grid=(bsz, tiles),
        in_specs=[pl.BlockSpec((1, ts, d), tile_in),
                  pl.BlockSpec((1, ts, D_CONV), tile_in),
                  pl.BlockSpec((1, D_SSM, ts), lambda b, s: (b + b0, 0, s)),
                  pl.BlockSpec((1, 1, d), row), pl.BlockSpec((1, 1, d), row),
                  pl.BlockSpec((1, 1, d), row), pl.BlockSpec((1, 1, d), row)]
                 + [full(a) for a in args],
        out_specs=[pl.BlockSpec((1, ts, d), tile),
                   pl.BlockSpec((1, ts, d // 2), tile),
                   pl.BlockSpec((8, ts), flat),
                   pl.BlockSpec((1, ts, 8), tile),
                   pl.BlockSpec((8, ts), flat),
                   pl.BlockSpec((N_EXPERTS, 128), const2)],
        out_shape=[jax.ShapeDtypeStruct((bsz, seq, d), F32),
                   jax.ShapeDtypeStruct((bsz, seq, d // 2), I32),
                   jax.ShapeDtypeStruct((8, n_tok), I32),
                   jax.ShapeDtypeStruct((bsz, seq, 8), F32),
                   jax.ShapeDtypeStruct((8, n_tok), I32),
                   jax.ShapeDtypeStruct((N_EXPERTS, 128), F32)],
        compiler_params=_params("arbitrary", "arbitrary"),
        name="mix_back",
    )(x, yconv, yt, gate1, shift2, scale2, gate2, *args)


def _plan_kernel(cnt_ref, e8_ref, r8_ref, pos_ref, *, bm):
    cnt = cnt_ref[...].astype(I32)
    padded = (cnt + (bm - 1)) // bm * bm
    rid = lax.broadcasted_iota(I32, padded.shape, 0)
    incl = padded
    d = 1
    while d < N_EXPERTS:
        incl = incl + jnp.where(rid >= d, pltpu.roll(incl, d, axis=0), 0)
        d *= 2
    pstart = incl - padded
    lanes = pstart.shape[1]

    def chunk(ci, carry):
        c0 = pl.multiple_of(ci * lanes, lanes)
        e = e8_ref[:, pl.ds(c0, lanes)]
        acc = r8_ref[:, pl.ds(c0, lanes)]
        for ex in range(N_EXPERTS):
            acc = acc + jnp.where(e == ex, pstart[ex:ex + 1, :], 0)
        pos_ref[:, pl.ds(c0, lanes)] = acc
        return carry

    lax.fori_loop(0, e8_ref.shape[1] // lanes, chunk, 0)


def _plan(counts, e8, r8, bm):
    n_tok = e8.shape[1]
    tb = min(8192, n_tok)
    return pl.pallas_call(
        functools.partial(_plan_kernel, bm=bm),
        grid=(n_tok // tb,),
        in_specs=[pl.BlockSpec(counts.shape, lambda i: (0, 0)),
                  pl.BlockSpec((8, tb), lambda i: (0, i)),
                  pl.BlockSpec((8, tb), lambda i: (0, i))],
        out_specs=pl.BlockSpec((8, tb), lambda i: (0, i)),
        out_shape=jax.ShapeDtypeStruct((8, n_tok), I32),
        compiler_params=_params("arbitrary"),
        name="plan",
    )(counts, e8, r8)


def _sc_mesh():
    return plsc.VectorSubcoreMesh(core_axis_name="c", subcore_axis_name="s",
                                  num_cores=V7X_SC_CORES, num_subcores=V7X_SC_SUBCORES)


def _sc_worker_base(per_worker):
    return (lax.axis_index("s") * V7X_SC_CORES + lax.axis_index("c")) * per_worker


def _sc_dispatch(h2p, posflat, n_rows):
    n_tok, half = h2p.shape
    win = SC_WINDOW
    per_worker = n_tok // SC_WORKERS
    n_win = per_worker // win

    wins_per_k = n_tok // win

    @functools.partial(
        pl.kernel, mesh=_sc_mesh(),
        out_type=jax.ShapeDtypeStruct((n_rows, half), I32),
        scratch_types=[pltpu.VMEM((TOP_K, n_win, win), I32), pltpu.VMEM((2, win, half), I32),
                       pltpu.SemaphoreType.DMA((2,)), pltpu.SemaphoreType.DMA((2,))],
        name="sc_dispatch")
    def run(h2p_hbm, pos_hbm, xs_hbm, idx_v, rows_v, sem_in, sem_out):
        wid = _sc_worker_base(1)
        for k in range(TOP_K):
            pltpu.sync_copy(pos_hbm.at[pl.ds(k * wins_per_k + wid * n_win, n_win)], idx_v.at[k])

        @pl.loop(0, n_win, step=2)
        def _(j):
            loads = []
            for b in range(2):
                t0 = pl.multiple_of((wid * n_win + j + b) * win, win)
                loads.append(pltpu.async_copy(h2p_hbm.at[pl.ds(t0, win)], rows_v.at[b], sem_in.at[b]))
            stores = []
            for b in range(2):
                loads[b].wait()
                for k in range(TOP_K):
                    stores.append(pltpu.async_copy(rows_v.at[b], xs_hbm.at[idx_v.at[k, j + b]], sem_out.at[b]))
            for cp in stores:
                cp.wait()

    return run(h2p, posflat.reshape(TOP_K * wins_per_k, win))


def _sc_gather(ys, posflat):
    n_idx = posflat.shape[0]
    half = ys.shape[1]
    win = SC_WINDOW
    per_worker = n_idx // SC_WORKERS
    n_win = per_worker // win

    @functools.partial(
        pl.kernel, mesh=_sc_mesh(),
        out_type=jax.ShapeDtypeStruct((n_idx, half), I32),
        scratch_types=[pltpu.VMEM((n_win, win), I32), pltpu.VMEM((2, win, half), I32),
                       pltpu.SemaphoreType.DMA((2,)), pltpu.SemaphoreType.DMA((2,))],
        name="sc_gather")
    def run(ys_hbm, pos_hbm, out_hbm, idx_v, rows_v, sem_in, sem_out):
        wid = _sc_worker_base(1)
        pltpu.sync_copy(pos_hbm.at[pl.ds(wid * n_win, n_win)], idx_v)

        @pl.loop(0, n_win, step=2)
        def _(j):
            gathers = [pltpu.async_copy(ys_hbm.at[idx_v.at[j + b]], rows_v.at[b], sem_in.at[b])
                       for b in range(2)]
            writes = []
            for b in range(2):
                gathers[b].wait()
                off = pl.multiple_of((wid * n_win + j + b) * win, win)
                writes.append(pltpu.async_copy(rows_v.at[b], out_hbm.at[pl.ds(off, win)], sem_out.at[b]))
            for cp in writes:
                cp.wait()

    return run(ys, posflat.reshape(n_idx // win, win))


def _experts_kernel(be_ref, x_ref, w1_ref, w3_ref, w2_ref, y_ref, w13_s, w2_s):
    i = pl.program_id(0)
    changed = jnp.logical_or(i == 0, be_ref[i] != be_ref[jnp.maximum(i - 1, 0)])

    @pl.when(changed)
    def _():
        f = w1_ref.shape[2]
        w13_s[:, :f] = w1_ref[0].astype(BF16)
        w13_s[:, f:] = w3_ref[0].astype(BF16)
        w2_s[...] = w2_ref[0].astype(BF16)

    half = x_ref.shape[1]
    sub = min(EXPERT_SUB, x_ref.shape[0])
    for r0 in range(0, x_ref.shape[0], sub):
        lo, hi = _unpack_halves(lax.bitcast_convert_type(x_ref[r0:r0 + sub, :], U32))
        a = jnp.dot(lo.astype(BF16), w13_s[:half, :], preferred_element_type=F32)
        a = a + jnp.dot(hi.astype(BF16), w13_s[half:, :], preferred_element_type=F32)
        f = a.shape[1] // 2
        act = _silu(a[:, :f]) * a[:, f:]
        y = jnp.dot(act.astype(BF16), w2_s[...], preferred_element_type=F32)
        y_ref[r0:r0 + sub, :] = lax.bitcast_convert_type(_pack_halves(y), I32)


def _experts(block_e, xs, w1, w3, w2, bm):
    n_rows, half = xs.shape
    d, f = w1.shape[1], w1.shape[2]
    grid_spec = pltpu.PrefetchScalarGridSpec(
        num_scalar_prefetch=1,
        grid=(n_rows // bm,),
        in_specs=[pl.BlockSpec((bm, half), lambda i, be: (i, 0)),
                  pl.BlockSpec((1, d, f), lambda i, be: (be[i], 0, 0)),
                  pl.BlockSpec((1, d, f), lambda i, be: (be[i], 0, 0)),
                  pl.BlockSpec((1, f, d), lambda i, be: (be[i], 0, 0))],
        out_specs=pl.BlockSpec((bm, half), lambda i, be: (i, 0)),
        scratch_shapes=[pltpu.VMEM((d, 2 * f), BF16), pltpu.VMEM((f, d), BF16)],
    )
    return pl.pallas_call(
        _experts_kernel,
        grid_spec=grid_spec,
        out_shape=jax.ShapeDtypeStruct((n_rows, half), I32),
        compiler_params=_params("arbitrary"),
        name="experts",
    )(block_e, xs, w1, w3, w2)


def _combine_kernel(g_ref, w8_ref, x1_ref, gate2_ref, gfin_ref, *rest):
    o_ref = rest[-1]
    w8 = w8_ref[...]
    acc_lo = acc_hi = None
    for k in range(TOP_K):
        lo, hi = _unpack_halves(lax.bitcast_convert_type(g_ref[k], U32))
        wk = w8[:, k:k + 1]
        acc_lo = wk * lo if acc_lo is None else acc_lo + wk * lo
        acc_hi = wk * hi if acc_hi is None else acc_hi + wk * hi
    routed = jnp.concatenate([acc_lo, acc_hi], axis=1)
    o_ref[...] = _rms(x1_ref[...] + gate2_ref[0] * routed, gfin_ref[...])


def _combine(g6, w8, x1, gate2, g_final, tt, tiles_per_seq, b0, n_tok_all, out_prev):
    n_tok, d = x1.shape
    half = g6.shape[2]
    tok = lambda i: (i, 0)
    tile0 = b0 * tiles_per_seq
    in_specs = [pl.BlockSpec((TOP_K, tt, half), lambda i: (0, i, 0)),
                pl.BlockSpec((tt, 8), tok),
                pl.BlockSpec((tt, d), tok),
                pl.BlockSpec((1, 1, d), lambda i: (i // tiles_per_seq + b0, 0, 0)),
                pl.BlockSpec((1, d), lambda i: (0, 0))]
    args = [g6, w8, x1, gate2, g_final.reshape(1, d)]
    aliases = {}
    if out_prev is not None:
        in_specs.append(pl.BlockSpec(memory_space=pl.ANY))
        args.append(out_prev)
        aliases = {len(args) - 1: 0}
    return pl.pallas_call(
        _combine_kernel,
        grid=(n_tok // tt,),
        in_specs=in_specs,
        out_specs=pl.BlockSpec((tt, d), lambda i: (i + tile0, 0)),
        out_shape=jax.ShapeDtypeStruct((n_tok_all, d), F32),
        input_output_aliases=aliases,
        compiler_params=_params("arbitrary"),
        name="combine",
    )(*args)


def _block_experts(counts, bm, n_blocks):
    padded = (counts.astype(I32) + (bm - 1)) // bm * bm
    pend = jnp.cumsum(padded)
    starts = jnp.arange(n_blocks, dtype=I32) * bm
    return jnp.minimum(jnp.sum((pend[None, :] <= starts[:, None]).astype(I32), axis=1), N_EXPERTS - 1)


def kernel(x, c, w_ada, b_ada, g_mix, w_in, conv_w, conv_b, g_conv, lam_re, lam_im, log_dt, b_re, b_im,
           c_re, c_im, d_skip, w_glu, b_glu, g_ssm, w_out, g_ffn, w_router, router_bias, w1, w3, w2,
           ws1, ws3, ws2, g_final):
    bsz, seq, d = x.shape
    n_tok = bsz * seq
    ts = min(512, seq)
    L = min(S5_CHUNK, seq)
    n_chunks = seq // L
    G, H = SSM_GROUPS, SSM_GROUP_CH

    mod = _adaln(c, w_ada, b_ada).reshape(bsz, 6, 1, d)
    shift1, scale1, gate1, shift2, scale2, gate2 = (mod[:, i] for i in range(6))

    yconv, ut = _mix_front(x, shift1, scale1, g_mix, w_in.astype(BF16), conv_w, conv_b, g_conv, ts)

    u_g = ut.reshape(bsz, G, H, n_chunks, L).transpose(1, 0, 3, 2, 4).reshape(G, bsz * n_chunks, H * L)
    kern, e_mat, f_mat, a_pow = _s5_tables(lam_re, lam_im, log_dt, b_re, b_im, c_re, c_im, d_skip,
                                           L, n_chunks)
    y_g = _s5_core(u_g, kern, e_mat, f_mat, a_pow, n_chunks, L)
    yt = y_g.reshape(G, bsz, n_chunks, H, L).transpose(1, 0, 3, 2, 4).reshape(bsz, G * H, seq)

    wr_hi = w_router.astype(BF16)
    wr_lo = (w_router - wr_hi.astype(F32)).astype(BF16)
    w_out_bf = w_out.astype(BF16)
    w_glu_bf = w_glu.astype(BF16)
    ws13 = jnp.concatenate([ws1, ws3], axis=1).astype(BF16)
    ws2_bf = ws2.astype(BF16)
    bm = EXPERT_ROWS
    tt = min(COMBINE_TOKENS, seq)

    n_parts = MOE_PARTS if bsz % MOE_PARTS == 0 else 1
    pb = bsz // n_parts
    pt = pb * seq
    n_blocks = -(-(pt * TOP_K + N_EXPERTS * (bm - 1)) // bm)
    parts = []
    for p in range(n_parts):
        x1, h2p, e8, w8, r8, counts = _mix_back(
            x, yconv, yt, gate1, shift2, scale2, gate2, w_glu_bf, b_glu, g_ssm,
            w_out_bf[:D_CONV], w_out_bf[D_CONV:], g_ffn, wr_hi, wr_lo, router_bias, ws13, ws2_bf,
            ts, p * pb, pb)
        pos8 = _plan(counts, e8, r8, bm)
        posflat = pos8[:TOP_K].reshape(TOP_K * pt)
        block_e = _block_experts(counts[:, 0], bm, n_blocks)
        xs = _sc_dispatch(h2p.reshape(pt, d // 2), posflat, n_blocks * bm)
        parts.append((x1, w8, posflat, block_e, xs))
    ys = [_experts(block_e, xs, w1, w3, w2, bm) for (_, _, _, block_e, xs) in parts]
    g6 = [_sc_gather(y, posflat).reshape(TOP_K, pt, d // 2) for y, (_, _, posflat, _, _) in zip(ys, parts)]
    out = None
    for p, (g, (x1, w8, _, _, _)) in enumerate(zip(g6, parts)):
        out = _combine(g, w8.reshape(pt, 8), x1.reshape(pt, d), gate2, g_final, tt, seq // tt,
                       p * pb, n_tok, out)
    return out.reshape(bsz, seq, d)
```

```python
import functools
import math

import jax
import jax.numpy as jnp
import numpy as np
from jax import lax
from jax.experimental import pallas as pl
from jax.experimental.pallas import tpu as pltpu
from jax.experimental.pallas import tpu_sc as plsc

F32 = jnp.float32
BF16 = jnp.bfloat16
U32 = jnp.uint32
I32 = jnp.int32

D_CONV = 768
D_SSM = 256
SSM_GROUPS = 16
SSM_GROUP_CH = 16
SSM_STATE = 64
N_EXPERTS = 64
TOP_K = 6
N_EXPERT_GROUPS = 8
TOPK_GROUPS = 4
EXPERTS_PER_GROUP = 8
ROUTED_SCALE = 2.5
RMS_EPS = 1e-6

S5_CHUNK = 128
EXPERT_ROWS = 1024
EXPERT_SUB = 512
MIX_FRONT_TOKENS = 1024
MIX_FRONT_SUB = 1024
MIX_SUB = 256
COMBINE_TOKENS = 512
MOE_PARTS = 2
V7X_SC_CORES = 2
V7X_SC_SUBCORES = 16
SC_WORKERS = V7X_SC_CORES * V7X_SC_SUBCORES
SC_WINDOW = 64
V7X_VMEM_LIMIT = 56 * 1024 * 1024
NEG_INF = float("-inf")
HIGH_HALF = np.uint32(0xFFFF0000)


def _rms(x, g):
    return x * lax.rsqrt(jnp.mean(x * x, axis=-1, keepdims=True) + RMS_EPS) * g


def _gelu_tanh(x):
    return 0.5 * x * (1.0 + jnp.tanh(math.sqrt(2.0 / math.pi) * (x + 0.044715 * (x * x * x))))


def _silu(x):
    return x * jax.nn.sigmoid(x)


def _params(*sem):
    return pltpu.CompilerParams(dimension_semantics=sem, vmem_limit_bytes=V7X_VMEM_LIMIT)


def _pack_halves(y):
    m = y.shape[1] // 2
    bits = lax.bitcast_convert_type(y.astype(BF16).astype(F32), U32)
    return (bits[:, m:] & HIGH_HALF) | (bits[:, :m] >> 16)


def _unpack_halves(w):
    lo = lax.bitcast_convert_type(w << 16, F32)
    hi = lax.bitcast_convert_type(w & HIGH_HALF, F32)
    return lo, hi


def _adaln_kernel(c_ref, w_ref, b_ref, o_ref):
    o_ref[...] = jnp.dot(_silu(c_ref[...]), w_ref[...], preferred_element_type=F32,
                         precision=lax.Precision.HIGHEST) + b_ref[...]


def _adaln(c, w_ada, b_ada):
    bsz, d = c.shape
    n = w_ada.shape[1]
    bn = 1024
    return pl.pallas_call(
        _adaln_kernel,
        grid=(n // bn,),
        in_specs=[pl.BlockSpec((bsz, d), lambda j: (0, 0)),
                  pl.BlockSpec((d, bn), lambda j: (0, j)),
                  pl.BlockSpec((1, bn), lambda j: (0, j))],
        out_specs=pl.BlockSpec((bsz, bn), lambda j: (0, j)),
        out_shape=jax.ShapeDtypeStruct((bsz, n), F32),
        compiler_params=_params("arbitrary"),
        name="adaln",
    )(c, w_ada, b_ada.reshape(1, n))


def _mix_front_kernel(x_ref, shift_ref, scale_ref, gmix_ref, win_ref, cw_ref, cb_ref, gconv_ref,
                      yconv_ref, ut_ref, zprev_ref):
    s = pl.program_id(1)

    @pl.when(s == 0)
    def _():
        zprev_ref[...] = jnp.zeros_like(zprev_ref)

    prev = zprev_ref[...]
    cw = cw_ref[...]
    sub = min(MIX_FRONT_SUB, x_ref.shape[1])
    for r0 in range(0, x_ref.shape[1], sub):
        x = x_ref[0, r0:r0 + sub, :]
        h = _rms(x, gmix_ref[...]) * (1.0 + scale_ref[0]) + shift_ref[0]
        proj = jnp.dot(h.astype(BF16), win_ref[...], preferred_element_type=F32)
        b_gate = proj[:, :D_CONV]
        c_gate = proj[:, D_CONV:2 * D_CONV]
        v = proj[:, 2 * D_CONV:3 * D_CONV]
        u = proj[:, 3 * D_CONV:]

        z = c_gate * v
        rid = lax.broadcasted_iota(I32, z.shape, 0)
        z1 = jnp.where(rid == 0, prev[7:8, :], pltpu.roll(z, 1, axis=0))
        z2 = jnp.where(rid == 0, prev[6:7, :], jnp.where(rid == 1, prev[7:8, :], pltpu.roll(z, 2, axis=0)))
        prev = z[sub - 8:, :]
        conv = cw[0:1, :] * z2 + cw[1:2, :] * z1 + cw[2:3, :] * z + cb_ref[...]
        yconv_ref[0, r0:r0 + sub, :] = _rms(b_gate * conv, gconv_ref[...]).astype(BF16)
        ut_ref[0, :, r0:r0 + sub] = u.T.astype(BF16)
    zprev_ref[...] = prev


def _mix_front(x, shift1, scale1, g_mix, w_in_bf, conv_w, conv_b, g_conv, ts):
    bsz, seq, d = x.shape
    d_in = w_in_bf.shape[1]
    row = lambda b, s: (b, 0, 0)
    const2 = lambda b, s: (0, 0)
    return pl.pallas_call(
        _mix_front_kernel,
        grid=(bsz, seq // ts),
        in_specs=[pl.BlockSpec((1, ts, d), lambda b, s: (b, s, 0)),
                  pl.BlockSpec((1, 1, d), row),
                  pl.BlockSpec((1, 1, d), row),
                  pl.BlockSpec((1, d), const2),
                  pl.BlockSpec((d, d_in), const2),
                  pl.BlockSpec((8, D_CONV), const2),
                  pl.BlockSpec((1, D_CONV), const2),
                  pl.BlockSpec((1, D_CONV), const2)],
        out_specs=[pl.BlockSpec((1, ts, D_CONV), lambda b, s: (b, s, 0)),
                   pl.BlockSpec((1, D_SSM, ts), lambda b, s: (b, 0, s))],
        out_shape=[jax.ShapeDtypeStruct((bsz, seq, D_CONV), BF16),
                   jax.ShapeDtypeStruct((bsz, D_SSM, seq), BF16)],
        scratch_shapes=[pltpu.VMEM((8, D_CONV), F32)],
        compiler_params=_params("arbitrary", "arbitrary"),
        name="mix_front",
    )(x, shift1, scale1, g_mix.reshape(1, d), w_in_bf,
      jnp.pad(conv_w, ((0, 8 - conv_w.shape[0]), (0, 0))), conv_b.reshape(1, D_CONV),
      g_conv.reshape(1, D_CONV))


def _s5_tables(lam_re, lam_im, log_dt, b_re, b_im, c_re, c_im, d_skip, chunk, n_chunks):
    hp = lax.Precision.HIGHEST
    G, H, P, L = SSM_GROUPS, SSM_GROUP_CH, SSM_STATE, chunk
    lr = lam_re.astype(F32)
    li = lam_im.astype(F32)
    dt = jnp.exp(log_dt.astype(F32))[:, None]
    mag = jnp.exp(lr * dt)
    ang = li * dt
    ab_re = mag * jnp.cos(ang)
    ab_im = mag * jnp.sin(ang)
    den = lr * lr + li * li
    nr = ab_re - 1.0
    ni = ab_im
    q_re = (nr * lr + ni * li) / den
    q_im = (ni * lr - nr * li) / den
    br = b_re.astype(F32)
    bi = b_im.astype(F32)
    bb_re = q_re[..., None] * br - q_im[..., None] * bi
    bb_im = q_re[..., None] * bi + q_im[..., None] * br

    def a_power(tau):
        t = tau.astype(F32)[None, :, None]
        m = jnp.exp(t * (lr * dt)[:, None, :])
        th = t * ang[:, None, :]
        return m * jnp.cos(th), m * jnp.sin(th)

    pw_re, pw_im = a_power(jnp.arange(L + 1))
    cr = c_re.astype(F32)[:, None]
    ci = c_im.astype(F32)[:, None]
    cp_re = cr * pw_re[:, :, None, :] - ci * pw_im[:, :, None, :]
    cp_im = cr * pw_im[:, :, None, :] + ci * pw_re[:, :, None, :]
    kern = (jnp.einsum('gthp,gpk->gthk', cp_re[:, :L], bb_re, precision=hp)
            - jnp.einsum('gthp,gpk->gthk', cp_im[:, :L], bb_im, precision=hp))
    kern = kern.at[:, 0].add(d_skip.astype(F32).reshape(G, H)[:, :, None] * jnp.eye(H, dtype=F32))
    kern = kern.transpose(0, 3, 2, 1).reshape(G, H * H, L)

    rev_re = pw_re[:, L - 1::-1]
    rev_im = pw_im[:, L - 1::-1]
    bt_re = bb_re.transpose(0, 2, 1)[:, :, None, :]
    bt_im = bb_im.transpose(0, 2, 1)[:, :, None, :]
    e_re = rev_re[:, None] * bt_re - rev_im[:, None] * bt_im
    e_im = rev_re[:, None] * bt_im + rev_im[:, None] * bt_re
    e_mat = jnp.concatenate([e_re, e_im], axis=-1).reshape(G, H * L, 2 * P)

    f_re = cp_re[:, 1:].transpose(0, 3, 2, 1)
    f_im = -cp_im[:, 1:].transpose(0, 3, 2, 1)
    f_mat = jnp.concatenate([f_re, f_im], axis=1).reshape(G, 2 * P, H * L)

    n_steps = max(1, (n_chunks - 1).bit_length())
    sr, si = a_power(L * (2 ** jnp.arange(n_steps)))
    a_pow = jnp.stack([jnp.concatenate([sr, sr], axis=-1),
                       jnp.concatenate([-si, si], axis=-1)], axis=2)
    return kern, e_mat.astype(BF16), f_mat.astype(BF16), a_pow


def _s5_kernel(u_ref, k_ref, e_ref, f_ref, a_ref, y_ref, toep_ref, *, n_chunks, n_steps, chunk):
    L, H = chunk, SSM_GROUP_CH

    causal = lax.broadcasted_iota(I32, (L, L), 1) >= lax.broadcasted_iota(I32, (L, L), 0)

    def build(hin, carry):
        r0 = pl.multiple_of(hin * L, L)
        for hout in range(H):
            krow = k_ref[0, pl.ds(hin * H + hout, 1), :]
            blk = pltpu.roll(jnp.broadcast_to(krow, (L, L)), 0, axis=1, stride=1, stride_axis=0)
            toep_ref[pl.ds(r0, L), hout * L:(hout + 1) * L] = jnp.where(causal, blk, 0.0).astype(BF16)
        return carry

    lax.fori_loop(0, H, build, 0)

    u = u_ref[0]
    st = jnp.dot(u, e_ref[0], preferred_element_type=F32)
    cidx = lax.broadcasted_iota(I32, st.shape, 0) % n_chunks
    p2 = st.shape[1]

    def shifted(xv, d):
        return jnp.where(cidx >= d, pltpu.roll(xv, d, axis=0), 0.0)

    for k in range(n_steps):
        d = 1 << k
        if d >= n_chunks:
            break
        z = shifted(st, d)
        st = st + a_ref[0, k, 0:1, :] * z + a_ref[0, k, 1:2, :] * pltpu.roll(z, p2 // 2, axis=1)
    s_in = shifted(st, 1)
    y = jnp.dot(u, toep_ref[...], preferred_element_type=F32)
    y = y + jnp.dot(s_in.astype(BF16), f_ref[0], preferred_element_type=F32)
    y_ref[0] = y.astype(BF16)


def _s5_core(u_g, kern, e_mat, f_mat, a_pow, n_chunks, chunk):
    g, rows, hl = u_g.shape
    n_steps = a_pow.shape[1]
    p2 = e_mat.shape[2]
    blk = lambda i: (i, 0, 0)
    return pl.pallas_call(
        functools.partial(_s5_kernel, n_chunks=n_chunks, n_steps=n_steps, chunk=chunk),
        grid=(g,),
        in_specs=[pl.BlockSpec((1, rows, hl), blk),
                  pl.BlockSpec((1,) + kern.shape[1:], blk),
                  pl.BlockSpec((1, hl, p2), blk),
                  pl.BlockSpec((1, p2, hl), blk),
                  pl.BlockSpec((1, n_steps, 2, p2), lambda i: (i, 0, 0, 0))],
        out_specs=pl.BlockSpec((1, rows, hl), blk),
        out_shape=jax.ShapeDtypeStruct((g, rows, hl), BF16),
        scratch_shapes=[pltpu.VMEM((hl, hl), BF16)],
        compiler_params=_params("arbitrary"),
        name="s5_core",
    )(u_g, kern, e_mat, f_mat, a_pow)


def _route(scores, biased, tri, base):
    n_tok = scores.shape[1]
    shape3 = (EXPERTS_PER_GROUP, N_EXPERT_GROUPS, n_tok)
    sc3 = scores.reshape(shape3)
    b3 = biased.reshape(shape3)
    j_iota = lax.broadcasted_iota(I32, shape3, 0)
    e_iota = lax.broadcasted_iota(I32, shape3, 1) * EXPERTS_PER_GROUP + j_iota

    def red(fn, x):
        return fn(fn(x, axis=0, keepdims=True), axis=1, keepdims=True)

    m1 = jnp.max(b3, axis=0, keepdims=True)
    i1 = jnp.min(jnp.where(b3 == m1, j_iota, EXPERTS_PER_GROUP), axis=0, keepdims=True)
    m2 = jnp.max(jnp.where(j_iota == i1, NEG_INF, b3), axis=0, keepdims=True)
    gs = m1 + m2

    g_iota = lax.broadcasted_iota(I32, gs.shape, 1)
    gsel = jnp.zeros(gs.shape, F32)
    cur = gs
    for _ in range(TOPK_GROUPS):
        m = jnp.max(cur, axis=1, keepdims=True)
        ig = jnp.min(jnp.where(cur == m, g_iota, N_EXPERT_GROUPS), axis=1, keepdims=True)
        pick = g_iota == ig
        gsel = jnp.where(pick, 1.0, gsel)
        cur = jnp.where(pick, NEG_INF, cur)

    cur = jnp.where(gsel > 0.0, b3, NEG_INF)
    sel = jnp.zeros(shape3, F32)
    ids, vals = [], []
    for _ in range(TOP_K):
        m = red(jnp.max, cur)
        ie = red(jnp.min, jnp.where(cur == m, e_iota, N_EXPERTS))
        pick = e_iota == ie
        ids.append(ie)
        vals.append(red(jnp.sum, jnp.where(pick, sc3, 0.0)))
        sel = jnp.where(pick, 1.0, sel)
        cur = jnp.where(pick, NEG_INF, cur)
    tot = functools.reduce(lambda a, b: a + b, vals)

    sel2 = sel.reshape(N_EXPERTS, n_tok)
    before = jnp.dot(sel2.astype(BF16), tri, preferred_element_type=F32) + base
    before3 = before.reshape(shape3)
    ranks = [red(jnp.sum, jnp.where(e_iota == ie, before3, 0.0)) for ie in ids]

    def rows(parts, dtype):
        parts = [p.reshape(1, n_tok).astype(dtype) for p in parts]
        return jnp.concatenate(parts + [jnp.zeros((8 - len(parts), n_tok), dtype)], axis=0)

    e8 = rows(ids, I32)
    w8 = rows([v / tot * ROUTED_SCALE for v in vals], F32)
    r8 = rows(ranks, I32)
    return e8, w8, r8, jnp.sum(sel2, axis=1, keepdims=True)


def _mix_back_kernel(x_ref, yconv_ref, yt_ref, gate1_ref, shift2_ref, scale2_ref, gate2_ref,
                     wglu_ref, bglu_ref, gssm_ref, woc_ref, wos_ref, gffn_ref,
                     wrh_ref, wrl_ref, rbias_ref, ws13_ref, ws2_ref, tri_ref,
                     x1_ref, h2p_ref, e8_ref, w8_ref, r8_ref, cnt_ref):
    first = jnp.logical_and(pl.program_id(0) == 0, pl.program_id(1) == 0)

    @pl.when(first)
    def _():
        cnt_ref[...] = jnp.zeros_like(cnt_ref)

    sub = tri_ref.shape[0]
    base = cnt_ref[:, 0:1]
    for r0 in range(0, x_ref.shape[1], sub):
        y = yt_ref[0, :, r0:r0 + sub].astype(F32).T
        y = _gelu_tanh(y)
        y = y * jax.nn.sigmoid(jnp.dot(y.astype(BF16), wglu_ref[...], preferred_element_type=F32)
                               + bglu_ref[...])
        y_ssm = _rms(y, gssm_ref[...])
        mix = jnp.dot(yconv_ref[0, r0:r0 + sub, :], woc_ref[...], preferred_element_type=F32)
        mix = mix + jnp.dot(y_ssm.astype(BF16), wos_ref[...], preferred_element_type=F32)
        x1 = x_ref[0, r0:r0 + sub, :] + gate1_ref[0] * mix

        h2 = _rms(x1, gffn_ref[...]) * (1.0 + scale2_ref[0]) + shift2_ref[0]
        h2b = h2.astype(BF16)
        h2p_ref[0, r0:r0 + sub, :] = lax.bitcast_convert_type(_pack_halves(h2), I32)

        a = jnp.dot(h2b, ws13_ref[...], preferred_element_type=F32)
        f = a.shape[1] // 2
        act = _silu(a[:, :f]) * a[:, f:]
        shared = jnp.dot(act.astype(BF16), ws2_ref[...], preferred_element_type=F32)
        x1_ref[0, r0:r0 + sub, :] = x1 + gate2_ref[0] * shared

        h2l = (h2 - h2b.astype(F32)).astype(BF16)
        lt = (jnp.dot(h2b, wrh_ref[...], preferred_element_type=F32)
              + jnp.dot(h2l, wrl_ref[...], preferred_element_type=F32)).T
        scores = jax.nn.sigmoid(lt[:N_EXPERTS] + lt[N_EXPERTS:])
        e8, w8, r8, cnt = _route(scores, scores + rbias_ref[...], tri_ref[...], base)
        e8_ref[:, r0:r0 + sub] = e8
        w8_ref[0, r0:r0 + sub, :] = w8.T
        r8_ref[:, r0:r0 + sub] = r8
        base = base + cnt
    cnt_ref[...] = jnp.broadcast_to(base, cnt_ref.shape)


def _mix_back(x, yconv, yt, gate1, shift2, scale2, gate2, w_glu_bf, b_glu, g_ssm, wo_conv, wo_ssm,
              g_ffn, wr_hi, wr_lo, router_bias, ws13, ws2_bf, ts, b0, bsz):
    _, seq, d = x.shape
    n_tok = bsz * seq
    tiles = seq // ts
    row = lambda b, s: (b + b0, 0, 0)
    const2 = lambda b, s: (0, 0)
    tile_in = lambda b, s: (b + b0, s, 0)
    tile = lambda b, s: (b, s, 0)
    flat = lambda b, s: (0, b * tiles + s)
    full = lambda a: pl.BlockSpec(a.shape, const2)
    sub = min(MIX_SUB, ts)
    tri = jnp.triu(jnp.ones((sub, sub), BF16), k=1)
    args = (w_glu_bf, b_glu.reshape(1, D_SSM), g_ssm.reshape(1, D_SSM), wo_conv, wo_ssm,
            g_ffn.reshape(1, d), wr_hi, wr_lo, router_bias.reshape(N_EXPERTS, 1), ws13, ws2_bf, tri)
    return pl.pallas_call(
        _mix_back_kernel,
        grid=(bsz, tiles),
        in_specs=[pl.BlockSpec((1, ts, d), tile_in),
                  pl.BlockSpec((1, ts, D_CONV), tile_in),
                  pl.BlockSpec((1, D_SSM, ts), lambda b, s: (b + b0, 0, s)),
                  pl.BlockSpec((1, 1, d), row), pl.BlockSpec((1, 1, d), row),
                  pl.BlockSpec((1, 1, d), row), pl.BlockSpec((1, 1, d), row)]
                 + [full(a) for a in args],
        out_specs=[pl.BlockSpec((1, ts, d), tile),
                   pl.BlockSpec((1, ts, d // 2), tile),
                   pl.BlockSpec((8, ts), flat),
                   pl.BlockSpec((1, ts, 8), tile),
                   pl.BlockSpec((8, ts), flat),
                   pl.BlockSpec((N_EXPERTS, 128), const2)],
        out_shape=[jax.ShapeDtypeStruct((bsz, seq, d), F32),
                   jax.ShapeDtypeStruct((bsz, seq, d // 2), I32),
                   jax.ShapeDtypeStruct((8, n_tok), I32),
                   jax.ShapeDtypeStruct((bsz, seq, 8), F32),
                   jax.ShapeDtypeStruct((8, n_tok), I32),
                   jax.ShapeDtypeStruct((N_EXPERTS, 128), F32)],
        compiler_params=_params("arbitrary", "arbitrary"),
        name="mix_back",
    )(x, yconv, yt, gate1, shift2, scale2, gate2, *args)


def _plan_kernel(cnt_ref, e8_ref, r8_ref, pos_ref, *, bm):
    cnt = cnt_ref[...].astype(I32)
    padded = (cnt + (bm - 1)) // bm * bm
    rid = lax.broadcasted_iota(I32, padded.shape, 0)
    incl = padded
    d = 1
    while d < N_EXPERTS:
        incl = incl + jnp.where(rid >= d, pltpu.roll(incl, d, axis=0), 0)
        d *= 2
    pstart = incl - padded
    lanes = pstart.shape[1]

    def chunk(ci, carry):
        c0 = pl.multiple_of(ci * lanes, lanes)
        e = e8_ref[:, pl.ds(c0, lanes)]
        acc = r8_ref[:, pl.ds(c0, lanes)]
        for ex in range(N_EXPERTS):
            acc = acc + jnp.where(e == ex, pstart[ex:ex + 1, :], 0)
        pos_ref[:, pl.ds(c0, lanes)] = acc
        return carry

    lax.fori_loop(0, e8_ref.shape[1] // lanes, chunk, 0)


def _plan(counts, e8, r8, bm):
    n_tok = e8.shape[1]
    tb = min(8192, n_tok)
    return pl.pallas_call(
        functools.partial(_plan_kernel, bm=bm),
        grid=(n_tok // tb,),
        in_specs=[pl.BlockSpec(counts.shape, lambda i: (0, 0)),
                  pl.BlockSpec((8, tb), lambda i: (0, i)),
                  pl.BlockSpec((8, tb), lambda i: (0, i))],
        out_specs=pl.BlockSpec((8, tb), lambda i: (0, i)),
        out_shape=jax.ShapeDtypeStruct((8, n_tok), I32),
        compiler_params=_params("arbitrary"),
        name="plan",
    )(counts, e8, r8)


def _sc_mesh():
    return plsc.VectorSubcoreMesh(core_axis_name="c", subcore_axis_name="s",
                                  num_cores=V7X_SC_CORES, num_subcores=V7X_SC_SUBCORES)


def _sc_worker_base(per_worker):
    return (lax.axis_index("s") * V7X_SC_CORES + lax.axis_index("c")) * per_worker


def _sc_dispatch(h2p, posflat, n_rows):
    n_tok, half = h2p.shape
    win = SC_WINDOW
    per_worker = n_tok // SC_WORKERS
    n_win = per_worker // win

    wins_per_k = n_tok // win

    @functools.partial(
        pl.kernel, mesh=_sc_mesh(),
        out_type=jax.ShapeDtypeStruct((n_rows, half), I32),
        scratch_types=[pltpu.VMEM((TOP_K, n_win, win), I32), pltpu.VMEM((2, win, half), I32),
                       pltpu.SemaphoreType.DMA((2,)), pltpu.SemaphoreType.DMA((2,))],
        name="sc_dispatch")
    def run(h2p_hbm, pos_hbm, xs_hbm, idx_v, rows_v, sem_in, sem_out):
        wid = _sc_worker_base(1)
        for k in range(TOP_K):
            pltpu.sync_copy(pos_hbm.at[pl.ds(k * wins_per_k + wid * n_win, n_win)], idx_v.at[k])

        @pl.loop(0, n_win, step=2)
        def _(j):
            loads = []
            for b in range(2):
                t0 = pl.multiple_of((wid * n_win + j + b) * win, win)
                loads.append(pltpu.async_copy(h2p_hbm.at[pl.ds(t0, win)], rows_v.at[b], sem_in.at[b]))
            stores = []
            for b in range(2):
                loads[b].wait()
                for k in range(TOP_K):
                    stores.append(pltpu.async_copy(rows_v.at[b], xs_hbm.at[idx_v.at[k, j + b]], sem_out.at[b]))
            for cp in stores:
                cp.wait()

    return run(h2p, posflat.reshape(TOP_K * wins_per_k, win))


def _sc_gather(ys, posflat):
    n_idx = posflat.shape[0]
    half = ys.shape[1]
    win = SC_WINDOW
    per_worker = n_idx // SC_WORKERS
    n_win = per_worker // win

    @functools.partial(
        pl.kernel, mesh=_sc_mesh(),
        out_type=jax.ShapeDtypeStruct((n_idx, half), I32),
        scratch_types=[pltpu.VMEM((n_win, win), I32), pltpu.VMEM((2, win, half), I32),
                       pltpu.SemaphoreType.DMA((2,)), pltpu.SemaphoreType.DMA((2,))],
        name="sc_gather")
    def run(ys_hbm, pos_hbm, out_hbm, idx_v, rows_v, sem_in, sem_out):
        wid = _sc_worker_base(1)
        pltpu.sync_copy(pos_hbm.at[pl.ds(wid * n_win, n_win)], idx_v)

        @pl.loop(0, n_win, step=2)
        def _(j):
            gathers = [pltpu.async_copy(ys_hbm.at[idx_v.at[j + b]], rows_v.at[b], sem_in.at[b])
                       for b in range(2)]
            writes = []
            for b in range(2):
                gathers[b].wait()
                off = pl.multiple_of((wid * n_win + j + b) * win, win)
                writes.append(pltpu.async_copy(rows_v.at[b], out_hbm.at[pl.ds(off, win)], sem_out.at[b]))
            for cp in writes:
                cp.wait()

    return run(ys, posflat.reshape(n_idx // win, win))


def _experts_kernel(be_ref, nu_ref, x_ref, w1_ref, w3_ref, w2_ref, y_ref, w13_s, w2_s):
    i = pl.program_id(0)
    changed = jnp.logical_or(i == 0, be_ref[i] != be_ref[jnp.maximum(i - 1, 0)])

    @pl.when(changed)
    def _():
        f = w1_ref.shape[2]
        w13_s[:, :f] = w1_ref[0].astype(BF16)
        w13_s[:, f:] = w3_ref[0].astype(BF16)
        w2_s[...] = w2_ref[0].astype(BF16)

    @pl.when(i < nu_ref[0])
    def _():
        half = x_ref.shape[1]
        sub = min(EXPERT_SUB, x_ref.shape[0])
        for r0 in range(0, x_ref.shape[0], sub):
            lo, hi = _unpack_halves(lax.bitcast_convert_type(x_ref[r0:r0 + sub, :], U32))
            a = jnp.dot(lo.astype(BF16), w13_s[:half, :], preferred_element_type=F32)
            a = a + jnp.dot(hi.astype(BF16), w13_s[half:, :], preferred_element_type=F32)
            f = a.shape[1] // 2
            act = _silu(a[:, :f]) * a[:, f:]
            y = jnp.dot(act.astype(BF16), w2_s[...], preferred_element_type=F32)
            y_ref[r0:r0 + sub, :] = lax.bitcast_convert_type(_pack_halves(y), I32)


def _experts(block_e, n_used, xs, w1, w3, w2, bm):
    n_rows, half = xs.shape
    d, f = w1.shape[1], w1.shape[2]
    rows = lambda i, be, nu: (jnp.minimum(i, nu[0] - 1), 0)
    wblk = lambda i, be, nu: (be[i], 0, 0)
    grid_spec = pltpu.PrefetchScalarGridSpec(
        num_scalar_prefetch=2,
        grid=(n_rows // bm,),
        in_specs=[pl.BlockSpec((bm, half), rows),
                  pl.BlockSpec((1, d, f), wblk),
                  pl.BlockSpec((1, d, f), wblk),
                  pl.BlockSpec((1, f, d), wblk)],
        out_specs=pl.BlockSpec((bm, half), rows),
        scratch_shapes=[pltpu.VMEM((d, 2 * f), BF16), pltpu.VMEM((f, d), BF16)],
    )
    return pl.pallas_call(
        _experts_kernel,
        grid_spec=grid_spec,
        out_shape=jax.ShapeDtypeStruct((n_rows, half), I32),
        compiler_params=_params("arbitrary"),
        name="experts",
    )(block_e, n_used, xs, w1, w3, w2)


def _combine_kernel(g_ref, w8_ref, x1_ref, gate2_ref, gfin_ref, *rest):
    o_ref = rest[-1]
    w8 = w8_ref[...]
    acc_lo = acc_hi = None
    for k in range(TOP_K):
        lo, hi = _unpack_halves(lax.bitcast_convert_type(g_ref[k], U32))
        wk = w8[:, k:k + 1]
        acc_lo = wk * lo if acc_lo is None else acc_lo + wk * lo
        acc_hi = wk * hi if acc_hi is None else acc_hi + wk * hi
    routed = jnp.concatenate([acc_lo, acc_hi], axis=1)
    o_ref[...] = _rms(x1_ref[...] + gate2_ref[0] * routed, gfin_ref[...])


def _combine(g6, w8, x1, gate2, g_final, tt, tiles_per_seq, b0, n_tok_all, out_prev):
    n_tok, d = x1.shape
    half = g6.shape[2]
    tok = lambda i: (i, 0)
    tile0 = b0 * tiles_per_seq
    in_specs = [pl.BlockSpec((TOP_K, tt, half), lambda i: (0, i, 0)),
                pl.BlockSpec((tt, 8), tok),
                pl.BlockSpec((tt, d), tok),
                pl.BlockSpec((1, 1, d), lambda i: (i // tiles_per_seq + b0, 0, 0)),
                pl.BlockSpec((1, d), lambda i: (0, 0))]
    args = [g6, w8, x1, gate2, g_final.reshape(1, d)]
    aliases = {}
    if out_prev is not None:
        in_specs.append(pl.BlockSpec(memory_space=pl.ANY))
        args.append(out_prev)
        aliases = {len(args) - 1: 0}
    return pl.pallas_call(
        _combine_kernel,
        grid=(n_tok // tt,),
        in_specs=in_specs,
        out_specs=pl.BlockSpec((tt, d), lambda i: (i + tile0, 0)),
        out_shape=jax.ShapeDtypeStruct((n_tok_all, d), F32),
        input_output_aliases=aliases,
        compiler_params=_params("arbitrary"),
        name="combine",
    )(*args)


def _block_experts(counts, bm, n_blocks):
    padded = (counts.astype(I32) + (bm - 1)) // bm * bm
    pend = jnp.cumsum(padded)
    starts = jnp.arange(n_blocks, dtype=I32) * bm
    block_e = jnp.minimum(jnp.sum((pend[None, :] <= starts[:, None]).astype(I32), axis=1), N_EXPERTS - 1)
    return block_e, (pend[-1:] // bm).astype(I32)


def kernel(x, c, w_ada, b_ada, g_mix, w_in, conv_w, conv_b, g_conv, lam_re, lam_im, log_dt, b_re, b_im,
           c_re, c_im, d_skip, w_glu, b_glu, g_ssm, w_out, g_ffn, w_router, router_bias, w1, w3, w2,
           ws1, ws3, ws2, g_final):
    bsz, seq, d = x.shape
    n_tok = bsz * seq
    ts = min(512, seq)
    L = min(S5_CHUNK, seq)
    n_chunks = seq // L
    G, H = SSM_GROUPS, SSM_GROUP_CH

    mod = _adaln(c, w_ada, b_ada).reshape(bsz, 6, 1, d)
    shift1, scale1, gate1, shift2, scale2, gate2 = (mod[:, i] for i in range(6))

    yconv, ut = _mix_front(x, shift1, scale1, g_mix, w_in.astype(BF16), conv_w, conv_b, g_conv,
                           min(MIX_FRONT_TOKENS, seq))

    u_g = ut.reshape(bsz, G, H, n_chunks, L).transpose(1, 0, 3, 2, 4).reshape(G, bsz * n_chunks, H * L)
    kern, e_mat, f_mat, a_pow = _s5_tables(lam_re, lam_im, log_dt, b_re, b_im, c_re, c_im, d_skip,
                                           L, n_chunks)
    y_g = _s5_core(u_g, kern, e_mat, f_mat, a_pow, n_chunks, L)
    yt = y_g.reshape(G, bsz, n_chunks, H, L).transpose(1, 0, 3, 2, 4).reshape(bsz, G * H, seq)

    NG, NJ = N_EXPERT_GROUPS, EXPERTS_PER_GROUP
    w_r = w_router.astype(F32).reshape(d, NG, NJ).transpose(0, 2, 1).reshape(d, N_EXPERTS)
    r_bias = router_bias.reshape(NG, NJ).T.reshape(N_EXPERTS)
    w_r_hi = w_r.astype(BF16)
    wr_hi = jnp.concatenate([w_r_hi, (w_r - w_r_hi.astype(F32)).astype(BF16)], axis=1)
    wr_lo = jnp.concatenate([w_r_hi, jnp.zeros_like(w_r_hi)], axis=1)
    w_out_bf = w_out.astype(BF16)
    w_glu_bf = w_glu.astype(BF16)
    ws13 = jnp.concatenate([ws1, ws3], axis=1).astype(BF16)
    ws2_bf = ws2.astype(BF16)
    bm = EXPERT_ROWS
    tt = min(COMBINE_TOKENS, seq)

    n_parts = MOE_PARTS if bsz % MOE_PARTS == 0 else 1
    pb = bsz // n_parts
    pt = pb * seq
    n_blocks = -(-(pt * TOP_K + N_EXPERTS * (bm - 1)) // bm)
    parts = []
    for p in range(n_parts):
        x1, h2p, e8, w8, r8, counts = _mix_back(
            x, yconv, yt, gate1, shift2, scale2, gate2, w_glu_bf, b_glu, g_ssm,
            w_out_bf[:D_CONV], w_out_bf[D_CONV:], g_ffn, wr_hi, wr_lo, r_bias, ws13, ws2_bf,
            ts, p * pb, pb)
        counts = counts.reshape(NJ, NG, -1).transpose(1, 0, 2).reshape(N_EXPERTS, -1)
        pos8 = _plan(counts, e8, r8, bm)
        posflat = pos8[:TOP_K].reshape(TOP_K * pt)
        block_e = _block_experts(counts[:, 0], bm, n_blocks)
        xs = _sc_dispatch(h2p.reshape(pt, d // 2), posflat, n_blocks * bm)
        parts.append((x1, w8, posflat, block_e, xs))
    ys = [_experts(*block_e, xs, w1, w3, w2, bm) for (_, _, _, block_e, xs) in parts]
    g6 = [_sc_gather(y, posflat).reshape(TOP_K, pt, d // 2) for y, (_, _, posflat, _, _) in zip(ys, parts)]
    out = None
    for p, (g, (x1, w8, _, _, _)) in enumerate(zip(g6, parts)):
        out = _combine(g, w8.reshape(pt, 8), x1.reshape(pt, d), gate2, g_final, tt, seq // tt,
                       p * pb, n_tok, out)
    return out.reshape(bsz, seq, d)
```

```python
import functools
import math

import jax
import jax.numpy as jnp
import numpy as np
from jax import lax
from jax.experimental import pallas as pl
from jax.experimental.pallas import tpu as pltpu
from jax.experimental.pallas import tpu_sc as plsc

F32 = jnp.float32
BF16 = jnp.bfloat16
U32 = jnp.uint32
I32 = jnp.int32

D_CONV = 768
D_SSM = 256
SSM_GROUPS = 16
SSM_GROUP_CH = 16
SSM_STATE = 64
N_EXPERTS = 64
TOP_K = 6
N_EXPERT_GROUPS = 8
TOPK_GROUPS = 4
EXPERTS_PER_GROUP = 8
ROUTED_SCALE = 2.5
RMS_EPS = 1e-6

S5_CHUNK = 128
EXPERT_ROWS = 1024
EXPERT_SUB = 512
MIX_FRONT_TOKENS = 1024
MIX_FRONT_SUB = 1024
MIX_SUB = 256
COMBINE_TOKENS = 512
MOE_PARTS = 2
V7X_SC_CORES = 2
V7X_SC_SUBCORES = 16
SC_WORKERS = V7X_SC_CORES * V7X_SC_SUBCORES
SC_WINDOW = 64
V7X_VMEM_LIMIT = 56 * 1024 * 1024
NEG_INF = float("-inf")
HIGH_HALF = np.uint32(0xFFFF0000)


def _rms(x, g):
    return x * lax.rsqrt(jnp.mean(x * x, axis=-1, keepdims=True) + RMS_EPS) * g


def _gelu_tanh(x):
    return 0.5 * x * (1.0 + jnp.tanh(math.sqrt(2.0 / math.pi) * (x + 0.044715 * (x * x * x))))


def _silu(x):
    return x * jax.nn.sigmoid(x)


def _params(*sem):
    return pltpu.CompilerParams(dimension_semantics=sem, vmem_limit_bytes=V7X_VMEM_LIMIT)


def _pack_halves(y):
    m = y.shape[1] // 2
    bits = lax.bitcast_convert_type(y.astype(BF16).astype(F32), U32)
    return (bits[:, m:] & HIGH_HALF) | (bits[:, :m] >> 16)


def _unpack_halves(w):
    lo = lax.bitcast_convert_type(w << 16, F32)
    hi = lax.bitcast_convert_type(w & HIGH_HALF, F32)
    return lo, hi


def _adaln_kernel(c_ref, w_ref, b_ref, o_ref):
    o_ref[...] = jnp.dot(_silu(c_ref[...]), w_ref[...], preferred_element_type=F32,
                         precision=lax.Precision.HIGHEST) + b_ref[...]


def _adaln(c, w_ada, b_ada):
    bsz, d = c.shape
    n = w_ada.shape[1]
    bn = 1024
    return pl.pallas_call(
        _adaln_kernel,
        grid=(n // bn,),
        in_specs=[pl.BlockSpec((bsz, d), lambda j: (0, 0)),
                  pl.BlockSpec((d, bn), lambda j: (0, j)),
                  pl.BlockSpec((1, bn), lambda j: (0, j))],
        out_specs=pl.BlockSpec((bsz, bn), lambda j: (0, j)),
        out_shape=jax.ShapeDtypeStruct((bsz, n), F32),
        compiler_params=_params("arbitrary"),
        name="adaln",
    )(c, w_ada, b_ada.reshape(1, n))


def _mix_front_kernel(x_ref, shift_ref, scale_ref, gmix_ref, win_ref, cw_ref, cb_ref, gconv_ref,
                      yconv_ref, ut_ref, zprev_ref):
    s = pl.program_id(1)

    @pl.when(s == 0)
    def _():
        zprev_ref[...] = jnp.zeros_like(zprev_ref)

    prev = zprev_ref[...]
    cw = cw_ref[...]
    sub = min(MIX_FRONT_SUB, x_ref.shape[1])
    for r0 in range(0, x_ref.shape[1], sub):
        x = x_ref[0, r0:r0 + sub, :]
        h = _rms(x, gmix_ref[...]) * (1.0 + scale_ref[0]) + shift_ref[0]
        proj = jnp.dot(h.astype(BF16), win_ref[...], preferred_element_type=F32)
        b_gate = proj[:, :D_CONV]
        c_gate = proj[:, D_CONV:2 * D_CONV]
        v = proj[:, 2 * D_CONV:3 * D_CONV]
        u = proj[:, 3 * D_CONV:]

        z = c_gate * v
        rid = lax.broadcasted_iota(I32, z.shape, 0)
        z1 = jnp.where(rid == 0, prev[7:8, :], pltpu.roll(z, 1, axis=0))
        z2 = jnp.where(rid == 0, prev[6:7, :], jnp.where(rid == 1, prev[7:8, :], pltpu.roll(z, 2, axis=0)))
        prev = z[sub - 8:, :]
        conv = cw[0:1, :] * z2 + cw[1:2, :] * z1 + cw[2:3, :] * z + cb_ref[...]
        yconv_ref[0, r0:r0 + sub, :] = _rms(b_gate * conv, gconv_ref[...]).astype(BF16)
        ut_ref[0, :, r0:r0 + sub] = u.T.astype(BF16)
    zprev_ref[...] = prev


def _mix_front(x, shift1, scale1, g_mix, w_in_bf, conv_w, conv_b, g_conv, ts):
    bsz, seq, d = x.shape
    d_in = w_in_bf.shape[1]
    row = lambda b, s: (b, 0, 0)
    const2 = lambda b, s: (0, 0)
    return pl.pallas_call(
        _mix_front_kernel,
        grid=(bsz, seq // ts),
        in_specs=[pl.BlockSpec((1, ts, d), lambda b, s: (b, s, 0)),
                  pl.BlockSpec((1, 1, d), row),
                  pl.BlockSpec((1, 1, d), row),
                  pl.BlockSpec((1, d), const2),
                  pl.BlockSpec((d, d_in), const2),
                  pl.BlockSpec((8, D_CONV), const2),
                  pl.BlockSpec((1, D_CONV), const2),
                  pl.BlockSpec((1, D_CONV), const2)],
        out_specs=[pl.BlockSpec((1, ts, D_CONV), lambda b, s: (b, s, 0)),
                   pl.BlockSpec((1, D_SSM, ts), lambda b, s: (b, 0, s))],
        out_shape=[jax.ShapeDtypeStruct((bsz, seq, D_CONV), BF16),
                   jax.ShapeDtypeStruct((bsz, D_SSM, seq), BF16)],
        scratch_shapes=[pltpu.VMEM((8, D_CONV), F32)],
        compiler_params=_params("arbitrary", "arbitrary"),
        name="mix_front",
    )(x, shift1, scale1, g_mix.reshape(1, d), w_in_bf,
      jnp.pad(conv_w, ((0, 8 - conv_w.shape[0]), (0, 0))), conv_b.reshape(1, D_CONV),
      g_conv.reshape(1, D_CONV))


def _s5_tables(lam_re, lam_im, log_dt, b_re, b_im, c_re, c_im, d_skip, chunk, n_chunks):
    hp = lax.Precision.HIGHEST
    G, H, P, L = SSM_GROUPS, SSM_GROUP_CH, SSM_STATE, chunk
    lr = lam_re.astype(F32)
    li = lam_im.astype(F32)
    dt = jnp.exp(log_dt.astype(F32))[:, None]
    mag = jnp.exp(lr * dt)
    ang = li * dt
    ab_re = mag * jnp.cos(ang)
    ab_im = mag * jnp.sin(ang)
    den = lr * lr + li * li
    nr = ab_re - 1.0
    ni = ab_im
    q_re = (nr * lr + ni * li) / den
    q_im = (ni * lr - nr * li) / den
    br = b_re.astype(F32)
    bi = b_im.astype(F32)
    bb_re = q_re[..., None] * br - q_im[..., None] * bi
    bb_im = q_re[..., None] * bi + q_im[..., None] * br

    def a_power(tau):
        t = tau.astype(F32)[None, :, None]
        m = jnp.exp(t * (lr * dt)[:, None, :])
        th = t * ang[:, None, :]
        return m * jnp.cos(th), m * jnp.sin(th)

    pw_re, pw_im = a_power(jnp.arange(L + 1))
    cr = c_re.astype(F32)[:, None]
    ci = c_im.astype(F32)[:, None]
    cp_re = cr * pw_re[:, :, None, :] - ci * pw_im[:, :, None, :]
    cp_im = cr * pw_im[:, :, None, :] + ci * pw_re[:, :, None, :]
    kern = (jnp.einsum('gthp,gpk->gthk', cp_re[:, :L], bb_re, precision=hp)
            - jnp.einsum('gthp,gpk->gthk', cp_im[:, :L], bb_im, precision=hp))
    kern = kern.at[:, 0].add(d_skip.astype(F32).reshape(G, H)[:, :, None] * jnp.eye(H, dtype=F32))
    kern = kern.transpose(0, 3, 2, 1).reshape(G, H * H, L)

    rev_re = pw_re[:, L - 1::-1]
    rev_im = pw_im[:, L - 1::-1]
    bt_re = bb_re.transpose(0, 2, 1)[:, :, None, :]
    bt_im = bb_im.transpose(0, 2, 1)[:, :, None, :]
    e_re = rev_re[:, None] * bt_re - rev_im[:, None] * bt_im
    e_im = rev_re[:, None] * bt_im + rev_im[:, None] * bt_re
    e_mat = jnp.concatenate([e_re, e_im], axis=-1).reshape(G, H * L, 2 * P)

    f_re = cp_re[:, 1:].transpose(0, 3, 2, 1)
    f_im = -cp_im[:, 1:].transpose(0, 3, 2, 1)
    f_mat = jnp.concatenate([f_re, f_im], axis=1).reshape(G, 2 * P, H * L)

    n_steps = max(1, (n_chunks - 1).bit_length())
    sr, si = a_power(L * (2 ** jnp.arange(n_steps)))
    a_pow = jnp.stack([jnp.concatenate([sr, sr], axis=-1),
                       jnp.concatenate([-si, si], axis=-1)], axis=2)
    return kern, e_mat.astype(BF16), f_mat.astype(BF16), a_pow


def _s5_kernel(u_ref, k_ref, e_ref, f_ref, a_ref, y_ref, toep_ref, *, n_chunks, n_steps, chunk):
    L, H = chunk, SSM_GROUP_CH

    causal = lax.broadcasted_iota(I32, (L, L), 1) >= lax.broadcasted_iota(I32, (L, L), 0)

    def build(hin, carry):
        r0 = pl.multiple_of(hin * L, L)
        for hout in range(H):
            krow = k_ref[0, pl.ds(hin * H + hout, 1), :]
            blk = pltpu.roll(jnp.broadcast_to(krow, (L, L)), 0, axis=1, stride=1, stride_axis=0)
            toep_ref[pl.ds(r0, L), hout * L:(hout + 1) * L] = jnp.where(causal, blk, 0.0).astype(BF16)
        return carry

    lax.fori_loop(0, H, build, 0)

    bsz = u_ref.shape[0]
    u = jnp.concatenate([u_ref[:, h].reshape(bsz * n_chunks, L) for h in range(H)], axis=1)
    st = jnp.dot(u, e_ref[0], preferred_element_type=F32)
    cidx = lax.broadcasted_iota(I32, st.shape, 0) % n_chunks
    p2 = st.shape[1]

    def shifted(xv, d):
        return jnp.where(cidx >= d, pltpu.roll(xv, d, axis=0), 0.0)

    for k in range(n_steps):
        d = 1 << k
        if d >= n_chunks:
            break
        z = shifted(st, d)
        st = st + a_ref[0, k, 0:1, :] * z + a_ref[0, k, 1:2, :] * pltpu.roll(z, p2 // 2, axis=1)
    s_in = shifted(st, 1)
    y = jnp.dot(u, toep_ref[...], preferred_element_type=F32)
    y = (y + jnp.dot(s_in.astype(BF16), f_ref[0], preferred_element_type=F32)).astype(BF16)
    for h in range(H):
        y_ref[:, h] = y[:, h * L:(h + 1) * L].reshape(bsz, n_chunks, L)


def _s5_core(ut5, kern, e_mat, f_mat, a_pow):
    bsz, g, h, n_chunks, chunk = ut5.shape
    hl = h * chunk
    n_steps = a_pow.shape[1]
    p2 = e_mat.shape[2]
    blk = lambda i: (i, 0, 0)
    seq_blk = pl.BlockSpec((bsz, None, h, n_chunks, chunk), lambda i: (0, i, 0, 0, 0))
    return pl.pallas_call(
        functools.partial(_s5_kernel, n_chunks=n_chunks, n_steps=n_steps, chunk=chunk),
        grid=(g,),
        in_specs=[seq_blk,
                  pl.BlockSpec((1,) + kern.shape[1:], blk),
                  pl.BlockSpec((1, hl, p2), blk),
                  pl.BlockSpec((1, p2, hl), blk),
                  pl.BlockSpec((1, n_steps, 2, p2), lambda i: (i, 0, 0, 0))],
        out_specs=seq_blk,
        out_shape=jax.ShapeDtypeStruct(ut5.shape, BF16),
        scratch_shapes=[pltpu.VMEM((hl, hl), BF16)],
        compiler_params=_params("arbitrary"),
        name="s5_core",
    )(ut5, kern, e_mat, f_mat, a_pow)


def _route(scores, biased, tri, base):
    n_tok = scores.shape[1]
    shape3 = (EXPERTS_PER_GROUP, N_EXPERT_GROUPS, n_tok)
    sc3 = scores.reshape(shape3)
    b3 = biased.reshape(shape3)
    j_iota = lax.broadcasted_iota(I32, shape3, 0)
    e_iota = lax.broadcasted_iota(I32, shape3, 1) * EXPERTS_PER_GROUP + j_iota

    def red(fn, x):
        return fn(fn(x, axis=0, keepdims=True), axis=1, keepdims=True)

    m1 = jnp.max(b3, axis=0, keepdims=True)
    i1 = jnp.min(jnp.where(b3 == m1, j_iota, EXPERTS_PER_GROUP), axis=0, keepdims=True)
    m2 = jnp.max(jnp.where(j_iota == i1, NEG_INF, b3), axis=0, keepdims=True)
    gs = m1 + m2

    g_iota = lax.broadcasted_iota(I32, gs.shape, 1)
    gsel = jnp.zeros(gs.shape, F32)
    cur = gs
    for _ in range(TOPK_GROUPS):
        m = jnp.max(cur, axis=1, keepdims=True)
        ig = jnp.min(jnp.where(cur == m, g_iota, N_EXPERT_GROUPS), axis=1, keepdims=True)
        pick = g_iota == ig
        gsel = jnp.where(pick, 1.0, gsel)
        cur = jnp.where(pick, NEG_INF, cur)

    cur = jnp.where(gsel > 0.0, b3, NEG_INF)
    sel = jnp.zeros(shape3, F32)
    ids, vals = [], []
    for _ in range(TOP_K):
        m = red(jnp.max, cur)
        ie = red(jnp.min, jnp.where(cur == m, e_iota, N_EXPERTS))
        pick = e_iota == ie
        ids.append(ie)
        vals.append(red(jnp.sum, jnp.where(pick, sc3, 0.0)))
        sel = jnp.where(pick, 1.0, sel)
        cur = jnp.where(pick, NEG_INF, cur)
    tot = functools.reduce(lambda a, b: a + b, vals)

    sel2 = sel.reshape(N_EXPERTS, n_tok)
    before = jnp.dot(sel2.astype(BF16), tri, preferred_element_type=F32) + base
    before3 = before.reshape(shape3)
    ranks = [red(jnp.sum, jnp.where(e_iota == ie, before3, 0.0)) for ie in ids]

    def rows(parts, dtype):
        parts = [p.reshape(1, n_tok).astype(dtype) for p in parts]
        return jnp.concatenate(parts + [jnp.zeros((8 - len(parts), n_tok), dtype)], axis=0)

    e8 = rows(ids, I32)
    w8 = rows([v / tot * ROUTED_SCALE for v in vals], F32)
    r8 = rows(ranks, I32)
    return e8, w8, r8, jnp.sum(sel2, axis=1, keepdims=True)


def _mix_back_kernel(x_ref, yconv_ref, yt_ref, gate1_ref, shift2_ref, scale2_ref, gate2_ref,
                     wglu_ref, bglu_ref, gssm_ref, woc_ref, wos_ref, gffn_ref,
                     wrh_ref, wrl_ref, rbias_ref, ws13_ref, ws2_ref, tri_ref,
                     x1_ref, h2p_ref, e8_ref, w8_ref, r8_ref, cnt_ref):
    first = jnp.logical_and(pl.program_id(0) == 0, pl.program_id(1) == 0)

    @pl.when(first)
    def _():
        cnt_ref[...] = jnp.zeros_like(cnt_ref)

    sub = tri_ref.shape[0]
    base = cnt_ref[:, 0:1]
    for r0 in range(0, x_ref.shape[1], sub):
        y = yt_ref[0, :, r0:r0 + sub].astype(F32).T
        y = _gelu_tanh(y)
        y = y * jax.nn.sigmoid(jnp.dot(y.astype(BF16), wglu_ref[...], preferred_element_type=F32)
                               + bglu_ref[...])
        y_ssm = _rms(y, gssm_ref[...])
        mix = jnp.dot(yconv_ref[0, r0:r0 + sub, :], woc_ref[...], preferred_element_type=F32)
        mix = mix + jnp.dot(y_ssm.astype(BF16), wos_ref[...], preferred_element_type=F32)
        x1 = x_ref[0, r0:r0 + sub, :] + gate1_ref[0] * mix

        h2 = _rms(x1, gffn_ref[...]) * (1.0 + scale2_ref[0]) + shift2_ref[0]
        h2b = h2.astype(BF16)
        h2p_ref[0, r0:r0 + sub, :] = lax.bitcast_convert_type(_pack_halves(h2), I32)

        a = jnp.dot(h2b, ws13_ref[...], preferred_element_type=F32)
        f = a.shape[1] // 2
        act = _silu(a[:, :f]) * a[:, f:]
        shared = jnp.dot(act.astype(BF16), ws2_ref[...], preferred_element_type=F32)
        x1_ref[0, r0:r0 + sub, :] = x1 + gate2_ref[0] * shared

        h2l = (h2 - h2b.astype(F32)).astype(BF16)
        lt = (jnp.dot(h2b, wrh_ref[...], preferred_element_type=F32)
              + jnp.dot(h2l, wrl_ref[...], preferred_element_type=F32)).T
        scores = jax.nn.sigmoid(lt[:N_EXPERTS] + lt[N_EXPERTS:])
        e8, w8, r8, cnt = _route(scores, scores + rbias_ref[...], tri_ref[...], base)
        e8_ref[:, r0:r0 + sub] = e8
        w8_ref[0, r0:r0 + sub, :] = w8.T
        r8_ref[:, r0:r0 + sub] = r8
        base = base + cnt
    cnt_ref[...] = jnp.broadcast_to(base, cnt_ref.shape)


def _mix_back(x, yconv, yt, gate1, shift2, scale2, gate2, w_glu_bf, b_glu, g_ssm, wo_conv, wo_ssm,
              g_ffn, wr_hi, wr_lo, router_bias, ws13, ws2_bf, ts, b0, bsz):
    _, seq, d = x.shape
    n_tok = bsz * seq
    tiles = seq // ts
    row = lambda b, s: (b + b0, 0, 0)
    const2 = lambda b, s: (0, 0)
    tile_in = lambda b, s: (b + b0, s, 0)
    tile = lambda b, s: (b, s, 0)
    flat = lambda b, s: (0, b * tiles + s)
    full = lambda a: pl.BlockSpec(a.shape, const2)
    sub = min(MIX_SUB, ts)
    tri = jnp.triu(jnp.ones((sub, sub), BF16), k=1)
    args = (w_glu_bf, b_glu.reshape(1, D_SSM), g_ssm.reshape(1, D_SSM), wo_conv, wo_ssm,
            g_ffn.reshape(1, d), wr_hi, wr_lo, router_bias.reshape(N_EXPERTS, 1), ws13, ws2_bf, tri)
    return pl.pallas_call(
        _mix_back_kernel,
        grid=(bsz, tiles),
        in_specs=[pl.BlockSpec((1, ts, d), tile_in),
                  pl.BlockSpec((1, ts, D_CONV), tile_in),
                  pl.BlockSpec((1, D_SSM, ts), lambda b, s: (b + b0, 0, s)),
                  pl.BlockSpec((1, 1, d), row), pl.BlockSpec((1, 1, d), row),
                  pl.BlockSpec((1, 1, d), row), pl.BlockSpec((1, 1, d), row)]
                 + [full(a) for a in args],
        out_specs=[pl.BlockSpec((1, ts, d), tile),
                   pl.BlockSpec((1, ts, d // 2), tile),
                   pl.BlockSpec((8, ts), flat),
                   pl.BlockSpec((1, ts, 8), tile),
                   pl.BlockSpec((8, ts), flat),
                   pl.BlockSpec((N_EXPERTS, 128), const2)],
        out_shape=[jax.ShapeDtypeStruct((bsz, seq, d), F32),
                   jax.ShapeDtypeStruct((bsz, seq, d // 2), I32),
                   jax.ShapeDtypeStruct((8, n_tok), I32),
                   jax.ShapeDtypeStruct((bsz, seq, 8), F32),
                   jax.ShapeDtypeStruct((8, n_tok), I32),
                   jax.ShapeDtypeStruct((N_EXPERTS, 128), F32)],
        compiler_params=_params("arbitrary", "arbitrary"),
        name="mix_back",
    )(x, yconv, yt, gate1, shift2, scale2, gate2, *args)


def _plan_kernel(cnt_ref, e8_ref, r8_ref, pos_ref, *, bm):
    cnt = cnt_ref[...].astype(I32)
    padded = (cnt + (bm - 1)) // bm * bm
    rid = lax.broadcasted_iota(I32, padded.shape, 0)
    incl = padded
    d = 1
    while d < N_EXPERTS:
        incl = incl + jnp.where(rid >= d, pltpu.roll(incl, d, axis=0), 0)
        d *= 2
    pstart = incl - padded
    lanes = pstart.shape[1]

    def chunk(ci, carry):
        c0 = pl.multiple_of(ci * lanes, lanes)
        e = e8_ref[:, pl.ds(c0, lanes)]
        acc = r8_ref[:, pl.ds(c0, lanes)]
        for ex in range(N_EXPERTS):
            acc = acc + jnp.where(e == ex, pstart[ex:ex + 1, :], 0)
        pos_ref[:, pl.ds(c0, lanes)] = acc
        return carry

    lax.fori_loop(0, e8_ref.shape[1] // lanes, chunk, 0)


def _plan(counts, e8, r8, bm):
    n_tok = e8.shape[1]
    tb = min(8192, n_tok)
    return pl.pallas_call(
        functools.partial(_plan_kernel, bm=bm),
        grid=(n_tok // tb,),
        in_specs=[pl.BlockSpec(counts.shape, lambda i: (0, 0)),
                  pl.BlockSpec((8, tb), lambda i: (0, i)),
                  pl.BlockSpec((8, tb), lambda i: (0, i))],
        out_specs=pl.BlockSpec((8, tb), lambda i: (0, i)),
        out_shape=jax.ShapeDtypeStruct((8, n_tok), I32),
        compiler_params=_params("arbitrary"),
        name="plan",
    )(counts, e8, r8)


def _sc_mesh():
    return plsc.VectorSubcoreMesh(core_axis_name="c", subcore_axis_name="s",
                                  num_cores=V7X_SC_CORES, num_subcores=V7X_SC_SUBCORES)


def _sc_worker_base(per_worker):
    return (lax.axis_index("s") * V7X_SC_CORES + lax.axis_index("c")) * per_worker


def _sc_dispatch(h2p, posflat, n_rows):
    n_tok, half = h2p.shape
    win = SC_WINDOW
    per_worker = n_tok // SC_WORKERS
    n_win = per_worker // win

    wins_per_k = n_tok // win

    @functools.partial(
        pl.kernel, mesh=_sc_mesh(),
        out_type=jax.ShapeDtypeStruct((n_rows, half), I32),
        scratch_types=[pltpu.VMEM((TOP_K, n_win, win), I32), pltpu.VMEM((2, win, half), I32),
                       pltpu.SemaphoreType.DMA((2,)), pltpu.SemaphoreType.DMA((2,))],
        name="sc_dispatch")
    def run(h2p_hbm, pos_hbm, xs_hbm, idx_v, rows_v, sem_in, sem_out):
        wid = _sc_worker_base(1)
        for k in range(TOP_K):
            pltpu.sync_copy(pos_hbm.at[pl.ds(k * wins_per_k + wid * n_win, n_win)], idx_v.at[k])

        @pl.loop(0, n_win, step=2)
        def _(j):
            loads = []
            for b in range(2):
                t0 = pl.multiple_of((wid * n_win + j + b) * win, win)
                loads.append(pltpu.async_copy(h2p_hbm.at[pl.ds(t0, win)], rows_v.at[b], sem_in.at[b]))
            stores = []
            for b in range(2):
                loads[b].wait()
                for k in range(TOP_K):
                    stores.append(pltpu.async_copy(rows_v.at[b], xs_hbm.at[idx_v.at[k, j + b]], sem_out.at[b]))
            for cp in stores:
                cp.wait()

    return run(h2p, posflat.reshape(TOP_K * wins_per_k, win))


def _sc_gather(ys, posflat):
    n_idx = posflat.shape[0]
    half = ys.shape[1]
    win = SC_WINDOW
    per_worker = n_idx // SC_WORKERS
    n_win = per_worker // win

    @functools.partial(
        pl.kernel, mesh=_sc_mesh(),
        out_type=jax.ShapeDtypeStruct((n_idx, half), I32),
        scratch_types=[pltpu.VMEM((n_win, win), I32), pltpu.VMEM((2, win, half), I32),
                       pltpu.SemaphoreType.DMA((2,)), pltpu.SemaphoreType.DMA((2,))],
        name="sc_gather")
    def run(ys_hbm, pos_hbm, out_hbm, idx_v, rows_v, sem_in, sem_out):
        wid = _sc_worker_base(1)
        pltpu.sync_copy(pos_hbm.at[pl.ds(wid * n_win, n_win)], idx_v)

        @pl.loop(0, n_win, step=2)
        def _(j):
            gathers = [pltpu.async_copy(ys_hbm.at[idx_v.at[j + b]], rows_v.at[b], sem_in.at[b])
                       for b in range(2)]
            writes = []
            for b in range(2):
                gathers[b].wait()
                off = pl.multiple_of((wid * n_win + j + b) * win, win)
                writes.append(pltpu.async_copy(rows_v.at[b], out_hbm.at[pl.ds(off, win)], sem_out.at[b]))
            for cp in writes:
                cp.wait()

    return run(ys, posflat.reshape(n_idx // win, win))


def _experts_kernel(be_ref, nu_ref, x_ref, w1_ref, w3_ref, w2_ref, y_ref, w13_s, w2_s):
    i = pl.program_id(0)
    changed = jnp.logical_or(i == 0, be_ref[i] != be_ref[jnp.maximum(i - 1, 0)])

    @pl.when(changed)
    def _():
        f = w1_ref.shape[2]
        w13_s[:, :f] = w1_ref[0].astype(BF16)
        w13_s[:, f:] = w3_ref[0].astype(BF16)
        w2_s[...] = w2_ref[0].astype(BF16)

    @pl.when(i < nu_ref[0])
    def _():
        half = x_ref.shape[1]
        sub = min(EXPERT_SUB, x_ref.shape[0])
        for r0 in range(0, x_ref.shape[0], sub):
            lo, hi = _unpack_halves(lax.bitcast_convert_type(x_ref[r0:r0 + sub, :], U32))
            a = jnp.dot(lo.astype(BF16), w13_s[:half, :], preferred_element_type=F32)
            a = a + jnp.dot(hi.astype(BF16), w13_s[half:, :], preferred_element_type=F32)
            f = a.shape[1] // 2
            act = _silu(a[:, :f]) * a[:, f:]
            y = jnp.dot(act.astype(BF16), w2_s[...], preferred_element_type=F32)
            y_ref[r0:r0 + sub, :] = lax.bitcast_convert_type(_pack_halves(y), I32)


def _experts(block_e, n_used, xs, w1, w3, w2, bm):
    n_rows, half = xs.shape
    d, f = w1.shape[1], w1.shape[2]
    rows = lambda i, be, nu: (jnp.minimum(i, nu[0] - 1), 0)
    wblk = lambda i, be, nu: (be[i], 0, 0)
    grid_spec = pltpu.PrefetchScalarGridSpec(
        num_scalar_prefetch=2,
        grid=(n_rows // bm,),
        in_specs=[pl.BlockSpec((bm, half), rows),
                  pl.BlockSpec((1, d, f), wblk),
                  pl.BlockSpec((1, d, f), wblk),
                  pl.BlockSpec((1, f, d), wblk)],
        out_specs=pl.BlockSpec((bm, half), rows),
        scratch_shapes=[pltpu.VMEM((d, 2 * f), BF16), pltpu.VMEM((f, d), BF16)],
    )
    return pl.pallas_call(
        _experts_kernel,
        grid_spec=grid_spec,
        out_shape=jax.ShapeDtypeStruct((n_rows, half), I32),
        compiler_params=_params("arbitrary"),
        name="experts",
    )(block_e, n_used, xs, w1, w3, w2)


def _combine_kernel(g_ref, w8_ref, x1_ref, gate2_ref, gfin_ref, *rest):
    o_ref = rest[-1]
    w8 = w8_ref[...]
    acc_lo = acc_hi = None
    for k in range(TOP_K):
        lo, hi = _unpack_halves(lax.bitcast_convert_type(g_ref[k], U32))
        wk = w8[:, k:k + 1]
        acc_lo = wk * lo if acc_lo is None else acc_lo + wk * lo
        acc_hi = wk * hi if acc_hi is None else acc_hi + wk * hi
    routed = jnp.concatenate([acc_lo, acc_hi], axis=1)
    o_ref[...] = _rms(x1_ref[...] + gate2_ref[0] * routed, gfin_ref[...])


def _combine(g6, w8, x1, gate2, g_final, tt, tiles_per_seq, b0, n_tok_all, out_prev):
    n_tok, d = x1.shape
    half = g6.shape[2]
    tok = lambda i: (i, 0)
    tile0 = b0 * tiles_per_seq
    in_specs = [pl.BlockSpec((TOP_K, tt, half), lambda i: (0, i, 0)),
                pl.BlockSpec((tt, 8), tok),
                pl.BlockSpec((tt, d), tok),
                pl.BlockSpec((1, 1, d), lambda i: (i // tiles_per_seq + b0, 0, 0)),
                pl.BlockSpec((1, d), lambda i: (0, 0))]
    args = [g6, w8, x1, gate2, g_final.reshape(1, d)]
    aliases = {}
    if out_prev is not None:
        in_specs.append(pl.BlockSpec(memory_space=pl.ANY))
        args.append(out_prev)
        aliases = {len(args) - 1: 0}
    return pl.pallas_call(
        _combine_kernel,
        grid=(n_tok // tt,),
        in_specs=in_specs,
        out_specs=pl.BlockSpec((tt, d), lambda i: (i + tile0, 0)),
        out_shape=jax.ShapeDtypeStruct((n_tok_all, d), F32),
        input_output_aliases=aliases,
        compiler_params=_params("arbitrary"),
        name="combine",
    )(*args)


def _block_experts(counts, bm, n_blocks):
    padded = (counts.astype(I32) + (bm - 1)) // bm * bm
    pend = jnp.cumsum(padded)
    starts = jnp.arange(n_blocks, dtype=I32) * bm
    block_e = jnp.minimum(jnp.sum((pend[None, :] <= starts[:, None]).astype(I32), axis=1), N_EXPERTS - 1)
    return block_e, (pend[-1:] // bm).astype(I32)


def kernel(x, c, w_ada, b_ada, g_mix, w_in, conv_w, conv_b, g_conv, lam_re, lam_im, log_dt, b_re, b_im,
           c_re, c_im, d_skip, w_glu, b_glu, g_ssm, w_out, g_ffn, w_router, router_bias, w1, w3, w2,
           ws1, ws3, ws2, g_final):
    bsz, seq, d = x.shape
    n_tok = bsz * seq
    ts = min(512, seq)
    L = min(S5_CHUNK, seq)
    n_chunks = seq // L
    G, H = SSM_GROUPS, SSM_GROUP_CH

    mod = _adaln(c, w_ada, b_ada).reshape(bsz, 6, 1, d)
    shift1, scale1, gate1, shift2, scale2, gate2 = (mod[:, i] for i in range(6))

    yconv, ut = _mix_front(x, shift1, scale1, g_mix, w_in.astype(BF16), conv_w, conv_b, g_conv,
                           min(MIX_FRONT_TOKENS, seq))

    kern, e_mat, f_mat, a_pow = _s5_tables(lam_re, lam_im, log_dt, b_re, b_im, c_re, c_im, d_skip,
                                           L, n_chunks)
    yt = _s5_core(ut.reshape(bsz, G, H, n_chunks, L), kern, e_mat, f_mat, a_pow).reshape(bsz, G * H, seq)

    NG, NJ = N_EXPERT_GROUPS, EXPERTS_PER_GROUP
    w_r = w_router.astype(F32).reshape(d, NG, NJ).transpose(0, 2, 1).reshape(d, N_EXPERTS)
    r_bias = router_bias.reshape(NG, NJ).T.reshape(N_EXPERTS)
    w_r_hi = w_r.astype(BF16)
    wr_hi = jnp.concatenate([w_r_hi, (w_r - w_r_hi.astype(F32)).astype(BF16)], axis=1)
    wr_lo = jnp.concatenate([w_r_hi, jnp.zeros_like(w_r_hi)], axis=1)
    w_out_bf = w_out.astype(BF16)
    w_glu_bf = w_glu.astype(BF16)
    ws13 = jnp.concatenate([ws1, ws3], axis=1).astype(BF16)
    ws2_bf = ws2.astype(BF16)
    bm = EXPERT_ROWS
    tt = min(COMBINE_TOKENS, seq)

    n_parts = MOE_PARTS if bsz % MOE_PARTS == 0 else 1
    pb = bsz // n_parts
    pt = pb * seq
    n_blocks = -(-(pt * TOP_K + N_EXPERTS * (bm - 1)) // bm)
    parts = []
    for p in range(n_parts):
        x1, h2p, e8, w8, r8, counts = _mix_back(
            x, yconv, yt, gate1, shift2, scale2, gate2, w_glu_bf, b_glu, g_ssm,
            w_out_bf[:D_CONV], w_out_bf[D_CONV:], g_ffn, wr_hi, wr_lo, r_bias, ws13, ws2_bf,
            ts, p * pb, pb)
        counts = counts.reshape(NJ, NG, -1).transpose(1, 0, 2).reshape(N_EXPERTS, -1)
        pos8 = _plan(counts, e8, r8, bm)
        posflat = pos8[:TOP_K].reshape(TOP_K * pt)
        block_e = _block_experts(counts[:, 0], bm, n_blocks)
        xs = _sc_dispatch(h2p.reshape(pt, d // 2), posflat, n_blocks * bm)
        parts.append((x1, w8, posflat, block_e, xs))
    ys = [_experts(*block_e, xs, w1, w3, w2, bm) for (_, _, _, block_e, xs) in parts]
    g6 = [_sc_gather(y, posflat).reshape(TOP_K, pt, d // 2) for y, (_, _, posflat, _, _) in zip(ys, parts)]
    out = None
    for p, (g, (x1, w8, _, _, _)) in enumerate(zip(g6, parts)):
        out = _combine(g, w8.reshape(pt, 8), x1.reshape(pt, d), gate2, g_final, tt, seq // tt,
                       p * pb, n_tok, out)
    return out.reshape(bsz, seq, d)
```

```python
import functools
import math

import jax
import jax.numpy as jnp
import numpy as np
from jax import lax
from jax.experimental import pallas as pl
from jax.experimental.pallas import tpu as pltpu
from jax.experimental.pallas import tpu_sc as plsc

F32 = jnp.float32
BF16 = jnp.bfloat16
U32 = jnp.uint32
I32 = jnp.int32

D_CONV = 768
D_SSM = 256
SSM_GROUPS = 16
SSM_GROUP_CH = 16
SSM_STATE = 64
N_EXPERTS = 64
TOP_K = 6
N_EXPERT_GROUPS = 8
TOPK_GROUPS = 4
EXPERTS_PER_GROUP = 8
ROUTED_SCALE = 2.5
RMS_EPS = 1e-6

S5_CHUNK = 128
EXPERT_ROWS = 1024
EXPERT_SUB = 512
MIX_FRONT_TOKENS = 1024
MIX_FRONT_SUB = 1024
MIX_SUB = 256
MIX_BACK_TOKENS = 1024
MIX_PHASE_LAG = 3
COMBINE_TOKENS = 512
MOE_PARTS = 2
V7X_SC_CORES = 2
V7X_SC_SUBCORES = 16
SC_WORKERS = V7X_SC_CORES * V7X_SC_SUBCORES
SC_WINDOW = 64
V7X_VMEM_LIMIT = 56 * 1024 * 1024
NEG_INF = float("-inf")
HIGH_HALF = np.uint32(0xFFFF0000)


def _rms(x, g):
    return x * lax.rsqrt(jnp.mean(x * x, axis=-1, keepdims=True) + RMS_EPS) * g


def _gelu_tanh(x):
    return 0.5 * x * (1.0 + jnp.tanh(math.sqrt(2.0 / math.pi) * (x + 0.044715 * (x * x * x))))


def _silu(x):
    return x * jax.nn.sigmoid(x)


def _params(*sem):
    return pltpu.CompilerParams(dimension_semantics=sem, vmem_limit_bytes=V7X_VMEM_LIMIT)


def _pack_halves(y):
    return _pack_rounded(y.astype(BF16).astype(F32))


def _pack_rounded(y):
    m = y.shape[1] // 2
    bits = lax.bitcast_convert_type(y, U32)
    return (bits[:, m:] & HIGH_HALF) | (bits[:, :m] >> 16)


def _unpack_halves(w):
    lo = lax.bitcast_convert_type(w << 16, F32)
    hi = lax.bitcast_convert_type(w & HIGH_HALF, F32)
    return lo, hi


def _adaln_kernel(c_ref, w_ref, b_ref, o_ref):
    o_ref[...] = jnp.dot(_silu(c_ref[...]), w_ref[...], preferred_element_type=F32,
                         precision=lax.Precision.HIGHEST) + b_ref[...]


def _adaln(c, w_ada, b_ada):
    bsz, d = c.shape
    n = w_ada.shape[1]
    bn = 1024
    return pl.pallas_call(
        _adaln_kernel,
        grid=(n // bn,),
        in_specs=[pl.BlockSpec((bsz, d), lambda j: (0, 0)),
                  pl.BlockSpec((d, bn), lambda j: (0, j)),
                  pl.BlockSpec((1, bn), lambda j: (0, j))],
        out_specs=pl.BlockSpec((bsz, bn), lambda j: (0, j)),
        out_shape=jax.ShapeDtypeStruct((bsz, n), F32),
        compiler_params=_params("arbitrary"),
        name="adaln",
    )(c, w_ada, b_ada.reshape(1, n))


def _mix_front_kernel(x_ref, shift_ref, scale_ref, gmix_ref, win_ref, cw_ref, cb_ref, gconv_ref,
                      yconv_ref, ut_ref, zprev_ref):
    s = pl.program_id(1)

    @pl.when(s == 0)
    def _():
        zprev_ref[...] = jnp.zeros_like(zprev_ref)

    cw = cw_ref[...]
    sub = min(MIX_FRONT_SUB, x_ref.shape[1])
    carry = {"prev": zprev_ref[...]}

    def norm(st):
        x = x_ref[0, st["r0"]:st["r0"] + sub, :]
        st["h"] = (_rms(x, gmix_ref[...]) * (1.0 + scale_ref[0]) + shift_ref[0]).astype(BF16)

    def in_proj(st):
        st["proj"] = jnp.dot(st.pop("h"), win_ref[...], preferred_element_type=F32)

    def mixers(st):
        r0 = st["r0"]
        proj = st.pop("proj")
        b_gate = proj[:, :D_CONV]
        c_gate = proj[:, D_CONV:2 * D_CONV]
        v = proj[:, 2 * D_CONV:3 * D_CONV]
        u = proj[:, 3 * D_CONV:]
        z = c_gate * v
        prev = carry["prev"]
        rid = lax.broadcasted_iota(I32, z.shape, 0)
        z1 = jnp.where(rid == 0, prev[7:8, :], pltpu.roll(z, 1, axis=0))
        z2 = jnp.where(rid == 0, prev[6:7, :], jnp.where(rid == 1, prev[7:8, :], pltpu.roll(z, 2, axis=0)))
        carry["prev"] = z[sub - 8:, :]
        conv = cw[0:1, :] * z2 + cw[1:2, :] * z1 + cw[2:3, :] * z + cb_ref[...]
        yconv_ref[0, r0:r0 + sub, :] = _rms(b_gate * conv, gconv_ref[...]).astype(BF16)
        ut_ref[0, :, r0:r0 + sub] = u.T.astype(BF16)

    phases = (norm, in_proj, mixers)
    chains = [{"r0": r0} for r0 in range(0, x_ref.shape[1], sub)]
    for step in range(len(phases) + len(chains) - 1):
        for ci, st in enumerate(chains):
            if 0 <= step - ci < len(phases):
                phases[step - ci](st)
    zprev_ref[...] = carry["prev"]


def _mix_front(x, shift1, scale1, g_mix, w_in_bf, conv_w, conv_b, g_conv, ts):
    bsz, seq, d = x.shape
    d_in = w_in_bf.shape[1]
    row = lambda b, s: (b, 0, 0)
    const2 = lambda b, s: (0, 0)
    return pl.pallas_call(
        _mix_front_kernel,
        grid=(bsz, seq // ts),
        in_specs=[pl.BlockSpec((1, ts, d), lambda b, s: (b, s, 0)),
                  pl.BlockSpec((1, 1, d), row),
                  pl.BlockSpec((1, 1, d), row),
                  pl.BlockSpec((1, d), const2),
                  pl.BlockSpec((d, d_in), const2),
                  pl.BlockSpec((8, D_CONV), const2),
                  pl.BlockSpec((1, D_CONV), const2),
                  pl.BlockSpec((1, D_CONV), const2)],
        out_specs=[pl.BlockSpec((1, ts, D_CONV), lambda b, s: (b, s, 0)),
                   pl.BlockSpec((1, D_SSM, ts), lambda b, s: (b, 0, s))],
        out_shape=[jax.ShapeDtypeStruct((bsz, seq, D_CONV), BF16),
                   jax.ShapeDtypeStruct((bsz, D_SSM, seq), BF16)],
        scratch_shapes=[pltpu.VMEM((8, D_CONV), F32)],
        compiler_params=_params("arbitrary", "arbitrary"),
        name="mix_front",
    )(x, shift1, scale1, g_mix.reshape(1, d), w_in_bf,
      jnp.pad(conv_w, ((0, 8 - conv_w.shape[0]), (0, 0))), conv_b.reshape(1, D_CONV),
      g_conv.reshape(1, D_CONV))


def _s5_tables(lam_re, lam_im, log_dt, b_re, b_im, c_re, c_im, d_skip, chunk, n_chunks):
    hp = lax.Precision.HIGHEST
    G, H, P, L = SSM_GROUPS, SSM_GROUP_CH, SSM_STATE, chunk
    lr = lam_re.astype(F32)
    li = lam_im.astype(F32)
    dt = jnp.exp(log_dt.astype(F32))[:, None]
    mag = jnp.exp(lr * dt)
    ang = li * dt
    ab_re = mag * jnp.cos(ang)
    ab_im = mag * jnp.sin(ang)
    den = lr * lr + li * li
    nr = ab_re - 1.0
    ni = ab_im
    q_re = (nr * lr + ni * li) / den
    q_im = (ni * lr - nr * li) / den
    br = b_re.astype(F32)
    bi = b_im.astype(F32)
    bb_re = q_re[..., None] * br - q_im[..., None] * bi
    bb_im = q_re[..., None] * bi + q_im[..., None] * br

    def a_power(tau):
        t = tau.astype(F32)[None, :, None]
        m = jnp.exp(t * (lr * dt)[:, None, :])
        th = t * ang[:, None, :]
        return m * jnp.cos(th), m * jnp.sin(th)

    pw_re, pw_im = a_power(jnp.arange(L + 1))
    cr = c_re.astype(F32)[:, None]
    ci = c_im.astype(F32)[:, None]
    cp_re = cr * pw_re[:, :, None, :] - ci * pw_im[:, :, None, :]
    cp_im = cr * pw_im[:, :, None, :] + ci * pw_re[:, :, None, :]
    kern = (jnp.einsum('gthp,gpk->gthk', cp_re[:, :L], bb_re, precision=hp)
            - jnp.einsum('gthp,gpk->gthk', cp_im[:, :L], bb_im, precision=hp))
    kern = kern.at[:, 0].add(d_skip.astype(F32).reshape(G, H)[:, :, None] * jnp.eye(H, dtype=F32))
    kern = kern.transpose(0, 3, 2, 1).reshape(G, H * H, L)

    rev_re = pw_re[:, L - 1::-1]
    rev_im = pw_im[:, L - 1::-1]
    bt_re = bb_re.transpose(0, 2, 1)[:, :, None, :]
    bt_im = bb_im.transpose(0, 2, 1)[:, :, None, :]
    e_re = rev_re[:, None] * bt_re - rev_im[:, None] * bt_im
    e_im = rev_re[:, None] * bt_im + rev_im[:, None] * bt_re
    e_mat = jnp.concatenate([e_re, e_im], axis=-1).reshape(G, H * L, 2 * P)

    f_re = cp_re[:, 1:].transpose(0, 3, 2, 1)
    f_im = -cp_im[:, 1:].transpose(0, 3, 2, 1)
    f_mat = jnp.concatenate([f_re, f_im], axis=1).reshape(G, 2 * P, H * L)

    n_steps = max(1, (n_chunks - 1).bit_length())
    sr, si = a_power(L * (2 ** jnp.arange(n_steps)))
    a_pow = jnp.stack([jnp.concatenate([sr, sr], axis=-1),
                       jnp.concatenate([-si, si], axis=-1)], axis=2)
    return kern, e_mat.astype(BF16), f_mat.astype(BF16), a_pow


def _s5_kernel(u_ref, k_ref, e_ref, f_ref, a_ref, y_ref, toep_ref, *, n_chunks, n_steps, chunk):
    L, H = chunk, SSM_GROUP_CH

    causal = lax.broadcasted_iota(I32, (L, L), 1) >= lax.broadcasted_iota(I32, (L, L), 0)

    def build(hin, carry):
        r0 = pl.multiple_of(hin * L, L)
        for hout in range(H):
            krow = k_ref[0, pl.ds(hin * H + hout, 1), :]
            blk = pltpu.roll(jnp.broadcast_to(krow, (L, L)), 0, axis=1, stride=1, stride_axis=0)
            toep_ref[pl.ds(r0, L), hout * L:(hout + 1) * L] = jnp.where(causal, blk, 0.0).astype(BF16)
        return carry

    lax.fori_loop(0, H, build, 0)

    bsz = u_ref.shape[0]
    u = jnp.concatenate([u_ref[:, h].reshape(bsz * n_chunks, L) for h in range(H)], axis=1)
    st = jnp.dot(u, e_ref[0], preferred_element_type=F32)
    cidx = lax.broadcasted_iota(I32, st.shape, 0) % n_chunks
    p2 = st.shape[1]

    def shifted(xv, d):
        return jnp.where(cidx >= d, pltpu.roll(xv, d, axis=0), 0.0)

    for k in range(n_steps):
        d = 1 << k
        if d >= n_chunks:
            break
        z = shifted(st, d)
        st = st + a_ref[0, k, 0:1, :] * z + a_ref[0, k, 1:2, :] * pltpu.roll(z, p2 // 2, axis=1)
    s_in = shifted(st, 1)
    y = jnp.dot(u, toep_ref[...], preferred_element_type=F32)
    y = (y + jnp.dot(s_in.astype(BF16), f_ref[0], preferred_element_type=F32)).astype(BF16)
    for h in range(H):
        y_ref[:, h] = y[:, h * L:(h + 1) * L].reshape(bsz, n_chunks, L)


def _s5_core(ut5, kern, e_mat, f_mat, a_pow):
    bsz, g, h, n_chunks, chunk = ut5.shape
    hl = h * chunk
    n_steps = a_pow.shape[1]
    p2 = e_mat.shape[2]
    blk = lambda i: (i, 0, 0)
    seq_blk = pl.BlockSpec((bsz, None, h, n_chunks, chunk), lambda i: (0, i, 0, 0, 0))
    return pl.pallas_call(
        functools.partial(_s5_kernel, n_chunks=n_chunks, n_steps=n_steps, chunk=chunk),
        grid=(g,),
        in_specs=[seq_blk,
                  pl.BlockSpec((1,) + kern.shape[1:], blk),
                  pl.BlockSpec((1, hl, p2), blk),
                  pl.BlockSpec((1, p2, hl), blk),
                  pl.BlockSpec((1, n_steps, 2, p2), lambda i: (i, 0, 0, 0))],
        out_specs=seq_blk,
        out_shape=jax.ShapeDtypeStruct(ut5.shape, BF16),
        scratch_shapes=[pltpu.VMEM((hl, hl), BF16)],
        compiler_params=_params("arbitrary"),
        name="s5_core",
    )(ut5, kern, e_mat, f_mat, a_pow)


def _route(scores, biased, tri, base):
    n_tok = scores.shape[1]
    shape3 = (EXPERTS_PER_GROUP, N_EXPERT_GROUPS, n_tok)
    sc3 = scores.reshape(shape3)
    b3 = biased.reshape(shape3)
    j_iota = lax.broadcasted_iota(I32, shape3, 0)
    e_iota = lax.broadcasted_iota(I32, shape3, 1) * EXPERTS_PER_GROUP + j_iota

    def red(fn, x):
        return fn(fn(x, axis=0, keepdims=True), axis=1, keepdims=True)

    m1 = jnp.max(b3, axis=0, keepdims=True)
    i1 = jnp.min(jnp.where(b3 == m1, j_iota, EXPERTS_PER_GROUP), axis=0, keepdims=True)
    m2 = jnp.max(jnp.where(j_iota == i1, NEG_INF, b3), axis=0, keepdims=True)
    gs = m1 + m2

    g_iota = lax.broadcasted_iota(I32, gs.shape, 1)
    gsel = jnp.zeros(gs.shape, F32)
    cur = gs
    for _ in range(TOPK_GROUPS):
        m = jnp.max(cur, axis=1, keepdims=True)
        ig = jnp.min(jnp.where(cur == m, g_iota, N_EXPERT_GROUPS), axis=1, keepdims=True)
        pick = g_iota == ig
        gsel = jnp.where(pick, 1.0, gsel)
        cur = jnp.where(pick, NEG_INF, cur)

    cur = jnp.where(gsel > 0.0, b3, NEG_INF)
    sel = jnp.zeros(shape3, F32)
    ids, vals = [], []
    for _ in range(TOP_K):
        m = red(jnp.max, cur)
        ie = red(jnp.min, jnp.where(cur == m, e_iota, N_EXPERTS))
        pick = e_iota == ie
        ids.append(ie)
        vals.append(red(jnp.sum, jnp.where(pick, sc3, 0.0)))
        sel = jnp.where(pick, 1.0, sel)
        cur = jnp.where(pick, NEG_INF, cur)
    tot = functools.reduce(lambda a, b: a + b, vals)

    sel2 = sel.reshape(N_EXPERTS, n_tok)
    before = jnp.dot(sel2.astype(BF16), tri, preferred_element_type=F32) + base
    before3 = before.reshape(shape3)
    ranks = [red(jnp.sum, jnp.where(e_iota == ie, before3, 0.0)) for ie in ids]

    def rows(parts, dtype):
        parts = [p.reshape(1, n_tok).astype(dtype) for p in parts]
        return jnp.concatenate(parts + [jnp.zeros((8 - len(parts), n_tok), dtype)], axis=0)

    e8 = rows(ids, I32)
    w8 = rows([v / tot * ROUTED_SCALE for v in vals], F32)
    r8 = rows(ranks, I32)
    return e8, w8, r8, jnp.sum(sel2, axis=1, keepdims=True)


def _mix_back_kernel(x_ref, yconv_ref, yt_ref, gate1_ref, shift2_ref, scale2_ref, gate2_ref,
                     wglu_ref, bglu_ref, gssm_ref, woc_ref, wos_ref, gffn_ref,
                     wrh_ref, wrl_ref, rbias_ref, ws13_ref, ws2_ref, tri_ref,
                     x1_ref, h2p_ref, e8_ref, w8_ref, r8_ref, cnt_ref):
    first = jnp.logical_and(pl.program_id(0) == 0, pl.program_id(1) == 0)

    @pl.when(first)
    def _():
        cnt_ref[...] = jnp.zeros_like(cnt_ref)

    sub = tri_ref.shape[0]

    def ssm_post(st):
        r0 = st["r0"]
        y = yt_ref[0, :, r0:r0 + sub].astype(F32).T
        y = _gelu_tanh(y)
        y = y * jax.nn.sigmoid(jnp.dot(y.astype(BF16), wglu_ref[...], preferred_element_type=F32)
                               + bglu_ref[...])
        st["y_ssm"] = _rms(y, gssm_ref[...]).astype(BF16)

    def out_proj(st):
        r0 = st["r0"]
        mix = jnp.dot(yconv_ref[0, r0:r0 + sub, :], woc_ref[...], preferred_element_type=F32)
        mix = mix + jnp.dot(st.pop("y_ssm"), wos_ref[...], preferred_element_type=F32)
        st["x1"] = x_ref[0, r0:r0 + sub, :] + gate1_ref[0] * mix

    def ffn_in(st):
        r0 = st["r0"]
        h2 = _rms(st["x1"], gffn_ref[...] * (1.0 + scale2_ref[0])) + shift2_ref[0]
        h2b = h2.astype(BF16)
        h2r = h2b.astype(F32)
        h2p_ref[0, r0:r0 + sub, :] = lax.bitcast_convert_type(_pack_rounded(h2r), I32)
        st["h2b"] = h2b
        st["h2l"] = (h2 - h2r).astype(BF16)

    def shared_expert(st):
        r0 = st["r0"]
        a = jnp.dot(st["h2b"], ws13_ref[...], preferred_element_type=F32)
        f = a.shape[1] // 2
        act = _silu(a[:, :f]) * a[:, f:]
        shared = jnp.dot(act.astype(BF16), ws2_ref[...], preferred_element_type=F32)
        x1_ref[0, r0:r0 + sub, :] = st.pop("x1") + gate2_ref[0] * shared

    def router(st):
        lt = (jnp.dot(st.pop("h2b"), wrh_ref[...], preferred_element_type=F32)
              + jnp.dot(st.pop("h2l"), wrl_ref[...], preferred_element_type=F32)).T
        st["scores"] = jax.nn.sigmoid(lt[:N_EXPERTS] + lt[N_EXPERTS:])

    def route(st):
        r0 = st["r0"]
        scores = st.pop("scores")
        e8, w8, r8, cnt = _route(scores, scores + rbias_ref[...], tri_ref[...], carry["base"])
        e8_ref[:, r0:r0 + sub] = e8
        w8_ref[0, r0:r0 + sub, :] = w8.T
        r8_ref[:, r0:r0 + sub] = r8
        carry["base"] = carry["base"] + cnt

    carry = {"base": cnt_ref[:, 0:1]}
    phases = (ssm_post, out_proj, ffn_in, shared_expert, router, route)
    chains = [{"r0": r0} for r0 in range(0, x_ref.shape[1], sub)]
    lag = MIX_PHASE_LAG
    for step in range(len(phases) + lag * (len(chains) - 1)):
        for ci, st in enumerate(chains):
            ph = step - lag * ci
            if 0 <= ph < len(phases):
                phases[ph](st)
    cnt_ref[...] = jnp.broadcast_to(carry["base"], cnt_ref.shape)


def _mix_back(x, yconv, yt, gate1, shift2, scale2, gate2, w_glu_bf, b_glu, g_ssm, wo_conv, wo_ssm,
              g_ffn, wr_hi, wr_lo, router_bias, ws13, ws2_bf, ts, b0, bsz):
    _, seq, d = x.shape
    n_tok = bsz * seq
    tiles = seq // ts
    row = lambda b, s: (b + b0, 0, 0)
    const2 = lambda b, s: (0, 0)
    tile_in = lambda b, s: (b + b0, s, 0)
    tile = lambda b, s: (b, s, 0)
    flat = lambda b, s: (0, b * tiles + s)
    full = lambda a: pl.BlockSpec(a.shape, const2)
    sub = min(MIX_SUB, ts)
    tri = jnp.triu(jnp.ones((sub, sub), BF16), k=1)
    args = (w_glu_bf, b_glu.reshape(1, D_SSM), g_ssm.reshape(1, D_SSM), wo_conv, wo_ssm,
            g_ffn.reshape(1, d), wr_hi, wr_lo, router_bias.reshape(N_EXPERTS, 1), ws13, ws2_bf, tri)
    return pl.pallas_call(
        _mix_back_kernel,
        grid=(bsz, tiles),
        in_specs=[pl.BlockSpec((1, ts, d), tile_in),
                  pl.BlockSpec((1, ts, D_CONV), tile_in),
                  pl.BlockSpec((1, D_SSM, ts), lambda b, s: (b + b0, 0, s)),
                  pl.BlockSpec((1, 1, d), row), pl.BlockSpec((1, 1, d), row),
                  pl.BlockSpec((1, 1, d), row), pl.BlockSpec((1, 1, d), row)]
                 + [full(a) for a in args],
        out_specs=[pl.BlockSpec((1, ts, d), tile),
                   pl.BlockSpec((1, ts, d // 2), tile),
                   pl.BlockSpec((8, ts), flat),
                   pl.BlockSpec((1, ts, 8), tile),
                   pl.BlockSpec((8, ts), flat),
                   pl.BlockSpec((N_EXPERTS, 128), const2)],
        out_shape=[jax.ShapeDtypeStruct((bsz, seq, d), F32),
                   jax.ShapeDtypeStruct((bsz, seq, d // 2), I32),
                   jax.ShapeDtypeStruct((8, n_tok), I32),
                   jax.ShapeDtypeStruct((bsz, seq, 8), F32),
                   jax.ShapeDtypeStruct((8, n_tok), I32),
                   jax.ShapeDtypeStruct((N_EXPERTS, 128), F32)],
        compiler_params=_params("arbitrary", "arbitrary"),
        name="mix_back",
    )(x, yconv, yt, gate1, shift2, scale2, gate2, *args)


def _plan_kernel(cnt_ref, e8_ref, r8_ref, pos_ref, *, bm):
    cnt = cnt_ref[...].astype(I32)
    padded = (cnt + (bm - 1)) // bm * bm
    rid = lax.broadcasted_iota(I32, padded.shape, 0)
    incl = padded
    d = 1
    while d < N_EXPERTS:
        incl = incl + jnp.where(rid >= d, pltpu.roll(incl, d, axis=0), 0)
        d *= 2
    pstart = incl - padded
    lanes = pstart.shape[1]

    def chunk(ci, carry):
        c0 = pl.multiple_of(ci * lanes, lanes)
        e = e8_ref[:, pl.ds(c0, lanes)]
        acc = r8_ref[:, pl.ds(c0, lanes)]
        for ex in range(N_EXPERTS):
            acc = acc + jnp.where(e == ex, pstart[ex:ex + 1, :], 0)
        pos_ref[:, pl.ds(c0, lanes)] = acc
        return carry

    lax.fori_loop(0, e8_ref.shape[1] // lanes, chunk, 0)


def _plan(counts, e8, r8, bm):
    n_tok = e8.shape[1]
    tb = min(8192, n_tok)
    return pl.pallas_call(
        functools.partial(_plan_kernel, bm=bm),
        grid=(n_tok // tb,),
        in_specs=[pl.BlockSpec(counts.shape, lambda i: (0, 0)),
                  pl.BlockSpec((8, tb), lambda i: (0, i)),
                  pl.BlockSpec((8, tb), lambda i: (0, i))],
        out_specs=pl.BlockSpec((8, tb), lambda i: (0, i)),
        out_shape=jax.ShapeDtypeStruct((8, n_tok), I32),
        compiler_params=_params("arbitrary"),
        name="plan",
    )(counts, e8, r8)


def _sc_mesh():
    return plsc.VectorSubcoreMesh(core_axis_name="c", subcore_axis_name="s",
                                  num_cores=V7X_SC_CORES, num_subcores=V7X_SC_SUBCORES)


def _sc_worker_base(per_worker):
    return (lax.axis_index("s") * V7X_SC_CORES + lax.axis_index("c")) * per_worker


def _sc_dispatch(h2p, posflat, n_rows):
    n_tok, half = h2p.shape
    win = SC_WINDOW
    per_worker = n_tok // SC_WORKERS
    n_win = per_worker // win

    wins_per_k = n_tok // win

    @functools.partial(
        pl.kernel, mesh=_sc_mesh(),
        out_type=jax.ShapeDtypeStruct((n_rows, half), I32),
        scratch_types=[pltpu.VMEM((TOP_K, n_win, win), I32), pltpu.VMEM((2, win, half), I32),
                       pltpu.SemaphoreType.DMA((2,)), pltpu.SemaphoreType.DMA((2,))],
        name="sc_dispatch")
    def run(h2p_hbm, pos_hbm, xs_hbm, idx_v, rows_v, sem_in, sem_out):
        wid = _sc_worker_base(1)
        for k in range(TOP_K):
            pltpu.sync_copy(pos_hbm.at[pl.ds(k * wins_per_k + wid * n_win, n_win)], idx_v.at[k])

        @pl.loop(0, n_win, step=2)
        def _(j):
            loads = []
            for b in range(2):
                t0 = pl.multiple_of((wid * n_win + j + b) * win, win)
                loads.append(pltpu.async_copy(h2p_hbm.at[pl.ds(t0, win)], rows_v.at[b], sem_in.at[b]))
            stores = []
            for b in range(2):
                loads[b].wait()
                for k in range(TOP_K):
                    stores.append(pltpu.async_copy(rows_v.at[b], xs_hbm.at[idx_v.at[k, j + b]], sem_out.at[b]))
            for cp in stores:
                cp.wait()

    return run(h2p, posflat.reshape(TOP_K * wins_per_k, win))


def _sc_gather(ys, posflat):
    n_idx = posflat.shape[0]
    half = ys.shape[1]
    win = SC_WINDOW
    per_worker = n_idx // SC_WORKERS
    n_win = per_worker // win

    @functools.partial(
        pl.kernel, mesh=_sc_mesh(),
        out_type=jax.ShapeDtypeStruct((n_idx, half), I32),
        scratch_types=[pltpu.VMEM((n_win, win), I32), pltpu.VMEM((2, win, half), I32),
                       pltpu.SemaphoreType.DMA((2,)), pltpu.SemaphoreType.DMA((2,))],
        name="sc_gather")
    def run(ys_hbm, pos_hbm, out_hbm, idx_v, rows_v, sem_in, sem_out):
        wid = _sc_worker_base(1)
        pltpu.sync_copy(pos_hbm.at[pl.ds(wid * n_win, n_win)], idx_v)

        @pl.loop(0, n_win, step=2)
        def _(j):
            gathers = [pltpu.async_copy(ys_hbm.at[idx_v.at[j + b]], rows_v.at[b], sem_in.at[b])
                       for b in range(2)]
            writes = []
            for b in range(2):
                gathers[b].wait()
                off = pl.multiple_of((wid * n_win + j + b) * win, win)
                writes.append(pltpu.async_copy(rows_v.at[b], out_hbm.at[pl.ds(off, win)], sem_out.at[b]))
            for cp in writes:
                cp.wait()

    return run(ys, posflat.reshape(n_idx // win, win))


def _experts_kernel(be_ref, nu_ref, x_ref, w1_ref, w3_ref, w2_ref, y_ref, w13_s, w2_s):
    i = pl.program_id(0)
    changed = jnp.logical_or(i == 0, be_ref[i] != be_ref[jnp.maximum(i - 1, 0)])

    @pl.when(changed)
    def _():
        f = w1_ref.shape[2]
        w13_s[:, :f] = w1_ref[0].astype(BF16)
        w13_s[:, f:] = w3_ref[0].astype(BF16)
        w2_s[...] = w2_ref[0].astype(BF16)

    @pl.when(i < nu_ref[0])
    def _():
        half = x_ref.shape[1]
        sub = min(EXPERT_SUB, x_ref.shape[0])
        for r0 in range(0, x_ref.shape[0], sub):
            lo, hi = _unpack_halves(lax.bitcast_convert_type(x_ref[r0:r0 + sub, :], U32))
            a = jnp.dot(lo.astype(BF16), w13_s[:half, :], preferred_element_type=F32)
            a = a + jnp.dot(hi.astype(BF16), w13_s[half:, :], preferred_element_type=F32)
            f = a.shape[1] // 2
            act = _silu(a[:, :f]) * a[:, f:]
            y = jnp.dot(act.astype(BF16), w2_s[...], preferred_element_type=F32)
            y_ref[r0:r0 + sub, :] = lax.bitcast_convert_type(_pack_halves(y), I32)


def _experts(block_e, n_used, xs, w1, w3, w2, bm):
    n_rows, half = xs.shape
    d, f = w1.shape[1], w1.shape[2]
    rows = lambda i, be, nu: (jnp.minimum(i, nu[0] - 1), 0)
    wblk = lambda i, be, nu: (be[i], 0, 0)
    grid_spec = pltpu.PrefetchScalarGridSpec(
        num_scalar_prefetch=2,
        grid=(n_rows // bm,),
        in_specs=[pl.BlockSpec((bm, half), rows),
                  pl.BlockSpec((1, d, f), wblk),
                  pl.BlockSpec((1, d, f), wblk),
                  pl.BlockSpec((1, f, d), wblk)],
        out_specs=pl.BlockSpec((bm, half), rows),
        scratch_shapes=[pltpu.VMEM((d, 2 * f), BF16), pltpu.VMEM((f, d), BF16)],
    )
    return pl.pallas_call(
        _experts_kernel,
        grid_spec=grid_spec,
        out_shape=jax.ShapeDtypeStruct((n_rows, half), I32),
        compiler_params=_params("arbitrary"),
        name="experts",
    )(block_e, n_used, xs, w1, w3, w2)


def _combine_kernel(g_ref, w8_ref, x1_ref, gate2_ref, gfin_ref, *rest):
    o_ref = rest[-1]
    w8 = w8_ref[...]
    acc_lo = acc_hi = None
    for k in range(TOP_K):
        lo, hi = _unpack_halves(lax.bitcast_convert_type(g_ref[k], U32))
        wk = w8[:, k:k + 1]
        acc_lo = wk * lo if acc_lo is None else acc_lo + wk * lo
        acc_hi = wk * hi if acc_hi is None else acc_hi + wk * hi
    routed = jnp.concatenate([acc_lo, acc_hi], axis=1)
    o_ref[...] = _rms(x1_ref[...] + gate2_ref[0] * routed, gfin_ref[...])


def _combine(g6, w8, x1, gate2, g_final, tt, tiles_per_seq, b0, n_tok_all, out_prev):
    n_tok, d = x1.shape
    half = g6.shape[2]
    tok = lambda i: (i, 0)
    tile0 = b0 * tiles_per_seq
    in_specs = [pl.BlockSpec((TOP_K, tt, half), lambda i: (0, i, 0)),
                pl.BlockSpec((tt, 8), tok),
                pl.BlockSpec((tt, d), tok),
                pl.BlockSpec((1, 1, d), lambda i: (i // tiles_per_seq + b0, 0, 0)),
                pl.BlockSpec((1, d), lambda i: (0, 0))]
    args = [g6, w8, x1, gate2, g_final.reshape(1, d)]
    aliases = {}
    if out_prev is not None:
        in_specs.append(pl.BlockSpec(memory_space=pl.ANY))
        args.append(out_prev)
        aliases = {len(args) - 1: 0}
    return pl.pallas_call(
        _combine_kernel,
        grid=(n_tok // tt,),
        in_specs=in_specs,
        out_specs=pl.BlockSpec((tt, d), lambda i: (i + tile0, 0)),
        out_shape=jax.ShapeDtypeStruct((n_tok_all, d), F32),
        input_output_aliases=aliases,
        compiler_params=_params("arbitrary"),
        name="combine",
    )(*args)


def _block_experts(counts, bm, n_blocks):
    padded = (counts.astype(I32) + (bm - 1)) // bm * bm
    pend = jnp.cumsum(padded)
    starts = jnp.arange(n_blocks, dtype=I32) * bm
    block_e = jnp.minimum(jnp.sum((pend[None, :] <= starts[:, None]).astype(I32), axis=1), N_EXPERTS - 1)
    return block_e, (pend[-1:] // bm).astype(I32)


def kernel(x, c, w_ada, b_ada, g_mix, w_in, conv_w, conv_b, g_conv, lam_re, lam_im, log_dt, b_re, b_im,
           c_re, c_im, d_skip, w_glu, b_glu, g_ssm, w_out, g_ffn, w_router, router_bias, w1, w3, w2,
           ws1, ws3, ws2, g_final):
    bsz, seq, d = x.shape
    n_tok = bsz * seq
    ts = min(MIX_BACK_TOKENS, seq)
    L = min(S5_CHUNK, seq)
    n_chunks = seq // L
    G, H = SSM_GROUPS, SSM_GROUP_CH

    mod = _adaln(c, w_ada, b_ada).reshape(bsz, 6, 1, d)
    shift1, scale1, gate1, shift2, scale2, gate2 = (mod[:, i] for i in range(6))

    yconv, ut = _mix_front(x, shift1, scale1, g_mix, w_in.astype(BF16), conv_w, conv_b, g_conv,
                           min(MIX_FRONT_TOKENS, seq))

    kern, e_mat, f_mat, a_pow = _s5_tables(lam_re, lam_im, log_dt, b_re, b_im, c_re, c_im, d_skip,
                                           L, n_chunks)
    yt = _s5_core(ut.reshape(bsz, G, H, n_chunks, L), kern, e_mat, f_mat, a_pow).reshape(bsz, G * H, seq)

    NG, NJ = N_EXPERT_GROUPS, EXPERTS_PER_GROUP
    w_r = w_router.astype(F32).reshape(d, NG, NJ).transpose(0, 2, 1).reshape(d, N_EXPERTS)
    r_bias = router_bias.reshape(NG, NJ).T.reshape(N_EXPERTS)
    w_r_hi = w_r.astype(BF16)
    wr_hi = jnp.concatenate([w_r_hi, (w_r - w_r_hi.astype(F32)).astype(BF16)], axis=1)
    wr_lo = jnp.concatenate([w_r_hi, jnp.zeros_like(w_r_hi)], axis=1)
    w_out_bf = w_out.astype(BF16)
    w_glu_bf = w_glu.astype(BF16)
    ws13 = jnp.concatenate([ws1, ws3], axis=1).astype(BF16)
    ws2_bf = ws2.astype(BF16)
    bm = EXPERT_ROWS
    tt = min(COMBINE_TOKENS, seq)

    n_parts = MOE_PARTS if bsz % MOE_PARTS == 0 else 1
    pb = bsz // n_parts
    pt = pb * seq
    n_blocks = -(-(pt * TOP_K + N_EXPERTS * (bm - 1)) // bm)
    parts = []
    for p in range(n_parts):
        x1, h2p, e8, w8, r8, counts = _mix_back(
            x, yconv, yt, gate1, shift2, scale2, gate2, w_glu_bf, b_glu, g_ssm,
            w_out_bf[:D_CONV], w_out_bf[D_CONV:], g_ffn, wr_hi, wr_lo, r_bias, ws13, ws2_bf,
            ts, p * pb, pb)
        counts = counts.reshape(NJ, NG, -1).transpose(1, 0, 2).reshape(N_EXPERTS, -1)
        pos8 = _plan(counts, e8, r8, bm)
        posflat = pos8[:TOP_K].reshape(TOP_K * pt)
        block_e = _block_experts(counts[:, 0], bm, n_blocks)
        xs = _sc_dispatch(h2p.reshape(pt, d // 2), posflat, n_blocks * bm)
        parts.append((x1, w8, posflat, block_e, xs))
    ys = [_experts(*block_e, xs, w1, w3, w2, bm) for (_, _, _, block_e, xs) in parts]
    g6 = [_sc_gather(y, posflat).reshape(TOP_K, pt, d // 2) for y, (_, _, posflat, _, _) in zip(ys, parts)]
    out = None
    for p, (g, (x1, w8, _, _, _)) in enumerate(zip(g6, parts)):
        out = _combine(g, w8.reshape(pt, 8), x1.reshape(pt, d), gate2, g_final, tt, seq // tt,
                       p * pb, n_tok, out)
    return out.reshape(bsz, seq, d)
```

```python
import functools
import math

import jax
import jax.numpy as jnp
import numpy as np
from jax import lax
from jax.experimental import pallas as pl
from jax.experimental.pallas import tpu as pltpu
from jax.experimental.pallas import tpu_sc as plsc

F32 = jnp.float32
BF16 = jnp.bfloat16
U32 = jnp.uint32
I32 = jnp.int32

D_CONV = 768
D_SSM = 256
SSM_GROUPS = 16
SSM_GROUP_CH = 16
SSM_STATE = 64
N_EXPERTS = 64
TOP_K = 6
N_EXPERT_GROUPS = 8
TOPK_GROUPS = 4
EXPERTS_PER_GROUP = 8
ROUTED_SCALE = 2.5
RMS_EPS = 1e-6

S5_CHUNK = 128
EXPERT_ROWS = 1024
EXPERT_SUB = 512
MIX_FRONT_TOKENS = 1024
MIX_FRONT_SUB = 1024
MIX_SUB = 256
MIX_BACK_TOKENS = 1024
MIX_PHASE_LAG = 3
COMBINE_TOKENS = 512
LAYER_PARTS = 2
LAST_PART_SPLIT = 2
V7X_SC_CORES = 2
V7X_SC_SUBCORES = 16
SC_WORKERS = V7X_SC_CORES * V7X_SC_SUBCORES
SC_WINDOW = 64
V7X_VMEM_LIMIT = 56 * 1024 * 1024
NEG_INF = float("-inf")
HIGH_HALF = np.uint32(0xFFFF0000)


def _rms(x, g):
    return x * lax.rsqrt(jnp.mean(x * x, axis=-1, keepdims=True) + RMS_EPS) * g


def _gelu_tanh(x):
    return 0.5 * x * (1.0 + jnp.tanh(math.sqrt(2.0 / math.pi) * (x + 0.044715 * (x * x * x))))


def _silu(x):
    return x * jax.nn.sigmoid(x)


def _params(*sem):
    return pltpu.CompilerParams(dimension_semantics=sem, vmem_limit_bytes=V7X_VMEM_LIMIT)


def _pack_halves(y):
    return _pack_rounded(y.astype(BF16).astype(F32))


def _pack_rounded(y):
    m = y.shape[1] // 2
    bits = lax.bitcast_convert_type(y, U32)
    return (bits[:, m:] & HIGH_HALF) | (bits[:, :m] >> 16)


def _unpack_halves(w):
    lo = lax.bitcast_convert_type(w << 16, F32)
    hi = lax.bitcast_convert_type(w & HIGH_HALF, F32)
    return lo, hi


def _adaln_kernel(c_ref, w_ref, b_ref, o_ref):
    o_ref[...] = jnp.dot(_silu(c_ref[...]), w_ref[...], preferred_element_type=F32,
                         precision=lax.Precision.HIGHEST) + b_ref[...]


def _adaln(c, w_ada, b_ada):
    bsz, d = c.shape
    n = w_ada.shape[1]
    bn = 1024
    return pl.pallas_call(
        _adaln_kernel,
        grid=(n // bn,),
        in_specs=[pl.BlockSpec((bsz, d), lambda j: (0, 0)),
                  pl.BlockSpec((d, bn), lambda j: (0, j)),
                  pl.BlockSpec((1, bn), lambda j: (0, j))],
        out_specs=pl.BlockSpec((bsz, bn), lambda j: (0, j)),
        out_shape=jax.ShapeDtypeStruct((bsz, n), F32),
        compiler_params=_params("arbitrary"),
        name="adaln",
    )(c, w_ada, b_ada.reshape(1, n))


def _mix_front_kernel(x_ref, shift_ref, scale_ref, gmix_ref, win_ref, cw_ref, cb_ref, gconv_ref, after_ref,
                      yconv_ref, ut_ref, zprev_ref):
    del after_ref
    s = pl.program_id(1)

    @pl.when(s == 0)
    def _():
        zprev_ref[...] = jnp.zeros_like(zprev_ref)

    cw = cw_ref[...]
    sub = min(MIX_FRONT_SUB, x_ref.shape[1])
    carry = {"prev": zprev_ref[...]}

    def norm(st):
        x = x_ref[0, st["r0"]:st["r0"] + sub, :]
        st["h"] = (_rms(x, gmix_ref[...]) * (1.0 + scale_ref[0]) + shift_ref[0]).astype(BF16)

    def in_proj(st):
        st["proj"] = jnp.dot(st.pop("h"), win_ref[...], preferred_element_type=F32)

    def mixers(st):
        r0 = st["r0"]
        proj = st.pop("proj")
        b_gate = proj[:, :D_CONV]
        c_gate = proj[:, D_CONV:2 * D_CONV]
        v = proj[:, 2 * D_CONV:3 * D_CONV]
        u = proj[:, 3 * D_CONV:]
        z = c_gate * v
        prev = carry["prev"]
        rid = lax.broadcasted_iota(I32, z.shape, 0)
        z1 = jnp.where(rid == 0, prev[7:8, :], pltpu.roll(z, 1, axis=0))
        z2 = jnp.where(rid == 0, prev[6:7, :], jnp.where(rid == 1, prev[7:8, :], pltpu.roll(z, 2, axis=0)))
        carry["prev"] = z[sub - 8:, :]
        conv = cw[0:1, :] * z2 + cw[1:2, :] * z1 + cw[2:3, :] * z + cb_ref[...]
        yconv_ref[0, r0:r0 + sub, :] = _rms(b_gate * conv, gconv_ref[...]).astype(BF16)
        ut_ref[0, :, r0:r0 + sub] = u.T.astype(BF16)

    phases = (norm, in_proj, mixers)
    chains = [{"r0": r0} for r0 in range(0, x_ref.shape[1], sub)]
    for step in range(len(phases) + len(chains) - 1):
        for ci, st in enumerate(chains):
            if 0 <= step - ci < len(phases):
                phases[step - ci](st)
    zprev_ref[...] = carry["prev"]


def _mix_front(x, shift1, scale1, g_mix, w_in_bf, conv_w, conv_b, g_conv, ts, b0, bsz, after):
    _, seq, d = x.shape
    d_in = w_in_bf.shape[1]
    row = lambda b, s: (b + b0, 0, 0)
    const2 = lambda b, s: (0, 0)
    return pl.pallas_call(
        _mix_front_kernel,
        grid=(bsz, seq // ts),
        in_specs=[pl.BlockSpec((1, ts, d), lambda b, s: (b + b0, s, 0)),
                  pl.BlockSpec((1, 1, d), row),
                  pl.BlockSpec((1, 1, d), row),
                  pl.BlockSpec((1, d), const2),
                  pl.BlockSpec((d, d_in), const2),
                  pl.BlockSpec((8, D_CONV), const2),
                  pl.BlockSpec((1, D_CONV), const2),
                  pl.BlockSpec((1, D_CONV), const2),
                  pl.BlockSpec(memory_space=pl.ANY)],
        out_specs=[pl.BlockSpec((1, ts, D_CONV), lambda b, s: (b, s, 0)),
                   pl.BlockSpec((1, D_SSM, ts), lambda b, s: (b, 0, s))],
        out_shape=[jax.ShapeDtypeStruct((bsz, seq, D_CONV), BF16),
                   jax.ShapeDtypeStruct((bsz, D_SSM, seq), BF16)],
        scratch_shapes=[pltpu.VMEM((8, D_CONV), F32)],
        compiler_params=_params("arbitrary", "arbitrary"),
        name="mix_front",
    )(x, shift1, scale1, g_mix.reshape(1, d), w_in_bf,
      jnp.pad(conv_w, ((0, 8 - conv_w.shape[0]), (0, 0))), conv_b.reshape(1, D_CONV),
      g_conv.reshape(1, D_CONV), after)


def _s5_tables(lam_re, lam_im, log_dt, b_re, b_im, c_re, c_im, d_skip, chunk, n_chunks):
    hp = lax.Precision.HIGHEST
    G, H, P, L = SSM_GROUPS, SSM_GROUP_CH, SSM_STATE, chunk
    lr = lam_re.astype(F32)
    li = lam_im.astype(F32)
    dt = jnp.exp(log_dt.astype(F32))[:, None]
    mag = jnp.exp(lr * dt)
    ang = li * dt
    ab_re = mag * jnp.cos(ang)
    ab_im = mag * jnp.sin(ang)
    den = lr * lr + li * li
    nr = ab_re - 1.0
    ni = ab_im
    q_re = (nr * lr + ni * li) / den
    q_im = (ni * lr - nr * li) / den
    br = b_re.astype(F32)
    bi = b_im.astype(F32)
    bb_re = q_re[..., None] * br - q_im[..., None] * bi
    bb_im = q_re[..., None] * bi + q_im[..., None] * br

    def a_power(tau):
        t = tau.astype(F32)[None, :, None]
        m = jnp.exp(t * (lr * dt)[:, None, :])
        th = t * ang[:, None, :]
        return m * jnp.cos(th), m * jnp.sin(th)

    pw_re, pw_im = a_power(jnp.arange(L + 1))
    cr = c_re.astype(F32)[:, None]
    ci = c_im.astype(F32)[:, None]
    cp_re = cr * pw_re[:, :, None, :] - ci * pw_im[:, :, None, :]
    cp_im = cr * pw_im[:, :, None, :] + ci * pw_re[:, :, None, :]
    kern = (jnp.einsum('gthp,gpk->gthk', cp_re[:, :L], bb_re, precision=hp)
            - jnp.einsum('gthp,gpk->gthk', cp_im[:, :L], bb_im, precision=hp))
    kern = kern.at[:, 0].add(d_skip.astype(F32).reshape(G, H)[:, :, None] * jnp.eye(H, dtype=F32))
    kern = kern.transpose(0, 3, 2, 1).reshape(G, H * H, L)

    rev_re = pw_re[:, L - 1::-1]
    rev_im = pw_im[:, L - 1::-1]
    bt_re = bb_re.transpose(0, 2, 1)[:, :, None, :]
    bt_im = bb_im.transpose(0, 2, 1)[:, :, None, :]
    e_re = rev_re[:, None] * bt_re - rev_im[:, None] * bt_im
    e_im = rev_re[:, None] * bt_im + rev_im[:, None] * bt_re
    e_mat = jnp.concatenate([e_re, e_im], axis=-1).reshape(G, H * L, 2 * P)

    f_re = cp_re[:, 1:].transpose(0, 3, 2, 1)
    f_im = -cp_im[:, 1:].transpose(0, 3, 2, 1)
    f_mat = jnp.concatenate([f_re, f_im], axis=1).reshape(G, 2 * P, H * L)

    n_steps = max(1, (n_chunks - 1).bit_length())
    sr, si = a_power(L * (2 ** jnp.arange(n_steps)))
    a_pow = jnp.stack([jnp.concatenate([sr, sr], axis=-1),
                       jnp.concatenate([-si, si], axis=-1)], axis=2)
    return kern, e_mat.astype(BF16), f_mat.astype(BF16), a_pow


def _s5_kernel(u_ref, k_ref, e_ref, f_ref, a_ref, after_ref, y_ref, toep_ref, *, n_chunks, n_steps, chunk):
    del after_ref
    L, H = chunk, SSM_GROUP_CH

    causal = lax.broadcasted_iota(I32, (L, L), 1) >= lax.broadcasted_iota(I32, (L, L), 0)

    def build(hin, carry):
        r0 = pl.multiple_of(hin * L, L)
        for hout in range(H):
            krow = k_ref[0, pl.ds(hin * H + hout, 1), :]
            blk = pltpu.roll(jnp.broadcast_to(krow, (L, L)), 0, axis=1, stride=1, stride_axis=0)
            toep_ref[pl.ds(r0, L), hout * L:(hout + 1) * L] = jnp.where(causal, blk, 0.0).astype(BF16)
        return carry

    lax.fori_loop(0, H, build, 0)

    bsz = u_ref.shape[0]
    u = jnp.concatenate([u_ref[:, h].reshape(bsz * n_chunks, L) for h in range(H)], axis=1)
    st = jnp.dot(u, e_ref[0], preferred_element_type=F32)
    cidx = lax.broadcasted_iota(I32, st.shape, 0) % n_chunks
    p2 = st.shape[1]

    def shifted(xv, d):
        return jnp.where(cidx >= d, pltpu.roll(xv, d, axis=0), 0.0)

    for k in range(n_steps):
        d = 1 << k
        if d >= n_chunks:
            break
        z = shifted(st, d)
        st = st + a_ref[0, k, 0:1, :] * z + a_ref[0, k, 1:2, :] * pltpu.roll(z, p2 // 2, axis=1)
    s_in = shifted(st, 1)
    y = jnp.dot(u, toep_ref[...], preferred_element_type=F32)
    y = (y + jnp.dot(s_in.astype(BF16), f_ref[0], preferred_element_type=F32)).astype(BF16)
    for h in range(H):
        y_ref[:, h] = y[:, h * L:(h + 1) * L].reshape(bsz, n_chunks, L)


def _s5_core(ut5, kern, e_mat, f_mat, a_pow, after):
    bsz, g, h, n_chunks, chunk = ut5.shape
    hl = h * chunk
    n_steps = a_pow.shape[1]
    p2 = e_mat.shape[2]
    blk = lambda i: (i, 0, 0)
    seq_blk = pl.BlockSpec((bsz, None, h, n_chunks, chunk), lambda i: (0, i, 0, 0, 0))
    return pl.pallas_call(
        functools.partial(_s5_kernel, n_chunks=n_chunks, n_steps=n_steps, chunk=chunk),
        grid=(g,),
        in_specs=[seq_blk,
                  pl.BlockSpec((1,) + kern.shape[1:], blk),
                  pl.BlockSpec((1, hl, p2), blk),
                  pl.BlockSpec((1, p2, hl), blk),
                  pl.BlockSpec((1, n_steps, 2, p2), lambda i: (i, 0, 0, 0)),
                  pl.BlockSpec(memory_space=pl.ANY)],
        out_specs=seq_blk,
        out_shape=jax.ShapeDtypeStruct(ut5.shape, BF16),
        scratch_shapes=[pltpu.VMEM((hl, hl), BF16)],
        compiler_params=_params("arbitrary"),
        name="s5_core",
    )(ut5, kern, e_mat, f_mat, a_pow, after)


def _route(scores, biased, tri, base):
    n_tok = scores.shape[1]
    shape3 = (EXPERTS_PER_GROUP, N_EXPERT_GROUPS, n_tok)
    sc3 = scores.reshape(shape3)
    b3 = biased.reshape(shape3)
    j_iota = lax.broadcasted_iota(I32, shape3, 0)
    e_iota = lax.broadcasted_iota(I32, shape3, 1) * EXPERTS_PER_GROUP + j_iota

    def red(fn, x):
        return fn(fn(x, axis=0, keepdims=True), axis=1, keepdims=True)

    m1 = jnp.max(b3, axis=0, keepdims=True)
    i1 = jnp.min(jnp.where(b3 == m1, j_iota, EXPERTS_PER_GROUP), axis=0, keepdims=True)
    m2 = jnp.max(jnp.where(j_iota == i1, NEG_INF, b3), axis=0, keepdims=True)
    gs = m1 + m2

    g_iota = lax.broadcasted_iota(I32, gs.shape, 1)
    gsel = jnp.zeros(gs.shape, F32)
    cur = gs
    for _ in range(TOPK_GROUPS):
        m = jnp.max(cur, axis=1, keepdims=True)
        ig = jnp.min(jnp.where(cur == m, g_iota, N_EXPERT_GROUPS), axis=1, keepdims=True)
        pick = g_iota == ig
        gsel = jnp.where(pick, 1.0, gsel)
        cur = jnp.where(pick, NEG_INF, cur)

    cur = jnp.where(gsel > 0.0, b3, NEG_INF)
    sel = jnp.zeros(shape3, F32)
    ids, vals = [], []
    for _ in range(TOP_K):
        m = red(jnp.max, cur)
        ie = red(jnp.min, jnp.where(cur == m, e_iota, N_EXPERTS))
        pick = e_iota == ie
        ids.append(ie)
        vals.append(red(jnp.sum, jnp.where(pick, sc3, 0.0)))
        sel = jnp.where(pick, 1.0, sel)
        cur = jnp.where(pick, NEG_INF, cur)
    tot = functools.reduce(lambda a, b: a + b, vals)

    sel2 = sel.reshape(N_EXPERTS, n_tok)
    before = jnp.dot(sel2.astype(BF16), tri, preferred_element_type=F32) + base
    before3 = before.reshape(shape3)
    ranks = [red(jnp.sum, jnp.where(e_iota == ie, before3, 0.0)) for ie in ids]

    def rows(parts, dtype):
        parts = [p.reshape(1, n_tok).astype(dtype) for p in parts]
        return jnp.concatenate(parts + [jnp.zeros((8 - len(parts), n_tok), dtype)], axis=0)

    e8 = rows(ids, I32)
    w8 = rows([v / tot * ROUTED_SCALE for v in vals], F32)
    r8 = rows(ranks, I32)
    return e8, w8, r8, jnp.sum(sel2, axis=1, keepdims=True)


def _mix_back_kernel(x_ref, yconv_ref, yt_ref, gate1_ref, shift2_ref, scale2_ref, gate2_ref,
                     wglu_ref, bglu_ref, gssm_ref, woc_ref, wos_ref, gffn_ref,
                     wrh_ref, wrl_ref, rbias_ref, ws13_ref, ws2_ref, tri_ref,
                     x1_ref, h2p_ref, e8_ref, w8_ref, r8_ref, cnt_ref):
    first = jnp.logical_and(pl.program_id(0) == 0, pl.program_id(1) == 0)

    @pl.when(first)
    def _():
        cnt_ref[...] = jnp.zeros_like(cnt_ref)

    sub = tri_ref.shape[0]

    def ssm_post(st):
        r0 = st["r0"]
        y = yt_ref[0, :, r0:r0 + sub].astype(F32).T
        y = _gelu_tanh(y)
        y = y * jax.nn.sigmoid(jnp.dot(y.astype(BF16), wglu_ref[...], preferred_element_type=F32)
                               + bglu_ref[...])
        st["y_ssm"] = _rms(y, gssm_ref[...]).astype(BF16)

    def out_proj(st):
        r0 = st["r0"]
        mix = jnp.dot(yconv_ref[0, r0:r0 + sub, :], woc_ref[...], preferred_element_type=F32)
        mix = mix + jnp.dot(st.pop("y_ssm"), wos_ref[...], preferred_element_type=F32)
        st["x1"] = x_ref[0, r0:r0 + sub, :] + gate1_ref[0] * mix

    def ffn_in(st):
        r0 = st["r0"]
        h2 = _rms(st["x1"], gffn_ref[...] * (1.0 + scale2_ref[0])) + shift2_ref[0]
        h2b = h2.astype(BF16)
        h2r = h2b.astype(F32)
        h2p_ref[0, r0:r0 + sub, :] = lax.bitcast_convert_type(_pack_rounded(h2r), I32)
        st["h2b"] = h2b
        st["h2l"] = (h2 - h2r).astype(BF16)

    def shared_expert(st):
        r0 = st["r0"]
        a = jnp.dot(st["h2b"], ws13_ref[...], preferred_element_type=F32)
        f = a.shape[1] // 2
        act = _silu(a[:, :f]) * a[:, f:]
        shared = jnp.dot(act.astype(BF16), ws2_ref[...], preferred_element_type=F32)
        x1_ref[0, r0:r0 + sub, :] = st.pop("x1") + gate2_ref[0] * shared

    def router(st):
        lt = (jnp.dot(st.pop("h2b"), wrh_ref[...], preferred_element_type=F32)
              + jnp.dot(st.pop("h2l"), wrl_ref[...], preferred_element_type=F32)).T
        st["scores"] = jax.nn.sigmoid(lt[:N_EXPERTS] + lt[N_EXPERTS:])

    def route(st):
        r0 = st["r0"]
        scores = st.pop("scores")
        e8, w8, r8, cnt = _route(scores, scores + rbias_ref[...], tri_ref[...], carry["base"])
        e8_ref[:, r0:r0 + sub] = e8
        w8_ref[0, r0:r0 + sub, :] = w8.T
        r8_ref[:, r0:r0 + sub] = r8
        carry["base"] = carry["base"] + cnt

    carry = {"base": cnt_ref[:, 0:1]}
    phases = (ssm_post, out_proj, ffn_in, shared_expert, router, route)
    chains = [{"r0": r0} for r0 in range(0, x_ref.shape[1], sub)]
    lag = MIX_PHASE_LAG
    for step in range(len(phases) + lag * (len(chains) - 1)):
        for ci, st in enumerate(chains):
            ph = step - lag * ci
            if 0 <= ph < len(phases):
                phases[ph](st)
    cnt_ref[...] = jnp.broadcast_to(carry["base"], cnt_ref.shape)


def _mix_back(x, yconv, yt, gate1, shift2, scale2, gate2, w_glu_bf, b_glu, g_ssm, wo_conv, wo_ssm,
              g_ffn, wr_hi, wr_lo, router_bias, ws13, ws2_bf, ts, b0, bsz):
    _, seq, d = x.shape
    n_tok = bsz * seq
    tiles = seq // ts
    row = lambda b, s: (b + b0, 0, 0)
    const2 = lambda b, s: (0, 0)
    tile_in = lambda b, s: (b + b0, s, 0)
    tile = lambda b, s: (b, s, 0)
    flat = lambda b, s: (0, b * tiles + s)
    full = lambda a: pl.BlockSpec(a.shape, const2)
    sub = min(MIX_SUB, ts)
    tri = jnp.triu(jnp.ones((sub, sub), BF16), k=1)
    args = (w_glu_bf, b_glu.reshape(1, D_SSM), g_ssm.reshape(1, D_SSM), wo_conv, wo_ssm,
            g_ffn.reshape(1, d), wr_hi, wr_lo, router_bias.reshape(N_EXPERTS, 1), ws13, ws2_bf, tri)
    return pl.pallas_call(
        _mix_back_kernel,
        grid=(bsz, tiles),
        in_specs=[pl.BlockSpec((1, ts, d), tile_in),
                  pl.BlockSpec((1, ts, D_CONV), tile),
                  pl.BlockSpec((1, D_SSM, ts), lambda b, s: (b, 0, s)),
                  pl.BlockSpec((1, 1, d), row), pl.BlockSpec((1, 1, d), row),
                  pl.BlockSpec((1, 1, d), row), pl.BlockSpec((1, 1, d), row)]
                 + [full(a) for a in args],
        out_specs=[pl.BlockSpec((1, ts, d), tile),
                   pl.BlockSpec((1, ts, d // 2), tile),
                   pl.BlockSpec((8, ts), flat),
                   pl.BlockSpec((1, ts, 8), tile),
                   pl.BlockSpec((8, ts), flat),
                   pl.BlockSpec((N_EXPERTS, 128), const2)],
        out_shape=[jax.ShapeDtypeStruct((bsz, seq, d), F32),
                   jax.ShapeDtypeStruct((bsz, seq, d // 2), I32),
                   jax.ShapeDtypeStruct((8, n_tok), I32),
                   jax.ShapeDtypeStruct((bsz, seq, 8), F32),
                   jax.ShapeDtypeStruct((8, n_tok), I32),
                   jax.ShapeDtypeStruct((N_EXPERTS, 128), F32)],
        compiler_params=_params("arbitrary", "arbitrary"),
        name="mix_back",
    )(x, yconv, yt, gate1, shift2, scale2, gate2, *args)


def _plan_kernel(cnt_ref, e8_ref, r8_ref, pos_ref, *, bm):
    cnt = cnt_ref[...].astype(I32)
    padded = (cnt + (bm - 1)) // bm * bm
    rid = lax.broadcasted_iota(I32, padded.shape, 0)
    incl = padded
    d = 1
    while d < N_EXPERTS:
        incl = incl + jnp.where(rid >= d, pltpu.roll(incl, d, axis=0), 0)
        d *= 2
    pstart = incl - padded
    lanes = pstart.shape[1]

    def chunk(ci, carry):
        c0 = pl.multiple_of(ci * lanes, lanes)
        e = e8_ref[:, pl.ds(c0, lanes)]
        acc = r8_ref[:, pl.ds(c0, lanes)]
        for ex in range(N_EXPERTS):
            acc = acc + jnp.where(e == ex, pstart[ex:ex + 1, :], 0)
        pos_ref[:, pl.ds(c0, lanes)] = acc
        return carry

    lax.fori_loop(0, e8_ref.shape[1] // lanes, chunk, 0)


def _plan(counts, e8, r8, bm):
    n_tok = e8.shape[1]
    tb = min(8192, n_tok)
    return pl.pallas_call(
        functools.partial(_plan_kernel, bm=bm),
        grid=(n_tok // tb,),
        in_specs=[pl.BlockSpec(counts.shape, lambda i: (0, 0)),
                  pl.BlockSpec((8, tb), lambda i: (0, i)),
                  pl.BlockSpec((8, tb), lambda i: (0, i))],
        out_specs=pl.BlockSpec((8, tb), lambda i: (0, i)),
        out_shape=jax.ShapeDtypeStruct((8, n_tok), I32),
        compiler_params=_params("arbitrary"),
        name="plan",
    )(counts, e8, r8)


def _sc_mesh():
    return plsc.VectorSubcoreMesh(core_axis_name="c", subcore_axis_name="s",
                                  num_cores=V7X_SC_CORES, num_subcores=V7X_SC_SUBCORES)


def _sc_worker_base(per_worker):
    return (lax.axis_index("s") * V7X_SC_CORES + lax.axis_index("c")) * per_worker


def _sc_dispatch(h2p, posflat, n_rows):
    n_tok, half = h2p.shape
    win = SC_WINDOW
    per_worker = n_tok // SC_WORKERS
    n_win = per_worker // win

    wins_per_k = n_tok // win

    @functools.partial(
        pl.kernel, mesh=_sc_mesh(),
        out_type=jax.ShapeDtypeStruct((n_rows, half), I32),
        scratch_types=[pltpu.VMEM((TOP_K, n_win, win), I32), pltpu.VMEM((2, win, half), I32),
                       pltpu.SemaphoreType.DMA((2,)), pltpu.SemaphoreType.DMA((2,))],
        name="sc_dispatch")
    def run(h2p_hbm, pos_hbm, xs_hbm, idx_v, rows_v, sem_in, sem_out):
        wid = _sc_worker_base(1)
        for k in range(TOP_K):
            pltpu.sync_copy(pos_hbm.at[pl.ds(k * wins_per_k + wid * n_win, n_win)], idx_v.at[k])

        @pl.loop(0, n_win, step=2)
        def _(j):
            loads = []
            for b in range(2):
                t0 = pl.multiple_of((wid * n_win + j + b) * win, win)
                loads.append(pltpu.async_copy(h2p_hbm.at[pl.ds(t0, win)], rows_v.at[b], sem_in.at[b]))
            stores = []
            for b in range(2):
                loads[b].wait()
                for k in range(TOP_K):
                    stores.append(pltpu.async_copy(rows_v.at[b], xs_hbm.at[idx_v.at[k, j + b]], sem_out.at[b]))
            for cp in stores:
                cp.wait()

    return run(h2p, posflat.reshape(TOP_K * wins_per_k, win))


def _sc_gather(ys, posflat):
    n_idx = posflat.shape[0]
    half = ys.shape[1]
    win = SC_WINDOW
    per_worker = n_idx // SC_WORKERS
    n_win = per_worker // win

    @functools.partial(
        pl.kernel, mesh=_sc_mesh(),
        out_type=jax.ShapeDtypeStruct((n_idx, half), I32),
        scratch_types=[pltpu.VMEM((n_win, win), I32), pltpu.VMEM((2, win, half), I32),
                       pltpu.SemaphoreType.DMA((2,)), pltpu.SemaphoreType.DMA((2,))],
        name="sc_gather")
    def run(ys_hbm, pos_hbm, out_hbm, idx_v, rows_v, sem_in, sem_out):
        wid = _sc_worker_base(1)
        pltpu.sync_copy(pos_hbm.at[pl.ds(wid * n_win, n_win)], idx_v)

        @pl.loop(0, n_win, step=2)
        def _(j):
            gathers = [pltpu.async_copy(ys_hbm.at[idx_v.at[j + b]], rows_v.at[b], sem_in.at[b])
                       for b in range(2)]
            writes = []
            for b in range(2):
                gathers[b].wait()
                off = pl.multiple_of((wid * n_win + j + b) * win, win)
                writes.append(pltpu.async_copy(rows_v.at[b], out_hbm.at[pl.ds(off, win)], sem_out.at[b]))
            for cp in writes:
                cp.wait()

    return run(ys, posflat.reshape(n_idx // win, win))


def _experts_kernel(be_ref, nu_ref, x_ref, w1_ref, w3_ref, w2_ref, after_ref, y_ref, w13_s, w2_s):
    del after_ref
    i = pl.program_id(0)
    changed = jnp.logical_or(i == 0, be_ref[i] != be_ref[jnp.maximum(i - 1, 0)])

    @pl.when(changed)
    def _():
        f = w1_ref.shape[2]
        w13_s[:, :f] = w1_ref[0].astype(BF16)
        w13_s[:, f:] = w3_ref[0].astype(BF16)
        w2_s[...] = w2_ref[0].astype(BF16)

    @pl.when(i < nu_ref[0])
    def _():
        half = x_ref.shape[1]
        sub = min(EXPERT_SUB, x_ref.shape[0])
        for r0 in range(0, x_ref.shape[0], sub):
            lo, hi = _unpack_halves(lax.bitcast_convert_type(x_ref[r0:r0 + sub, :], U32))
            a = jnp.dot(lo.astype(BF16), w13_s[:half, :], preferred_element_type=F32)
            a = a + jnp.dot(hi.astype(BF16), w13_s[half:, :], preferred_element_type=F32)
            f = a.shape[1] // 2
            act = _silu(a[:, :f]) * a[:, f:]
            y = jnp.dot(act.astype(BF16), w2_s[...], preferred_element_type=F32)
            y_ref[r0:r0 + sub, :] = lax.bitcast_convert_type(_pack_halves(y), I32)


def _experts(block_e, n_used, xs, w1, w3, w2, bm, after):
    n_rows, half = xs.shape
    d, f = w1.shape[1], w1.shape[2]
    rows = lambda i, be, nu: (jnp.minimum(i, nu[0] - 1), 0)
    wblk = lambda i, be, nu: (be[i], 0, 0)
    grid_spec = pltpu.PrefetchScalarGridSpec(
        num_scalar_prefetch=2,
        grid=(n_rows // bm,),
        in_specs=[pl.BlockSpec((bm, half), rows),
                  pl.BlockSpec((1, d, f), wblk),
                  pl.BlockSpec((1, d, f), wblk),
                  pl.BlockSpec((1, f, d), wblk),
                  pl.BlockSpec(memory_space=pl.ANY)],
        out_specs=pl.BlockSpec((bm, half), rows),
        scratch_shapes=[pltpu.VMEM((d, 2 * f), BF16), pltpu.VMEM((f, d), BF16)],
    )
    return pl.pallas_call(
        _experts_kernel,
        grid_spec=grid_spec,
        out_shape=jax.ShapeDtypeStruct((n_rows, half), I32),
        compiler_params=_params("arbitrary"),
        name="experts",
    )(block_e, n_used, xs, w1, w3, w2, after)


def _combine_kernel(g_ref, w8_ref, x1_ref, gate2_ref, gfin_ref, *rest):
    o_ref = rest[-1]
    w8 = w8_ref[...]
    acc_lo = acc_hi = None
    for k in range(TOP_K):
        lo, hi = _unpack_halves(lax.bitcast_convert_type(g_ref[k], U32))
        wk = w8[:, k:k + 1]
        acc_lo = wk * lo if acc_lo is None else acc_lo + wk * lo
        acc_hi = wk * hi if acc_hi is None else acc_hi + wk * hi
    routed = jnp.concatenate([acc_lo, acc_hi], axis=1)
    o_ref[...] = _rms(x1_ref[...] + gate2_ref[0] * routed, gfin_ref[...])


def _combine(g6, w8, x1, gate2, g_final, tt, tiles_per_seq, b_src, b0, n_tok_all, out_prev):
    d = x1.shape[1]
    n_tok, half = g6.shape[1:]
    src0 = b_src * tiles_per_seq
    tok = lambda i: (i + src0, 0)
    tile0 = b0 * tiles_per_seq
    in_specs = [pl.BlockSpec((TOP_K, tt, half), lambda i: (0, i, 0)),
                pl.BlockSpec((tt, 8), tok),
                pl.BlockSpec((tt, d), tok),
                pl.BlockSpec((1, 1, d), lambda i: (i // tiles_per_seq + b0, 0, 0)),
                pl.BlockSpec((1, d), lambda i: (0, 0))]
    args = [g6, w8, x1, gate2, g_final.reshape(1, d)]
    aliases = {}
    if out_prev is not None:
        in_specs.append(pl.BlockSpec(memory_space=pl.ANY))
        args.append(out_prev)
        aliases = {len(args) - 1: 0}
    return pl.pallas_call(
        _combine_kernel,
        grid=(n_tok // tt,),
        in_specs=in_specs,
        out_specs=pl.BlockSpec((tt, d), lambda i: (i + tile0, 0)),
        out_shape=jax.ShapeDtypeStruct((n_tok_all, d), F32),
        input_output_aliases=aliases,
        compiler_params=_params("arbitrary"),
        name="combine",
    )(*args)


def _block_experts(counts, bm, n_blocks):
    padded = (counts.astype(I32) + (bm - 1)) // bm * bm
    pend = jnp.cumsum(padded)
    starts = jnp.arange(n_blocks, dtype=I32) * bm
    block_e = jnp.minimum(jnp.sum((pend[None, :] <= starts[:, None]).astype(I32), axis=1), N_EXPERTS - 1)
    return block_e, (pend[-1:] // bm).astype(I32)


def kernel(x, c, w_ada, b_ada, g_mix, w_in, conv_w, conv_b, g_conv, lam_re, lam_im, log_dt, b_re, b_im,
           c_re, c_im, d_skip, w_glu, b_glu, g_ssm, w_out, g_ffn, w_router, router_bias, w1, w3, w2,
           ws1, ws3, ws2, g_final):
    bsz, seq, d = x.shape
    n_tok = bsz * seq
    ts = min(MIX_BACK_TOKENS, seq)
    L = min(S5_CHUNK, seq)
    n_chunks = seq // L
    G, H = SSM_GROUPS, SSM_GROUP_CH

    mod = _adaln(c, w_ada, b_ada).reshape(bsz, 6, 1, d)
    shift1, scale1, gate1, shift2, scale2, gate2 = (mod[:, i] for i in range(6))

    kern, e_mat, f_mat, a_pow = _s5_tables(lam_re, lam_im, log_dt, b_re, b_im, c_re, c_im, d_skip,
                                           L, n_chunks)
    w_in_bf = w_in.astype(BF16)

    NG, NJ = N_EXPERT_GROUPS, EXPERTS_PER_GROUP
    w_r = w_router.astype(F32).reshape(d, NG, NJ).transpose(0, 2, 1).reshape(d, N_EXPERTS)
    r_bias = router_bias.reshape(NG, NJ).T.reshape(N_EXPERTS)
    w_r_hi = w_r.astype(BF16)
    wr_hi = jnp.concatenate([w_r_hi, (w_r - w_r_hi.astype(F32)).astype(BF16)], axis=1)
    wr_lo = jnp.concatenate([w_r_hi, jnp.zeros_like(w_r_hi)], axis=1)
    w_out_bf = w_out.astype(BF16)
    w_glu_bf = w_glu.astype(BF16)
    ws13 = jnp.concatenate([ws1, ws3], axis=1).astype(BF16)
    ws2_bf = ws2.astype(BF16)
    bm = EXPERT_ROWS
    tt = min(COMBINE_TOKENS, seq)

    n_parts = LAYER_PARTS if bsz % LAYER_PARTS == 0 else 1
    pb = bsz // n_parts
    pt = pb * seq
    n_blocks = -(-(pt * TOP_K + N_EXPERTS * (bm - 1)) // bm)
    none = jnp.zeros((8, 128), F32)

    def run_experts(part, after):
        return _experts(*part["block_e"], part["xs"], w1, w3, w2, bm, after)

    def finish(part, ys, out, n_split):
        nb = pb // n_split
        for h in range(n_split):
            pos = part["pos8"][:TOP_K, h * nb * seq:(h + 1) * nb * seq].reshape(TOP_K * nb * seq)
            g6 = _sc_gather(ys, pos).reshape(TOP_K, nb * seq, d // 2)
            out = _combine(g6, part["w8"].reshape(pt, 8), part["x1"].reshape(pt, d), gate2, g_final, tt,
                           seq // tt, h * nb, part["b0"] + h * nb, n_tok, out)
        return out

    out, prev, after_front = None, None, none
    for p in range(n_parts):
        b0 = p * pb
        yconv, ut = _mix_front(x, shift1, scale1, g_mix, w_in_bf, conv_w, conv_b, g_conv,
                               min(MIX_FRONT_TOKENS, seq), b0, pb, after_front)
        ys_prev = run_experts(prev, ut) if prev is not None else None
        yt = _s5_core(ut.reshape(pb, G, H, n_chunks, L), kern, e_mat, f_mat, a_pow,
                      none if ys_prev is None else ys_prev).reshape(pb, G * H, seq)
        x1, h2p, e8, w8, r8, counts = _mix_back(
            x, yconv, yt, gate1, shift2, scale2, gate2, w_glu_bf, b_glu, g_ssm,
            w_out_bf[:D_CONV], w_out_bf[D_CONV:], g_ffn, wr_hi, wr_lo, r_bias, ws13, ws2_bf,
            ts, b0, pb)
        counts = counts.reshape(NJ, NG, -1).transpose(1, 0, 2).reshape(N_EXPERTS, -1)
        pos8 = _plan(counts, e8, r8, bm)
        xs = _sc_dispatch(h2p.reshape(pt, d // 2), pos8[:TOP_K].reshape(TOP_K * pt), n_blocks * bm)
        if prev is not None:
            out = finish(prev, ys_prev, out, 1)
        prev = dict(b0=b0, x1=x1, w8=w8, pos8=pos8, xs=xs,
                    block_e=_block_experts(counts[:, 0], bm, n_blocks))
        after_front = pos8
    ys_last = run_experts(prev, none if out is None else out)
    out = finish(prev, ys_last, out, LAST_PART_SPLIT if pb % LAST_PART_SPLIT == 0 else 1)
    return out.reshape(bsz, seq, d)
```

```python
import functools
import math

import jax
import jax.numpy as jnp
import numpy as np
from jax import lax
from jax.experimental import pallas as pl
from jax.experimental.pallas import tpu as pltpu
from jax.experimental.pallas import tpu_sc as plsc

F32 = jnp.float32
BF16 = jnp.bfloat16
U32 = jnp.uint32
I32 = jnp.int32

D_CONV = 768
D_SSM = 256
SSM_GROUPS = 16
SSM_GROUP_CH = 16
SSM_STATE = 64
N_EXPERTS = 64
TOP_K = 6
N_EXPERT_GROUPS = 8
TOPK_GROUPS = 4
EXPERTS_PER_GROUP = 8
ROUTED_SCALE = 2.5
RMS_EPS = 1e-6

S5_CHUNK = 128
EXPERT_ROWS = 1024
EXPERT_SUB = 512
MIX_FRONT_TOKENS = 1024
MIX_FRONT_SUB = 1024
MIX_SUB = 256
MIX_BACK_TOKENS = 1024
MIX_PHASE_LAG = 3
COMBINE_TOKENS = 512
LAYER_PARTS = 2
LAST_PART_SPLIT = 4
RESIDUAL_DTYPE = BF16
V7X_SC_CORES = 2
V7X_SC_SUBCORES = 16
SC_WORKERS = V7X_SC_CORES * V7X_SC_SUBCORES
SC_WINDOW = 64
V7X_VMEM_LIMIT = 56 * 1024 * 1024
NEG_INF = float("-inf")
HIGH_HALF = np.uint32(0xFFFF0000)


def _rms(x, g):
    return x * lax.rsqrt(jnp.mean(x * x, axis=-1, keepdims=True) + RMS_EPS) * g


def _gelu_tanh(x):
    return 0.5 * x * (1.0 + jnp.tanh(math.sqrt(2.0 / math.pi) * (x + 0.044715 * (x * x * x))))


def _silu(x):
    return x * jax.nn.sigmoid(x)


def _params(*sem):
    return pltpu.CompilerParams(dimension_semantics=sem, vmem_limit_bytes=V7X_VMEM_LIMIT)


def _pack_halves(y):
    return _pack_rounded(y.astype(BF16).astype(F32))


def _pack_rounded(y):
    m = y.shape[1] // 2
    bits = lax.bitcast_convert_type(y, U32)
    return (bits[:, m:] & HIGH_HALF) | (bits[:, :m] >> 16)


def _unpack_halves(w):
    lo = lax.bitcast_convert_type(w << 16, F32)
    hi = lax.bitcast_convert_type(w & HIGH_HALF, F32)
    return lo, hi


def _adaln_kernel(c_ref, w_ref, b_ref, o_ref):
    o_ref[...] = jnp.dot(_silu(c_ref[...]), w_ref[...], preferred_element_type=F32,
                         precision=lax.Precision.HIGHEST) + b_ref[...]


def _adaln(c, w_ada, b_ada):
    bsz, d = c.shape
    n = w_ada.shape[1]
    bn = 1024
    return pl.pallas_call(
        _adaln_kernel,
        grid=(n // bn,),
        in_specs=[pl.BlockSpec((bsz, d), lambda j: (0, 0)),
                  pl.BlockSpec((d, bn), lambda j: (0, j)),
                  pl.BlockSpec((1, bn), lambda j: (0, j))],
        out_specs=pl.BlockSpec((bsz, bn), lambda j: (0, j)),
        out_shape=jax.ShapeDtypeStruct((bsz, n), F32),
        compiler_params=_params("arbitrary"),
        name="adaln",
    )(c, w_ada, b_ada.reshape(1, n))


def _mix_front_kernel(x_ref, shift_ref, scale_ref, gmix_ref, win_ref, cw_ref, cb_ref, gconv_ref, after_ref,
                      yconv_ref, ut_ref, zprev_ref):
    del after_ref
    s = pl.program_id(1)

    @pl.when(s == 0)
    def _():
        zprev_ref[...] = jnp.zeros_like(zprev_ref)

    cw = cw_ref[...]
    sub = min(MIX_FRONT_SUB, x_ref.shape[1])
    carry = {"prev": zprev_ref[...]}

    def norm(st):
        x = x_ref[0, st["r0"]:st["r0"] + sub, :]
        st["h"] = (_rms(x, gmix_ref[...]) * (1.0 + scale_ref[0]) + shift_ref[0]).astype(BF16)

    def in_proj(st):
        st["proj"] = jnp.dot(st.pop("h"), win_ref[...], preferred_element_type=F32)

    def mixers(st):
        r0 = st["r0"]
        proj = st.pop("proj")
        b_gate = proj[:, :D_CONV]
        c_gate = proj[:, D_CONV:2 * D_CONV]
        v = proj[:, 2 * D_CONV:3 * D_CONV]
        u = proj[:, 3 * D_CONV:]
        z = c_gate * v
        prev = carry["prev"]
        rid = lax.broadcasted_iota(I32, z.shape, 0)
        z1 = jnp.where(rid == 0, prev[7:8, :], pltpu.roll(z, 1, axis=0))
        z2 = jnp.where(rid == 0, prev[6:7, :], jnp.where(rid == 1, prev[7:8, :], pltpu.roll(z, 2, axis=0)))
        carry["prev"] = z[sub - 8:, :]
        conv = cw[0:1, :] * z2 + cw[1:2, :] * z1 + cw[2:3, :] * z + cb_ref[...]
        yconv_ref[0, r0:r0 + sub, :] = _rms(b_gate * conv, gconv_ref[...]).astype(BF16)
        ut_ref[0, :, r0:r0 + sub] = u.T.astype(BF16)

    phases = (norm, in_proj, mixers)
    chains = [{"r0": r0} for r0 in range(0, x_ref.shape[1], sub)]
    for step in range(len(phases) + len(chains) - 1):
        for ci, st in enumerate(chains):
            if 0 <= step - ci < len(phases):
                phases[step - ci](st)
    zprev_ref[...] = carry["prev"]


def _mix_front(x, shift1, scale1, g_mix, w_in_bf, conv_w, conv_b, g_conv, ts, b0, bsz, after):
    _, seq, d = x.shape
    d_in = w_in_bf.shape[1]
    row = lambda b, s: (b + b0, 0, 0)
    const2 = lambda b, s: (0, 0)
    return pl.pallas_call(
        _mix_front_kernel,
        grid=(bsz, seq // ts),
        in_specs=[pl.BlockSpec((1, ts, d), lambda b, s: (b + b0, s, 0)),
                  pl.BlockSpec((1, 1, d), row),
                  pl.BlockSpec((1, 1, d), row),
                  pl.BlockSpec((1, d), const2),
                  pl.BlockSpec((d, d_in), const2),
                  pl.BlockSpec((8, D_CONV), const2),
                  pl.BlockSpec((1, D_CONV), const2),
                  pl.BlockSpec((1, D_CONV), const2),
                  pl.BlockSpec(memory_space=pl.ANY)],
        out_specs=[pl.BlockSpec((1, ts, D_CONV), lambda b, s: (b, s, 0)),
                   pl.BlockSpec((1, D_SSM, ts), lambda b, s: (b, 0, s))],
        out_shape=[jax.ShapeDtypeStruct((bsz, seq, D_CONV), BF16),
                   jax.ShapeDtypeStruct((bsz, D_SSM, seq), BF16)],
        scratch_shapes=[pltpu.VMEM((8, D_CONV), F32)],
        compiler_params=_params("arbitrary", "arbitrary"),
        name="mix_front",
    )(x, shift1, scale1, g_mix.reshape(1, d), w_in_bf,
      jnp.pad(conv_w, ((0, 8 - conv_w.shape[0]), (0, 0))), conv_b.reshape(1, D_CONV),
      g_conv.reshape(1, D_CONV), after)


def _s5_tables(lam_re, lam_im, log_dt, b_re, b_im, c_re, c_im, d_skip, chunk, n_chunks):
    hp = lax.Precision.HIGHEST
    G, H, P, L = SSM_GROUPS, SSM_GROUP_CH, SSM_STATE, chunk
    lr = lam_re.astype(F32)
    li = lam_im.astype(F32)
    dt = jnp.exp(log_dt.astype(F32))[:, None]
    mag = jnp.exp(lr * dt)
    ang = li * dt
    ab_re = mag * jnp.cos(ang)
    ab_im = mag * jnp.sin(ang)
    den = lr * lr + li * li
    nr = ab_re - 1.0
    ni = ab_im
    q_re = (nr * lr + ni * li) / den
    q_im = (ni * lr - nr * li) / den
    br = b_re.astype(F32)
    bi = b_im.astype(F32)
    bb_re = q_re[..., None] * br - q_im[..., None] * bi
    bb_im = q_re[..., None] * bi + q_im[..., None] * br

    def a_power(tau):
        t = tau.astype(F32)[None, :, None]
        m = jnp.exp(t * (lr * dt)[:, None, :])
        th = t * ang[:, None, :]
        return m * jnp.cos(th), m * jnp.sin(th)

    pw_re, pw_im = a_power(jnp.arange(L + 1))
    cr = c_re.astype(F32)[:, None]
    ci = c_im.astype(F32)[:, None]
    cp_re = cr * pw_re[:, :, None, :] - ci * pw_im[:, :, None, :]
    cp_im = cr * pw_im[:, :, None, :] + ci * pw_re[:, :, None, :]
    kern = (jnp.einsum('gthp,gpk->gthk', cp_re[:, :L], bb_re, precision=hp)
            - jnp.einsum('gthp,gpk->gthk', cp_im[:, :L], bb_im, precision=hp))
    kern = kern.at[:, 0].add(d_skip.astype(F32).reshape(G, H)[:, :, None] * jnp.eye(H, dtype=F32))
    kern = kern.transpose(0, 3, 2, 1).reshape(G, H * H, L)

    rev_re = pw_re[:, L - 1::-1]
    rev_im = pw_im[:, L - 1::-1]
    bt_re = bb_re.transpose(0, 2, 1)[:, :, None, :]
    bt_im = bb_im.transpose(0, 2, 1)[:, :, None, :]
    e_re = rev_re[:, None] * bt_re - rev_im[:, None] * bt_im
    e_im = rev_re[:, None] * bt_im + rev_im[:, None] * bt_re
    e_mat = jnp.concatenate([e_re, e_im], axis=-1).reshape(G, H * L, 2 * P)

    f_re = cp_re[:, 1:].transpose(0, 3, 2, 1)
    f_im = -cp_im[:, 1:].transpose(0, 3, 2, 1)
    f_mat = jnp.concatenate([f_re, f_im], axis=1).reshape(G, 2 * P, H * L)

    n_steps = max(1, (n_chunks - 1).bit_length())
    sr, si = a_power(L * (2 ** jnp.arange(n_steps)))
    a_pow = jnp.stack([jnp.concatenate([sr, sr], axis=-1),
                       jnp.concatenate([-si, si], axis=-1)], axis=2)
    return kern, e_mat.astype(BF16), f_mat.astype(BF16), a_pow


def _s5_kernel(u_ref, k_ref, e_ref, f_ref, a_ref, after_ref, y_ref, toep_ref, *, n_chunks, n_steps, chunk):
    del after_ref
    L, H = chunk, SSM_GROUP_CH

    causal = lax.broadcasted_iota(I32, (L, L), 1) >= lax.broadcasted_iota(I32, (L, L), 0)

    def build(hin, carry):
        r0 = pl.multiple_of(hin * L, L)
        for hout in range(H):
            krow = k_ref[0, pl.ds(hin * H + hout, 1), :]
            blk = pltpu.roll(jnp.broadcast_to(krow, (L, L)), 0, axis=1, stride=1, stride_axis=0)
            toep_ref[pl.ds(r0, L), hout * L:(hout + 1) * L] = jnp.where(causal, blk, 0.0).astype(BF16)
        return carry

    lax.fori_loop(0, H, build, 0)

    bsz = u_ref.shape[0]
    u = jnp.concatenate([u_ref[:, h].reshape(bsz * n_chunks, L) for h in range(H)], axis=1)
    st = jnp.dot(u, e_ref[0], preferred_element_type=F32)
    cidx = lax.broadcasted_iota(I32, st.shape, 0) % n_chunks
    p2 = st.shape[1]

    def shifted(xv, d):
        return jnp.where(cidx >= d, pltpu.roll(xv, d, axis=0), 0.0)

    for k in range(n_steps):
        d = 1 << k
        if d >= n_chunks:
            break
        z = shifted(st, d)
        st = st + a_ref[0, k, 0:1, :] * z + a_ref[0, k, 1:2, :] * pltpu.roll(z, p2 // 2, axis=1)
    s_in = shifted(st, 1)
    y = jnp.dot(u, toep_ref[...], preferred_element_type=F32)
    y = (y + jnp.dot(s_in.astype(BF16), f_ref[0], preferred_element_type=F32)).astype(BF16)
    for h in range(H):
        y_ref[:, h] = y[:, h * L:(h + 1) * L].reshape(bsz, n_chunks, L)


def _s5_core(ut5, kern, e_mat, f_mat, a_pow, after):
    bsz, g, h, n_chunks, chunk = ut5.shape
    hl = h * chunk
    n_steps = a_pow.shape[1]
    p2 = e_mat.shape[2]
    blk = lambda i: (i, 0, 0)
    seq_blk = pl.BlockSpec((bsz, None, h, n_chunks, chunk), lambda i: (0, i, 0, 0, 0))
    return pl.pallas_call(
        functools.partial(_s5_kernel, n_chunks=n_chunks, n_steps=n_steps, chunk=chunk),
        grid=(g,),
        in_specs=[seq_blk,
                  pl.BlockSpec((1,) + kern.shape[1:], blk),
                  pl.BlockSpec((1, hl, p2), blk),
                  pl.BlockSpec((1, p2, hl), blk),
                  pl.BlockSpec((1, n_steps, 2, p2), lambda i: (i, 0, 0, 0)),
                  pl.BlockSpec(memory_space=pl.ANY)],
        out_specs=seq_blk,
        out_shape=jax.ShapeDtypeStruct(ut5.shape, BF16),
        scratch_shapes=[pltpu.VMEM((hl, hl), BF16)],
        compiler_params=_params("arbitrary"),
        name="s5_core",
    )(ut5, kern, e_mat, f_mat, a_pow, after)


def _route(scores, biased, tri, base):
    n_tok = scores.shape[1]
    shape3 = (EXPERTS_PER_GROUP, N_EXPERT_GROUPS, n_tok)
    sc3 = scores.reshape(shape3)
    b3 = biased.reshape(shape3)
    j_iota = lax.broadcasted_iota(I32, shape3, 0)
    e_iota = lax.broadcasted_iota(I32, shape3, 1) * EXPERTS_PER_GROUP + j_iota

    def red(fn, x):
        return fn(fn(x, axis=0, keepdims=True), axis=1, keepdims=True)

    m1 = jnp.max(b3, axis=0, keepdims=True)
    i1 = jnp.min(jnp.where(b3 == m1, j_iota, EXPERTS_PER_GROUP), axis=0, keepdims=True)
    m2 = jnp.max(jnp.where(j_iota == i1, NEG_INF, b3), axis=0, keepdims=True)
    gs = m1 + m2

    g_iota = lax.broadcasted_iota(I32, gs.shape, 1)
    gsel = jnp.zeros(gs.shape, F32)
    cur = gs
    for _ in range(TOPK_GROUPS):
        m = jnp.max(cur, axis=1, keepdims=True)
        ig = jnp.min(jnp.where(cur == m, g_iota, N_EXPERT_GROUPS), axis=1, keepdims=True)
        pick = g_iota == ig
        gsel = jnp.where(pick, 1.0, gsel)
        cur = jnp.where(pick, NEG_INF, cur)

    cur = jnp.where(gsel > 0.0, b3, NEG_INF)
    sel = jnp.zeros(shape3, F32)
    ids, vals = [], []
    for _ in range(TOP_K):
        m = red(jnp.max, cur)
        ie = red(jnp.min, jnp.where(cur == m, e_iota, N_EXPERTS))
        pick = e_iota == ie
        ids.append(ie)
        vals.append(red(jnp.sum, jnp.where(pick, sc3, 0.0)))
        sel = jnp.where(pick, 1.0, sel)
        cur = jnp.where(pick, NEG_INF, cur)
    tot = functools.reduce(lambda a, b: a + b, vals)

    sel2 = sel.reshape(N_EXPERTS, n_tok)
    before = jnp.dot(sel2.astype(BF16), tri, preferred_element_type=F32) + base
    before3 = before.reshape(shape3)
    ranks = [red(jnp.sum, jnp.where(e_iota == ie, before3, 0.0)) for ie in ids]

    def rows(parts, dtype):
        parts = [p.reshape(1, n_tok).astype(dtype) for p in parts]
        return jnp.concatenate(parts + [jnp.zeros((8 - len(parts), n_tok), dtype)], axis=0)

    e8 = rows(ids, I32)
    w8 = rows([v / tot * ROUTED_SCALE for v in vals], F32)
    r8 = rows(ranks, I32)
    return e8, w8, r8, jnp.sum(sel2, axis=1, keepdims=True)


def _mix_back_kernel(x_ref, yconv_ref, yt_ref, gate1_ref, shift2_ref, scale2_ref, gate2_ref,
                     wglu_ref, bglu_ref, gssm_ref, woc_ref, wos_ref, gffn_ref,
                     wrh_ref, wrl_ref, rbias_ref, ws13_ref, ws2_ref, tri_ref,
                     x1_ref, h2p_ref, e8_ref, w8_ref, r8_ref, cnt_ref):
    first = jnp.logical_and(pl.program_id(0) == 0, pl.program_id(1) == 0)

    @pl.when(first)
    def _():
        cnt_ref[...] = jnp.zeros_like(cnt_ref)

    sub = tri_ref.shape[0]

    def ssm_post(st):
        r0 = st["r0"]
        y = yt_ref[0, :, r0:r0 + sub].astype(F32).T
        y = _gelu_tanh(y)
        y = y * jax.nn.sigmoid(jnp.dot(y.astype(BF16), wglu_ref[...], preferred_element_type=F32)
                               + bglu_ref[...])
        st["y_ssm"] = _rms(y, gssm_ref[...]).astype(BF16)

    def out_proj(st):
        r0 = st["r0"]
        mix = jnp.dot(yconv_ref[0, r0:r0 + sub, :], woc_ref[...], preferred_element_type=F32)
        mix = mix + jnp.dot(st.pop("y_ssm"), wos_ref[...], preferred_element_type=F32)
        st["x1"] = x_ref[0, r0:r0 + sub, :] + gate1_ref[0] * mix

    def ffn_in(st):
        r0 = st["r0"]
        h2 = _rms(st["x1"], gffn_ref[...] * (1.0 + scale2_ref[0])) + shift2_ref[0]
        h2b = h2.astype(BF16)
        h2r = h2b.astype(F32)
        h2p_ref[0, r0:r0 + sub, :] = lax.bitcast_convert_type(_pack_rounded(h2r), I32)
        st["h2b"] = h2b
        st["h2l"] = (h2 - h2r).astype(BF16)

    def shared_expert(st):
        r0 = st["r0"]
        a = jnp.dot(st["h2b"], ws13_ref[...], preferred_element_type=F32)
        f = a.shape[1] // 2
        act = _silu(a[:, :f]) * a[:, f:]
        shared = jnp.dot(act.astype(BF16), ws2_ref[...], preferred_element_type=F32)
        x1_ref[0, r0:r0 + sub, :] = (st.pop("x1") + gate2_ref[0] * shared).astype(x1_ref.dtype)

    def router(st):
        lt = (jnp.dot(st.pop("h2b"), wrh_ref[...], preferred_element_type=F32)
              + jnp.dot(st.pop("h2l"), wrl_ref[...], preferred_element_type=F32)).T
        st["scores"] = jax.nn.sigmoid(lt[:N_EXPERTS] + lt[N_EXPERTS:])

    def route(st):
        r0 = st["r0"]
        scores = st.pop("scores")
        e8, w8, r8, cnt = _route(scores, scores + rbias_ref[...], tri_ref[...], carry["base"])
        e8_ref[:, r0:r0 + sub] = e8
        w8_ref[0, r0:r0 + sub, :] = w8.T
        r8_ref[:, r0:r0 + sub] = r8
        carry["base"] = carry["base"] + cnt

    carry = {"base": cnt_ref[:, 0:1]}
    phases = (ssm_post, out_proj, ffn_in, shared_expert, router, route)
    chains = [{"r0": r0} for r0 in range(0, x_ref.shape[1], sub)]
    lag = MIX_PHASE_LAG
    for step in range(len(phases) + lag * (len(chains) - 1)):
        for ci, st in enumerate(chains):
            ph = step - lag * ci
            if 0 <= ph < len(phases):
                phases[ph](st)
    cnt_ref[...] = jnp.broadcast_to(carry["base"], cnt_ref.shape)


def _mix_back(x, yconv, yt, gate1, shift2, scale2, gate2, w_glu_bf, b_glu, g_ssm, wo_conv, wo_ssm,
              g_ffn, wr_hi, wr_lo, router_bias, ws13, ws2_bf, ts, b0, bsz):
    _, seq, d = x.shape
    n_tok = bsz * seq
    tiles = seq // ts
    row = lambda b, s: (b + b0, 0, 0)
    const2 = lambda b, s: (0, 0)
    tile_in = lambda b, s: (b + b0, s, 0)
    tile = lambda b, s: (b, s, 0)
    flat = lambda b, s: (0, b * tiles + s)
    full = lambda a: pl.BlockSpec(a.shape, const2)
    sub = min(MIX_SUB, ts)
    tri = jnp.triu(jnp.ones((sub, sub), BF16), k=1)
    args = (w_glu_bf, b_glu.reshape(1, D_SSM), g_ssm.reshape(1, D_SSM), wo_conv, wo_ssm,
            g_ffn.reshape(1, d), wr_hi, wr_lo, router_bias.reshape(N_EXPERTS, 1), ws13, ws2_bf, tri)
    return pl.pallas_call(
        _mix_back_kernel,
        grid=(bsz, tiles),
        in_specs=[pl.BlockSpec((1, ts, d), tile_in),
                  pl.BlockSpec((1, ts, D_CONV), tile),
                  pl.BlockSpec((1, D_SSM, ts), lambda b, s: (b, 0, s)),
                  pl.BlockSpec((1, 1, d), row), pl.BlockSpec((1, 1, d), row),
                  pl.BlockSpec((1, 1, d), row), pl.BlockSpec((1, 1, d), row)]
                 + [full(a) for a in args],
        out_specs=[pl.BlockSpec((1, ts, d), tile),
                   pl.BlockSpec((1, ts, d // 2), tile),
                   pl.BlockSpec((8, ts), flat),
                   pl.BlockSpec((1, ts, 8), tile),
                   pl.BlockSpec((8, ts), flat),
                   pl.BlockSpec((N_EXPERTS, 128), const2)],
        out_shape=[jax.ShapeDtypeStruct((bsz, seq, d), RESIDUAL_DTYPE),
                   jax.ShapeDtypeStruct((bsz, seq, d // 2), I32),
                   jax.ShapeDtypeStruct((8, n_tok), I32),
                   jax.ShapeDtypeStruct((bsz, seq, 8), F32),
                   jax.ShapeDtypeStruct((8, n_tok), I32),
                   jax.ShapeDtypeStruct((N_EXPERTS, 128), F32)],
        compiler_params=_params("arbitrary", "arbitrary"),
        name="mix_back",
    )(x, yconv, yt, gate1, shift2, scale2, gate2, *args)


def _plan_kernel(cnt_ref, e8_ref, r8_ref, pos_ref, *, bm):
    cnt = cnt_ref[...].astype(I32)
    padded = (cnt + (bm - 1)) // bm * bm
    rid = lax.broadcasted_iota(I32, padded.shape, 0)
    incl = padded
    d = 1
    while d < N_EXPERTS:
        incl = incl + jnp.where(rid >= d, pltpu.roll(incl, d, axis=0), 0)
        d *= 2
    pstart = incl - padded
    lanes = pstart.shape[1]

    def chunk(ci, carry):
        c0 = pl.multiple_of(ci * lanes, lanes)
        e = e8_ref[:, pl.ds(c0, lanes)]
        acc = r8_ref[:, pl.ds(c0, lanes)]
        for ex in range(N_EXPERTS):
            acc = acc + jnp.where(e == ex, pstart[ex:ex + 1, :], 0)
        pos_ref[:, pl.ds(c0, lanes)] = acc
        return carry

    lax.fori_loop(0, e8_ref.shape[1] // lanes, chunk, 0)


def _plan(counts, e8, r8, bm):
    n_tok = e8.shape[1]
    tb = min(8192, n_tok)
    return pl.pallas_call(
        functools.partial(_plan_kernel, bm=bm),
        grid=(n_tok // tb,),
        in_specs=[pl.BlockSpec(counts.shape, lambda i: (0, 0)),
                  pl.BlockSpec((8, tb), lambda i: (0, i)),
                  pl.BlockSpec((8, tb), lambda i: (0, i))],
        out_specs=pl.BlockSpec((8, tb), lambda i: (0, i)),
        out_shape=jax.ShapeDtypeStruct((8, n_tok), I32),
        compiler_params=_params("arbitrary"),
        name="plan",
    )(counts, e8, r8)


def _sc_mesh():
    return plsc.VectorSubcoreMesh(core_axis_name="c", subcore_axis_name="s",
                                  num_cores=V7X_SC_CORES, num_subcores=V7X_SC_SUBCORES)


def _sc_worker_base(per_worker):
    return (lax.axis_index("s") * V7X_SC_CORES + lax.axis_index("c")) * per_worker


def _sc_dispatch(h2p, posflat, n_rows):
    n_tok, half = h2p.shape
    win = SC_WINDOW
    per_worker = n_tok // SC_WORKERS
    n_win = per_worker // win

    wins_per_k = n_tok // win

    @functools.partial(
        pl.kernel, mesh=_sc_mesh(),
        out_type=jax.ShapeDtypeStruct((n_rows, half), I32),
        scratch_types=[pltpu.VMEM((TOP_K, n_win, win), I32), pltpu.VMEM((2, win, half), I32),
                       pltpu.SemaphoreType.DMA((2,)), pltpu.SemaphoreType.DMA((2,))],
        name="sc_dispatch")
    def run(h2p_hbm, pos_hbm, xs_hbm, idx_v, rows_v, sem_in, sem_out):
        wid = _sc_worker_base(1)
        for k in range(TOP_K):
            pltpu.sync_copy(pos_hbm.at[pl.ds(k * wins_per_k + wid * n_win, n_win)], idx_v.at[k])

        @pl.loop(0, n_win, step=2)
        def _(j):
            loads = []
            for b in range(2):
                t0 = pl.multiple_of((wid * n_win + j + b) * win, win)
                loads.append(pltpu.async_copy(h2p_hbm.at[pl.ds(t0, win)], rows_v.at[b], sem_in.at[b]))
            stores = []
            for b in range(2):
                loads[b].wait()
                for k in range(TOP_K):
                    stores.append(pltpu.async_copy(rows_v.at[b], xs_hbm.at[idx_v.at[k, j + b]], sem_out.at[b]))
            for cp in stores:
                cp.wait()

    return run(h2p, posflat.reshape(TOP_K * wins_per_k, win))


def _sc_gather(ys, posflat):
    n_idx = posflat.shape[0]
    half = ys.shape[1]
    win = SC_WINDOW
    per_worker = n_idx // SC_WORKERS
    n_win = per_worker // win

    @functools.partial(
        pl.kernel, mesh=_sc_mesh(),
        out_type=jax.ShapeDtypeStruct((n_idx, half), I32),
        scratch_types=[pltpu.VMEM((n_win, win), I32), pltpu.VMEM((2, win, half), I32),
                       pltpu.SemaphoreType.DMA((2,)), pltpu.SemaphoreType.DMA((2,))],
        name="sc_gather")
    def run(ys_hbm, pos_hbm, out_hbm, idx_v, rows_v, sem_in, sem_out):
        wid = _sc_worker_base(1)
        pltpu.sync_copy(pos_hbm.at[pl.ds(wid * n_win, n_win)], idx_v)

        @pl.loop(0, n_win, step=2)
        def _(j):
            gathers = [pltpu.async_copy(ys_hbm.at[idx_v.at[j + b]], rows_v.at[b], sem_in.at[b])
                       for b in range(2)]
            writes = []
            for b in range(2):
                gathers[b].wait()
                off = pl.multiple_of((wid * n_win + j + b) * win, win)
                writes.append(pltpu.async_copy(rows_v.at[b], out_hbm.at[pl.ds(off, win)], sem_out.at[b]))
            for cp in writes:
                cp.wait()

    return run(ys, posflat.reshape(n_idx // win, win))


def _experts_kernel(be_ref, nu_ref, x_ref, w1_ref, w3_ref, w2_ref, after_ref, y_ref, w13_s, w2_s):
    del after_ref
    i = pl.program_id(0)
    changed = jnp.logical_or(i == 0, be_ref[i] != be_ref[jnp.maximum(i - 1, 0)])

    @pl.when(changed)
    def _():
        f = w1_ref.shape[2]
        w13_s[:, :f] = w1_ref[0].astype(BF16)
        w13_s[:, f:] = w3_ref[0].astype(BF16)
        w2_s[...] = w2_ref[0].astype(BF16)

    @pl.when(i < nu_ref[0])
    def _():
        half = x_ref.shape[1]
        sub = min(EXPERT_SUB, x_ref.shape[0])
        for r0 in range(0, x_ref.shape[0], sub):
            lo, hi = _unpack_halves(lax.bitcast_convert_type(x_ref[r0:r0 + sub, :], U32))
            a = jnp.dot(lo.astype(BF16), w13_s[:half, :], preferred_element_type=F32)
            a = a + jnp.dot(hi.astype(BF16), w13_s[half:, :], preferred_element_type=F32)
            f = a.shape[1] // 2
            act = _silu(a[:, :f]) * a[:, f:]
            y = jnp.dot(act.astype(BF16), w2_s[...], preferred_element_type=F32)
            y_ref[r0:r0 + sub, :] = lax.bitcast_convert_type(_pack_halves(y), I32)


def _experts(block_e, n_used, xs, w1, w3, w2, bm, after):
    n_rows, half = xs.shape
    d, f = w1.shape[1], w1.shape[2]
    rows = lambda i, be, nu: (jnp.minimum(i, nu[0] - 1), 0)
    wblk = lambda i, be, nu: (be[i], 0, 0)
    grid_spec = pltpu.PrefetchScalarGridSpec(
        num_scalar_prefetch=2,
        grid=(n_rows // bm,),
        in_specs=[pl.BlockSpec((bm, half), rows),
                  pl.BlockSpec((1, d, f), wblk),
                  pl.BlockSpec((1, d, f), wblk),
                  pl.BlockSpec((1, f, d), wblk),
                  pl.BlockSpec(memory_space=pl.ANY)],
        out_specs=pl.BlockSpec((bm, half), rows),
        scratch_shapes=[pltpu.VMEM((d, 2 * f), BF16), pltpu.VMEM((f, d), BF16)],
    )
    return pl.pallas_call(
        _experts_kernel,
        grid_spec=grid_spec,
        out_shape=jax.ShapeDtypeStruct((n_rows, half), I32),
        compiler_params=_params("arbitrary"),
        name="experts",
    )(block_e, n_used, xs, w1, w3, w2, after)


def _combine_kernel(g_ref, w8_ref, x1_ref, gate2_ref, gfin_ref, *rest):
    o_ref = rest[-1]
    w8 = w8_ref[...]
    acc_lo = acc_hi = None
    for k in range(TOP_K):
        lo, hi = _unpack_halves(lax.bitcast_convert_type(g_ref[k], U32))
        wk = w8[:, k:k + 1]
        acc_lo = wk * lo if acc_lo is None else acc_lo + wk * lo
        acc_hi = wk * hi if acc_hi is None else acc_hi + wk * hi
    routed = jnp.concatenate([acc_lo, acc_hi], axis=1)
    o_ref[...] = _rms(x1_ref[...].astype(F32) + gate2_ref[0] * routed, gfin_ref[...])


def _combine(g6, w8, x1, gate2, g_final, tt, tiles_per_seq, b_src, b0, n_tok_all, out_prev):
    d = x1.shape[1]
    n_tok, half = g6.shape[1:]
    src0 = b_src * tiles_per_seq
    tok = lambda i: (i + src0, 0)
    tile0 = b0 * tiles_per_seq
    in_specs = [pl.BlockSpec((TOP_K, tt, half), lambda i: (0, i, 0)),
                pl.BlockSpec((tt, 8), tok),
                pl.BlockSpec((tt, d), tok),
                pl.BlockSpec((1, 1, d), lambda i: (i // tiles_per_seq + b0, 0, 0)),
                pl.BlockSpec((1, d), lambda i: (0, 0))]
    args = [g6, w8, x1, gate2, g_final.reshape(1, d)]
    aliases = {}
    if out_prev is not None:
        in_specs.append(pl.BlockSpec(memory_space=pl.ANY))
        args.append(out_prev)
        aliases = {len(args) - 1: 0}
    return pl.pallas_call(
        _combine_kernel,
        grid=(n_tok // tt,),
        in_specs=in_specs,
        out_specs=pl.BlockSpec((tt, d), lambda i: (i + tile0, 0)),
        out_shape=jax.ShapeDtypeStruct((n_tok_all, d), F32),
        input_output_aliases=aliases,
        compiler_params=_params("arbitrary"),
        name="combine",
    )(*args)


def _block_experts(counts, bm, n_blocks):
    padded = (counts.astype(I32) + (bm - 1)) // bm * bm
    pend = jnp.cumsum(padded)
    starts = jnp.arange(n_blocks, dtype=I32) * bm
    block_e = jnp.minimum(jnp.sum((pend[None, :] <= starts[:, None]).astype(I32), axis=1), N_EXPERTS - 1)
    return block_e, (pend[-1:] // bm).astype(I32)


def kernel(x, c, w_ada, b_ada, g_mix, w_in, conv_w, conv_b, g_conv, lam_re, lam_im, log_dt, b_re, b_im,
           c_re, c_im, d_skip, w_glu, b_glu, g_ssm, w_out, g_ffn, w_router, router_bias, w1, w3, w2,
           ws1, ws3, ws2, g_final):
    bsz, seq, d = x.shape
    n_tok = bsz * seq
    ts = min(MIX_BACK_TOKENS, seq)
    L = min(S5_CHUNK, seq)
    n_chunks = seq // L
    G, H = SSM_GROUPS, SSM_GROUP_CH

    mod = _adaln(c, w_ada, b_ada).reshape(bsz, 6, 1, d)
    shift1, scale1, gate1, shift2, scale2, gate2 = (mod[:, i] for i in range(6))

    kern, e_mat, f_mat, a_pow = _s5_tables(lam_re, lam_im, log_dt, b_re, b_im, c_re, c_im, d_skip,
                                           L, n_chunks)
    w_in_bf = w_in.astype(BF16)

    NG, NJ = N_EXPERT_GROUPS, EXPERTS_PER_GROUP
    w_r = w_router.astype(F32).reshape(d, NG, NJ).transpose(0, 2, 1).reshape(d, N_EXPERTS)
    r_bias = router_bias.reshape(NG, NJ).T.reshape(N_EXPERTS)
    w_r_hi = w_r.astype(BF16)
    wr_hi = jnp.concatenate([w_r_hi, (w_r - w_r_hi.astype(F32)).astype(BF16)], axis=1)
    wr_lo = jnp.concatenate([w_r_hi, jnp.zeros_like(w_r_hi)], axis=1)
    w_out_bf = w_out.astype(BF16)
    w_glu_bf = w_glu.astype(BF16)
    ws13 = jnp.concatenate([ws1, ws3], axis=1).astype(BF16)
    ws2_bf = ws2.astype(BF16)
    bm = EXPERT_ROWS
    tt = min(COMBINE_TOKENS, seq)

    n_parts = LAYER_PARTS if bsz % LAYER_PARTS == 0 else 1
    pb = bsz // n_parts
    pt = pb * seq
    n_blocks = -(-(pt * TOP_K + N_EXPERTS * (bm - 1)) // bm)
    none = jnp.zeros((8, 128), F32)

    def run_experts(part, after):
        return _experts(*part["block_e"], part["xs"], w1, w3, w2, bm, after)

    def finish(part, ys, out, n_split):
        nb = pb // n_split
        for h in range(n_split):
            pos = part["pos8"][:TOP_K, h * nb * seq:(h + 1) * nb * seq].reshape(TOP_K * nb * seq)
            g6 = _sc_gather(ys, pos).reshape(TOP_K, nb * seq, d // 2)
            out = _combine(g6, part["w8"].reshape(pt, 8), part["x1"].reshape(pt, d), gate2, g_final, tt,
                           seq // tt, h * nb, part["b0"] + h * nb, n_tok, out)
        return out

    out, prev, after_front = None, None, none
    for p in range(n_parts):
        b0 = p * pb
        yconv, ut = _mix_front(x, shift1, scale1, g_mix, w_in_bf, conv_w, conv_b, g_conv,
                               min(MIX_FRONT_TOKENS, seq), b0, pb, after_front)
        ys_prev = run_experts(prev, ut) if prev is not None else None
        yt = _s5_core(ut.reshape(pb, G, H, n_chunks, L), kern, e_mat, f_mat, a_pow,
                      none if ys_prev is None else ys_prev).reshape(pb, G * H, seq)
        x1, h2p, e8, w8, r8, counts = _mix_back(
            x, yconv, yt, gate1, shift2, scale2, gate2, w_glu_bf, b_glu, g_ssm,
            w_out_bf[:D_CONV], w_out_bf[D_CONV:], g_ffn, wr_hi, wr_lo, r_bias, ws13, ws2_bf,
            ts, b0, pb)
        counts = counts.reshape(NJ, NG, -1).transpose(1, 0, 2).reshape(N_EXPERTS, -1)
        pos8 = _plan(counts, e8, r8, bm)
        xs = _sc_dispatch(h2p.reshape(pt, d // 2), pos8[:TOP_K].reshape(TOP_K * pt), n_blocks * bm)
        if prev is not None:
            out = finish(prev, ys_prev, out, 1)
        prev = dict(b0=b0, x1=x1, w8=w8, pos8=pos8, xs=xs,
                    block_e=_block_experts(counts[:, 0], bm, n_blocks))
        after_front = pos8
    ys_last = run_experts(prev, none if out is None else out)
    out = finish(prev, ys_last, out, LAST_PART_SPLIT if pb % LAST_PART_SPLIT == 0 else 1)
    return out.reshape(bsz, seq, d)
```

```python
import functools
import math

import jax
import jax.numpy as jnp
import numpy as np
from jax import lax
from jax.experimental import pallas as pl
from jax.experimental.pallas import tpu as pltpu
from jax.experimental.pallas import tpu_sc as plsc

F32 = jnp.float32
BF16 = jnp.bfloat16
U32 = jnp.uint32
I32 = jnp.int32

D_CONV = 768
D_SSM = 256
SSM_GROUPS = 16
SSM_GROUP_CH = 16
SSM_STATE = 64
N_EXPERTS = 64
TOP_K = 6
N_EXPERT_GROUPS = 8
TOPK_GROUPS = 4
EXPERTS_PER_GROUP = 8
ROUTED_SCALE = 2.5
RMS_EPS = 1e-6

S5_CHUNK = 128
EXPERT_ROWS = 1024
EXPERT_SUB = 512
MIX_FRONT_TOKENS = 1024
MIX_FRONT_SUB = 1024
MIX_SUB = 256
MIX_BACK_TOKENS = 1024
MIX_PHASE_LAG = 3
COMBINE_TOKENS = 512
LAYER_PARTS = 2
LAST_PART_SPLIT = 4
RESIDUAL_DTYPE = BF16
V7X_SC_CORES = 2
V7X_SC_SUBCORES = 16
V7X_SC_LANES = 16
SC_WORKERS = V7X_SC_CORES * V7X_SC_SUBCORES
SC_COMBINE_TOKENS = 16
SC_WINDOW = 64
V7X_VMEM_LIMIT = 56 * 1024 * 1024
NEG_INF = float("-inf")
HIGH_HALF = np.uint32(0xFFFF0000)


def _rms(x, g):
    return x * lax.rsqrt(jnp.mean(x * x, axis=-1, keepdims=True) + RMS_EPS) * g


def _gelu_tanh(x):
    return 0.5 * x * (1.0 + jnp.tanh(math.sqrt(2.0 / math.pi) * (x + 0.044715 * (x * x * x))))


def _silu(x):
    return x * jax.nn.sigmoid(x)


def _params(*sem):
    return pltpu.CompilerParams(dimension_semantics=sem, vmem_limit_bytes=V7X_VMEM_LIMIT)


def _pack_halves(y):
    return _pack_rounded(y.astype(BF16).astype(F32))


def _pack_rounded(y):
    m = y.shape[1] // 2
    bits = lax.bitcast_convert_type(y, U32)
    return (bits[:, m:] & HIGH_HALF) | (bits[:, :m] >> 16)


def _unpack_halves(w):
    lo = lax.bitcast_convert_type(w << 16, F32)
    hi = lax.bitcast_convert_type(w & HIGH_HALF, F32)
    return lo, hi


def _adaln_kernel(c_ref, w_ref, b_ref, o_ref):
    o_ref[...] = jnp.dot(_silu(c_ref[...]), w_ref[...], preferred_element_type=F32,
                         precision=lax.Precision.HIGHEST) + b_ref[...]


def _adaln(c, w_ada, b_ada):
    bsz, d = c.shape
    n = w_ada.shape[1]
    bn = 1024
    return pl.pallas_call(
        _adaln_kernel,
        grid=(n // bn,),
        in_specs=[pl.BlockSpec((bsz, d), lambda j: (0, 0)),
                  pl.BlockSpec((d, bn), lambda j: (0, j)),
                  pl.BlockSpec((1, bn), lambda j: (0, j))],
        out_specs=pl.BlockSpec((bsz, bn), lambda j: (0, j)),
        out_shape=jax.ShapeDtypeStruct((bsz, n), F32),
        compiler_params=_params("arbitrary"),
        name="adaln",
    )(c, w_ada, b_ada.reshape(1, n))


def _mix_front_kernel(x_ref, shift_ref, scale_ref, gmix_ref, win_ref, cw_ref, cb_ref, gconv_ref, after_ref,
                      yconv_ref, ut_ref, zprev_ref):
    del after_ref
    s = pl.program_id(1)

    @pl.when(s == 0)
    def _():
        zprev_ref[...] = jnp.zeros_like(zprev_ref)

    cw = cw_ref[...]
    sub = min(MIX_FRONT_SUB, x_ref.shape[1])
    carry = {"prev": zprev_ref[...]}

    def norm(st):
        x = x_ref[0, st["r0"]:st["r0"] + sub, :]
        st["h"] = (_rms(x, gmix_ref[...]) * (1.0 + scale_ref[0]) + shift_ref[0]).astype(BF16)

    def in_proj(st):
        st["proj"] = jnp.dot(st.pop("h"), win_ref[...], preferred_element_type=F32)

    def mixers(st):
        r0 = st["r0"]
        proj = st.pop("proj")
        b_gate = proj[:, :D_CONV]
        c_gate = proj[:, D_CONV:2 * D_CONV]
        v = proj[:, 2 * D_CONV:3 * D_CONV]
        u = proj[:, 3 * D_CONV:]
        z = c_gate * v
        prev = carry["prev"]
        rid = lax.broadcasted_iota(I32, z.shape, 0)
        z1 = jnp.where(rid == 0, prev[7:8, :], pltpu.roll(z, 1, axis=0))
        z2 = jnp.where(rid == 0, prev[6:7, :], jnp.where(rid == 1, prev[7:8, :], pltpu.roll(z, 2, axis=0)))
        carry["prev"] = z[sub - 8:, :]
        conv = cw[0:1, :] * z2 + cw[1:2, :] * z1 + cw[2:3, :] * z + cb_ref[...]
        yconv_ref[0, r0:r0 + sub, :] = _rms(b_gate * conv, gconv_ref[...]).astype(BF16)
        ut_ref[0, :, r0:r0 + sub] = u.T.astype(BF16)

    phases = (norm, in_proj, mixers)
    chains = [{"r0": r0} for r0 in range(0, x_ref.shape[1], sub)]
    for step in range(len(phases) + len(chains) - 1):
        for ci, st in enumerate(chains):
            if 0 <= step - ci < len(phases):
                phases[step - ci](st)
    zprev_ref[...] = carry["prev"]


def _mix_front(x, shift1, scale1, g_mix, w_in_bf, conv_w, conv_b, g_conv, ts, b0, bsz, after):
    _, seq, d = x.shape
    d_in = w_in_bf.shape[1]
    row = lambda b, s: (b + b0, 0, 0)
    const2 = lambda b, s: (0, 0)
    return pl.pallas_call(
        _mix_front_kernel,
        grid=(bsz, seq // ts),
        in_specs=[pl.BlockSpec((1, ts, d), lambda b, s: (b + b0, s, 0)),
                  pl.BlockSpec((1, 1, d), row),
                  pl.BlockSpec((1, 1, d), row),
                  pl.BlockSpec((1, d), const2),
                  pl.BlockSpec((d, d_in), const2),
                  pl.BlockSpec((8, D_CONV), const2),
                  pl.BlockSpec((1, D_CONV), const2),
                  pl.BlockSpec((1, D_CONV), const2),
                  pl.BlockSpec(memory_space=pl.ANY)],
        out_specs=[pl.BlockSpec((1, ts, D_CONV), lambda b, s: (b, s, 0)),
                   pl.BlockSpec((1, D_SSM, ts), lambda b, s: (b, 0, s))],
        out_shape=[jax.ShapeDtypeStruct((bsz, seq, D_CONV), BF16),
                   jax.ShapeDtypeStruct((bsz, D_SSM, seq), BF16)],
        scratch_shapes=[pltpu.VMEM((8, D_CONV), F32)],
        compiler_params=_params("arbitrary", "arbitrary"),
        name="mix_front",
    )(x, shift1, scale1, g_mix.reshape(1, d), w_in_bf,
      jnp.pad(conv_w, ((0, 8 - conv_w.shape[0]), (0, 0))), conv_b.reshape(1, D_CONV),
      g_conv.reshape(1, D_CONV), after)


def _s5_tables(lam_re, lam_im, log_dt, b_re, b_im, c_re, c_im, d_skip, chunk, n_chunks):
    hp = lax.Precision.HIGHEST
    G, H, P, L = SSM_GROUPS, SSM_GROUP_CH, SSM_STATE, chunk
    lr = lam_re.astype(F32)
    li = lam_im.astype(F32)
    dt = jnp.exp(log_dt.astype(F32))[:, None]
    mag = jnp.exp(lr * dt)
    ang = li * dt
    ab_re = mag * jnp.cos(ang)
    ab_im = mag * jnp.sin(ang)
    den = lr * lr + li * li
    nr = ab_re - 1.0
    ni = ab_im
    q_re = (nr * lr + ni * li) / den
    q_im = (ni * lr - nr * li) / den
    br = b_re.astype(F32)
    bi = b_im.astype(F32)
    bb_re = q_re[..., None] * br - q_im[..., None] * bi
    bb_im = q_re[..., None] * bi + q_im[..., None] * br

    def a_power(tau):
        t = tau.astype(F32)[None, :, None]
        m = jnp.exp(t * (lr * dt)[:, None, :])
        th = t * ang[:, None, :]
        return m * jnp.cos(th), m * jnp.sin(th)

    pw_re, pw_im = a_power(jnp.arange(L + 1))
    cr = c_re.astype(F32)[:, None]
    ci = c_im.astype(F32)[:, None]
    cp_re = cr * pw_re[:, :, None, :] - ci * pw_im[:, :, None, :]
    cp_im = cr * pw_im[:, :, None, :] + ci * pw_re[:, :, None, :]
    kern = (jnp.einsum('gthp,gpk->gthk', cp_re[:, :L], bb_re, precision=hp)
            - jnp.einsum('gthp,gpk->gthk', cp_im[:, :L], bb_im, precision=hp))
    kern = kern.at[:, 0].add(d_skip.astype(F32).reshape(G, H)[:, :, None] * jnp.eye(H, dtype=F32))
    kern = kern.transpose(0, 3, 2, 1).reshape(G, H * H, L)

    rev_re = pw_re[:, L - 1::-1]
    rev_im = pw_im[:, L - 1::-1]
    bt_re = bb_re.transpose(0, 2, 1)[:, :, None, :]
    bt_im = bb_im.transpose(0, 2, 1)[:, :, None, :]
    e_re = rev_re[:, None] * bt_re - rev_im[:, None] * bt_im
    e_im = rev_re[:, None] * bt_im + rev_im[:, None] * bt_re
    e_mat = jnp.concatenate([e_re, e_im], axis=-1).reshape(G, H * L, 2 * P)

    f_re = cp_re[:, 1:].transpose(0, 3, 2, 1)
    f_im = -cp_im[:, 1:].transpose(0, 3, 2, 1)
    f_mat = jnp.concatenate([f_re, f_im], axis=1).reshape(G, 2 * P, H * L)

    n_steps = max(1, (n_chunks - 1).bit_length())
    sr, si = a_power(L * (2 ** jnp.arange(n_steps)))
    a_pow = jnp.stack([jnp.concatenate([sr, sr], axis=-1),
                       jnp.concatenate([-si, si], axis=-1)], axis=2)
    return kern, e_mat.astype(BF16), f_mat.astype(BF16), a_pow


def _s5_kernel(u_ref, k_ref, e_ref, f_ref, a_ref, after_ref, y_ref, toep_ref, *, n_chunks, n_steps, chunk):
    del after_ref
    L, H = chunk, SSM_GROUP_CH

    causal = lax.broadcasted_iota(I32, (L, L), 1) >= lax.broadcasted_iota(I32, (L, L), 0)

    def build(hin, carry):
        r0 = pl.multiple_of(hin * L, L)
        for hout in range(H):
            krow = k_ref[0, pl.ds(hin * H + hout, 1), :]
            blk = pltpu.roll(jnp.broadcast_to(krow, (L, L)), 0, axis=1, stride=1, stride_axis=0)
            toep_ref[pl.ds(r0, L), hout * L:(hout + 1) * L] = jnp.where(causal, blk, 0.0).astype(BF16)
        return carry

    lax.fori_loop(0, H, build, 0)

    bsz = u_ref.shape[0]
    u = jnp.concatenate([u_ref[:, h].reshape(bsz * n_chunks, L) for h in range(H)], axis=1)
    st = jnp.dot(u, e_ref[0], preferred_element_type=F32)
    cidx = lax.broadcasted_iota(I32, st.shape, 0) % n_chunks
    p2 = st.shape[1]

    def shifted(xv, d):
        return jnp.where(cidx >= d, pltpu.roll(xv, d, axis=0), 0.0)

    for k in range(n_steps):
        d = 1 << k
        if d >= n_chunks:
            break
        z = shifted(st, d)
        st = st + a_ref[0, k, 0:1, :] * z + a_ref[0, k, 1:2, :] * pltpu.roll(z, p2 // 2, axis=1)
    s_in = shifted(st, 1)
    y = jnp.dot(u, toep_ref[...], preferred_element_type=F32)
    y = (y + jnp.dot(s_in.astype(BF16), f_ref[0], preferred_element_type=F32)).astype(BF16)
    for h in range(H):
        y_ref[:, h] = y[:, h * L:(h + 1) * L].reshape(bsz, n_chunks, L)


def _s5_core(ut5, kern, e_mat, f_mat, a_pow, after):
    bsz, g, h, n_chunks, chunk = ut5.shape
    hl = h * chunk
    n_steps = a_pow.shape[1]
    p2 = e_mat.shape[2]
    blk = lambda i: (i, 0, 0)
    seq_blk = pl.BlockSpec((bsz, None, h, n_chunks, chunk), lambda i: (0, i, 0, 0, 0))
    return pl.pallas_call(
        functools.partial(_s5_kernel, n_chunks=n_chunks, n_steps=n_steps, chunk=chunk),
        grid=(g,),
        in_specs=[seq_blk,
                  pl.BlockSpec((1,) + kern.shape[1:], blk),
                  pl.BlockSpec((1, hl, p2), blk),
                  pl.BlockSpec((1, p2, hl), blk),
                  pl.BlockSpec((1, n_steps, 2, p2), lambda i: (i, 0, 0, 0)),
                  pl.BlockSpec(memory_space=pl.ANY)],
        out_specs=seq_blk,
        out_shape=jax.ShapeDtypeStruct(ut5.shape, BF16),
        scratch_shapes=[pltpu.VMEM((hl, hl), BF16)],
        compiler_params=_params("arbitrary"),
        name="s5_core",
    )(ut5, kern, e_mat, f_mat, a_pow, after)


def _route(scores, biased, tri, base):
    n_tok = scores.shape[1]
    shape3 = (EXPERTS_PER_GROUP, N_EXPERT_GROUPS, n_tok)
    sc3 = scores.reshape(shape3)
    b3 = biased.reshape(shape3)
    j_iota = lax.broadcasted_iota(I32, shape3, 0)
    e_iota = lax.broadcasted_iota(I32, shape3, 1) * EXPERTS_PER_GROUP + j_iota

    def red(fn, x):
        return fn(fn(x, axis=0, keepdims=True), axis=1, keepdims=True)

    m1 = jnp.max(b3, axis=0, keepdims=True)
    i1 = jnp.min(jnp.where(b3 == m1, j_iota, EXPERTS_PER_GROUP), axis=0, keepdims=True)
    m2 = jnp.max(jnp.where(j_iota == i1, NEG_INF, b3), axis=0, keepdims=True)
    gs = m1 + m2

    g_iota = lax.broadcasted_iota(I32, gs.shape, 1)
    gsel = jnp.zeros(gs.shape, F32)
    cur = gs
    for _ in range(TOPK_GROUPS):
        m = jnp.max(cur, axis=1, keepdims=True)
        ig = jnp.min(jnp.where(cur == m, g_iota, N_EXPERT_GROUPS), axis=1, keepdims=True)
        pick = g_iota == ig
        gsel = jnp.where(pick, 1.0, gsel)
        cur = jnp.where(pick, NEG_INF, cur)

    cur = jnp.where(gsel > 0.0, b3, NEG_INF)
    sel = jnp.zeros(shape3, F32)
    ids, vals = [], []
    for _ in range(TOP_K):
        m = red(jnp.max, cur)
        ie = red(jnp.min, jnp.where(cur == m, e_iota, N_EXPERTS))
        pick = e_iota == ie
        ids.append(ie)
        vals.append(red(jnp.sum, jnp.where(pick, sc3, 0.0)))
        sel = jnp.where(pick, 1.0, sel)
        cur = jnp.where(pick, NEG_INF, cur)
    tot = functools.reduce(lambda a, b: a + b, vals)

    sel2 = sel.reshape(N_EXPERTS, n_tok)
    before = jnp.dot(sel2.astype(BF16), tri, preferred_element_type=F32) + base
    before3 = before.reshape(shape3)
    ranks = [red(jnp.sum, jnp.where(e_iota == ie, before3, 0.0)) for ie in ids]

    def rows(parts, dtype):
        parts = [p.reshape(1, n_tok).astype(dtype) for p in parts]
        return jnp.concatenate(parts + [jnp.zeros((8 - len(parts), n_tok), dtype)], axis=0)

    e8 = rows(ids, I32)
    w8 = rows([v / tot * ROUTED_SCALE for v in vals], F32)
    r8 = rows(ranks, I32)
    return e8, w8, r8, jnp.sum(sel2, axis=1, keepdims=True)


def _mix_back_kernel(x_ref, yconv_ref, yt_ref, gate1_ref, shift2_ref, scale2_ref, gate2_ref,
                     wglu_ref, bglu_ref, gssm_ref, woc_ref, wos_ref, gffn_ref,
                     wrh_ref, wrl_ref, rbias_ref, ws13_ref, ws2_ref, tri_ref,
                     x1_ref, h2p_ref, e8_ref, w8_ref, r8_ref, cnt_ref):
    first = jnp.logical_and(pl.program_id(0) == 0, pl.program_id(1) == 0)

    @pl.when(first)
    def _():
        cnt_ref[...] = jnp.zeros_like(cnt_ref)

    sub = tri_ref.shape[0]

    def ssm_post(st):
        r0 = st["r0"]
        y = yt_ref[0, :, r0:r0 + sub].astype(F32).T
        y = _gelu_tanh(y)
        y = y * jax.nn.sigmoid(jnp.dot(y.astype(BF16), wglu_ref[...], preferred_element_type=F32)
                               + bglu_ref[...])
        st["y_ssm"] = _rms(y, gssm_ref[...]).astype(BF16)

    def out_proj(st):
        r0 = st["r0"]
        mix = jnp.dot(yconv_ref[0, r0:r0 + sub, :], woc_ref[...], preferred_element_type=F32)
        mix = mix + jnp.dot(st.pop("y_ssm"), wos_ref[...], preferred_element_type=F32)
        st["x1"] = x_ref[0, r0:r0 + sub, :] + gate1_ref[0] * mix

    def ffn_in(st):
        r0 = st["r0"]
        h2 = _rms(st["x1"], gffn_ref[...] * (1.0 + scale2_ref[0])) + shift2_ref[0]
        h2b = h2.astype(BF16)
        h2r = h2b.astype(F32)
        h2p_ref[0, r0:r0 + sub, :] = lax.bitcast_convert_type(_pack_rounded(h2r), I32)
        st["h2b"] = h2b
        st["h2l"] = (h2 - h2r).astype(BF16)

    def shared_expert(st):
        r0 = st["r0"]
        a = jnp.dot(st["h2b"], ws13_ref[...], preferred_element_type=F32)
        f = a.shape[1] // 2
        act = _silu(a[:, :f]) * a[:, f:]
        shared = jnp.dot(act.astype(BF16), ws2_ref[...], preferred_element_type=F32)
        x1_ref[0, r0:r0 + sub, :] = (st.pop("x1") + gate2_ref[0] * shared).astype(x1_ref.dtype)

    def router(st):
        lt = (jnp.dot(st.pop("h2b"), wrh_ref[...], preferred_element_type=F32)
              + jnp.dot(st.pop("h2l"), wrl_ref[...], preferred_element_type=F32)).T
        st["scores"] = jax.nn.sigmoid(lt[:N_EXPERTS] + lt[N_EXPERTS:])

    def route(st):
        r0 = st["r0"]
        scores = st.pop("scores")
        e8, w8, r8, cnt = _route(scores, scores + rbias_ref[...], tri_ref[...], carry["base"])
        e8_ref[:, r0:r0 + sub] = e8
        w8_ref[0, r0:r0 + sub, :] = w8.T
        r8_ref[:, r0:r0 + sub] = r8
        carry["base"] = carry["base"] + cnt

    carry = {"base": cnt_ref[:, 0:1]}
    phases = (ssm_post, out_proj, ffn_in, shared_expert, router, route)
    chains = [{"r0": r0} for r0 in range(0, x_ref.shape[1], sub)]
    lag = MIX_PHASE_LAG
    for step in range(len(phases) + lag * (len(chains) - 1)):
        for ci, st in enumerate(chains):
            ph = step - lag * ci
            if 0 <= ph < len(phases):
                phases[ph](st)
    cnt_ref[...] = jnp.broadcast_to(carry["base"], cnt_ref.shape)


def _mix_back(x, yconv, yt, gate1, shift2, scale2, gate2, w_glu_bf, b_glu, g_ssm, wo_conv, wo_ssm,
              g_ffn, wr_hi, wr_lo, router_bias, ws13, ws2_bf, ts, b0, bsz):
    _, seq, d = x.shape
    n_tok = bsz * seq
    tiles = seq // ts
    row = lambda b, s: (b + b0, 0, 0)
    const2 = lambda b, s: (0, 0)
    tile_in = lambda b, s: (b + b0, s, 0)
    tile = lambda b, s: (b, s, 0)
    flat = lambda b, s: (0, b * tiles + s)
    full = lambda a: pl.BlockSpec(a.shape, const2)
    sub = min(MIX_SUB, ts)
    tri = jnp.triu(jnp.ones((sub, sub), BF16), k=1)
    args = (w_glu_bf, b_glu.reshape(1, D_SSM), g_ssm.reshape(1, D_SSM), wo_conv, wo_ssm,
            g_ffn.reshape(1, d), wr_hi, wr_lo, router_bias.reshape(N_EXPERTS, 1), ws13, ws2_bf, tri)
    return pl.pallas_call(
        _mix_back_kernel,
        grid=(bsz, tiles),
        in_specs=[pl.BlockSpec((1, ts, d), tile_in),
                  pl.BlockSpec((1, ts, D_CONV), tile),
                  pl.BlockSpec((1, D_SSM, ts), lambda b, s: (b, 0, s)),
                  pl.BlockSpec((1, 1, d), row), pl.BlockSpec((1, 1, d), row),
                  pl.BlockSpec((1, 1, d), row), pl.BlockSpec((1, 1, d), row)]
                 + [full(a) for a in args],
        out_specs=[pl.BlockSpec((1, ts, d), tile),
                   pl.BlockSpec((1, ts, d // 2), tile),
                   pl.BlockSpec((8, ts), flat),
                   pl.BlockSpec((1, ts, 8), tile),
                   pl.BlockSpec((8, ts), flat),
                   pl.BlockSpec((N_EXPERTS, 128), const2)],
        out_shape=[jax.ShapeDtypeStruct((bsz, seq, d), RESIDUAL_DTYPE),
                   jax.ShapeDtypeStruct((bsz, seq, d // 2), I32),
                   jax.ShapeDtypeStruct((8, n_tok), I32),
                   jax.ShapeDtypeStruct((bsz, seq, 8), F32),
                   jax.ShapeDtypeStruct((8, n_tok), I32),
                   jax.ShapeDtypeStruct((N_EXPERTS, 128), F32)],
        compiler_params=_params("arbitrary", "arbitrary"),
        name="mix_back",
    )(x, yconv, yt, gate1, shift2, scale2, gate2, *args)


def _plan_kernel(cnt_ref, e8_ref, r8_ref, pos_ref, *, bm):
    cnt = cnt_ref[...].astype(I32)
    padded = (cnt + (bm - 1)) // bm * bm
    rid = lax.broadcasted_iota(I32, padded.shape, 0)
    incl = padded
    d = 1
    while d < N_EXPERTS:
        incl = incl + jnp.where(rid >= d, pltpu.roll(incl, d, axis=0), 0)
        d *= 2
    pstart = incl - padded
    lanes = pstart.shape[1]

    def chunk(ci, carry):
        c0 = pl.multiple_of(ci * lanes, lanes)
        e = e8_ref[:, pl.ds(c0, lanes)]
        acc = r8_ref[:, pl.ds(c0, lanes)]
        for ex in range(N_EXPERTS):
            acc = acc + jnp.where(e == ex, pstart[ex:ex + 1, :], 0)
        pos_ref[:, pl.ds(c0, lanes)] = acc
        return carry

    lax.fori_loop(0, e8_ref.shape[1] // lanes, chunk, 0)


def _plan(counts, e8, r8, bm):
    n_tok = e8.shape[1]
    tb = min(8192, n_tok)
    return pl.pallas_call(
        functools.partial(_plan_kernel, bm=bm),
        grid=(n_tok // tb,),
        in_specs=[pl.BlockSpec(counts.shape, lambda i: (0, 0)),
                  pl.BlockSpec((8, tb), lambda i: (0, i)),
                  pl.BlockSpec((8, tb), lambda i: (0, i))],
        out_specs=pl.BlockSpec((8, tb), lambda i: (0, i)),
        out_shape=jax.ShapeDtypeStruct((8, n_tok), I32),
        compiler_params=_params("arbitrary"),
        name="plan",
    )(counts, e8, r8)


def _sc_mesh():
    return plsc.VectorSubcoreMesh(core_axis_name="c", subcore_axis_name="s",
                                  num_cores=V7X_SC_CORES, num_subcores=V7X_SC_SUBCORES)


def _sc_worker_base(per_worker):
    return (lax.axis_index("s") * V7X_SC_CORES + lax.axis_index("c")) * per_worker


def _sc_dispatch(h2p, posflat, n_rows):
    n_tok, half = h2p.shape
    win = SC_WINDOW
    per_worker = n_tok // SC_WORKERS
    n_win = per_worker // win

    wins_per_k = n_tok // win

    @functools.partial(
        pl.kernel, mesh=_sc_mesh(),
        out_type=jax.ShapeDtypeStruct((n_rows, half), I32),
        scratch_types=[pltpu.VMEM((TOP_K, n_win, win), I32), pltpu.VMEM((2, win, half), I32),
                       pltpu.SemaphoreType.DMA((2,)), pltpu.SemaphoreType.DMA((2,))],
        name="sc_dispatch")
    def run(h2p_hbm, pos_hbm, xs_hbm, idx_v, rows_v, sem_in, sem_out):
        wid = _sc_worker_base(1)
        for k in range(TOP_K):
            pltpu.sync_copy(pos_hbm.at[pl.ds(k * wins_per_k + wid * n_win, n_win)], idx_v.at[k])

        @pl.loop(0, n_win, step=2)
        def _(j):
            loads = []
            for b in range(2):
                t0 = pl.multiple_of((wid * n_win + j + b) * win, win)
                loads.append(pltpu.async_copy(h2p_hbm.at[pl.ds(t0, win)], rows_v.at[b], sem_in.at[b]))
            stores = []
            for b in range(2):
                loads[b].wait()
                for k in range(TOP_K):
                    stores.append(pltpu.async_copy(rows_v.at[b], xs_hbm.at[idx_v.at[k, j + b]], sem_out.at[b]))
            for cp in stores:
                cp.wait()

    return run(h2p, posflat.reshape(TOP_K * wins_per_k, win))


def _sc_combine(ys, posflat, w_lanes):
    n_tok = w_lanes.shape[0]
    half = ys.shape[1]
    win, step, lanes = SC_WINDOW, SC_COMBINE_TOKENS, V7X_SC_LANES
    per_worker = n_tok // SC_WORKERS
    n_win = per_worker // win
    wins_per_k = n_tok // win

    @functools.partial(
        pl.kernel, mesh=_sc_mesh(),
        out_type=jax.ShapeDtypeStruct((n_tok, half), I32),
        scratch_types=[pltpu.VMEM((TOP_K, n_win, win), I32), pltpu.VMEM((TOP_K, step, half), I32),
                       pltpu.VMEM((step, w_lanes.shape[1]), I32), pltpu.VMEM((step, half), I32),
                       pltpu.SemaphoreType.DMA],
        compiler_params=pltpu.CompilerParams(needs_layout_passes=False),
        name="sc_combine")
    def run(ys_hbm, pos_hbm, w_hbm, out_hbm, idx_v, rows_v, w_v, out_v, sem):
        wid = _sc_worker_base(1)
        for k in range(TOP_K):
            pltpu.sync_copy(pos_hbm.at[pl.ds(k * wins_per_k + wid * n_win, n_win)], idx_v.at[k])

        @pl.loop(0, per_worker // step)
        def _(s):
            wdw = s // (win // step)
            q0 = pl.multiple_of((s % (win // step)) * step, step)
            t0 = pl.multiple_of(wid * per_worker + s * step, step)
            copies = [pltpu.async_copy(ys_hbm.at[idx_v[k, wdw, pl.ds(q0, step)]], rows_v.at[k], sem)
                      for k in range(TOP_K)]
            copies.append(pltpu.async_copy(w_hbm.at[pl.ds(t0, step)], w_v, sem))
            for cp in copies:
                cp.wait()

            @pl.loop(0, step)
            def _(i):
                wb = [plsc.bitcast(w_v[i, pl.ds(k * lanes, lanes)], BF16) for k in range(TOP_K)]

                @pl.loop(0, half, step=lanes)
                def _(c):
                    acc = None
                    for k in range(TOP_K):
                        term = wb[k] * plsc.bitcast(rows_v[k, i, pl.ds(c, lanes)], BF16)
                        acc = term if acc is None else acc + term
                    out_v[i, pl.ds(c, lanes)] = plsc.bitcast(acc, I32)

            pltpu.sync_copy(out_v, out_hbm.at[pl.ds(t0, step)])

    return run(ys, posflat.reshape(TOP_K * wins_per_k, win), w_lanes)


def _experts_kernel(be_ref, nu_ref, x_ref, w1_ref, w3_ref, w2_ref, after_ref, y_ref, w13_s, w2_s):
    del after_ref
    i = pl.program_id(0)
    changed = jnp.logical_or(i == 0, be_ref[i] != be_ref[jnp.maximum(i - 1, 0)])

    @pl.when(changed)
    def _():
        f = w1_ref.shape[2]
        w13_s[:, :f] = w1_ref[0].astype(BF16)
        w13_s[:, f:] = w3_ref[0].astype(BF16)
        w2_s[...] = w2_ref[0].astype(BF16)

    @pl.when(i < nu_ref[0])
    def _():
        half = x_ref.shape[1]
        sub = min(EXPERT_SUB, x_ref.shape[0])
        for r0 in range(0, x_ref.shape[0], sub):
            lo, hi = _unpack_halves(lax.bitcast_convert_type(x_ref[r0:r0 + sub, :], U32))
            a = jnp.dot(lo.astype(BF16), w13_s[:half, :], preferred_element_type=F32)
            a = a + jnp.dot(hi.astype(BF16), w13_s[half:, :], preferred_element_type=F32)
            f = a.shape[1] // 2
            act = _silu(a[:, :f]) * a[:, f:]
            y = jnp.dot(act.astype(BF16), w2_s[...], preferred_element_type=F32)
            y_ref[r0:r0 + sub, :] = lax.bitcast_convert_type(_pack_halves(y), I32)


def _experts(block_e, n_used, xs, w1, w3, w2, bm, after):
    n_rows, half = xs.shape
    d, f = w1.shape[1], w1.shape[2]
    rows = lambda i, be, nu: (jnp.minimum(i, nu[0] - 1), 0)
    wblk = lambda i, be, nu: (be[i], 0, 0)
    grid_spec = pltpu.PrefetchScalarGridSpec(
        num_scalar_prefetch=2,
        grid=(n_rows // bm,),
        in_specs=[pl.BlockSpec((bm, half), rows),
                  pl.BlockSpec((1, d, f), wblk),
                  pl.BlockSpec((1, d, f), wblk),
                  pl.BlockSpec((1, f, d), wblk),
                  pl.BlockSpec(memory_space=pl.ANY)],
        out_specs=pl.BlockSpec((bm, half), rows),
        scratch_shapes=[pltpu.VMEM((d, 2 * f), BF16), pltpu.VMEM((f, d), BF16)],
    )
    return pl.pallas_call(
        _experts_kernel,
        grid_spec=grid_spec,
        out_shape=jax.ShapeDtypeStruct((n_rows, half), I32),
        compiler_params=_params("arbitrary"),
        name="experts",
    )(block_e, n_used, xs, w1, w3, w2, after)


def _combine_kernel(r_ref, x1_ref, gate2_ref, gfin_ref, *rest):
    o_ref = rest[-1]
    routed = jnp.concatenate(_unpack_halves(lax.bitcast_convert_type(r_ref[...], U32)), axis=1)
    o_ref[...] = _rms(x1_ref[...].astype(F32) + gate2_ref[0] * routed, gfin_ref[...])


def _combine(routed, x1, gate2, g_final, tt, tiles_per_seq, b_src, b0, n_tok_all, out_prev):
    d = x1.shape[1]
    n_tok, half = routed.shape
    src0 = b_src * tiles_per_seq
    tile0 = b0 * tiles_per_seq
    in_specs = [pl.BlockSpec((tt, half), lambda i: (i, 0)),
                pl.BlockSpec((tt, d), lambda i: (i + src0, 0)),
                pl.BlockSpec((1, 1, d), lambda i: (i // tiles_per_seq + b0, 0, 0)),
                pl.BlockSpec((1, d), lambda i: (0, 0))]
    args = [routed, x1, gate2, g_final.reshape(1, d)]
    aliases = {}
    if out_prev is not None:
        in_specs.append(pl.BlockSpec(memory_space=pl.ANY))
        args.append(out_prev)
        aliases = {len(args) - 1: 0}
    return pl.pallas_call(
        _combine_kernel,
        grid=(n_tok // tt,),
        in_specs=in_specs,
        out_specs=pl.BlockSpec((tt, d), lambda i: (i + tile0, 0)),
        out_shape=jax.ShapeDtypeStruct((n_tok_all, d), F32),
        input_output_aliases=aliases,
        compiler_params=_params("arbitrary"),
        name="combine",
    )(*args)


def _block_experts(counts, bm, n_blocks):
    padded = (counts.astype(I32) + (bm - 1)) // bm * bm
    pend = jnp.cumsum(padded)
    starts = jnp.arange(n_blocks, dtype=I32) * bm
    block_e = jnp.minimum(jnp.sum((pend[None, :] <= starts[:, None]).astype(I32), axis=1), N_EXPERTS - 1)
    return block_e, (pend[-1:] // bm).astype(I32)


def kernel(x, c, w_ada, b_ada, g_mix, w_in, conv_w, conv_b, g_conv, lam_re, lam_im, log_dt, b_re, b_im,
           c_re, c_im, d_skip, w_glu, b_glu, g_ssm, w_out, g_ffn, w_router, router_bias, w1, w3, w2,
           ws1, ws3, ws2, g_final):
    bsz, seq, d = x.shape
    n_tok = bsz * seq
    ts = min(MIX_BACK_TOKENS, seq)
    L = min(S5_CHUNK, seq)
    n_chunks = seq // L
    G, H = SSM_GROUPS, SSM_GROUP_CH

    mod = _adaln(c, w_ada, b_ada).reshape(bsz, 6, 1, d)
    shift1, scale1, gate1, shift2, scale2, gate2 = (mod[:, i] for i in range(6))

    kern, e_mat, f_mat, a_pow = _s5_tables(lam_re, lam_im, log_dt, b_re, b_im, c_re, c_im, d_skip,
                                           L, n_chunks)
    w_in_bf = w_in.astype(BF16)

    NG, NJ = N_EXPERT_GROUPS, EXPERTS_PER_GROUP
    w_r = w_router.astype(F32).reshape(d, NG, NJ).transpose(0, 2, 1).reshape(d, N_EXPERTS)
    r_bias = router_bias.reshape(NG, NJ).T.reshape(N_EXPERTS)
    w_r_hi = w_r.astype(BF16)
    wr_hi = jnp.concatenate([w_r_hi, (w_r - w_r_hi.astype(F32)).astype(BF16)], axis=1)
    wr_lo = jnp.concatenate([w_r_hi, jnp.zeros_like(w_r_hi)], axis=1)
    w_out_bf = w_out.astype(BF16)
    w_glu_bf = w_glu.astype(BF16)
    ws13 = jnp.concatenate([ws1, ws3], axis=1).astype(BF16)
    ws2_bf = ws2.astype(BF16)
    bm = EXPERT_ROWS
    tt = min(COMBINE_TOKENS, seq)

    n_parts = LAYER_PARTS if bsz % LAYER_PARTS == 0 else 1
    pb = bsz // n_parts
    pt = pb * seq
    n_blocks = -(-(pt * TOP_K + N_EXPERTS * (bm - 1)) // bm)
    none = jnp.zeros((8, 128), F32)

    def run_experts(part, after):
        return _experts(*part["block_e"], part["xs"], w1, w3, w2, bm, after)

    def finish(part, ys, out, n_split):
        nb = pb // n_split
        for h in range(n_split):
            lo, hi = h * nb * seq, (h + 1) * nb * seq
            pos = part["pos8"][:TOP_K, lo:hi].reshape(TOP_K * nb * seq)
            routed = _sc_combine(ys, pos, part["w_lanes"][lo:hi])
            out = _combine(routed, part["x1"].reshape(pt, d), gate2, g_final, tt,
                           seq // tt, h * nb, part["b0"] + h * nb, n_tok, out)
        return out

    out, prev, after_front = None, None, none
    for p in range(n_parts):
        b0 = p * pb
        yconv, ut = _mix_front(x, shift1, scale1, g_mix, w_in_bf, conv_w, conv_b, g_conv,
                               min(MIX_FRONT_TOKENS, seq), b0, pb, after_front)
        ys_prev = run_experts(prev, ut) if prev is not None else None
        yt = _s5_core(ut.reshape(pb, G, H, n_chunks, L), kern, e_mat, f_mat, a_pow,
                      none if ys_prev is None else ys_prev).reshape(pb, G * H, seq)
        x1, h2p, e8, w8, r8, counts = _mix_back(
            x, yconv, yt, gate1, shift2, scale2, gate2, w_glu_bf, b_glu, g_ssm,
            w_out_bf[:D_CONV], w_out_bf[D_CONV:], g_ffn, wr_hi, wr_lo, r_bias, ws13, ws2_bf,
            ts, b0, pb)
        counts = counts.reshape(NJ, NG, -1).transpose(1, 0, 2).reshape(N_EXPERTS, -1)
        pos8 = _plan(counts, e8, r8, bm)
        xs = _sc_dispatch(h2p.reshape(pt, d // 2), pos8[:TOP_K].reshape(TOP_K * pt), n_blocks * bm)
        if prev is not None:
            out = finish(prev, ys_prev, out, 1)
        w_bits = lax.bitcast_convert_type(w8.reshape(pt, 8).astype(BF16), jnp.uint16).astype(U32)
        w_lanes = jnp.repeat(lax.bitcast_convert_type((w_bits << 16) | w_bits, I32), V7X_SC_LANES, axis=1)
        prev = dict(b0=b0, x1=x1, w_lanes=w_lanes, pos8=pos8, xs=xs,
                    block_e=_block_experts(counts[:, 0], bm, n_blocks))
        after_front = pos8
    ys_last = run_experts(prev, none if out is None else out)
    out = finish(prev, ys_last, out, LAST_PART_SPLIT if pb % LAST_PART_SPLIT == 0 else 1)
    return out.reshape(bsz, seq, d)
```

```python
import functools
import math

import jax
import jax.numpy as jnp
import numpy as np
from jax import lax
from jax.experimental import pallas as pl
from jax.experimental.pallas import tpu as pltpu
from jax.experimental.pallas import tpu_sc as plsc

F32 = jnp.float32
BF16 = jnp.bfloat16
U32 = jnp.uint32
I32 = jnp.int32

D_CONV = 768
D_SSM = 256
SSM_GROUPS = 16
SSM_GROUP_CH = 16
SSM_STATE = 64
N_EXPERTS = 64
TOP_K = 6
N_EXPERT_GROUPS = 8
TOPK_GROUPS = 4
EXPERTS_PER_GROUP = 8
ROUTED_SCALE = 2.5
RMS_EPS = 1e-6

S5_CHUNK = 128
EXPERT_ROWS = 1024
EXPERT_SUB = 512
MIX_FRONT_TOKENS = 1024
MIX_FRONT_SUB = 1024
MIX_SUB = 256
MIX_BACK_TOKENS = 1024
MIX_PHASE_LAG = 3
COMBINE_TOKENS = 512
LAYER_PARTS = 2
LAST_PART_SPLIT = 4
RESIDUAL_DTYPE = BF16
V7X_SC_CORES = 2
V7X_SC_SUBCORES = 16
V7X_SC_LANES = 16
SC_WORKERS = V7X_SC_CORES * V7X_SC_SUBCORES
SC_COMBINE_TOKENS = 16
SC_WINDOW = 64
V7X_VMEM_LIMIT = 56 * 1024 * 1024
NEG_INF = float("-inf")
HIGH_HALF = np.uint32(0xFFFF0000)


def _rms(x, g):
    return x * lax.rsqrt(jnp.mean(x * x, axis=-1, keepdims=True) + RMS_EPS) * g


def _gelu_tanh(x):
    return 0.5 * x * (1.0 + jnp.tanh(math.sqrt(2.0 / math.pi) * (x + 0.044715 * (x * x * x))))


def _silu(x):
    return x * jax.nn.sigmoid(x)


def _params(*sem):
    return pltpu.CompilerParams(dimension_semantics=sem, vmem_limit_bytes=V7X_VMEM_LIMIT)


def _pack_halves(y):
    return _pack_rounded(y.astype(BF16).astype(F32))


def _pack_rounded(y):
    m = y.shape[1] // 2
    bits = lax.bitcast_convert_type(y, U32)
    return (bits[:, m:] & HIGH_HALF) | (bits[:, :m] >> 16)


def _unpack_halves(w):
    lo = lax.bitcast_convert_type(w << 16, F32)
    hi = lax.bitcast_convert_type(w & HIGH_HALF, F32)
    return lo, hi


def _adaln_kernel(c_ref, w_ref, b_ref, o_ref):
    o_ref[...] = jnp.dot(_silu(c_ref[...]), w_ref[...], preferred_element_type=F32,
                         precision=lax.Precision.HIGHEST) + b_ref[...]


def _adaln(c, w_ada, b_ada):
    bsz, d = c.shape
    n = w_ada.shape[1]
    bn = 1024
    return pl.pallas_call(
        _adaln_kernel,
        grid=(n // bn,),
        in_specs=[pl.BlockSpec((bsz, d), lambda j: (0, 0)),
                  pl.BlockSpec((d, bn), lambda j: (0, j)),
                  pl.BlockSpec((1, bn), lambda j: (0, j))],
        out_specs=pl.BlockSpec((bsz, bn), lambda j: (0, j)),
        out_shape=jax.ShapeDtypeStruct((bsz, n), F32),
        compiler_params=_params("arbitrary"),
        name="adaln",
    )(c, w_ada, b_ada.reshape(1, n))


def _mix_front_kernel(x_ref, shift_ref, scale_ref, gmix_ref, win_ref, cw_ref, cb_ref, gconv_ref, after_ref,
                      yconv_ref, ut_ref, zprev_ref):
    del after_ref
    s = pl.program_id(1)

    @pl.when(s == 0)
    def _():
        zprev_ref[...] = jnp.zeros_like(zprev_ref)

    cw = cw_ref[...]
    sub = min(MIX_FRONT_SUB, x_ref.shape[1])
    carry = {"prev": zprev_ref[...]}

    def norm(st):
        x = x_ref[0, st["r0"]:st["r0"] + sub, :]
        st["h"] = (_rms(x, gmix_ref[...]) * (1.0 + scale_ref[0]) + shift_ref[0]).astype(BF16)

    def in_proj(st):
        st["proj"] = jnp.dot(st.pop("h"), win_ref[...], preferred_element_type=F32)

    def mixers(st):
        r0 = st["r0"]
        proj = st.pop("proj")
        b_gate = proj[:, :D_CONV]
        c_gate = proj[:, D_CONV:2 * D_CONV]
        v = proj[:, 2 * D_CONV:3 * D_CONV]
        u = proj[:, 3 * D_CONV:]
        z = c_gate * v
        prev = carry["prev"]
        rid = lax.broadcasted_iota(I32, z.shape, 0)
        z1 = jnp.where(rid == 0, prev[7:8, :], pltpu.roll(z, 1, axis=0))
        z2 = jnp.where(rid == 0, prev[6:7, :], jnp.where(rid == 1, prev[7:8, :], pltpu.roll(z, 2, axis=0)))
        carry["prev"] = z[sub - 8:, :]
        conv = cw[0:1, :] * z2 + cw[1:2, :] * z1 + cw[2:3, :] * z + cb_ref[...]
        yconv_ref[0, r0:r0 + sub, :] = _rms(b_gate * conv, gconv_ref[...]).astype(BF16)
        ut_ref[0, :, r0:r0 + sub] = u.T.astype(BF16)

    phases = (norm, in_proj, mixers)
    chains = [{"r0": r0} for r0 in range(0, x_ref.shape[1], sub)]
    for step in range(len(phases) + len(chains) - 1):
        for ci, st in enumerate(chains):
            if 0 <= step - ci < len(phases):
                phases[step - ci](st)
    zprev_ref[...] = carry["prev"]


def _mix_front(x, shift1, scale1, g_mix, w_in_bf, conv_w, conv_b, g_conv, ts, b0, bsz, after):
    _, seq, d = x.shape
    d_in = w_in_bf.shape[1]
    row = lambda b, s: (b + b0, 0, 0)
    const2 = lambda b, s: (0, 0)
    return pl.pallas_call(
        _mix_front_kernel,
        grid=(bsz, seq // ts),
        in_specs=[pl.BlockSpec((1, ts, d), lambda b, s: (b + b0, s, 0)),
                  pl.BlockSpec((1, 1, d), row),
                  pl.BlockSpec((1, 1, d), row),
                  pl.BlockSpec((1, d), const2),
                  pl.BlockSpec((d, d_in), const2),
                  pl.BlockSpec((8, D_CONV), const2),
                  pl.BlockSpec((1, D_CONV), const2),
                  pl.BlockSpec((1, D_CONV), const2),
                  pl.BlockSpec(memory_space=pl.ANY)],
        out_specs=[pl.BlockSpec((1, ts, D_CONV), lambda b, s: (b, s, 0)),
                   pl.BlockSpec((1, D_SSM, ts), lambda b, s: (b, 0, s))],
        out_shape=[jax.ShapeDtypeStruct((bsz, seq, D_CONV), BF16),
                   jax.ShapeDtypeStruct((bsz, D_SSM, seq), BF16)],
        scratch_shapes=[pltpu.VMEM((8, D_CONV), F32)],
        compiler_params=_params("arbitrary", "arbitrary"),
        name="mix_front",
    )(x, shift1, scale1, g_mix.reshape(1, d), w_in_bf,
      jnp.pad(conv_w, ((0, 8 - conv_w.shape[0]), (0, 0))), conv_b.reshape(1, D_CONV),
      g_conv.reshape(1, D_CONV), after)


def _s5_tables(lam_re, lam_im, log_dt, b_re, b_im, c_re, c_im, d_skip, chunk, n_chunks):
    hp = lax.Precision.HIGHEST
    G, H, P, L = SSM_GROUPS, SSM_GROUP_CH, SSM_STATE, chunk
    lr = lam_re.astype(F32)
    li = lam_im.astype(F32)
    dt = jnp.exp(log_dt.astype(F32))[:, None]
    mag = jnp.exp(lr * dt)
    ang = li * dt
    ab_re = mag * jnp.cos(ang)
    ab_im = mag * jnp.sin(ang)
    den = lr * lr + li * li
    nr = ab_re - 1.0
    ni = ab_im
    q_re = (nr * lr + ni * li) / den
    q_im = (ni * lr - nr * li) / den
    br = b_re.astype(F32)
    bi = b_im.astype(F32)
    bb_re = q_re[..., None] * br - q_im[..., None] * bi
    bb_im = q_re[..., None] * bi + q_im[..., None] * br

    def a_power(tau):
        t = tau.astype(F32)[None, :, None]
        m = jnp.exp(t * (lr * dt)[:, None, :])
        th = t * ang[:, None, :]
        return m * jnp.cos(th), m * jnp.sin(th)

    pw_re, pw_im = a_power(jnp.arange(L + 1))
    cr = c_re.astype(F32)[:, None]
    ci = c_im.astype(F32)[:, None]
    cp_re = cr * pw_re[:, :, None, :] - ci * pw_im[:, :, None, :]
    cp_im = cr * pw_im[:, :, None, :] + ci * pw_re[:, :, None, :]
    kern = (jnp.einsum('gthp,gpk->gthk', cp_re[:, :L], bb_re, precision=hp)
            - jnp.einsum('gthp,gpk->gthk', cp_im[:, :L], bb_im, precision=hp))
    kern = kern.at[:, 0].add(d_skip.astype(F32).reshape(G, H)[:, :, None] * jnp.eye(H, dtype=F32))
    kern = kern.transpose(0, 3, 2, 1).reshape(G, H * H, L)

    rev_re = pw_re[:, L - 1::-1]
    rev_im = pw_im[:, L - 1::-1]
    bt_re = bb_re.transpose(0, 2, 1)[:, :, None, :]
    bt_im = bb_im.transpose(0, 2, 1)[:, :, None, :]
    e_re = rev_re[:, None] * bt_re - rev_im[:, None] * bt_im
    e_im = rev_re[:, None] * bt_im + rev_im[:, None] * bt_re
    e_mat = jnp.concatenate([e_re, e_im], axis=-1).reshape(G, H * L, 2 * P)

    f_re = cp_re[:, 1:].transpose(0, 3, 2, 1)
    f_im = -cp_im[:, 1:].transpose(0, 3, 2, 1)
    f_mat = jnp.concatenate([f_re, f_im], axis=1).reshape(G, 2 * P, H * L)

    n_steps = max(1, (n_chunks - 1).bit_length())
    sr, si = a_power(L * (2 ** jnp.arange(n_steps)))
    a_pow = jnp.stack([jnp.concatenate([sr, sr], axis=-1),
                       jnp.concatenate([-si, si], axis=-1)], axis=2)
    return kern, e_mat.astype(BF16), f_mat.astype(BF16), a_pow


def _s5_kernel(u_ref, k_ref, e_ref, f_ref, a_ref, after_ref, y_ref, toep_ref, *, n_chunks, n_steps, chunk):
    del after_ref
    L, H = chunk, SSM_GROUP_CH

    causal = lax.broadcasted_iota(I32, (L, L), 1) >= lax.broadcasted_iota(I32, (L, L), 0)

    def build(hin, carry):
        r0 = pl.multiple_of(hin * L, L)
        for hout in range(H):
            krow = k_ref[0, pl.ds(hin * H + hout, 1), :]
            blk = pltpu.roll(jnp.broadcast_to(krow, (L, L)), 0, axis=1, stride=1, stride_axis=0)
            toep_ref[pl.ds(r0, L), hout * L:(hout + 1) * L] = jnp.where(causal, blk, 0.0).astype(BF16)
        return carry

    lax.fori_loop(0, H, build, 0)

    bsz = u_ref.shape[0]
    u = jnp.concatenate([u_ref[:, h].reshape(bsz * n_chunks, L) for h in range(H)], axis=1)
    st = jnp.dot(u, e_ref[0], preferred_element_type=F32)
    cidx = lax.broadcasted_iota(I32, st.shape, 0) % n_chunks
    p2 = st.shape[1]

    def shifted(xv, d):
        return jnp.where(cidx >= d, pltpu.roll(xv, d, axis=0), 0.0)

    for k in range(n_steps):
        d = 1 << k
        if d >= n_chunks:
            break
        z = shifted(st, d)
        st = st + a_ref[0, k, 0:1, :] * z + a_ref[0, k, 1:2, :] * pltpu.roll(z, p2 // 2, axis=1)
    s_in = shifted(st, 1)
    y = jnp.dot(u, toep_ref[...], preferred_element_type=F32)
    y = (y + jnp.dot(s_in.astype(BF16), f_ref[0], preferred_element_type=F32)).astype(BF16)
    for h in range(H):
        y_ref[:, h] = y[:, h * L:(h + 1) * L].reshape(bsz, n_chunks, L)


def _s5_core(ut5, kern, e_mat, f_mat, a_pow, after):
    bsz, g, h, n_chunks, chunk = ut5.shape
    hl = h * chunk
    n_steps = a_pow.shape[1]
    p2 = e_mat.shape[2]
    blk = lambda i: (i, 0, 0)
    seq_blk = pl.BlockSpec((bsz, None, h, n_chunks, chunk), lambda i: (0, i, 0, 0, 0))
    return pl.pallas_call(
        functools.partial(_s5_kernel, n_chunks=n_chunks, n_steps=n_steps, chunk=chunk),
        grid=(g,),
        in_specs=[seq_blk,
                  pl.BlockSpec((1,) + kern.shape[1:], blk),
                  pl.BlockSpec((1, hl, p2), blk),
                  pl.BlockSpec((1, p2, hl), blk),
                  pl.BlockSpec((1, n_steps, 2, p2), lambda i: (i, 0, 0, 0)),
                  pl.BlockSpec(memory_space=pl.ANY)],
        out_specs=seq_blk,
        out_shape=jax.ShapeDtypeStruct(ut5.shape, BF16),
        scratch_shapes=[pltpu.VMEM((hl, hl), BF16)],
        compiler_params=_params("arbitrary"),
        name="s5_core",
    )(ut5, kern, e_mat, f_mat, a_pow, after)


def _route(scores, biased, tri, base):
    n_tok = scores.shape[1]
    shape3 = (EXPERTS_PER_GROUP, N_EXPERT_GROUPS, n_tok)
    sc3 = scores.reshape(shape3)
    b3 = biased.reshape(shape3)
    j_iota = lax.broadcasted_iota(I32, shape3, 0)
    e_iota = lax.broadcasted_iota(I32, shape3, 1) * EXPERTS_PER_GROUP + j_iota

    def red(fn, x):
        return fn(fn(x, axis=0, keepdims=True), axis=1, keepdims=True)

    m1 = jnp.max(b3, axis=0, keepdims=True)
    i1 = jnp.min(jnp.where(b3 == m1, j_iota, EXPERTS_PER_GROUP), axis=0, keepdims=True)
    m2 = jnp.max(jnp.where(j_iota == i1, NEG_INF, b3), axis=0, keepdims=True)
    gs = m1 + m2

    g_iota = lax.broadcasted_iota(I32, gs.shape, 1)
    gsel = jnp.zeros(gs.shape, F32)
    cur = gs
    for _ in range(TOPK_GROUPS):
        m = jnp.max(cur, axis=1, keepdims=True)
        ig = jnp.min(jnp.where(cur == m, g_iota, N_EXPERT_GROUPS), axis=1, keepdims=True)
        pick = g_iota == ig
        gsel = jnp.where(pick, 1.0, gsel)
        cur = jnp.where(pick, NEG_INF, cur)

    cur = jnp.where(gsel > 0.0, b3, NEG_INF)
    sel = jnp.zeros(shape3, F32)
    ids, vals = [], []
    for _ in range(TOP_K):
        m = red(jnp.max, cur)
        ie = red(jnp.min, jnp.where(cur == m, e_iota, N_EXPERTS))
        pick = e_iota == ie
        ids.append(ie)
        vals.append(red(jnp.sum, jnp.where(pick, sc3, 0.0)))
        sel = jnp.where(pick, 1.0, sel)
        cur = jnp.where(pick, NEG_INF, cur)
    tot = functools.reduce(lambda a, b: a + b, vals)

    sel2 = sel.reshape(N_EXPERTS, n_tok)
    before = jnp.dot(sel2.astype(BF16), tri, preferred_element_type=F32) + base
    before3 = before.reshape(shape3)
    ranks = [red(jnp.sum, jnp.where(e_iota == ie, before3, 0.0)) for ie in ids]

    def rows(parts, dtype):
        parts = [p.reshape(1, n_tok).astype(dtype) for p in parts]
        return jnp.concatenate(parts + [jnp.zeros((8 - len(parts), n_tok), dtype)], axis=0)

    e8 = rows(ids, I32)
    w8 = rows([v / tot * ROUTED_SCALE for v in vals], F32)
    r8 = rows(ranks, I32)
    return e8, w8, r8, jnp.sum(sel2, axis=1, keepdims=True)


def _mix_back_kernel(x_ref, yconv_ref, yt_ref, gate1_ref, shift2_ref, scale2_ref, gate2_ref,
                     wglu_ref, bglu_ref, gssm_ref, woc_ref, wos_ref, gffn_ref,
                     wrh_ref, wrl_ref, rbias_ref, ws13_ref, ws2_ref, tri_ref,
                     x1_ref, h2p_ref, e8_ref, w8_ref, r8_ref, cnt_ref):
    first = jnp.logical_and(pl.program_id(0) == 0, pl.program_id(1) == 0)

    @pl.when(first)
    def _():
        cnt_ref[...] = jnp.zeros_like(cnt_ref)

    sub = tri_ref.shape[0]

    def ssm_post(st):
        r0 = st["r0"]
        y = yt_ref[0, :, r0:r0 + sub].astype(F32).T
        y = _gelu_tanh(y)
        y = y * jax.nn.sigmoid(jnp.dot(y.astype(BF16), wglu_ref[...], preferred_element_type=F32)
                               + bglu_ref[...])
        st["y_ssm"] = _rms(y, gssm_ref[...]).astype(BF16)

    def out_proj(st):
        r0 = st["r0"]
        mix = jnp.dot(yconv_ref[0, r0:r0 + sub, :], woc_ref[...], preferred_element_type=F32)
        mix = mix + jnp.dot(st.pop("y_ssm"), wos_ref[...], preferred_element_type=F32)
        st["x1"] = x_ref[0, r0:r0 + sub, :] + gate1_ref[0] * mix

    def ffn_in(st):
        r0 = st["r0"]
        h2 = _rms(st["x1"], gffn_ref[...] * (1.0 + scale2_ref[0])) + shift2_ref[0]
        h2b = h2.astype(BF16)
        h2r = h2b.astype(F32)
        h2p_ref[0, r0:r0 + sub, :] = lax.bitcast_convert_type(_pack_rounded(h2r), I32)
        st["h2b"] = h2b
        st["h2l"] = (h2 - h2r).astype(BF16)

    def shared_expert(st):
        r0 = st["r0"]
        a = jnp.dot(st["h2b"], ws13_ref[...], preferred_element_type=F32)
        f = a.shape[1] // 2
        act = _silu(a[:, :f]) * a[:, f:]
        shared = jnp.dot(act.astype(BF16), ws2_ref[...], preferred_element_type=F32)
        x1_ref[0, r0:r0 + sub, :] = (st.pop("x1") + gate2_ref[0] * shared).astype(x1_ref.dtype)

    def router(st):
        lt = (jnp.dot(st.pop("h2b"), wrh_ref[...], preferred_element_type=F32)
              + jnp.dot(st.pop("h2l"), wrl_ref[...], preferred_element_type=F32)).T
        st["scores"] = jax.nn.sigmoid(lt[:N_EXPERTS] + lt[N_EXPERTS:])

    def route(st):
        r0 = st["r0"]
        scores = st.pop("scores")
        e8, w8, r8, cnt = _route(scores, scores + rbias_ref[...], tri_ref[...], carry["base"])
        e8_ref[:, r0:r0 + sub] = e8
        w8_ref[0, r0:r0 + sub, :] = w8.T
        r8_ref[:, r0:r0 + sub] = r8
        carry["base"] = carry["base"] + cnt

    carry = {"base": cnt_ref[:, 0:1]}
    phases = (ssm_post, out_proj, ffn_in, shared_expert, router, route)
    chains = [{"r0": r0} for r0 in range(0, x_ref.shape[1], sub)]
    lag = MIX_PHASE_LAG
    for step in range(len(phases) + lag * (len(chains) - 1)):
        for ci, st in enumerate(chains):
            ph = step - lag * ci
            if 0 <= ph < len(phases):
                phases[ph](st)
    cnt_ref[...] = jnp.broadcast_to(carry["base"], cnt_ref.shape)


def _mix_back(x, yconv, yt, gate1, shift2, scale2, gate2, w_glu_bf, b_glu, g_ssm, wo_conv, wo_ssm,
              g_ffn, wr_hi, wr_lo, router_bias, ws13, ws2_bf, ts, b0, bsz):
    _, seq, d = x.shape
    n_tok = bsz * seq
    tiles = seq // ts
    row = lambda b, s: (b + b0, 0, 0)
    const2 = lambda b, s: (0, 0)
    tile_in = lambda b, s: (b + b0, s, 0)
    tile = lambda b, s: (b, s, 0)
    flat = lambda b, s: (0, b * tiles + s)
    full = lambda a: pl.BlockSpec(a.shape, const2)
    sub = min(MIX_SUB, ts)
    tri = jnp.triu(jnp.ones((sub, sub), BF16), k=1)
    args = (w_glu_bf, b_glu.reshape(1, D_SSM), g_ssm.reshape(1, D_SSM), wo_conv, wo_ssm,
            g_ffn.reshape(1, d), wr_hi, wr_lo, router_bias.reshape(N_EXPERTS, 1), ws13, ws2_bf, tri)
    return pl.pallas_call(
        _mix_back_kernel,
        grid=(bsz, tiles),
        in_specs=[pl.BlockSpec((1, ts, d), tile_in),
                  pl.BlockSpec((1, ts, D_CONV), tile),
                  pl.BlockSpec((1, D_SSM, ts), lambda b, s: (b, 0, s)),
                  pl.BlockSpec((1, 1, d), row), pl.BlockSpec((1, 1, d), row),
                  pl.BlockSpec((1, 1, d), row), pl.BlockSpec((1, 1, d), row)]
                 + [full(a) for a in args],
        out_specs=[pl.BlockSpec((1, ts, d), tile),
                   pl.BlockSpec((1, ts, d // 2), tile),
                   pl.BlockSpec((8, ts), flat),
                   pl.BlockSpec((1, ts, 8), tile),
                   pl.BlockSpec((8, ts), flat),
                   pl.BlockSpec((N_EXPERTS, 128), const2)],
        out_shape=[jax.ShapeDtypeStruct((bsz, seq, d), RESIDUAL_DTYPE),
                   jax.ShapeDtypeStruct((bsz, seq, d // 2), I32),
                   jax.ShapeDtypeStruct((8, n_tok), I32),
                   jax.ShapeDtypeStruct((bsz, seq, 8), F32),
                   jax.ShapeDtypeStruct((8, n_tok), I32),
                   jax.ShapeDtypeStruct((N_EXPERTS, 128), F32)],
        compiler_params=_params("arbitrary", "arbitrary"),
        name="mix_back",
    )(x, yconv, yt, gate1, shift2, scale2, gate2, *args)


def _plan_kernel(cnt_ref, e8_ref, r8_ref, pos_ref, *, bm):
    cnt = cnt_ref[...].astype(I32)
    padded = (cnt + (bm - 1)) // bm * bm
    rid = lax.broadcasted_iota(I32, padded.shape, 0)
    incl = padded
    d = 1
    while d < N_EXPERTS:
        incl = incl + jnp.where(rid >= d, pltpu.roll(incl, d, axis=0), 0)
        d *= 2
    pstart = incl - padded
    lanes = pstart.shape[1]

    def chunk(ci, carry):
        c0 = pl.multiple_of(ci * lanes, lanes)
        e = e8_ref[:, pl.ds(c0, lanes)]
        acc = r8_ref[:, pl.ds(c0, lanes)]
        for ex in range(N_EXPERTS):
            acc = acc + jnp.where(e == ex, pstart[ex:ex + 1, :], 0)
        pos_ref[:, pl.ds(c0, lanes)] = acc
        return carry

    lax.fori_loop(0, e8_ref.shape[1] // lanes, chunk, 0)


def _plan(counts, e8, r8, bm):
    n_tok = e8.shape[1]
    tb = min(8192, n_tok)
    return pl.pallas_call(
        functools.partial(_plan_kernel, bm=bm),
        grid=(n_tok // tb,),
        in_specs=[pl.BlockSpec(counts.shape, lambda i: (0, 0)),
                  pl.BlockSpec((8, tb), lambda i: (0, i)),
                  pl.BlockSpec((8, tb), lambda i: (0, i))],
        out_specs=pl.BlockSpec((8, tb), lambda i: (0, i)),
        out_shape=jax.ShapeDtypeStruct((8, n_tok), I32),
        compiler_params=_params("arbitrary"),
        name="plan",
    )(counts, e8, r8)


def _sc_mesh():
    return plsc.VectorSubcoreMesh(core_axis_name="c", subcore_axis_name="s",
                                  num_cores=V7X_SC_CORES, num_subcores=V7X_SC_SUBCORES)


def _sc_worker_base(per_worker):
    return (lax.axis_index("s") * V7X_SC_CORES + lax.axis_index("c")) * per_worker


def _sc_dispatch(h2p, posflat, n_rows):
    n_tok, half = h2p.shape
    win = SC_WINDOW
    per_worker = n_tok // SC_WORKERS
    n_win = per_worker // win

    wins_per_k = n_tok // win

    @functools.partial(
        pl.kernel, mesh=_sc_mesh(),
        out_type=jax.ShapeDtypeStruct((n_rows, half), I32),
        scratch_types=[pltpu.VMEM((TOP_K, n_win, win), I32), pltpu.VMEM((2, win, half), I32),
                       pltpu.SemaphoreType.DMA((2,)), pltpu.SemaphoreType.DMA((2,))],
        name="sc_dispatch")
    def run(h2p_hbm, pos_hbm, xs_hbm, idx_v, rows_v, sem_in, sem_out):
        wid = _sc_worker_base(1)
        for k in range(TOP_K):
            pltpu.sync_copy(pos_hbm.at[pl.ds(k * wins_per_k + wid * n_win, n_win)], idx_v.at[k])

        @pl.loop(0, n_win, step=2)
        def _(j):
            loads = []
            for b in range(2):
                t0 = pl.multiple_of((wid * n_win + j + b) * win, win)
                loads.append(pltpu.async_copy(h2p_hbm.at[pl.ds(t0, win)], rows_v.at[b], sem_in.at[b]))
            stores = []
            for b in range(2):
                loads[b].wait()
                for k in range(TOP_K):
                    stores.append(pltpu.async_copy(rows_v.at[b], xs_hbm.at[idx_v.at[k, j + b]], sem_out.at[b]))
            for cp in stores:
                cp.wait()

    return run(h2p, posflat.reshape(TOP_K * wins_per_k, win))


def _sc_combine(ys, posflat, w_lanes):
    n_tok = w_lanes.shape[0]
    half = ys.shape[1]
    win, step, lanes = SC_WINDOW, SC_COMBINE_TOKENS, V7X_SC_LANES
    per_worker = n_tok // SC_WORKERS
    n_win = per_worker // win
    wins_per_k = n_tok // win

    @functools.partial(
        pl.kernel, mesh=_sc_mesh(),
        out_type=jax.ShapeDtypeStruct((n_tok, half), I32),
        scratch_types=[pltpu.VMEM((TOP_K, n_win, win), I32), pltpu.VMEM((2, TOP_K, step, half), I32),
                       pltpu.VMEM((2, step, w_lanes.shape[1]), I32), pltpu.VMEM((step, half), I32),
                       pltpu.SemaphoreType.DMA((2,))],
        compiler_params=pltpu.CompilerParams(needs_layout_passes=False),
        name="sc_combine")
    def run(ys_hbm, pos_hbm, w_hbm, out_hbm, idx_v, rows_v, w_v, out_v, sem):
        wid = _sc_worker_base(1)
        n_steps = per_worker // step
        for k in range(TOP_K):
            pltpu.sync_copy(pos_hbm.at[pl.ds(k * wins_per_k + wid * n_win, n_win)], idx_v.at[k])

        def copies(s, slot):
            wdw = s // (win // step)
            q0 = pl.multiple_of((s % (win // step)) * step, step)
            t0 = pl.multiple_of(wid * per_worker + s * step, step)
            cps = [pltpu.make_async_copy(ys_hbm.at[idx_v[k, wdw, pl.ds(q0, step)]], rows_v.at[slot, k],
                                         sem.at[slot]) for k in range(TOP_K)]
            return cps + [pltpu.make_async_copy(w_hbm.at[pl.ds(t0, step)], w_v.at[slot], sem.at[slot])]

        def compute(s, slot):
            @pl.loop(0, step)
            def _(i):
                wb = [plsc.bitcast(w_v[slot, i, pl.ds(k * lanes, lanes)], BF16) for k in range(TOP_K)]
                for c in range(0, half, lanes):
                    t = [wb[k] * plsc.bitcast(rows_v[slot, k, i, pl.ds(c, lanes)], BF16) for k in range(TOP_K)]
                    while len(t) > 1:
                        t = [t[a] + t[a + 1] for a in range(0, len(t) - 1, 2)] + t[len(t) & ~1:]
                    out_v[i, pl.ds(c, lanes)] = plsc.bitcast(t[0], I32)

            t0 = pl.multiple_of(wid * per_worker + s * step, step)
            pltpu.sync_copy(out_v, out_hbm.at[pl.ds(t0, step)])

        for cp in copies(0, 0):
            cp.start()

        @pl.loop(0, n_steps, step=2)
        def _(s):
            for cp in copies(s + 1, 1):
                cp.start()
            for cp in copies(s, 0):
                cp.wait()
            compute(s, 0)

            @pl.when(s + 2 < n_steps)
            def _():
                for cp in copies(s + 2, 0):
                    cp.start()

            for cp in copies(s + 1, 1):
                cp.wait()
            compute(s + 1, 1)

    return run(ys, posflat.reshape(TOP_K * wins_per_k, win), w_lanes)


def _experts_kernel(be_ref, nu_ref, x_ref, w1_ref, w3_ref, w2_ref, after_ref, y_ref, w13_s, w2_s):
    del after_ref
    i = pl.program_id(0)
    changed = jnp.logical_or(i == 0, be_ref[i] != be_ref[jnp.maximum(i - 1, 0)])

    @pl.when(changed)
    def _():
        f = w1_ref.shape[2]
        w13_s[:, :f] = w1_ref[0].astype(BF16)
        w13_s[:, f:] = w3_ref[0].astype(BF16)
        w2_s[...] = w2_ref[0].astype(BF16)

    @pl.when(i < nu_ref[0])
    def _():
        half = x_ref.shape[1]
        sub = min(EXPERT_SUB, x_ref.shape[0])
        for r0 in range(0, x_ref.shape[0], sub):
            lo, hi = _unpack_halves(lax.bitcast_convert_type(x_ref[r0:r0 + sub, :], U32))
            a = jnp.dot(lo.astype(BF16), w13_s[:half, :], preferred_element_type=F32)
            a = a + jnp.dot(hi.astype(BF16), w13_s[half:, :], preferred_element_type=F32)
            f = a.shape[1] // 2
            act = _silu(a[:, :f]) * a[:, f:]
            y = jnp.dot(act.astype(BF16), w2_s[...], preferred_element_type=F32)
            y_ref[r0:r0 + sub, :] = lax.bitcast_convert_type(_pack_halves(y), I32)


def _experts(block_e, n_used, xs, w1, w3, w2, bm, after):
    n_rows, half = xs.shape
    d, f = w1.shape[1], w1.shape[2]
    rows = lambda i, be, nu: (jnp.minimum(i, nu[0] - 1), 0)
    wblk = lambda i, be, nu: (be[i], 0, 0)
    grid_spec = pltpu.PrefetchScalarGridSpec(
        num_scalar_prefetch=2,
        grid=(n_rows // bm,),
        in_specs=[pl.BlockSpec((bm, half), rows),
                  pl.BlockSpec((1, d, f), wblk),
                  pl.BlockSpec((1, d, f), wblk),
                  pl.BlockSpec((1, f, d), wblk),
                  pl.BlockSpec(memory_space=pl.ANY)],
        out_specs=pl.BlockSpec((bm, half), rows),
        scratch_shapes=[pltpu.VMEM((d, 2 * f), BF16), pltpu.VMEM((f, d), BF16)],
    )
    return pl.pallas_call(
        _experts_kernel,
        grid_spec=grid_spec,
        out_shape=jax.ShapeDtypeStruct((n_rows, half), I32),
        compiler_params=_params("arbitrary"),
        name="experts",
    )(block_e, n_used, xs, w1, w3, w2, after)


def _combine_kernel(r_ref, x1_ref, gate2_ref, gfin_ref, *rest):
    o_ref = rest[-1]
    routed = jnp.concatenate(_unpack_halves(lax.bitcast_convert_type(r_ref[...], U32)), axis=1)
    o_ref[...] = _rms(x1_ref[...].astype(F32) + gate2_ref[0] * routed, gfin_ref[...])


def _combine(routed, x1, gate2, g_final, tt, tiles_per_seq, b_src, b0, n_tok_all, out_prev):
    d = x1.shape[1]
    n_tok, half = routed.shape
    src0 = b_src * tiles_per_seq
    tile0 = b0 * tiles_per_seq
    in_specs = [pl.BlockSpec((tt, half), lambda i: (i, 0)),
                pl.BlockSpec((tt, d), lambda i: (i + src0, 0)),
                pl.BlockSpec((1, 1, d), lambda i: (i // tiles_per_seq + b0, 0, 0)),
                pl.BlockSpec((1, d), lambda i: (0, 0))]
    args = [routed, x1, gate2, g_final.reshape(1, d)]
    aliases = {}
    if out_prev is not None:
        in_specs.append(pl.BlockSpec(memory_space=pl.ANY))
        args.append(out_prev)
        aliases = {len(args) - 1: 0}
    return pl.pallas_call(
        _combine_kernel,
        grid=(n_tok // tt,),
        in_specs=in_specs,
        out_specs=pl.BlockSpec((tt, d), lambda i: (i + tile0, 0)),
        out_shape=jax.ShapeDtypeStruct((n_tok_all, d), F32),
        input_output_aliases=aliases,
        compiler_params=_params("arbitrary"),
        name="combine",
    )(*args)


def _block_experts(counts, bm, n_blocks):
    padded = (counts.astype(I32) + (bm - 1)) // bm * bm
    pend = jnp.cumsum(padded)
    starts = jnp.arange(n_blocks, dtype=I32) * bm
    block_e = jnp.minimum(jnp.sum((pend[None, :] <= starts[:, None]).astype(I32), axis=1), N_EXPERTS - 1)
    return block_e, (pend[-1:] // bm).astype(I32)


def kernel(x, c, w_ada, b_ada, g_mix, w_in, conv_w, conv_b, g_conv, lam_re, lam_im, log_dt, b_re, b_im,
           c_re, c_im, d_skip, w_glu, b_glu, g_ssm, w_out, g_ffn, w_router, router_bias, w1, w3, w2,
           ws1, ws3, ws2, g_final):
    bsz, seq, d = x.shape
    n_tok = bsz * seq
    ts = min(MIX_BACK_TOKENS, seq)
    L = min(S5_CHUNK, seq)
    n_chunks = seq // L
    G, H = SSM_GROUPS, SSM_GROUP_CH

    mod = _adaln(c, w_ada, b_ada).reshape(bsz, 6, 1, d)
    shift1, scale1, gate1, shift2, scale2, gate2 = (mod[:, i] for i in range(6))

    kern, e_mat, f_mat, a_pow = _s5_tables(lam_re, lam_im, log_dt, b_re, b_im, c_re, c_im, d_skip,
                                           L, n_chunks)
    w_in_bf = w_in.astype(BF16)

    NG, NJ = N_EXPERT_GROUPS, EXPERTS_PER_GROUP
    w_r = w_router.astype(F32).reshape(d, NG, NJ).transpose(0, 2, 1).reshape(d, N_EXPERTS)
    r_bias = router_bias.reshape(NG, NJ).T.reshape(N_EXPERTS)
    w_r_hi = w_r.astype(BF16)
    wr_hi = jnp.concatenate([w_r_hi, (w_r - w_r_hi.astype(F32)).astype(BF16)], axis=1)
    wr_lo = jnp.concatenate([w_r_hi, jnp.zeros_like(w_r_hi)], axis=1)
    w_out_bf = w_out.astype(BF16)
    w_glu_bf = w_glu.astype(BF16)
    ws13 = jnp.concatenate([ws1, ws3], axis=1).astype(BF16)
    ws2_bf = ws2.astype(BF16)
    bm = EXPERT_ROWS
    tt = min(COMBINE_TOKENS, seq)

    n_parts = LAYER_PARTS if bsz % LAYER_PARTS == 0 else 1
    pb = bsz // n_parts
    pt = pb * seq
    n_blocks = -(-(pt * TOP_K + N_EXPERTS * (bm - 1)) // bm)
    none = jnp.zeros((8, 128), F32)

    def run_experts(part, after):
        return _experts(*part["block_e"], part["xs"], w1, w3, w2, bm, after)

    def finish(part, ys, out, n_split):
        nb = pb // n_split
        for h in range(n_split):
            lo, hi = h * nb * seq, (h + 1) * nb * seq
            pos = part["pos8"][:TOP_K, lo:hi].reshape(TOP_K * nb * seq)
            routed = _sc_combine(ys, pos, part["w_lanes"][lo:hi])
            out = _combine(routed, part["x1"].reshape(pt, d), gate2, g_final, tt,
                           seq // tt, h * nb, part["b0"] + h * nb, n_tok, out)
        return out

    out, prev, after_front = None, None, none
    for p in range(n_parts):
        b0 = p * pb
        yconv, ut = _mix_front(x, shift1, scale1, g_mix, w_in_bf, conv_w, conv_b, g_conv,
                               min(MIX_FRONT_TOKENS, seq), b0, pb, after_front)
        ys_prev = run_experts(prev, ut) if prev is not None else None
        yt = _s5_core(ut.reshape(pb, G, H, n_chunks, L), kern, e_mat, f_mat, a_pow,
                      none if ys_prev is None else ys_prev).reshape(pb, G * H, seq)
        x1, h2p, e8, w8, r8, counts = _mix_back(
            x, yconv, yt, gate1, shift2, scale2, gate2, w_glu_bf, b_glu, g_ssm,
            w_out_bf[:D_CONV], w_out_bf[D_CONV:], g_ffn, wr_hi, wr_lo, r_bias, ws13, ws2_bf,
            ts, b0, pb)
        counts = counts.reshape(NJ, NG, -1).transpose(1, 0, 2).reshape(N_EXPERTS, -1)
        pos8 = _plan(counts, e8, r8, bm)
        xs = _sc_dispatch(h2p.reshape(pt, d // 2), pos8[:TOP_K].reshape(TOP_K * pt), n_blocks * bm)
        if prev is not None:
            out = finish(prev, ys_prev, out, 1)
        w_bits = lax.bitcast_convert_type(w8.reshape(pt, 8).astype(BF16), jnp.uint16).astype(U32)
        w_lanes = jnp.repeat(lax.bitcast_convert_type((w_bits << 16) | w_bits, I32), V7X_SC_LANES, axis=1)
        prev = dict(b0=b0, x1=x1, w_lanes=w_lanes, pos8=pos8, xs=xs,
                    block_e=_block_experts(counts[:, 0], bm, n_blocks))
        after_front = pos8
    ys_last = run_experts(prev, none if out is None else out)
    out = finish(prev, ys_last, out, LAST_PART_SPLIT if pb % LAST_PART_SPLIT == 0 else 1)
    return out.reshape(bsz, seq, d)
```

```python
import functools
import math

import jax
import jax.numpy as jnp
import numpy as np
from jax import lax
from jax.experimental import pallas as pl
from jax.experimental.pallas import tpu as pltpu
from jax.experimental.pallas import tpu_sc as plsc

F32 = jnp.float32
BF16 = jnp.bfloat16
U32 = jnp.uint32
I32 = jnp.int32

D_CONV = 768
D_SSM = 256
SSM_GROUPS = 16
SSM_GROUP_CH = 16
SSM_STATE = 64
N_EXPERTS = 64
TOP_K = 6
N_EXPERT_GROUPS = 8
TOPK_GROUPS = 4
EXPERTS_PER_GROUP = 8
ROUTED_SCALE = 2.5
RMS_EPS = 1e-6

S5_CHUNK = 128
EXPERT_ROWS = 1024
EXPERT_SUB = 512
MIX_FRONT_TOKENS = 1024
MIX_FRONT_SUB = 1024
MIX_SUB = 256
MIX_BACK_TOKENS = 1024
MIX_PHASE_LAG = 3
COMBINE_TOKENS = 512
LAYER_PARTS = 2
LAST_PART_SPLIT = 4
RESIDUAL_DTYPE = BF16
V7X_SC_CORES = 2
V7X_SC_SUBCORES = 16
V7X_SC_LANES = 16
SC_WORKERS = V7X_SC_CORES * V7X_SC_SUBCORES
SC_COMBINE_TOKENS = 16
SC_WINDOW = 64
V7X_VMEM_LIMIT = 56 * 1024 * 1024
NEG_INF = float("-inf")
HIGH_HALF = np.uint32(0xFFFF0000)


def _rms(x, g):
    return x * lax.rsqrt(jnp.mean(x * x, axis=-1, keepdims=True) + RMS_EPS) * g


def _gelu_tanh(x):
    return 0.5 * x * (1.0 + jnp.tanh(math.sqrt(2.0 / math.pi) * (x + 0.044715 * (x * x * x))))


def _silu(x):
    return x * jax.nn.sigmoid(x)


def _params(*sem):
    return pltpu.CompilerParams(dimension_semantics=sem, vmem_limit_bytes=V7X_VMEM_LIMIT)


def _pack_halves(y):
    return _pack_rounded(y.astype(BF16).astype(F32))


def _pack_rounded(y):
    m = y.shape[1] // 2
    bits = lax.bitcast_convert_type(y, U32)
    return (bits[:, m:] & HIGH_HALF) | (bits[:, :m] >> 16)


def _unpack_halves(w):
    lo = lax.bitcast_convert_type(w << 16, F32)
    hi = lax.bitcast_convert_type(w & HIGH_HALF, F32)
    return lo, hi


def _adaln_kernel(c_ref, w_ref, b_ref, o_ref):
    o_ref[...] = jnp.dot(_silu(c_ref[...]), w_ref[...], preferred_element_type=F32,
                         precision=lax.Precision.HIGHEST) + b_ref[...]


def _adaln(c, w_ada, b_ada):
    bsz, d = c.shape
    n = w_ada.shape[1]
    bn = 1024
    return pl.pallas_call(
        _adaln_kernel,
        grid=(n // bn,),
        in_specs=[pl.BlockSpec((bsz, d), lambda j: (0, 0)),
                  pl.BlockSpec((d, bn), lambda j: (0, j)),
                  pl.BlockSpec((1, bn), lambda j: (0, j))],
        out_specs=pl.BlockSpec((bsz, bn), lambda j: (0, j)),
        out_shape=jax.ShapeDtypeStruct((bsz, n), F32),
        compiler_params=_params("arbitrary"),
        name="adaln",
    )(c, w_ada, b_ada.reshape(1, n))


def _mix_front_kernel(x_ref, shift_ref, scale_ref, gmix_ref, win_ref, cw_ref, cb_ref, gconv_ref, after_ref,
                      yconv_ref, ut_ref, zprev_ref):
    del after_ref
    s = pl.program_id(1)

    @pl.when(s == 0)
    def _():
        zprev_ref[...] = jnp.zeros_like(zprev_ref)

    cw = cw_ref[...]
    sub = min(MIX_FRONT_SUB, x_ref.shape[1])
    carry = {"prev": zprev_ref[...]}

    def norm(st):
        x = x_ref[0, st["r0"]:st["r0"] + sub, :]
        st["h"] = (_rms(x, gmix_ref[...]) * (1.0 + scale_ref[0]) + shift_ref[0]).astype(BF16)

    def in_proj(st):
        st["proj"] = jnp.dot(st.pop("h"), win_ref[...], preferred_element_type=F32)

    def mixers(st):
        r0 = st["r0"]
        proj = st.pop("proj")
        b_gate = proj[:, :D_CONV]
        c_gate = proj[:, D_CONV:2 * D_CONV]
        v = proj[:, 2 * D_CONV:3 * D_CONV]
        u = proj[:, 3 * D_CONV:]
        z = c_gate * v
        prev = carry["prev"]
        rid = lax.broadcasted_iota(I32, z.shape, 0)
        z1 = jnp.where(rid == 0, prev[7:8, :], pltpu.roll(z, 1, axis=0))
        z2 = jnp.where(rid == 0, prev[6:7, :], jnp.where(rid == 1, prev[7:8, :], pltpu.roll(z, 2, axis=0)))
        carry["prev"] = z[sub - 8:, :]
        conv = cw[0:1, :] * z2 + cw[1:2, :] * z1 + cw[2:3, :] * z + cb_ref[...]
        yconv_ref[0, r0:r0 + sub, :] = _rms(b_gate * conv, gconv_ref[...]).astype(BF16)
        ut_ref[0, :, r0:r0 + sub] = u.T.astype(BF16)

    phases = (norm, in_proj, mixers)
    chains = [{"r0": r0} for r0 in range(0, x_ref.shape[1], sub)]
    for step in range(len(phases) + len(chains) - 1):
        for ci, st in enumerate(chains):
            if 0 <= step - ci < len(phases):
                phases[step - ci](st)
    zprev_ref[...] = carry["prev"]


def _mix_front(x, shift1, scale1, g_mix, w_in_bf, conv_w, conv_b, g_conv, ts, b0, bsz, after):
    _, seq, d = x.shape
    d_in = w_in_bf.shape[1]
    row = lambda b, s: (b + b0, 0, 0)
    const2 = lambda b, s: (0, 0)
    return pl.pallas_call(
        _mix_front_kernel,
        grid=(bsz, seq // ts),
        in_specs=[pl.BlockSpec((1, ts, d), lambda b, s: (b + b0, s, 0)),
                  pl.BlockSpec((1, 1, d), row),
                  pl.BlockSpec((1, 1, d), row),
                  pl.BlockSpec((1, d), const2),
                  pl.BlockSpec((d, d_in), const2),
                  pl.BlockSpec((8, D_CONV), const2),
                  pl.BlockSpec((1, D_CONV), const2),
                  pl.BlockSpec((1, D_CONV), const2),
                  pl.BlockSpec(memory_space=pl.ANY)],
        out_specs=[pl.BlockSpec((1, ts, D_CONV), lambda b, s: (b, s, 0)),
                   pl.BlockSpec((1, D_SSM, ts), lambda b, s: (b, 0, s))],
        out_shape=[jax.ShapeDtypeStruct((bsz, seq, D_CONV), BF16),
                   jax.ShapeDtypeStruct((bsz, D_SSM, seq), BF16)],
        scratch_shapes=[pltpu.VMEM((8, D_CONV), F32)],
        compiler_params=_params("arbitrary", "arbitrary"),
        name="mix_front",
    )(x, shift1, scale1, g_mix.reshape(1, d), w_in_bf,
      jnp.pad(conv_w, ((0, 8 - conv_w.shape[0]), (0, 0))), conv_b.reshape(1, D_CONV),
      g_conv.reshape(1, D_CONV), after)


def _s5_tables(lam_re, lam_im, log_dt, b_re, b_im, c_re, c_im, d_skip, chunk, n_chunks):
    hp = lax.Precision.HIGHEST
    G, H, P, L = SSM_GROUPS, SSM_GROUP_CH, SSM_STATE, chunk
    lr = lam_re.astype(F32)
    li = lam_im.astype(F32)
    dt = jnp.exp(log_dt.astype(F32))[:, None]
    mag = jnp.exp(lr * dt)
    ang = li * dt
    ab_re = mag * jnp.cos(ang)
    ab_im = mag * jnp.sin(ang)
    den = lr * lr + li * li
    nr = ab_re - 1.0
    ni = ab_im
    q_re = (nr * lr + ni * li) / den
    q_im = (ni * lr - nr * li) / den
    br = b_re.astype(F32)
    bi = b_im.astype(F32)
    bb_re = q_re[..., None] * br - q_im[..., None] * bi
    bb_im = q_re[..., None] * bi + q_im[..., None] * br

    def a_power(tau):
        t = tau.astype(F32)[None, :, None]
        m = jnp.exp(t * (lr * dt)[:, None, :])
        th = t * ang[:, None, :]
        return m * jnp.cos(th), m * jnp.sin(th)

    pw_re, pw_im = a_power(jnp.arange(L + 1))
    cr = c_re.astype(F32)[:, None]
    ci = c_im.astype(F32)[:, None]
    cp_re = cr * pw_re[:, :, None, :] - ci * pw_im[:, :, None, :]
    cp_im = cr * pw_im[:, :, None, :] + ci * pw_re[:, :, None, :]
    kern = (jnp.einsum('gthp,gpk->gthk', cp_re[:, :L], bb_re, precision=hp)
            - jnp.einsum('gthp,gpk->gthk', cp_im[:, :L], bb_im, precision=hp))
    kern = kern.at[:, 0].add(d_skip.astype(F32).reshape(G, H)[:, :, None] * jnp.eye(H, dtype=F32))
    kern = kern.transpose(0, 3, 2, 1).reshape(G, H * H, L)

    rev_re = pw_re[:, L - 1::-1]
    rev_im = pw_im[:, L - 1::-1]
    bt_re = bb_re.transpose(0, 2, 1)[:, :, None, :]
    bt_im = bb_im.transpose(0, 2, 1)[:, :, None, :]
    e_re = rev_re[:, None] * bt_re - rev_im[:, None] * bt_im
    e_im = rev_re[:, None] * bt_im + rev_im[:, None] * bt_re
    e_mat = jnp.concatenate([e_re, e_im], axis=-1).reshape(G, H * L, 2 * P)

    f_re = cp_re[:, 1:].transpose(0, 3, 2, 1)
    f_im = -cp_im[:, 1:].transpose(0, 3, 2, 1)
    f_mat = jnp.concatenate([f_re, f_im], axis=1).reshape(G, 2 * P, H * L)

    n_steps = max(1, (n_chunks - 1).bit_length())
    sr, si = a_power(L * (2 ** jnp.arange(n_steps)))
    a_pow = jnp.stack([jnp.concatenate([sr, sr], axis=-1),
                       jnp.concatenate([-si, si], axis=-1)], axis=2)
    return kern, e_mat.astype(BF16), f_mat.astype(BF16), a_pow


def _s5_kernel(u_ref, k_ref, e_ref, f_ref, a_ref, after_ref, y_ref, toep_ref, *, n_chunks, n_steps, chunk):
    del after_ref
    L, H = chunk, SSM_GROUP_CH

    causal = lax.broadcasted_iota(I32, (L, L), 1) >= lax.broadcasted_iota(I32, (L, L), 0)

    def build(hin, carry):
        r0 = pl.multiple_of(hin * L, L)
        for hout in range(H):
            krow = k_ref[0, pl.ds(hin * H + hout, 1), :]
            blk = pltpu.roll(jnp.broadcast_to(krow, (L, L)), 0, axis=1, stride=1, stride_axis=0)
            toep_ref[pl.ds(r0, L), hout * L:(hout + 1) * L] = jnp.where(causal, blk, 0.0).astype(BF16)
        return carry

    lax.fori_loop(0, H, build, 0)

    bsz = u_ref.shape[0]
    u = jnp.concatenate([u_ref[:, h].reshape(bsz * n_chunks, L) for h in range(H)], axis=1)
    st = jnp.dot(u, e_ref[0], preferred_element_type=F32)
    cidx = lax.broadcasted_iota(I32, st.shape, 0) % n_chunks
    p2 = st.shape[1]

    def shifted(xv, d):
        return jnp.where(cidx >= d, pltpu.roll(xv, d, axis=0), 0.0)

    for k in range(n_steps):
        d = 1 << k
        if d >= n_chunks:
            break
        z = shifted(st, d)
        st = st + a_ref[0, k, 0:1, :] * z + a_ref[0, k, 1:2, :] * pltpu.roll(z, p2 // 2, axis=1)
    s_in = shifted(st, 1)
    y = jnp.dot(u, toep_ref[...], preferred_element_type=F32)
    y = (y + jnp.dot(s_in.astype(BF16), f_ref[0], preferred_element_type=F32)).astype(BF16)
    for h in range(H):
        y_ref[:, h] = y[:, h * L:(h + 1) * L].reshape(bsz, n_chunks, L)


def _s5_core(ut5, kern, e_mat, f_mat, a_pow, after):
    bsz, g, h, n_chunks, chunk = ut5.shape
    hl = h * chunk
    n_steps = a_pow.shape[1]
    p2 = e_mat.shape[2]
    blk = lambda i: (i, 0, 0)
    seq_blk = pl.BlockSpec((bsz, None, h, n_chunks, chunk), lambda i: (0, i, 0, 0, 0))
    return pl.pallas_call(
        functools.partial(_s5_kernel, n_chunks=n_chunks, n_steps=n_steps, chunk=chunk),
        grid=(g,),
        in_specs=[seq_blk,
                  pl.BlockSpec((1,) + kern.shape[1:], blk),
                  pl.BlockSpec((1, hl, p2), blk),
                  pl.BlockSpec((1, p2, hl), blk),
                  pl.BlockSpec((1, n_steps, 2, p2), lambda i: (i, 0, 0, 0)),
                  pl.BlockSpec(memory_space=pl.ANY)],
        out_specs=seq_blk,
        out_shape=jax.ShapeDtypeStruct(ut5.shape, BF16),
        scratch_shapes=[pltpu.VMEM((hl, hl), BF16)],
        compiler_params=_params("arbitrary"),
        name="s5_core",
    )(ut5, kern, e_mat, f_mat, a_pow, after)


def _route(scores, biased, tri, base):
    n_tok = scores.shape[1]
    shape3 = (EXPERTS_PER_GROUP, N_EXPERT_GROUPS, n_tok)
    sc3 = scores.reshape(shape3)
    b3 = biased.reshape(shape3)
    j_iota = lax.broadcasted_iota(I32, shape3, 0)
    e_iota = lax.broadcasted_iota(I32, shape3, 1) * EXPERTS_PER_GROUP + j_iota

    def red(fn, x):
        return fn(fn(x, axis=0, keepdims=True), axis=1, keepdims=True)

    m1 = jnp.max(b3, axis=0, keepdims=True)
    i1 = jnp.min(jnp.where(b3 == m1, j_iota, EXPERTS_PER_GROUP), axis=0, keepdims=True)
    m2 = jnp.max(jnp.where(j_iota == i1, NEG_INF, b3), axis=0, keepdims=True)
    gs = m1 + m2

    g_iota = lax.broadcasted_iota(I32, gs.shape, 1)
    gsel = jnp.zeros(gs.shape, F32)
    cur = gs
    for _ in range(TOPK_GROUPS):
        m = jnp.max(cur, axis=1, keepdims=True)
        ig = jnp.min(jnp.where(cur == m, g_iota, N_EXPERT_GROUPS), axis=1, keepdims=True)
        pick = g_iota == ig
        gsel = jnp.where(pick, 1.0, gsel)
        cur = jnp.where(pick, NEG_INF, cur)

    cur = jnp.where(gsel > 0.0, b3, NEG_INF)
    sel = jnp.zeros(shape3, F32)
    ids, vals = [], []
    for _ in range(TOP_K):
        m = red(jnp.max, cur)
        ie = red(jnp.min, jnp.where(cur == m, e_iota, N_EXPERTS))
        pick = e_iota == ie
        ids.append(ie)
        vals.append(red(jnp.sum, jnp.where(pick, sc3, 0.0)))
        sel = jnp.where(pick, 1.0, sel)
        cur = jnp.where(pick, NEG_INF, cur)
    tot = functools.reduce(lambda a, b: a + b, vals)

    sel2 = sel.reshape(N_EXPERTS, n_tok)
    before = jnp.dot(sel2.astype(BF16), tri, preferred_element_type=F32) + base
    before3 = before.reshape(shape3)
    ranks = [red(jnp.sum, jnp.where(e_iota == ie, before3, 0.0)) for ie in ids]

    def rows(parts, dtype):
        parts = [p.reshape(1, n_tok).astype(dtype) for p in parts]
        return jnp.concatenate(parts + [jnp.zeros((8 - len(parts), n_tok), dtype)], axis=0)

    e8 = rows(ids, I32)
    w8 = rows([v / tot * ROUTED_SCALE for v in vals], F32)
    r8 = rows(ranks, I32)
    return e8, w8, r8, jnp.sum(sel2, axis=1, keepdims=True)


def _mix_back_kernel(x_ref, yconv_ref, yt_ref, gate1_ref, shift2_ref, scale2_ref, gate2_ref,
                     wglu_ref, bglu_ref, gssm_ref, woc_ref, wos_ref, gffn_ref,
                     wrh_ref, wrl_ref, rbias_ref, ws13_ref, ws2_ref, tri_ref,
                     x1_ref, h2p_ref, e8_ref, w8_ref, r8_ref, cnt_ref):
    first = jnp.logical_and(pl.program_id(0) == 0, pl.program_id(1) == 0)

    @pl.when(first)
    def _():
        cnt_ref[...] = jnp.zeros_like(cnt_ref)

    sub = tri_ref.shape[0]

    def ssm_post(st):
        r0 = st["r0"]
        y = yt_ref[0, :, r0:r0 + sub].astype(F32).T
        y = _gelu_tanh(y)
        y = y * jax.nn.sigmoid(jnp.dot(y.astype(BF16), wglu_ref[...], preferred_element_type=F32)
                               + bglu_ref[...])
        st["y_ssm"] = _rms(y, gssm_ref[...]).astype(BF16)

    def out_proj(st):
        r0 = st["r0"]
        mix = jnp.dot(yconv_ref[0, r0:r0 + sub, :], woc_ref[...], preferred_element_type=F32)
        mix = mix + jnp.dot(st.pop("y_ssm"), wos_ref[...], preferred_element_type=F32)
        st["x1"] = x_ref[0, r0:r0 + sub, :] + gate1_ref[0] * mix

    def ffn_in(st):
        r0 = st["r0"]
        h2 = _rms(st["x1"], gffn_ref[...] * (1.0 + scale2_ref[0])) + shift2_ref[0]
        h2b = h2.astype(BF16)
        h2r = h2b.astype(F32)
        h2p_ref[0, r0:r0 + sub, :] = lax.bitcast_convert_type(_pack_rounded(h2r), I32)
        st["h2b"] = h2b
        st["h2l"] = (h2 - h2r).astype(BF16)

    def shared_expert(st):
        r0 = st["r0"]
        a = jnp.dot(st["h2b"], ws13_ref[...], preferred_element_type=F32)
        f = a.shape[1] // 2
        act = _silu(a[:, :f]) * a[:, f:]
        shared = jnp.dot(act.astype(BF16), ws2_ref[...], preferred_element_type=F32)
        x1_ref[0, r0:r0 + sub, :] = (st.pop("x1") + gate2_ref[0] * shared).astype(x1_ref.dtype)

    def router(st):
        lt = (jnp.dot(st.pop("h2b"), wrh_ref[...], preferred_element_type=F32)
              + jnp.dot(st.pop("h2l"), wrl_ref[...], preferred_element_type=F32)).T
        st["scores"] = jax.nn.sigmoid(lt[:N_EXPERTS] + lt[N_EXPERTS:])

    def route(st):
        r0 = st["r0"]
        scores = st.pop("scores")
        e8, w8, r8, cnt = _route(scores, scores + rbias_ref[...], tri_ref[...], carry["base"])
        e8_ref[:, r0:r0 + sub] = e8
        w8_ref[0, r0:r0 + sub, :] = w8.T
        r8_ref[:, r0:r0 + sub] = r8
        carry["base"] = carry["base"] + cnt

    carry = {"base": cnt_ref[:, 0:1]}
    phases = (ssm_post, out_proj, ffn_in, shared_expert, router, route)
    chains = [{"r0": r0} for r0 in range(0, x_ref.shape[1], sub)]
    lag = MIX_PHASE_LAG
    for step in range(len(phases) + lag * (len(chains) - 1)):
        for ci, st in enumerate(chains):
            ph = step - lag * ci
            if 0 <= ph < len(phases):
                phases[ph](st)
    cnt_ref[...] = jnp.broadcast_to(carry["base"], cnt_ref.shape)


def _mix_back(x, yconv, yt, gate1, shift2, scale2, gate2, w_glu_bf, b_glu, g_ssm, wo_conv, wo_ssm,
              g_ffn, wr_hi, wr_lo, router_bias, ws13, ws2_bf, ts, b0, bsz):
    _, seq, d = x.shape
    n_tok = bsz * seq
    tiles = seq // ts
    row = lambda b, s: (b + b0, 0, 0)
    const2 = lambda b, s: (0, 0)
    tile_in = lambda b, s: (b + b0, s, 0)
    tile = lambda b, s: (b, s, 0)
    flat = lambda b, s: (0, b * tiles + s)
    full = lambda a: pl.BlockSpec(a.shape, const2)
    sub = min(MIX_SUB, ts)
    tri = jnp.triu(jnp.ones((sub, sub), BF16), k=1)
    args = (w_glu_bf, b_glu.reshape(1, D_SSM), g_ssm.reshape(1, D_SSM), wo_conv, wo_ssm,
            g_ffn.reshape(1, d), wr_hi, wr_lo, router_bias.reshape(N_EXPERTS, 1), ws13, ws2_bf, tri)
    return pl.pallas_call(
        _mix_back_kernel,
        grid=(bsz, tiles),
        in_specs=[pl.BlockSpec((1, ts, d), tile_in),
                  pl.BlockSpec((1, ts, D_CONV), tile),
                  pl.BlockSpec((1, D_SSM, ts), lambda b, s: (b, 0, s)),
                  pl.BlockSpec((1, 1, d), row), pl.BlockSpec((1, 1, d), row),
                  pl.BlockSpec((1, 1, d), row), pl.BlockSpec((1, 1, d), row)]
                 + [full(a) for a in args],
        out_specs=[pl.BlockSpec((1, ts, d), tile),
                   pl.BlockSpec((1, ts, d // 2), tile),
                   pl.BlockSpec((8, ts), flat),
                   pl.BlockSpec((1, ts, 8), tile),
                   pl.BlockSpec((8, ts), flat),
                   pl.BlockSpec((N_EXPERTS, 128), const2)],
        out_shape=[jax.ShapeDtypeStruct((bsz, seq, d), RESIDUAL_DTYPE),
                   jax.ShapeDtypeStruct((bsz, seq, d // 2), I32),
                   jax.ShapeDtypeStruct((8, n_tok), I32),
                   jax.ShapeDtypeStruct((bsz, seq, 8), F32),
                   jax.ShapeDtypeStruct((8, n_tok), I32),
                   jax.ShapeDtypeStruct((N_EXPERTS, 128), F32)],
        compiler_params=_params("arbitrary", "arbitrary"),
        name="mix_back",
    )(x, yconv, yt, gate1, shift2, scale2, gate2, *args)


def _plan_kernel(cnt_ref, e8_ref, r8_ref, pos_ref, *, bm):
    cnt = cnt_ref[...].astype(I32)
    padded = (cnt + (bm - 1)) // bm * bm
    rid = lax.broadcasted_iota(I32, padded.shape, 0)
    incl = padded
    d = 1
    while d < N_EXPERTS:
        incl = incl + jnp.where(rid >= d, pltpu.roll(incl, d, axis=0), 0)
        d *= 2
    pstart = incl - padded
    lanes = pstart.shape[1]

    def chunk(ci, carry):
        c0 = pl.multiple_of(ci * lanes, lanes)
        e = e8_ref[:, pl.ds(c0, lanes)]
        acc = r8_ref[:, pl.ds(c0, lanes)]
        for ex in range(N_EXPERTS):
            acc = acc + jnp.where(e == ex, pstart[ex:ex + 1, :], 0)
        pos_ref[:, pl.ds(c0, lanes)] = acc
        return carry

    lax.fori_loop(0, e8_ref.shape[1] // lanes, chunk, 0)


def _plan(counts, e8, r8, bm):
    n_tok = e8.shape[1]
    tb = min(8192, n_tok)
    return pl.pallas_call(
        functools.partial(_plan_kernel, bm=bm),
        grid=(n_tok // tb,),
        in_specs=[pl.BlockSpec(counts.shape, lambda i: (0, 0)),
                  pl.BlockSpec((8, tb), lambda i: (0, i)),
                  pl.BlockSpec((8, tb), lambda i: (0, i))],
        out_specs=pl.BlockSpec((8, tb), lambda i: (0, i)),
        out_shape=jax.ShapeDtypeStruct((8, n_tok), I32),
        compiler_params=_params("arbitrary"),
        name="plan",
    )(counts, e8, r8)


def _sc_mesh():
    return plsc.VectorSubcoreMesh(core_axis_name="c", subcore_axis_name="s",
                                  num_cores=V7X_SC_CORES, num_subcores=V7X_SC_SUBCORES)


def _sc_worker_base(per_worker):
    return (lax.axis_index("s") * V7X_SC_CORES + lax.axis_index("c")) * per_worker


def _sc_dispatch(h2p, posflat, n_rows):
    n_tok, half = h2p.shape
    win = SC_WINDOW
    per_worker = n_tok // SC_WORKERS
    n_win = per_worker // win

    wins_per_k = n_tok // win

    @functools.partial(
        pl.kernel, mesh=_sc_mesh(),
        out_type=jax.ShapeDtypeStruct((n_rows, half), I32),
        scratch_types=[pltpu.VMEM((TOP_K, n_win, win), I32), pltpu.VMEM((2, win, half), I32),
                       pltpu.SemaphoreType.DMA((2,)), pltpu.SemaphoreType.DMA((2,))],
        name="sc_dispatch")
    def run(h2p_hbm, pos_hbm, xs_hbm, idx_v, rows_v, sem_in, sem_out):
        wid = _sc_worker_base(1)
        for k in range(TOP_K):
            pltpu.sync_copy(pos_hbm.at[pl.ds(k * wins_per_k + wid * n_win, n_win)], idx_v.at[k])

        @pl.loop(0, n_win, step=2)
        def _(j):
            loads = []
            for b in range(2):
                t0 = pl.multiple_of((wid * n_win + j + b) * win, win)
                loads.append(pltpu.async_copy(h2p_hbm.at[pl.ds(t0, win)], rows_v.at[b], sem_in.at[b]))
            stores = []
            for b in range(2):
                loads[b].wait()
                for k in range(TOP_K):
                    stores.append(pltpu.async_copy(rows_v.at[b], xs_hbm.at[idx_v.at[k, j + b]], sem_out.at[b]))
            for cp in stores:
                cp.wait()

    return run(h2p, posflat.reshape(TOP_K * wins_per_k, win))


def _sc_combine(ys, pos_steps, w_lanes):
    n_tok = w_lanes.shape[0]
    half = ys.shape[1]
    step, lanes = SC_COMBINE_TOKENS, V7X_SC_LANES
    per_worker = n_tok // SC_WORKERS
    n_steps = per_worker // step

    @functools.partial(
        pl.kernel, mesh=_sc_mesh(),
        out_type=jax.ShapeDtypeStruct((n_tok, half), I32),
        scratch_types=[pltpu.VMEM((n_steps, TOP_K * step), I32), pltpu.VMEM((2, TOP_K * step, half), I32),
                       pltpu.VMEM((2, step, w_lanes.shape[1]), I32), pltpu.VMEM((step, half), I32),
                       pltpu.SemaphoreType.DMA((2,))],
        compiler_params=pltpu.CompilerParams(needs_layout_passes=False),
        name="sc_combine")
    def run(ys_hbm, pos_hbm, w_hbm, out_hbm, idx_v, rows_v, w_v, out_v, sem):
        wid = _sc_worker_base(1)
        pltpu.sync_copy(pos_hbm.at[pl.ds(wid * n_steps, n_steps)], idx_v)

        def copies(s, slot):
            t0 = pl.multiple_of(wid * per_worker + s * step, step)
            return [pltpu.make_async_copy(ys_hbm.at[idx_v.at[s]], rows_v.at[slot], sem.at[slot]),
                    pltpu.make_async_copy(w_hbm.at[pl.ds(t0, step)], w_v.at[slot], sem.at[slot])]

        def compute(s, slot):
            @pl.loop(0, step)
            def _(i):
                wb = [plsc.bitcast(w_v[slot, i, pl.ds(k * lanes, lanes)], BF16) for k in range(TOP_K)]
                for c in range(0, half, lanes):
                    t = [wb[k] * plsc.bitcast(rows_v[slot, k * step + i, pl.ds(c, lanes)], BF16)
                         for k in range(TOP_K)]
                    while len(t) > 1:
                        t = [t[a] + t[a + 1] for a in range(0, len(t) - 1, 2)] + t[len(t) & ~1:]
                    out_v[i, pl.ds(c, lanes)] = plsc.bitcast(t[0], I32)

            t0 = pl.multiple_of(wid * per_worker + s * step, step)
            pltpu.sync_copy(out_v, out_hbm.at[pl.ds(t0, step)])

        for cp in copies(0, 0):
            cp.start()

        @pl.loop(0, n_steps, step=2)
        def _(s):
            for cp in copies(s + 1, 1):
                cp.start()
            for cp in copies(s, 0):
                cp.wait()
            compute(s, 0)

            @pl.when(s + 2 < n_steps)
            def _():
                for cp in copies(s + 2, 0):
                    cp.start()

            for cp in copies(s + 1, 1):
                cp.wait()
            compute(s + 1, 1)

    return run(ys, pos_steps, w_lanes)


def _experts_kernel(be_ref, nu_ref, bv_ref, x_ref, w1_ref, w3_ref, w2_ref, after_ref, y_ref, w13_s, w2_s):
    del after_ref, nu_ref
    i = pl.program_id(0)
    changed = jnp.logical_or(i == 0, be_ref[i] != be_ref[jnp.maximum(i - 1, 0)])

    @pl.when(changed)
    def _():
        f = w1_ref.shape[2]
        w13_s[:, :f] = w1_ref[0].astype(BF16)
        w13_s[:, f:] = w3_ref[0].astype(BF16)
        w2_s[...] = w2_ref[0].astype(BF16)

    half = x_ref.shape[1]
    sub = min(EXPERT_SUB, x_ref.shape[0])

    def chains(n_rows):
        for r0 in range(0, n_rows, sub):
            lo, hi = _unpack_halves(lax.bitcast_convert_type(x_ref[r0:r0 + sub, :], U32))
            a = jnp.dot(lo.astype(BF16), w13_s[:half, :], preferred_element_type=F32)
            a = a + jnp.dot(hi.astype(BF16), w13_s[half:, :], preferred_element_type=F32)
            f = a.shape[1] // 2
            act = _silu(a[:, :f]) * a[:, f:]
            y = jnp.dot(act.astype(BF16), w2_s[...], preferred_element_type=F32)
            y_ref[r0:r0 + sub, :] = lax.bitcast_convert_type(_pack_halves(y), I32)

    valid = bv_ref[i]

    @pl.when(valid > sub)
    def _():
        chains(x_ref.shape[0])

    @pl.when(jnp.logical_and(valid > 0, valid <= sub))
    def _():
        chains(sub)


def _experts(block_e, n_used, block_valid, xs, w1, w3, w2, bm, after):
    n_rows, half = xs.shape
    d, f = w1.shape[1], w1.shape[2]
    rows = lambda i, be, nu, bv: (jnp.minimum(i, nu[0] - 1), 0)
    wblk = lambda i, be, nu, bv: (be[i], 0, 0)
    grid_spec = pltpu.PrefetchScalarGridSpec(
        num_scalar_prefetch=3,
        grid=(n_rows // bm,),
        in_specs=[pl.BlockSpec((bm, half), rows),
                  pl.BlockSpec((1, d, f), wblk),
                  pl.BlockSpec((1, d, f), wblk),
                  pl.BlockSpec((1, f, d), wblk),
                  pl.BlockSpec(memory_space=pl.ANY)],
        out_specs=pl.BlockSpec((bm, half), rows),
        scratch_shapes=[pltpu.VMEM((d, 2 * f), BF16), pltpu.VMEM((f, d), BF16)],
    )
    return pl.pallas_call(
        _experts_kernel,
        grid_spec=grid_spec,
        out_shape=jax.ShapeDtypeStruct((n_rows, half), I32),
        compiler_params=_params("arbitrary"),
        name="experts",
    )(block_e, n_used, block_valid, xs, w1, w3, w2, after)


def _combine_kernel(r_ref, x1_ref, gate2_ref, gfin_ref, *rest):
    o_ref = rest[-1]
    routed = jnp.concatenate(_unpack_halves(lax.bitcast_convert_type(r_ref[...], U32)), axis=1)
    o_ref[...] = _rms(x1_ref[...].astype(F32) + gate2_ref[0] * routed, gfin_ref[...])


def _combine(routed, x1, gate2, g_final, tt, tiles_per_seq, b_src, b0, n_tok_all, out_prev):
    d = x1.shape[1]
    n_tok, half = routed.shape
    src0 = b_src * tiles_per_seq
    tile0 = b0 * tiles_per_seq
    in_specs = [pl.BlockSpec((tt, half), lambda i: (i, 0)),
                pl.BlockSpec((tt, d), lambda i: (i + src0, 0)),
                pl.BlockSpec((1, 1, d), lambda i: (i // tiles_per_seq + b0, 0, 0)),
                pl.BlockSpec((1, d), lambda i: (0, 0))]
    args = [routed, x1, gate2, g_final.reshape(1, d)]
    aliases = {}
    if out_prev is not None:
        in_specs.append(pl.BlockSpec(memory_space=pl.ANY))
        args.append(out_prev)
        aliases = {len(args) - 1: 0}
    return pl.pallas_call(
        _combine_kernel,
        grid=(n_tok // tt,),
        in_specs=in_specs,
        out_specs=pl.BlockSpec((tt, d), lambda i: (i + tile0, 0)),
        out_shape=jax.ShapeDtypeStruct((n_tok_all, d), F32),
        input_output_aliases=aliases,
        compiler_params=_params("arbitrary"),
        name="combine",
    )(*args)


def _block_experts(counts, bm, n_blocks):
    cnt = counts.astype(I32)
    padded = (cnt + (bm - 1)) // bm * bm
    pend = jnp.cumsum(padded)
    starts = jnp.arange(n_blocks, dtype=I32) * bm
    block_e = jnp.minimum(jnp.sum((pend[None, :] <= starts[:, None]).astype(I32), axis=1), N_EXPERTS - 1)
    onehot = (block_e[:, None] == jnp.arange(N_EXPERTS, dtype=I32)[None, :]).astype(I32)
    token_end = jnp.sum(onehot * (pend - padded + cnt)[None, :], axis=1)
    return block_e, (pend[-1:] // bm).astype(I32), jnp.clip(token_end - starts, 0, bm)


def kernel(x, c, w_ada, b_ada, g_mix, w_in, conv_w, conv_b, g_conv, lam_re, lam_im, log_dt, b_re, b_im,
           c_re, c_im, d_skip, w_glu, b_glu, g_ssm, w_out, g_ffn, w_router, router_bias, w1, w3, w2,
           ws1, ws3, ws2, g_final):
    bsz, seq, d = x.shape
    n_tok = bsz * seq
    ts = min(MIX_BACK_TOKENS, seq)
    L = min(S5_CHUNK, seq)
    n_chunks = seq // L
    G, H = SSM_GROUPS, SSM_GROUP_CH

    mod = _adaln(c, w_ada, b_ada).reshape(bsz, 6, 1, d)
    shift1, scale1, gate1, shift2, scale2, gate2 = (mod[:, i] for i in range(6))

    kern, e_mat, f_mat, a_pow = _s5_tables(lam_re, lam_im, log_dt, b_re, b_im, c_re, c_im, d_skip,
                                           L, n_chunks)
    w_in_bf = w_in.astype(BF16)

    NG, NJ = N_EXPERT_GROUPS, EXPERTS_PER_GROUP
    w_r = w_router.astype(F32).reshape(d, NG, NJ).transpose(0, 2, 1).reshape(d, N_EXPERTS)
    r_bias = router_bias.reshape(NG, NJ).T.reshape(N_EXPERTS)
    w_r_hi = w_r.astype(BF16)
    wr_hi = jnp.concatenate([w_r_hi, (w_r - w_r_hi.astype(F32)).astype(BF16)], axis=1)
    wr_lo = jnp.concatenate([w_r_hi, jnp.zeros_like(w_r_hi)], axis=1)
    w_out_bf = w_out.astype(BF16)
    w_glu_bf = w_glu.astype(BF16)
    ws13 = jnp.concatenate([ws1, ws3], axis=1).astype(BF16)
    ws2_bf = ws2.astype(BF16)
    bm = EXPERT_ROWS
    tt = min(COMBINE_TOKENS, seq)

    n_parts = LAYER_PARTS if bsz % LAYER_PARTS == 0 else 1
    pb = bsz // n_parts
    pt = pb * seq
    n_blocks = -(-(pt * TOP_K + N_EXPERTS * (bm - 1)) // bm)
    none = jnp.zeros((8, 128), F32)

    def run_experts(part, after):
        return _experts(*part["block_e"], part["xs"], w1, w3, w2, bm, after)

    def finish(part, ys, out, n_split):
        nb = pb // n_split
        for h in range(n_split):
            lo, hi = h * nb * seq, (h + 1) * nb * seq
            st = SC_COMBINE_TOKENS
            pos = part["pos8"][:TOP_K, lo:hi].reshape(TOP_K, (hi - lo) // st, st)
            pos = pos.transpose(1, 0, 2).reshape((hi - lo) // st, TOP_K * st)
            routed = _sc_combine(ys, pos, part["w_lanes"][lo:hi])
            out = _combine(routed, part["x1"].reshape(pt, d), gate2, g_final, tt,
                           seq // tt, h * nb, part["b0"] + h * nb, n_tok, out)
        return out

    out, prev, after_front = None, None, none
    for p in range(n_parts):
        b0 = p * pb
        yconv, ut = _mix_front(x, shift1, scale1, g_mix, w_in_bf, conv_w, conv_b, g_conv,
                               min(MIX_FRONT_TOKENS, seq), b0, pb, after_front)
        ys_prev = run_experts(prev, ut) if prev is not None else None
        yt = _s5_core(ut.reshape(pb, G, H, n_chunks, L), kern, e_mat, f_mat, a_pow,
                      none if ys_prev is None else ys_prev).reshape(pb, G * H, seq)
        x1, h2p, e8, w8, r8, counts = _mix_back(
            x, yconv, yt, gate1, shift2, scale2, gate2, w_glu_bf, b_glu, g_ssm,
            w_out_bf[:D_CONV], w_out_bf[D_CONV:], g_ffn, wr_hi, wr_lo, r_bias, ws13, ws2_bf,
            ts, b0, pb)
        counts = counts.reshape(NJ, NG, -1).transpose(1, 0, 2).reshape(N_EXPERTS, -1)
        pos8 = _plan(counts, e8, r8, bm)
        xs = _sc_dispatch(h2p.reshape(pt, d // 2), pos8[:TOP_K].reshape(TOP_K * pt), n_blocks * bm)
        if prev is not None:
            out = finish(prev, ys_prev, out, 1)
        w_bits = lax.bitcast_convert_type(w8.reshape(pt, 8).astype(BF16), jnp.uint16).astype(U32)
        w_lanes = jnp.repeat(lax.bitcast_convert_type((w_bits << 16) | w_bits, I32), V7X_SC_LANES, axis=1)
        prev = dict(b0=b0, x1=x1, w_lanes=w_lanes, pos8=pos8, xs=xs,
                    block_e=_block_experts(counts[:, 0], bm, n_blocks))
        after_front = pos8
    ys_last = run_experts(prev, none if out is None else out)
    out = finish(prev, ys_last, out, LAST_PART_SPLIT if pb % LAST_PART_SPLIT == 0 else 1)
    return out.reshape(bsz, seq, d)
```

```python
import functools
import math

import jax
import jax.numpy as jnp
import numpy as np
from jax import lax
from jax.experimental import pallas as pl
from jax.experimental.pallas import tpu as pltpu
from jax.experimental.pallas import tpu_sc as plsc

F32 = jnp.float32
BF16 = jnp.bfloat16
U32 = jnp.uint32
I32 = jnp.int32

D_CONV = 768
D_SSM = 256
SSM_GROUPS = 16
SSM_GROUP_CH = 16
SSM_STATE = 64
N_EXPERTS = 64
TOP_K = 6
N_EXPERT_GROUPS = 8
TOPK_GROUPS = 4
EXPERTS_PER_GROUP = 8
ROUTED_SCALE = 2.5
RMS_EPS = 1e-6

S5_CHUNK = 128
EXPERT_ROWS = 1024
EXPERT_SUB = 512
MIX_FRONT_TOKENS = 1024
MIX_FRONT_SUB = 1024
MIX_SUB = 256
MIX_BACK_TOKENS = 1024
MIX_PHASE_LAG = 3
COMBINE_TOKENS = 512
LAYER_PARTS = 2
LAST_PART_SPLIT = 4
RESIDUAL_DTYPE = BF16
V7X_SC_CORES = 2
V7X_SC_SUBCORES = 16
V7X_SC_LANES = 16
SC_WORKERS = V7X_SC_CORES * V7X_SC_SUBCORES
SC_COMBINE_TOKENS = 8
SC_COMBINE_SLOTS = 4
SC_WINDOW = 64
V7X_VMEM_LIMIT = 56 * 1024 * 1024
NEG_INF = float("-inf")
HIGH_HALF = np.uint32(0xFFFF0000)


def _rms(x, g):
    return x * lax.rsqrt(jnp.mean(x * x, axis=-1, keepdims=True) + RMS_EPS) * g


def _gelu_tanh(x):
    return 0.5 * x * (1.0 + jnp.tanh(math.sqrt(2.0 / math.pi) * (x + 0.044715 * (x * x * x))))


def _silu(x):
    return x * jax.nn.sigmoid(x)


def _params(*sem):
    return pltpu.CompilerParams(dimension_semantics=sem, vmem_limit_bytes=V7X_VMEM_LIMIT)


def _pack_halves(y):
    return _pack_rounded(y.astype(BF16).astype(F32))


def _pack_rounded(y):
    m = y.shape[1] // 2
    bits = lax.bitcast_convert_type(y, U32)
    return (bits[:, m:] & HIGH_HALF) | (bits[:, :m] >> 16)


def _unpack_halves(w):
    lo = lax.bitcast_convert_type(w << 16, F32)
    hi = lax.bitcast_convert_type(w & HIGH_HALF, F32)
    return lo, hi


def _adaln_kernel(c_ref, w_ref, b_ref, o_ref):
    o_ref[...] = jnp.dot(_silu(c_ref[...]), w_ref[...], preferred_element_type=F32,
                         precision=lax.Precision.HIGHEST) + b_ref[...]


def _adaln(c, w_ada, b_ada):
    bsz, d = c.shape
    n = w_ada.shape[1]
    bn = 1024
    return pl.pallas_call(
        _adaln_kernel,
        grid=(n // bn,),
        in_specs=[pl.BlockSpec((bsz, d), lambda j: (0, 0)),
                  pl.BlockSpec((d, bn), lambda j: (0, j)),
                  pl.BlockSpec((1, bn), lambda j: (0, j))],
        out_specs=pl.BlockSpec((bsz, bn), lambda j: (0, j)),
        out_shape=jax.ShapeDtypeStruct((bsz, n), F32),
        compiler_params=_params("arbitrary"),
        name="adaln",
    )(c, w_ada, b_ada.reshape(1, n))


def _mix_front_kernel(x_ref, shift_ref, scale_ref, gmix_ref, win_ref, cw_ref, cb_ref, gconv_ref, after_ref,
                      yconv_ref, ut_ref, zprev_ref):
    del after_ref
    s = pl.program_id(1)

    @pl.when(s == 0)
    def _():
        zprev_ref[...] = jnp.zeros_like(zprev_ref)

    cw = cw_ref[...]
    sub = min(MIX_FRONT_SUB, x_ref.shape[1])
    carry = {"prev": zprev_ref[...]}

    def norm(st):
        x = x_ref[0, st["r0"]:st["r0"] + sub, :]
        st["h"] = (_rms(x, gmix_ref[...]) * (1.0 + scale_ref[0]) + shift_ref[0]).astype(BF16)

    def in_proj(st):
        st["proj"] = jnp.dot(st.pop("h"), win_ref[...], preferred_element_type=F32)

    def mixers(st):
        r0 = st["r0"]
        proj = st.pop("proj")
        b_gate = proj[:, :D_CONV]
        c_gate = proj[:, D_CONV:2 * D_CONV]
        v = proj[:, 2 * D_CONV:3 * D_CONV]
        u = proj[:, 3 * D_CONV:]
        z = c_gate * v
        prev = carry["prev"]
        rid = lax.broadcasted_iota(I32, z.shape, 0)
        z1 = jnp.where(rid == 0, prev[7:8, :], pltpu.roll(z, 1, axis=0))
        z2 = jnp.where(rid == 0, prev[6:7, :], jnp.where(rid == 1, prev[7:8, :], pltpu.roll(z, 2, axis=0)))
        carry["prev"] = z[sub - 8:, :]
        conv = cw[0:1, :] * z2 + cw[1:2, :] * z1 + cw[2:3, :] * z + cb_ref[...]
        yconv_ref[0, r0:r0 + sub, :] = _rms(b_gate * conv, gconv_ref[...]).astype(BF16)
        ut_ref[0, :, r0:r0 + sub] = u.T.astype(BF16)

    phases = (norm, in_proj, mixers)
    chains = [{"r0": r0} for r0 in range(0, x_ref.shape[1], sub)]
    for step in range(len(phases) + len(chains) - 1):
        for ci, st in enumerate(chains):
            if 0 <= step - ci < len(phases):
                phases[step - ci](st)
    zprev_ref[...] = carry["prev"]


def _mix_front(x, shift1, scale1, g_mix, w_in_bf, conv_w, conv_b, g_conv, ts, b0, bsz, after):
    _, seq, d = x.shape
    d_in = w_in_bf.shape[1]
    row = lambda b, s: (b + b0, 0, 0)
    const2 = lambda b, s: (0, 0)
    return pl.pallas_call(
        _mix_front_kernel,
        grid=(bsz, seq // ts),
        in_specs=[pl.BlockSpec((1, ts, d), lambda b, s: (b + b0, s, 0)),
                  pl.BlockSpec((1, 1, d), row),
                  pl.BlockSpec((1, 1, d), row),
                  pl.BlockSpec((1, d), const2),
                  pl.BlockSpec((d, d_in), const2),
                  pl.BlockSpec((8, D_CONV), const2),
                  pl.BlockSpec((1, D_CONV), const2),
                  pl.BlockSpec((1, D_CONV), const2),
                  pl.BlockSpec(memory_space=pl.ANY)],
        out_specs=[pl.BlockSpec((1, ts, D_CONV), lambda b, s: (b, s, 0)),
                   pl.BlockSpec((1, D_SSM, ts), lambda b, s: (b, 0, s))],
        out_shape=[jax.ShapeDtypeStruct((bsz, seq, D_CONV), BF16),
                   jax.ShapeDtypeStruct((bsz, D_SSM, seq), BF16)],
        scratch_shapes=[pltpu.VMEM((8, D_CONV), F32)],
        compiler_params=_params("arbitrary", "arbitrary"),
        name="mix_front",
    )(x, shift1, scale1, g_mix.reshape(1, d), w_in_bf,
      jnp.pad(conv_w, ((0, 8 - conv_w.shape[0]), (0, 0))), conv_b.reshape(1, D_CONV),
      g_conv.reshape(1, D_CONV), after)


def _s5_tables(lam_re, lam_im, log_dt, b_re, b_im, c_re, c_im, d_skip, chunk, n_chunks):
    hp = lax.Precision.HIGHEST
    G, H, P, L = SSM_GROUPS, SSM_GROUP_CH, SSM_STATE, chunk
    lr = lam_re.astype(F32)
    li = lam_im.astype(F32)
    dt = jnp.exp(log_dt.astype(F32))[:, None]
    mag = jnp.exp(lr * dt)
    ang = li * dt
    ab_re = mag * jnp.cos(ang)
    ab_im = mag * jnp.sin(ang)
    den = lr * lr + li * li
    nr = ab_re - 1.0
    ni = ab_im
    q_re = (nr * lr + ni * li) / den
    q_im = (ni * lr - nr * li) / den
    br = b_re.astype(F32)
    bi = b_im.astype(F32)
    bb_re = q_re[..., None] * br - q_im[..., None] * bi
    bb_im = q_re[..., None] * bi + q_im[..., None] * br

    def a_power(tau):
        t = tau.astype(F32)[None, :, None]
        m = jnp.exp(t * (lr * dt)[:, None, :])
        th = t * ang[:, None, :]
        return m * jnp.cos(th), m * jnp.sin(th)

    pw_re, pw_im = a_power(jnp.arange(L + 1))
    cr = c_re.astype(F32)[:, None]
    ci = c_im.astype(F32)[:, None]
    cp_re = cr * pw_re[:, :, None, :] - ci * pw_im[:, :, None, :]
    cp_im = cr * pw_im[:, :, None, :] + ci * pw_re[:, :, None, :]
    kern = (jnp.einsum('gthp,gpk->gthk', cp_re[:, :L], bb_re, precision=hp)
            - jnp.einsum('gthp,gpk->gthk', cp_im[:, :L], bb_im, precision=hp))
    kern = kern.at[:, 0].add(d_skip.astype(F32).reshape(G, H)[:, :, None] * jnp.eye(H, dtype=F32))
    kern = kern.transpose(0, 3, 2, 1).reshape(G, H * H, L)

    rev_re = pw_re[:, L - 1::-1]
    rev_im = pw_im[:, L - 1::-1]
    bt_re = bb_re.transpose(0, 2, 1)[:, :, None, :]
    bt_im = bb_im.transpose(0, 2, 1)[:, :, None, :]
    e_re = rev_re[:, None] * bt_re - rev_im[:, None] * bt_im
    e_im = rev_re[:, None] * bt_im + rev_im[:, None] * bt_re
    e_mat = jnp.concatenate([e_re, e_im], axis=-1).reshape(G, H * L, 2 * P)

    f_re = cp_re[:, 1:].transpose(0, 3, 2, 1)
    f_im = -cp_im[:, 1:].transpose(0, 3, 2, 1)
    f_mat = jnp.concatenate([f_re, f_im], axis=1).reshape(G, 2 * P, H * L)

    n_steps = max(1, (n_chunks - 1).bit_length())
    sr, si = a_power(L * (2 ** jnp.arange(n_steps)))
    a_pow = jnp.stack([jnp.concatenate([sr, sr], axis=-1),
                       jnp.concatenate([-si, si], axis=-1)], axis=2)
    return kern, e_mat.astype(BF16), f_mat.astype(BF16), a_pow


def _s5_kernel(u_ref, k_ref, e_ref, f_ref, a_ref, after_ref, y_ref, toep_ref, *, n_chunks, n_steps, chunk):
    del after_ref
    L, H = chunk, SSM_GROUP_CH

    causal = lax.broadcasted_iota(I32, (L, L), 1) >= lax.broadcasted_iota(I32, (L, L), 0)

    def build(hin, carry):
        r0 = pl.multiple_of(hin * L, L)
        for hout in range(H):
            krow = k_ref[0, pl.ds(hin * H + hout, 1), :]
            blk = pltpu.roll(jnp.broadcast_to(krow, (L, L)), 0, axis=1, stride=1, stride_axis=0)
            toep_ref[pl.ds(r0, L), hout * L:(hout + 1) * L] = jnp.where(causal, blk, 0.0).astype(BF16)
        return carry

    lax.fori_loop(0, H, build, 0)

    bsz = u_ref.shape[0]
    u = jnp.concatenate([u_ref[:, h].reshape(bsz * n_chunks, L) for h in range(H)], axis=1)
    st = jnp.dot(u, e_ref[0], preferred_element_type=F32)
    cidx = lax.broadcasted_iota(I32, st.shape, 0) % n_chunks
    p2 = st.shape[1]

    def shifted(xv, d):
        return jnp.where(cidx >= d, pltpu.roll(xv, d, axis=0), 0.0)

    for k in range(n_steps):
        d = 1 << k
        if d >= n_chunks:
            break
        z = shifted(st, d)
        st = st + a_ref[0, k, 0:1, :] * z + a_ref[0, k, 1:2, :] * pltpu.roll(z, p2 // 2, axis=1)
    s_in = shifted(st, 1)
    y = jnp.dot(u, toep_ref[...], preferred_element_type=F32)
    y = (y + jnp.dot(s_in.astype(BF16), f_ref[0], preferred_element_type=F32)).astype(BF16)
    for h in range(H):
        y_ref[:, h] = y[:, h * L:(h + 1) * L].reshape(bsz, n_chunks, L)


def _s5_core(ut5, kern, e_mat, f_mat, a_pow, after):
    bsz, g, h, n_chunks, chunk = ut5.shape
    hl = h * chunk
    n_steps = a_pow.shape[1]
    p2 = e_mat.shape[2]
    blk = lambda i: (i, 0, 0)
    seq_blk = pl.BlockSpec((bsz, None, h, n_chunks, chunk), lambda i: (0, i, 0, 0, 0))
    return pl.pallas_call(
        functools.partial(_s5_kernel, n_chunks=n_chunks, n_steps=n_steps, chunk=chunk),
        grid=(g,),
        in_specs=[seq_blk,
                  pl.BlockSpec((1,) + kern.shape[1:], blk),
                  pl.BlockSpec((1, hl, p2), blk),
                  pl.BlockSpec((1, p2, hl), blk),
                  pl.BlockSpec((1, n_steps, 2, p2), lambda i: (i, 0, 0, 0)),
                  pl.BlockSpec(memory_space=pl.ANY)],
        out_specs=seq_blk,
        out_shape=jax.ShapeDtypeStruct(ut5.shape, BF16),
        scratch_shapes=[pltpu.VMEM((hl, hl), BF16)],
        compiler_params=_params("arbitrary"),
        name="s5_core",
    )(ut5, kern, e_mat, f_mat, a_pow, after)


def _route(scores, biased, tri, base):
    n_tok = scores.shape[1]
    shape3 = (EXPERTS_PER_GROUP, N_EXPERT_GROUPS, n_tok)
    sc3 = scores.reshape(shape3)
    b3 = biased.reshape(shape3)
    j_iota = lax.broadcasted_iota(I32, shape3, 0)
    e_iota = lax.broadcasted_iota(I32, shape3, 1) * EXPERTS_PER_GROUP + j_iota

    def red(fn, x):
        return fn(fn(x, axis=0, keepdims=True), axis=1, keepdims=True)

    m1 = jnp.max(b3, axis=0, keepdims=True)
    i1 = jnp.min(jnp.where(b3 == m1, j_iota, EXPERTS_PER_GROUP), axis=0, keepdims=True)
    m2 = jnp.max(jnp.where(j_iota == i1, NEG_INF, b3), axis=0, keepdims=True)
    gs = m1 + m2

    g_iota = lax.broadcasted_iota(I32, gs.shape, 1)
    gsel = jnp.zeros(gs.shape, F32)
    cur = gs
    for _ in range(TOPK_GROUPS):
        m = jnp.max(cur, axis=1, keepdims=True)
        ig = jnp.min(jnp.where(cur == m, g_iota, N_EXPERT_GROUPS), axis=1, keepdims=True)
        pick = g_iota == ig
        gsel = jnp.where(pick, 1.0, gsel)
        cur = jnp.where(pick, NEG_INF, cur)

    cur = jnp.where(gsel > 0.0, b3, NEG_INF)
    sel = jnp.zeros(shape3, F32)
    ids, vals = [], []
    for _ in range(TOP_K):
        m = red(jnp.max, cur)
        ie = red(jnp.min, jnp.where(cur == m, e_iota, N_EXPERTS))
        pick = e_iota == ie
        ids.append(ie)
        vals.append(red(jnp.sum, jnp.where(pick, sc3, 0.0)))
        sel = jnp.where(pick, 1.0, sel)
        cur = jnp.where(pick, NEG_INF, cur)
    tot = functools.reduce(lambda a, b: a + b, vals)

    sel2 = sel.reshape(N_EXPERTS, n_tok)
    before = jnp.dot(sel2.astype(BF16), tri, preferred_element_type=F32) + base
    before3 = before.reshape(shape3)
    ranks = [red(jnp.sum, jnp.where(e_iota == ie, before3, 0.0)) for ie in ids]

    def rows(parts, dtype):
        parts = [p.reshape(1, n_tok).astype(dtype) for p in parts]
        return jnp.concatenate(parts + [jnp.zeros((8 - len(parts), n_tok), dtype)], axis=0)

    e8 = rows(ids, I32)
    w8 = rows([v / tot * ROUTED_SCALE for v in vals], F32)
    r8 = rows(ranks, I32)
    return e8, w8, r8, jnp.sum(sel2, axis=1, keepdims=True)


def _mix_back_kernel(x_ref, yconv_ref, yt_ref, gate1_ref, shift2_ref, scale2_ref, gate2_ref,
                     wglu_ref, bglu_ref, gssm_ref, woc_ref, wos_ref, gffn_ref,
                     wrh_ref, wrl_ref, rbias_ref, ws13_ref, ws2_ref, tri_ref,
                     x1_ref, h2p_ref, e8_ref, w8_ref, r8_ref, cnt_ref):
    first = jnp.logical_and(pl.program_id(0) == 0, pl.program_id(1) == 0)

    @pl.when(first)
    def _():
        cnt_ref[...] = jnp.zeros_like(cnt_ref)

    sub = tri_ref.shape[0]

    def ssm_post(st):
        r0 = st["r0"]
        y = yt_ref[0, :, r0:r0 + sub].astype(F32).T
        y = _gelu_tanh(y)
        y = y * jax.nn.sigmoid(jnp.dot(y.astype(BF16), wglu_ref[...], preferred_element_type=F32)
                               + bglu_ref[...])
        st["y_ssm"] = _rms(y, gssm_ref[...]).astype(BF16)

    def out_proj(st):
        r0 = st["r0"]
        mix = jnp.dot(yconv_ref[0, r0:r0 + sub, :], woc_ref[...], preferred_element_type=F32)
        mix = mix + jnp.dot(st.pop("y_ssm"), wos_ref[...], preferred_element_type=F32)
        st["x1"] = x_ref[0, r0:r0 + sub, :] + gate1_ref[0] * mix

    def ffn_in(st):
        r0 = st["r0"]
        h2 = _rms(st["x1"], gffn_ref[...] * (1.0 + scale2_ref[0])) + shift2_ref[0]
        h2b = h2.astype(BF16)
        h2r = h2b.astype(F32)
        h2p_ref[0, r0:r0 + sub, :] = lax.bitcast_convert_type(_pack_rounded(h2r), I32)
        st["h2b"] = h2b
        st["h2l"] = (h2 - h2r).astype(BF16)

    def shared_expert(st):
        r0 = st["r0"]
        a = jnp.dot(st["h2b"], ws13_ref[...], preferred_element_type=F32)
        f = a.shape[1] // 2
        act = _silu(a[:, :f]) * a[:, f:]
        shared = jnp.dot(act.astype(BF16), ws2_ref[...], preferred_element_type=F32)
        x1_ref[0, r0:r0 + sub, :] = (st.pop("x1") + gate2_ref[0] * shared).astype(x1_ref.dtype)

    def router(st):
        lt = (jnp.dot(st.pop("h2b"), wrh_ref[...], preferred_element_type=F32)
              + jnp.dot(st.pop("h2l"), wrl_ref[...], preferred_element_type=F32)).T
        st["scores"] = jax.nn.sigmoid(lt[:N_EXPERTS] + lt[N_EXPERTS:])

    def route(st):
        r0 = st["r0"]
        scores = st.pop("scores")
        e8, w8, r8, cnt = _route(scores, scores + rbias_ref[...], tri_ref[...], carry["base"])
        e8_ref[:, r0:r0 + sub] = e8
        w8_ref[0, r0:r0 + sub, :] = w8.T
        r8_ref[:, r0:r0 + sub] = r8
        carry["base"] = carry["base"] + cnt

    carry = {"base": cnt_ref[:, 0:1]}
    phases = (ssm_post, out_proj, ffn_in, shared_expert, router, route)
    chains = [{"r0": r0} for r0 in range(0, x_ref.shape[1], sub)]
    lag = MIX_PHASE_LAG
    for step in range(len(phases) + lag * (len(chains) - 1)):
        for ci, st in enumerate(chains):
            ph = step - lag * ci
            if 0 <= ph < len(phases):
                phases[ph](st)
    cnt_ref[...] = jnp.broadcast_to(carry["base"], cnt_ref.shape)


def _mix_back(x, yconv, yt, gate1, shift2, scale2, gate2, w_glu_bf, b_glu, g_ssm, wo_conv, wo_ssm,
              g_ffn, wr_hi, wr_lo, router_bias, ws13, ws2_bf, ts, b0, bsz):
    _, seq, d = x.shape
    n_tok = bsz * seq
    tiles = seq // ts
    row = lambda b, s: (b + b0, 0, 0)
    const2 = lambda b, s: (0, 0)
    tile_in = lambda b, s: (b + b0, s, 0)
    tile = lambda b, s: (b, s, 0)
    flat = lambda b, s: (0, b * tiles + s)
    full = lambda a: pl.BlockSpec(a.shape, const2)
    sub = min(MIX_SUB, ts)
    tri = jnp.triu(jnp.ones((sub, sub), BF16), k=1)
    args = (w_glu_bf, b_glu.reshape(1, D_SSM), g_ssm.reshape(1, D_SSM), wo_conv, wo_ssm,
            g_ffn.reshape(1, d), wr_hi, wr_lo, router_bias.reshape(N_EXPERTS, 1), ws13, ws2_bf, tri)
    return pl.pallas_call(
        _mix_back_kernel,
        grid=(bsz, tiles),
        in_specs=[pl.BlockSpec((1, ts, d), tile_in),
                  pl.BlockSpec((1, ts, D_CONV), tile),
                  pl.BlockSpec((1, D_SSM, ts), lambda b, s: (b, 0, s)),
                  pl.BlockSpec((1, 1, d), row), pl.BlockSpec((1, 1, d), row),
                  pl.BlockSpec((1, 1, d), row), pl.BlockSpec((1, 1, d), row)]
                 + [full(a) for a in args],
        out_specs=[pl.BlockSpec((1, ts, d), tile),
                   pl.BlockSpec((1, ts, d // 2), tile),
                   pl.BlockSpec((8, ts), flat),
                   pl.BlockSpec((1, ts, 8), tile),
                   pl.BlockSpec((8, ts), flat),
                   pl.BlockSpec((N_EXPERTS, 128), const2)],
        out_shape=[jax.ShapeDtypeStruct((bsz, seq, d), RESIDUAL_DTYPE),
                   jax.ShapeDtypeStruct((bsz, seq, d // 2), I32),
                   jax.ShapeDtypeStruct((8, n_tok), I32),
                   jax.ShapeDtypeStruct((bsz, seq, 8), F32),
                   jax.ShapeDtypeStruct((8, n_tok), I32),
                   jax.ShapeDtypeStruct((N_EXPERTS, 128), F32)],
        compiler_params=_params("arbitrary", "arbitrary"),
        name="mix_back",
    )(x, yconv, yt, gate1, shift2, scale2, gate2, *args)


def _plan_kernel(cnt_ref, e8_ref, r8_ref, pos_ref, *, bm):
    cnt = cnt_ref[...].astype(I32)
    padded = (cnt + (bm - 1)) // bm * bm
    rid = lax.broadcasted_iota(I32, padded.shape, 0)
    incl = padded
    d = 1
    while d < N_EXPERTS:
        incl = incl + jnp.where(rid >= d, pltpu.roll(incl, d, axis=0), 0)
        d *= 2
    pstart = incl - padded
    lanes = pstart.shape[1]

    def chunk(ci, carry):
        c0 = pl.multiple_of(ci * lanes, lanes)
        e = e8_ref[:, pl.ds(c0, lanes)]
        acc = r8_ref[:, pl.ds(c0, lanes)]
        for ex in range(N_EXPERTS):
            acc = acc + jnp.where(e == ex, pstart[ex:ex + 1, :], 0)
        pos_ref[:, pl.ds(c0, lanes)] = acc
        return carry

    lax.fori_loop(0, e8_ref.shape[1] // lanes, chunk, 0)


def _plan(counts, e8, r8, bm):
    n_tok = e8.shape[1]
    tb = min(8192, n_tok)
    return pl.pallas_call(
        functools.partial(_plan_kernel, bm=bm),
        grid=(n_tok // tb,),
        in_specs=[pl.BlockSpec(counts.shape, lambda i: (0, 0)),
                  pl.BlockSpec((8, tb), lambda i: (0, i)),
                  pl.BlockSpec((8, tb), lambda i: (0, i))],
        out_specs=pl.BlockSpec((8, tb), lambda i: (0, i)),
        out_shape=jax.ShapeDtypeStruct((8, n_tok), I32),
        compiler_params=_params("arbitrary"),
        name="plan",
    )(counts, e8, r8)


def _sc_mesh():
    return plsc.VectorSubcoreMesh(core_axis_name="c", subcore_axis_name="s",
                                  num_cores=V7X_SC_CORES, num_subcores=V7X_SC_SUBCORES)


def _sc_worker_base(per_worker):
    return (lax.axis_index("s") * V7X_SC_CORES + lax.axis_index("c")) * per_worker


def _sc_dispatch(h2p, posflat, n_rows):
    n_tok, half = h2p.shape
    win = SC_WINDOW
    per_worker = n_tok // SC_WORKERS
    n_win = per_worker // win

    wins_per_k = n_tok // win

    @functools.partial(
        pl.kernel, mesh=_sc_mesh(),
        out_type=jax.ShapeDtypeStruct((n_rows, half), I32),
        scratch_types=[pltpu.VMEM((TOP_K, n_win, win), I32), pltpu.VMEM((2, win, half), I32),
                       pltpu.SemaphoreType.DMA((2,)), pltpu.SemaphoreType.DMA((2,))],
        name="sc_dispatch")
    def run(h2p_hbm, pos_hbm, xs_hbm, idx_v, rows_v, sem_in, sem_out):
        wid = _sc_worker_base(1)
        for k in range(TOP_K):
            pltpu.sync_copy(pos_hbm.at[pl.ds(k * wins_per_k + wid * n_win, n_win)], idx_v.at[k])

        @pl.loop(0, n_win, step=2)
        def _(j):
            loads = []
            for b in range(2):
                t0 = pl.multiple_of((wid * n_win + j + b) * win, win)
                loads.append(pltpu.async_copy(h2p_hbm.at[pl.ds(t0, win)], rows_v.at[b], sem_in.at[b]))
            stores = []
            for b in range(2):
                loads[b].wait()
                for k in range(TOP_K):
                    stores.append(pltpu.async_copy(rows_v.at[b], xs_hbm.at[idx_v.at[k, j + b]], sem_out.at[b]))
            for cp in stores:
                cp.wait()

    return run(h2p, posflat.reshape(TOP_K * wins_per_k, win))


def _sc_combine(ys, pos_steps, w_lanes):
    n_tok = w_lanes.shape[0]
    half = ys.shape[1]
    step, slots, lanes = SC_COMBINE_TOKENS, SC_COMBINE_SLOTS, V7X_SC_LANES
    per_worker = n_tok // SC_WORKERS
    n_steps = per_worker // step

    @functools.partial(
        pl.kernel, mesh=_sc_mesh(),
        out_type=jax.ShapeDtypeStruct((n_tok, half), I32),
        scratch_types=[pltpu.VMEM((n_steps, TOP_K * step), I32), pltpu.VMEM((slots, TOP_K * step, half), I32),
                       pltpu.VMEM((slots, step, w_lanes.shape[1]), I32), pltpu.VMEM((step, half), I32),
                       pltpu.SemaphoreType.DMA((slots,))],
        compiler_params=pltpu.CompilerParams(needs_layout_passes=False),
        name="sc_combine")
    def run(ys_hbm, pos_hbm, w_hbm, out_hbm, idx_v, rows_v, w_v, out_v, sem):
        wid = _sc_worker_base(1)
        pltpu.sync_copy(pos_hbm.at[pl.ds(wid * n_steps, n_steps)], idx_v)

        def copies(s, slot):
            t0 = pl.multiple_of(wid * per_worker + s * step, step)
            return [pltpu.make_async_copy(ys_hbm.at[idx_v.at[s]], rows_v.at[slot], sem.at[slot]),
                    pltpu.make_async_copy(w_hbm.at[pl.ds(t0, step)], w_v.at[slot], sem.at[slot])]

        def compute(s, slot):
            @pl.loop(0, step)
            def _(i):
                wb = [plsc.bitcast(w_v[slot, i, pl.ds(k * lanes, lanes)], BF16) for k in range(TOP_K)]
                for c in range(0, half, lanes):
                    t = [wb[k] * plsc.bitcast(rows_v[slot, k * step + i, pl.ds(c, lanes)], BF16)
                         for k in range(TOP_K)]
                    while len(t) > 1:
                        t = [t[a] + t[a + 1] for a in range(0, len(t) - 1, 2)] + t[len(t) & ~1:]
                    out_v[i, pl.ds(c, lanes)] = plsc.bitcast(t[0], I32)

            t0 = pl.multiple_of(wid * per_worker + s * step, step)
            pltpu.sync_copy(out_v, out_hbm.at[pl.ds(t0, step)])

        for j in range(slots - 1):
            for cp in copies(j, j):
                cp.start()

        @pl.loop(0, n_steps, step=slots)
        def _(s):
            for j in range(slots):
                ahead = s + j + slots - 1

                @pl.when(ahead < n_steps)
                def _():
                    for cp in copies(ahead, (j + slots - 1) % slots):
                        cp.start()

                for cp in copies(s + j, j):
                    cp.wait()
                compute(s + j, j)

    return run(ys, pos_steps, w_lanes)


def _experts_kernel(be_ref, nu_ref, x_ref, w1_ref, w3_ref, w2_ref, after_ref, y_ref, w13_s, w2_s):
    del after_ref
    i = pl.program_id(0)
    changed = jnp.logical_or(i == 0, be_ref[i] != be_ref[jnp.maximum(i - 1, 0)])

    @pl.when(changed)
    def _():
        f = w1_ref.shape[2]
        w13_s[:, :f] = w1_ref[0].astype(BF16)
        w13_s[:, f:] = w3_ref[0].astype(BF16)
        w2_s[...] = w2_ref[0].astype(BF16)

    @pl.when(i < nu_ref[0])
    def _():
        half = x_ref.shape[1]
        sub = min(EXPERT_SUB, x_ref.shape[0])
        for r0 in range(0, x_ref.shape[0], sub):
            lo, hi = _unpack_halves(lax.bitcast_convert_type(x_ref[r0:r0 + sub, :], U32))
            a = jnp.dot(lo.astype(BF16), w13_s[:half, :], preferred_element_type=F32)
            a = a + jnp.dot(hi.astype(BF16), w13_s[half:, :], preferred_element_type=F32)
            f = a.shape[1] // 2
            act = _silu(a[:, :f]) * a[:, f:]
            y = jnp.dot(act.astype(BF16), w2_s[...], preferred_element_type=F32)
            y_ref[r0:r0 + sub, :] = lax.bitcast_convert_type(_pack_halves(y), I32)


def _experts(block_e, n_used, xs, w1, w3, w2, bm, after):
    n_rows, half = xs.shape
    d, f = w1.shape[1], w1.shape[2]
    rows = lambda i, be, nu: (jnp.minimum(i, nu[0] - 1), 0)
    wblk = lambda i, be, nu: (be[i], 0, 0)
    grid_spec = pltpu.PrefetchScalarGridSpec(
        num_scalar_prefetch=2,
        grid=(n_rows // bm,),
        in_specs=[pl.BlockSpec((bm, half), rows),
                  pl.BlockSpec((1, d, f), wblk),
                  pl.BlockSpec((1, d, f), wblk),
                  pl.BlockSpec((1, f, d), wblk),
                  pl.BlockSpec(memory_space=pl.ANY)],
        out_specs=pl.BlockSpec((bm, half), rows),
        scratch_shapes=[pltpu.VMEM((d, 2 * f), BF16), pltpu.VMEM((f, d), BF16)],
    )
    return pl.pallas_call(
        _experts_kernel,
        grid_spec=grid_spec,
        out_shape=jax.ShapeDtypeStruct((n_rows, half), I32),
        compiler_params=_params("arbitrary"),
        name="experts",
    )(block_e, n_used, xs, w1, w3, w2, after)


def _combine_kernel(r_ref, x1_ref, gate2_ref, gfin_ref, *rest):
    o_ref = rest[-1]
    routed = jnp.concatenate(_unpack_halves(lax.bitcast_convert_type(r_ref[...], U32)), axis=1)
    o_ref[...] = _rms(x1_ref[...].astype(F32) + gate2_ref[0] * routed, gfin_ref[...])


def _combine(routed, x1, gate2, g_final, tt, tiles_per_seq, b_src, b0, n_tok_all, out_prev):
    d = x1.shape[1]
    n_tok, half = routed.shape
    src0 = b_src * tiles_per_seq
    tile0 = b0 * tiles_per_seq
    in_specs = [pl.BlockSpec((tt, half), lambda i: (i, 0)),
                pl.BlockSpec((tt, d), lambda i: (i + src0, 0)),
                pl.BlockSpec((1, 1, d), lambda i: (i // tiles_per_seq + b0, 0, 0)),
                pl.BlockSpec((1, d), lambda i: (0, 0))]
    args = [routed, x1, gate2, g_final.reshape(1, d)]
    aliases = {}
    if out_prev is not None:
        in_specs.append(pl.BlockSpec(memory_space=pl.ANY))
        args.append(out_prev)
        aliases = {len(args) - 1: 0}
    return pl.pallas_call(
        _combine_kernel,
        grid=(n_tok // tt,),
        in_specs=in_specs,
        out_specs=pl.BlockSpec((tt, d), lambda i: (i + tile0, 0)),
        out_shape=jax.ShapeDtypeStruct((n_tok_all, d), F32),
        input_output_aliases=aliases,
        compiler_params=_params("arbitrary"),
        name="combine",
    )(*args)


def _block_experts(counts, bm, n_blocks):
    padded = (counts.astype(I32) + (bm - 1)) // bm * bm
    pend = jnp.cumsum(padded)
    starts = jnp.arange(n_blocks, dtype=I32) * bm
    block_e = jnp.minimum(jnp.sum((pend[None, :] <= starts[:, None]).astype(I32), axis=1), N_EXPERTS - 1)
    return block_e, (pend[-1:] // bm).astype(I32)


def kernel(x, c, w_ada, b_ada, g_mix, w_in, conv_w, conv_b, g_conv, lam_re, lam_im, log_dt, b_re, b_im,
           c_re, c_im, d_skip, w_glu, b_glu, g_ssm, w_out, g_ffn, w_router, router_bias, w1, w3, w2,
           ws1, ws3, ws2, g_final):
    bsz, seq, d = x.shape
    n_tok = bsz * seq
    ts = min(MIX_BACK_TOKENS, seq)
    L = min(S5_CHUNK, seq)
    n_chunks = seq // L
    G, H = SSM_GROUPS, SSM_GROUP_CH

    mod = _adaln(c, w_ada, b_ada).reshape(bsz, 6, 1, d)
    shift1, scale1, gate1, shift2, scale2, gate2 = (mod[:, i] for i in range(6))

    kern, e_mat, f_mat, a_pow = _s5_tables(lam_re, lam_im, log_dt, b_re, b_im, c_re, c_im, d_skip,
                                           L, n_chunks)
    w_in_bf = w_in.astype(BF16)

    NG, NJ = N_EXPERT_GROUPS, EXPERTS_PER_GROUP
    w_r = w_router.astype(F32).reshape(d, NG, NJ).transpose(0, 2, 1).reshape(d, N_EXPERTS)
    r_bias = router_bias.reshape(NG, NJ).T.reshape(N_EXPERTS)
    w_r_hi = w_r.astype(BF16)
    wr_hi = jnp.concatenate([w_r_hi, (w_r - w_r_hi.astype(F32)).astype(BF16)], axis=1)
    wr_lo = jnp.concatenate([w_r_hi, jnp.zeros_like(w_r_hi)], axis=1)
    w_out_bf = w_out.astype(BF16)
    w_glu_bf = w_glu.astype(BF16)
    ws13 = jnp.concatenate([ws1, ws3], axis=1).astype(BF16)
    ws2_bf = ws2.astype(BF16)
    bm = EXPERT_ROWS
    tt = min(COMBINE_TOKENS, seq)

    n_parts = LAYER_PARTS if bsz % LAYER_PARTS == 0 else 1
    pb = bsz // n_parts
    pt = pb * seq
    n_blocks = -(-(pt * TOP_K + N_EXPERTS * (bm - 1)) // bm)
    none = jnp.zeros((8, 128), F32)

    def run_experts(part, after):
        return _experts(*part["block_e"], part["xs"], w1, w3, w2, bm, after)

    def finish(part, ys, out, n_split):
        nb = pb // n_split
        for h in range(n_split):
            lo, hi = h * nb * seq, (h + 1) * nb * seq
            st = SC_COMBINE_TOKENS
            pos = part["pos8"][:TOP_K, lo:hi].reshape(TOP_K, (hi - lo) // st, st)
            pos = pos.transpose(1, 0, 2).reshape((hi - lo) // st, TOP_K * st)
            routed = _sc_combine(ys, pos, part["w_lanes"][lo:hi])
            out = _combine(routed, part["x1"].reshape(pt, d), gate2, g_final, tt,
                           seq // tt, h * nb, part["b0"] + h * nb, n_tok, out)
        return out

    out, prev, after_front = None, None, none
    for p in range(n_parts):
        b0 = p * pb
        yconv, ut = _mix_front(x, shift1, scale1, g_mix, w_in_bf, conv_w, conv_b, g_conv,
                               min(MIX_FRONT_TOKENS, seq), b0, pb, after_front)
        ys_prev = run_experts(prev, ut) if prev is not None else None
        yt = _s5_core(ut.reshape(pb, G, H, n_chunks, L), kern, e_mat, f_mat, a_pow,
                      none if ys_prev is None else ys_prev).reshape(pb, G * H, seq)
        x1, h2p, e8, w8, r8, counts = _mix_back(
            x, yconv, yt, gate1, shift2, scale2, gate2, w_glu_bf, b_glu, g_ssm,
            w_out_bf[:D_CONV], w_out_bf[D_CONV:], g_ffn, wr_hi, wr_lo, r_bias, ws13, ws2_bf,
            ts, b0, pb)
        counts = counts.reshape(NJ, NG, -1).transpose(1, 0, 2).reshape(N_EXPERTS, -1)
        pos8 = _plan(counts, e8, r8, bm)
        xs = _sc_dispatch(h2p.reshape(pt, d // 2), pos8[:TOP_K].reshape(TOP_K * pt), n_blocks * bm)
        if prev is not None:
            out = finish(prev, ys_prev, out, 1)
        w_bits = lax.bitcast_convert_type(w8.reshape(pt, 8).astype(BF16), jnp.uint16).astype(U32)
        w_lanes = jnp.repeat(lax.bitcast_convert_type((w_bits << 16) | w_bits, I32), V7X_SC_LANES, axis=1)
        prev = dict(b0=b0, x1=x1, w_lanes=w_lanes, pos8=pos8, xs=xs,
                    block_e=_block_experts(counts[:, 0], bm, n_blocks))
        after_front = pos8
    ys_last = run_experts(prev, none if out is None else out)
    out = finish(prev, ys_last, out, LAST_PART_SPLIT if pb % LAST_PART_SPLIT == 0 else 1)
    return out.reshape(bsz, seq, d)
```

```python
import functools
import math

import jax
import jax.numpy as jnp
import numpy as np
from jax import lax
from jax.experimental import pallas as pl
from jax.experimental.pallas import tpu as pltpu
from jax.experimental.pallas import tpu_sc as plsc

F32 = jnp.float32
BF16 = jnp.bfloat16
U32 = jnp.uint32
I32 = jnp.int32

D_CONV = 768
D_SSM = 256
SSM_GROUPS = 16
SSM_GROUP_CH = 16
SSM_STATE = 64
N_EXPERTS = 64
TOP_K = 6
N_EXPERT_GROUPS = 8
TOPK_GROUPS = 4
EXPERTS_PER_GROUP = 8
ROUTED_SCALE = 2.5
RMS_EPS = 1e-6

S5_CHUNK = 128
EXPERT_ROWS = 1024
EXPERT_SUB = 512
MIX_FRONT_TOKENS = 1024
MIX_FRONT_SUB = 1024
MIX_SUB = 256
MIX_BACK_TOKENS = 1024
MIX_PHASE_LAG = 3
COMBINE_TOKENS = 512
LAYER_PARTS = 2
LAST_PART_SPLIT = 4
RESIDUAL_DTYPE = BF16
V7X_SC_CORES = 2
V7X_SC_SUBCORES = 16
V7X_SC_LANES = 16
SC_WORKERS = V7X_SC_CORES * V7X_SC_SUBCORES
SC_COMBINE_TOKENS = 16
SC_WINDOW = 64
V7X_VMEM_LIMIT = 56 * 1024 * 1024
NEG_INF = float("-inf")
HIGH_HALF = np.uint32(0xFFFF0000)


def _rms(x, g):
    return x * lax.rsqrt(jnp.mean(x * x, axis=-1, keepdims=True) + RMS_EPS) * g


def _gelu_tanh(x):
    return 0.5 * x * (1.0 + jnp.tanh(math.sqrt(2.0 / math.pi) * (x + 0.044715 * (x * x * x))))


def _silu(x):
    return x * jax.nn.sigmoid(x)


def _params(*sem):
    return pltpu.CompilerParams(dimension_semantics=sem, vmem_limit_bytes=V7X_VMEM_LIMIT)


def _pack_halves(y):
    return _pack_rounded(y.astype(BF16).astype(F32))


def _pack_rounded(y):
    m = y.shape[1] // 2
    bits = lax.bitcast_convert_type(y, U32)
    return (bits[:, m:] & HIGH_HALF) | (bits[:, :m] >> 16)


def _unpack_halves(w):
    lo = lax.bitcast_convert_type(w << 16, F32)
    hi = lax.bitcast_convert_type(w & HIGH_HALF, F32)
    return lo, hi


def _adaln_kernel(c_ref, w_ref, b_ref, o_ref):
    o_ref[...] = jnp.dot(_silu(c_ref[...]), w_ref[...], preferred_element_type=F32,
                         precision=lax.Precision.HIGHEST) + b_ref[...]


def _adaln(c, w_ada, b_ada):
    bsz, d = c.shape
    n = w_ada.shape[1]
    bn = 1024
    return pl.pallas_call(
        _adaln_kernel,
        grid=(n // bn,),
        in_specs=[pl.BlockSpec((bsz, d), lambda j: (0, 0)),
                  pl.BlockSpec((d, bn), lambda j: (0, j)),
                  pl.BlockSpec((1, bn), lambda j: (0, j))],
        out_specs=pl.BlockSpec((bsz, bn), lambda j: (0, j)),
        out_shape=jax.ShapeDtypeStruct((bsz, n), F32),
        compiler_params=_params("arbitrary"),
        name="adaln",
    )(c, w_ada, b_ada.reshape(1, n))


def _mix_front_kernel(x_ref, shift_ref, scale_ref, gmix_ref, win_ref, cw_ref, cb_ref, gconv_ref, after_ref,
                      yconv_ref, ut_ref, zprev_ref):
    del after_ref
    s = pl.program_id(1)

    @pl.when(s == 0)
    def _():
        zprev_ref[...] = jnp.zeros_like(zprev_ref)

    cw = cw_ref[...]
    sub = min(MIX_FRONT_SUB, x_ref.shape[1])
    carry = {"prev": zprev_ref[...]}

    def norm(st):
        x = x_ref[0, st["r0"]:st["r0"] + sub, :]
        st["h"] = (_rms(x, gmix_ref[...]) * (1.0 + scale_ref[0]) + shift_ref[0]).astype(BF16)

    def in_proj(st):
        st["proj"] = jnp.dot(st.pop("h"), win_ref[...], preferred_element_type=F32)

    def mixers(st):
        r0 = st["r0"]
        proj = st.pop("proj")
        b_gate = proj[:, :D_CONV]
        c_gate = proj[:, D_CONV:2 * D_CONV]
        v = proj[:, 2 * D_CONV:3 * D_CONV]
        u = proj[:, 3 * D_CONV:]
        z = c_gate * v
        prev = carry["prev"]
        rid = lax.broadcasted_iota(I32, z.shape, 0)
        z1 = jnp.where(rid == 0, prev[7:8, :], pltpu.roll(z, 1, axis=0))
        z2 = jnp.where(rid == 0, prev[6:7, :], jnp.where(rid == 1, prev[7:8, :], pltpu.roll(z, 2, axis=0)))
        carry["prev"] = z[sub - 8:, :]
        conv = cw[0:1, :] * z2 + cw[1:2, :] * z1 + cw[2:3, :] * z + cb_ref[...]
        yconv_ref[0, r0:r0 + sub, :] = _rms(b_gate * conv, gconv_ref[...]).astype(BF16)
        ut_ref[0, :, r0:r0 + sub] = u.T.astype(BF16)

    phases = (norm, in_proj, mixers)
    chains = [{"r0": r0} for r0 in range(0, x_ref.shape[1], sub)]
    for step in range(len(phases) + len(chains) - 1):
        for ci, st in enumerate(chains):
            if 0 <= step - ci < len(phases):
                phases[step - ci](st)
    zprev_ref[...] = carry["prev"]


def _mix_front(x, shift1, scale1, g_mix, w_in_bf, conv_w, conv_b, g_conv, ts, b0, bsz, after):
    _, seq, d = x.shape
    d_in = w_in_bf.shape[1]
    row = lambda b, s: (b + b0, 0, 0)
    const2 = lambda b, s: (0, 0)
    return pl.pallas_call(
        _mix_front_kernel,
        grid=(bsz, seq // ts),
        in_specs=[pl.BlockSpec((1, ts, d), lambda b, s: (b + b0, s, 0)),
                  pl.BlockSpec((1, 1, d), row),
                  pl.BlockSpec((1, 1, d), row),
                  pl.BlockSpec((1, d), const2),
                  pl.BlockSpec((d, d_in), const2),
                  pl.BlockSpec((8, D_CONV), const2),
                  pl.BlockSpec((1, D_CONV), const2),
                  pl.BlockSpec((1, D_CONV), const2),
                  pl.BlockSpec(memory_space=pl.ANY)],
        out_specs=[pl.BlockSpec((1, ts, D_CONV), lambda b, s: (b, s, 0)),
                   pl.BlockSpec((1, D_SSM, ts), lambda b, s: (b, 0, s))],
        out_shape=[jax.ShapeDtypeStruct((bsz, seq, D_CONV), BF16),
                   jax.ShapeDtypeStruct((bsz, D_SSM, seq), BF16)],
        scratch_shapes=[pltpu.VMEM((8, D_CONV), F32)],
        compiler_params=_params("arbitrary", "arbitrary"),
        name="mix_front",
    )(x, shift1, scale1, g_mix.reshape(1, d), w_in_bf,
      jnp.pad(conv_w, ((0, 8 - conv_w.shape[0]), (0, 0))), conv_b.reshape(1, D_CONV),
      g_conv.reshape(1, D_CONV), after)


def _s5_tables(lam_re, lam_im, log_dt, b_re, b_im, c_re, c_im, d_skip, chunk, n_chunks):
    hp = lax.Precision.HIGHEST
    G, H, P, L = SSM_GROUPS, SSM_GROUP_CH, SSM_STATE, chunk
    lr = lam_re.astype(F32)
    li = lam_im.astype(F32)
    dt = jnp.exp(log_dt.astype(F32))[:, None]
    mag = jnp.exp(lr * dt)
    ang = li * dt
    ab_re = mag * jnp.cos(ang)
    ab_im = mag * jnp.sin(ang)
    den = lr * lr + li * li
    nr = ab_re - 1.0
    ni = ab_im
    q_re = (nr * lr + ni * li) / den
    q_im = (ni * lr - nr * li) / den
    br = b_re.astype(F32)
    bi = b_im.astype(F32)
    bb_re = q_re[..., None] * br - q_im[..., None] * bi
    bb_im = q_re[..., None] * bi + q_im[..., None] * br

    def a_power(tau):
        t = tau.astype(F32)[None, :, None]
        m = jnp.exp(t * (lr * dt)[:, None, :])
        th = t * ang[:, None, :]
        return m * jnp.cos(th), m * jnp.sin(th)

    pw_re, pw_im = a_power(jnp.arange(L + 1))
    cr = c_re.astype(F32)[:, None]
    ci = c_im.astype(F32)[:, None]
    cp_re = cr * pw_re[:, :, None, :] - ci * pw_im[:, :, None, :]
    cp_im = cr * pw_im[:, :, None, :] + ci * pw_re[:, :, None, :]
    kern = (jnp.einsum('gthp,gpk->gthk', cp_re[:, :L], bb_re, precision=hp)
            - jnp.einsum('gthp,gpk->gthk', cp_im[:, :L], bb_im, precision=hp))
    kern = kern.at[:, 0].add(d_skip.astype(F32).reshape(G, H)[:, :, None] * jnp.eye(H, dtype=F32))
    kern = kern.transpose(0, 3, 2, 1).reshape(G, H * H, L)

    rev_re = pw_re[:, L - 1::-1]
    rev_im = pw_im[:, L - 1::-1]
    bt_re = bb_re.transpose(0, 2, 1)[:, :, None, :]
    bt_im = bb_im.transpose(0, 2, 1)[:, :, None, :]
    e_re = rev_re[:, None] * bt_re - rev_im[:, None] * bt_im
    e_im = rev_re[:, None] * bt_im + rev_im[:, None] * bt_re
    e_mat = jnp.concatenate([e_re, e_im], axis=-1).reshape(G, H * L, 2 * P)

    f_re = cp_re[:, 1:].transpose(0, 3, 2, 1)
    f_im = -cp_im[:, 1:].transpose(0, 3, 2, 1)
    f_mat = jnp.concatenate([f_re, f_im], axis=1).reshape(G, 2 * P, H * L)

    n_steps = max(1, (n_chunks - 1).bit_length())
    sr, si = a_power(L * (2 ** jnp.arange(n_steps)))
    a_pow = jnp.stack([jnp.concatenate([sr, sr], axis=-1),
                       jnp.concatenate([-si, si], axis=-1)], axis=2)
    return kern, e_mat.astype(BF16), f_mat.astype(BF16), a_pow


def _s5_kernel(u_ref, k_ref, e_ref, f_ref, a_ref, after_ref, y_ref, toep_ref, *, n_chunks, n_steps, chunk):
    del after_ref
    L, H = chunk, SSM_GROUP_CH

    causal = lax.broadcasted_iota(I32, (L, L), 1) >= lax.broadcasted_iota(I32, (L, L), 0)

    def build(hin, carry):
        r0 = pl.multiple_of(hin * L, L)
        for hout in range(H):
            krow = k_ref[0, pl.ds(hin * H + hout, 1), :]
            blk = pltpu.roll(jnp.broadcast_to(krow, (L, L)), 0, axis=1, stride=1, stride_axis=0)
            toep_ref[pl.ds(r0, L), hout * L:(hout + 1) * L] = jnp.where(causal, blk, 0.0).astype(BF16)
        return carry

    lax.fori_loop(0, H, build, 0)

    bsz = u_ref.shape[0]
    u = jnp.concatenate([u_ref[:, h].reshape(bsz * n_chunks, L) for h in range(H)], axis=1)
    st = jnp.dot(u, e_ref[0], preferred_element_type=F32)
    cidx = lax.broadcasted_iota(I32, st.shape, 0) % n_chunks
    p2 = st.shape[1]

    def shifted(xv, d):
        return jnp.where(cidx >= d, pltpu.roll(xv, d, axis=0), 0.0)

    for k in range(n_steps):
        d = 1 << k
        if d >= n_chunks:
            break
        z = shifted(st, d)
        st = st + a_ref[0, k, 0:1, :] * z + a_ref[0, k, 1:2, :] * pltpu.roll(z, p2 // 2, axis=1)
    s_in = shifted(st, 1)
    y = jnp.dot(u, toep_ref[...], preferred_element_type=F32)
    y = (y + jnp.dot(s_in.astype(BF16), f_ref[0], preferred_element_type=F32)).astype(BF16)
    for h in range(H):
        y_ref[:, h] = y[:, h * L:(h + 1) * L].reshape(bsz, n_chunks, L)


def _s5_core(ut5, kern, e_mat, f_mat, a_pow, after):
    bsz, g, h, n_chunks, chunk = ut5.shape
    hl = h * chunk
    n_steps = a_pow.shape[1]
    p2 = e_mat.shape[2]
    blk = lambda i: (i, 0, 0)
    seq_blk = pl.BlockSpec((bsz, None, h, n_chunks, chunk), lambda i: (0, i, 0, 0, 0))
    return pl.pallas_call(
        functools.partial(_s5_kernel, n_chunks=n_chunks, n_steps=n_steps, chunk=chunk),
        grid=(g,),
        in_specs=[seq_blk,
                  pl.BlockSpec((1,) + kern.shape[1:], blk),
                  pl.BlockSpec((1, hl, p2), blk),
                  pl.BlockSpec((1, p2, hl), blk),
                  pl.BlockSpec((1, n_steps, 2, p2), lambda i: (i, 0, 0, 0)),
                  pl.BlockSpec(memory_space=pl.ANY)],
        out_specs=seq_blk,
        out_shape=jax.ShapeDtypeStruct(ut5.shape, BF16),
        scratch_shapes=[pltpu.VMEM((hl, hl), BF16)],
        compiler_params=_params("arbitrary"),
        name="s5_core",
    )(ut5, kern, e_mat, f_mat, a_pow, after)


def _route(scores, biased, tri, base):
    n_tok = scores.shape[1]
    shape3 = (EXPERTS_PER_GROUP, N_EXPERT_GROUPS, n_tok)
    sc3 = scores.reshape(shape3)
    b3 = biased.reshape(shape3)
    j_iota = lax.broadcasted_iota(I32, shape3, 0)
    e_iota = lax.broadcasted_iota(I32, shape3, 1) * EXPERTS_PER_GROUP + j_iota

    def red(fn, x):
        return fn(fn(x, axis=0, keepdims=True), axis=1, keepdims=True)

    m1 = jnp.max(b3, axis=0, keepdims=True)
    i1 = jnp.min(jnp.where(b3 == m1, j_iota, EXPERTS_PER_GROUP), axis=0, keepdims=True)
    m2 = jnp.max(jnp.where(j_iota == i1, NEG_INF, b3), axis=0, keepdims=True)
    gs = m1 + m2

    g_iota = lax.broadcasted_iota(I32, gs.shape, 1)
    gsel = jnp.zeros(gs.shape, F32)
    cur = gs
    for _ in range(TOPK_GROUPS):
        m = jnp.max(cur, axis=1, keepdims=True)
        ig = jnp.min(jnp.where(cur == m, g_iota, N_EXPERT_GROUPS), axis=1, keepdims=True)
        pick = g_iota == ig
        gsel = jnp.where(pick, 1.0, gsel)
        cur = jnp.where(pick, NEG_INF, cur)

    cur = jnp.where(gsel > 0.0, b3, NEG_INF)
    sel = jnp.zeros(shape3, F32)
    ids, vals = [], []
    for _ in range(TOP_K):
        m = red(jnp.max, cur)
        ie = red(jnp.min, jnp.where(cur == m, e_iota, N_EXPERTS))
        pick = e_iota == ie
        ids.append(ie)
        vals.append(red(jnp.sum, jnp.where(pick, sc3, 0.0)))
        sel = jnp.where(pick, 1.0, sel)
        cur = jnp.where(pick, NEG_INF, cur)
    tot = functools.reduce(lambda a, b: a + b, vals)

    sel2 = sel.reshape(N_EXPERTS, n_tok)
    before = jnp.dot(sel2.astype(BF16), tri, preferred_element_type=F32) + base
    before3 = before.reshape(shape3)
    ranks = [red(jnp.sum, jnp.where(e_iota == ie, before3, 0.0)) for ie in ids]

    def rows(parts, dtype):
        parts = [p.reshape(1, n_tok).astype(dtype) for p in parts]
        return jnp.concatenate(parts + [jnp.zeros((8 - len(parts), n_tok), dtype)], axis=0)

    e8 = rows(ids, I32)
    w8 = rows([v / tot * ROUTED_SCALE for v in vals], F32)
    r8 = rows(ranks, I32)
    return e8, w8, r8, jnp.sum(sel2, axis=1, keepdims=True)


def _mix_back_kernel(x_ref, yconv_ref, yt_ref, gate1_ref, shift2_ref, scale2_ref, gate2_ref,
                     wglu_ref, bglu_ref, gssm_ref, woc_ref, wos_ref, gffn_ref,
                     wrh_ref, wrl_ref, rbias_ref, ws13_ref, ws2_ref, tri_ref,
                     x1_ref, h2p_ref, e8_ref, w8_ref, r8_ref, cnt_ref):
    first = jnp.logical_and(pl.program_id(0) == 0, pl.program_id(1) == 0)

    @pl.when(first)
    def _():
        cnt_ref[...] = jnp.zeros_like(cnt_ref)

    sub = tri_ref.shape[0]

    def ssm_post(st):
        r0 = st["r0"]
        y = yt_ref[0, :, r0:r0 + sub].astype(F32).T
        y = _gelu_tanh(y)
        y = y * jax.nn.sigmoid(jnp.dot(y.astype(BF16), wglu_ref[...], preferred_element_type=F32)
                               + bglu_ref[...])
        st["y_ssm"] = _rms(y, gssm_ref[...]).astype(BF16)

    def out_proj(st):
        r0 = st["r0"]
        mix = jnp.dot(yconv_ref[0, r0:r0 + sub, :], woc_ref[...], preferred_element_type=F32)
        mix = mix + jnp.dot(st.pop("y_ssm"), wos_ref[...], preferred_element_type=F32)
        st["x1"] = x_ref[0, r0:r0 + sub, :] + gate1_ref[0] * mix

    def ffn_in(st):
        r0 = st["r0"]
        h2 = _rms(st["x1"], gffn_ref[...] * (1.0 + scale2_ref[0])) + shift2_ref[0]
        h2b = h2.astype(BF16)
        h2r = h2b.astype(F32)
        h2p_ref[0, r0:r0 + sub, :] = lax.bitcast_convert_type(_pack_rounded(h2r), I32)
        st["h2b"] = h2b
        st["h2l"] = (h2 - h2r).astype(BF16)

    def shared_expert(st):
        r0 = st["r0"]
        a = jnp.dot(st["h2b"], ws13_ref[...], preferred_element_type=F32)
        f = a.shape[1] // 2
        act = _silu(a[:, :f]) * a[:, f:]
        shared = jnp.dot(act.astype(BF16), ws2_ref[...], preferred_element_type=F32)
        x1_ref[0, r0:r0 + sub, :] = (st.pop("x1") + gate2_ref[0] * shared).astype(x1_ref.dtype)

    def router(st):
        lt = (jnp.dot(st.pop("h2b"), wrh_ref[...], preferred_element_type=F32)
              + jnp.dot(st.pop("h2l"), wrl_ref[...], preferred_element_type=F32)).T
        st["scores"] = jax.nn.sigmoid(lt[:N_EXPERTS] + lt[N_EXPERTS:])

    def route(st):
        r0 = st["r0"]
        scores = st.pop("scores")
        e8, w8, r8, cnt = _route(scores, scores + rbias_ref[...], tri_ref[...], carry["base"])
        e8_ref[:, r0:r0 + sub] = e8
        w8_ref[0, r0:r0 + sub, :] = w8.T
        r8_ref[:, r0:r0 + sub] = r8
        carry["base"] = carry["base"] + cnt

    carry = {"base": cnt_ref[:, 0:1]}
    phases = (ssm_post, out_proj, ffn_in, shared_expert, router, route)
    chains = [{"r0": r0} for r0 in range(0, x_ref.shape[1], sub)]
    lag = MIX_PHASE_LAG
    for step in range(len(phases) + lag * (len(chains) - 1)):
        for ci, st in enumerate(chains):
            ph = step - lag * ci
            if 0 <= ph < len(phases):
                phases[ph](st)
    cnt_ref[...] = jnp.broadcast_to(carry["base"], cnt_ref.shape)


def _mix_back(x, yconv, yt, gate1, shift2, scale2, gate2, w_glu_bf, b_glu, g_ssm, wo_conv, wo_ssm,
              g_ffn, wr_hi, wr_lo, router_bias, ws13, ws2_bf, ts, b0, bsz):
    _, seq, d = x.shape
    n_tok = bsz * seq
    tiles = seq // ts
    row = lambda b, s: (b + b0, 0, 0)
    const2 = lambda b, s: (0, 0)
    tile_in = lambda b, s: (b + b0, s, 0)
    tile = lambda b, s: (b, s, 0)
    flat = lambda b, s: (0, b * tiles + s)
    full = lambda a: pl.BlockSpec(a.shape, const2)
    sub = min(MIX_SUB, ts)
    tri = jnp.triu(jnp.ones((sub, sub), BF16), k=1)
    args = (w_glu_bf, b_glu.reshape(1, D_SSM), g_ssm.reshape(1, D_SSM), wo_conv, wo_ssm,
            g_ffn.reshape(1, d), wr_hi, wr_lo, router_bias.reshape(N_EXPERTS, 1), ws13, ws2_bf, tri)
    return pl.pallas_call(
        _mix_back_kernel,
        grid=(bsz, tiles),
        in_specs=[pl.BlockSpec((1, ts, d), tile_in),
                  pl.BlockSpec((1, ts, D_CONV), tile),
                  pl.BlockSpec((1, D_SSM, ts), lambda b, s: (b, 0, s)),
                  pl.BlockSpec((1, 1, d), row), pl.BlockSpec((1, 1, d), row),
                  pl.BlockSpec((1, 1, d), row), pl.BlockSpec((1, 1, d), row)]
                 + [full(a) for a in args],
        out_specs=[pl.BlockSpec((1, ts, d), tile),
                   pl.BlockSpec((1, ts, d // 2), tile),
                   pl.BlockSpec((8, ts), flat),
                   pl.BlockSpec((1, ts, 8), tile),
                   pl.BlockSpec((8, ts), flat),
                   pl.BlockSpec((N_EXPERTS, 128), const2)],
        out_shape=[jax.ShapeDtypeStruct((bsz, seq, d), RESIDUAL_DTYPE),
                   jax.ShapeDtypeStruct((bsz, seq, d // 2), I32),
                   jax.ShapeDtypeStruct((8, n_tok), I32),
                   jax.ShapeDtypeStruct((bsz, seq, 8), F32),
                   jax.ShapeDtypeStruct((8, n_tok), I32),
                   jax.ShapeDtypeStruct((N_EXPERTS, 128), F32)],
        compiler_params=_params("arbitrary", "arbitrary"),
        name="mix_back",
    )(x, yconv, yt, gate1, shift2, scale2, gate2, *args)


def _plan_kernel(cnt_ref, e8_ref, r8_ref, pos_ref, *, bm):
    cnt = cnt_ref[...].astype(I32)
    padded = (cnt + (bm - 1)) // bm * bm
    rid = lax.broadcasted_iota(I32, padded.shape, 0)
    incl = padded
    d = 1
    while d < N_EXPERTS:
        incl = incl + jnp.where(rid >= d, pltpu.roll(incl, d, axis=0), 0)
        d *= 2
    pstart = incl - padded
    lanes = pstart.shape[1]

    def chunk(ci, carry):
        c0 = pl.multiple_of(ci * lanes, lanes)
        e = e8_ref[:, pl.ds(c0, lanes)]
        acc = r8_ref[:, pl.ds(c0, lanes)]
        for ex in range(N_EXPERTS):
            acc = acc + jnp.where(e == ex, pstart[ex:ex + 1, :], 0)
        pos_ref[:, pl.ds(c0, lanes)] = acc
        return carry

    lax.fori_loop(0, e8_ref.shape[1] // lanes, chunk, 0)


def _plan(counts, e8, r8, bm):
    n_tok = e8.shape[1]
    tb = min(8192, n_tok)
    return pl.pallas_call(
        functools.partial(_plan_kernel, bm=bm),
        grid=(n_tok // tb,),
        in_specs=[pl.BlockSpec(counts.shape, lambda i: (0, 0)),
                  pl.BlockSpec((8, tb), lambda i: (0, i)),
                  pl.BlockSpec((8, tb), lambda i: (0, i))],
        out_specs=pl.BlockSpec((8, tb), lambda i: (0, i)),
        out_shape=jax.ShapeDtypeStruct((8, n_tok), I32),
        compiler_params=_params("arbitrary"),
        name="plan",
    )(counts, e8, r8)


def _sc_mesh():
    return plsc.VectorSubcoreMesh(core_axis_name="c", subcore_axis_name="s",
                                  num_cores=V7X_SC_CORES, num_subcores=V7X_SC_SUBCORES)


def _sc_worker_base(per_worker):
    return (lax.axis_index("s") * V7X_SC_CORES + lax.axis_index("c")) * per_worker


def _sc_dispatch(h2p, posflat, n_rows):
    n_tok, half = h2p.shape
    win = SC_WINDOW
    per_worker = n_tok // SC_WORKERS
    n_win = per_worker // win

    wins_per_k = n_tok // win

    @functools.partial(
        pl.kernel, mesh=_sc_mesh(),
        out_type=jax.ShapeDtypeStruct((n_rows, half), I32),
        scratch_types=[pltpu.VMEM((TOP_K, n_win, win), I32), pltpu.VMEM((2, win, half), I32),
                       pltpu.SemaphoreType.DMA((2,)), pltpu.SemaphoreType.DMA((2,))],
        name="sc_dispatch")
    def run(h2p_hbm, pos_hbm, xs_hbm, idx_v, rows_v, sem_in, sem_out):
        wid = _sc_worker_base(1)
        for k in range(TOP_K):
            pltpu.sync_copy(pos_hbm.at[pl.ds(k * wins_per_k + wid * n_win, n_win)], idx_v.at[k])

        @pl.loop(0, n_win, step=2)
        def _(j):
            loads = []
            for b in range(2):
                t0 = pl.multiple_of((wid * n_win + j + b) * win, win)
                loads.append(pltpu.async_copy(h2p_hbm.at[pl.ds(t0, win)], rows_v.at[b], sem_in.at[b]))
            stores = []
            for b in range(2):
                loads[b].wait()
                for k in range(TOP_K):
                    stores.append(pltpu.async_copy(rows_v.at[b], xs_hbm.at[idx_v.at[k, j + b]], sem_out.at[b]))
            for cp in stores:
                cp.wait()

    return run(h2p, posflat.reshape(TOP_K * wins_per_k, win))


def _sc_gather(ys, posflat):
    n_idx = posflat.shape[0]
    half = ys.shape[1]
    win = SC_WINDOW
    per_worker = n_idx // SC_WORKERS
    n_win = per_worker // win

    @functools.partial(
        pl.kernel, mesh=_sc_mesh(),
        out_type=jax.ShapeDtypeStruct((n_idx, half), I32),
        scratch_types=[pltpu.VMEM((n_win, win), I32), pltpu.VMEM((2, win, half), I32),
                       pltpu.SemaphoreType.DMA((2,)), pltpu.SemaphoreType.DMA((2,))],
        name="sc_gather")
    def run(ys_hbm, pos_hbm, out_hbm, idx_v, rows_v, sem_in, sem_out):
        wid = _sc_worker_base(1)
        pltpu.sync_copy(pos_hbm.at[pl.ds(wid * n_win, n_win)], idx_v)

        @pl.loop(0, n_win, step=2)
        def _(j):
            gathers = [pltpu.async_copy(ys_hbm.at[idx_v.at[j + b]], rows_v.at[b], sem_in.at[b])
                       for b in range(2)]
            writes = []
            for b in range(2):
                gathers[b].wait()
                off = pl.multiple_of((wid * n_win + j + b) * win, win)
                writes.append(pltpu.async_copy(rows_v.at[b], out_hbm.at[pl.ds(off, win)], sem_out.at[b]))
            for cp in writes:
                cp.wait()

    return run(ys, posflat.reshape(n_idx // win, win))


def _sc_combine(ys, posflat, w_lanes):
    n_tok = w_lanes.shape[0]
    half = ys.shape[1]
    win, step, lanes = SC_WINDOW, SC_COMBINE_TOKENS, V7X_SC_LANES
    per_worker = n_tok // SC_WORKERS
    n_win = per_worker // win
    wins_per_k = n_tok // win

    @functools.partial(
        pl.kernel, mesh=_sc_mesh(),
        out_type=jax.ShapeDtypeStruct((n_tok, half), I32),
        scratch_types=[pltpu.VMEM((TOP_K, n_win, win), I32), pltpu.VMEM((2, TOP_K, step, half), I32),
                       pltpu.VMEM((2, step, w_lanes.shape[1]), I32), pltpu.VMEM((step, half), I32),
                       pltpu.SemaphoreType.DMA((2,))],
        compiler_params=pltpu.CompilerParams(needs_layout_passes=False),
        name="sc_combine")
    def run(ys_hbm, pos_hbm, w_hbm, out_hbm, idx_v, rows_v, w_v, out_v, sem):
        wid = _sc_worker_base(1)
        n_steps = per_worker // step
        for k in range(TOP_K):
            pltpu.sync_copy(pos_hbm.at[pl.ds(k * wins_per_k + wid * n_win, n_win)], idx_v.at[k])

        def copies(s, slot):
            wdw = s // (win // step)
            q0 = pl.multiple_of((s % (win // step)) * step, step)
            t0 = pl.multiple_of(wid * per_worker + s * step, step)
            cps = [pltpu.make_async_copy(ys_hbm.at[idx_v[k, wdw, pl.ds(q0, step)]], rows_v.at[slot, k],
                                         sem.at[slot]) for k in range(TOP_K)]
            return cps + [pltpu.make_async_copy(w_hbm.at[pl.ds(t0, step)], w_v.at[slot], sem.at[slot])]

        def compute(s, slot):
            @pl.loop(0, step)
            def _(i):
                wb = [plsc.bitcast(w_v[slot, i, pl.ds(k * lanes, lanes)], BF16) for k in range(TOP_K)]
                for c in range(0, half, lanes):
                    t = [wb[k] * plsc.bitcast(rows_v[slot, k, i, pl.ds(c, lanes)], BF16) for k in range(TOP_K)]
                    while len(t) > 1:
                        t = [t[a] + t[a + 1] for a in range(0, len(t) - 1, 2)] + t[len(t) & ~1:]
                    out_v[i, pl.ds(c, lanes)] = plsc.bitcast(t[0], I32)

            t0 = pl.multiple_of(wid * per_worker + s * step, step)
            pltpu.sync_copy(out_v, out_hbm.at[pl.ds(t0, step)])

        for cp in copies(0, 0):
            cp.start()

        @pl.loop(0, n_steps, step=2)
        def _(s):
            for cp in copies(s + 1, 1):
                cp.start()
            for cp in copies(s, 0):
                cp.wait()
            compute(s, 0)

            @pl.when(s + 2 < n_steps)
            def _():
                for cp in copies(s + 2, 0):
                    cp.start()

            for cp in copies(s + 1, 1):
                cp.wait()
            compute(s + 1, 1)

    return run(ys, posflat.reshape(TOP_K * wins_per_k, win), w_lanes)


def _experts_kernel(be_ref, nu_ref, x_ref, w1_ref, w3_ref, w2_ref, after_ref, y_ref, w13_s, w2_s):
    del after_ref
    i = pl.program_id(0)
    changed = jnp.logical_or(i == 0, be_ref[i] != be_ref[jnp.maximum(i - 1, 0)])

    @pl.when(changed)
    def _():
        f = w1_ref.shape[2]
        w13_s[:, :f] = w1_ref[0].astype(BF16)
        w13_s[:, f:] = w3_ref[0].astype(BF16)
        w2_s[...] = w2_ref[0].astype(BF16)

    @pl.when(i < nu_ref[0])
    def _():
        half = x_ref.shape[1]
        sub = min(EXPERT_SUB, x_ref.shape[0])
        for r0 in range(0, x_ref.shape[0], sub):
            lo, hi = _unpack_halves(lax.bitcast_convert_type(x_ref[r0:r0 + sub, :], U32))
            a = jnp.dot(lo.astype(BF16), w13_s[:half, :], preferred_element_type=F32)
            a = a + jnp.dot(hi.astype(BF16), w13_s[half:, :], preferred_element_type=F32)
            f = a.shape[1] // 2
            act = _silu(a[:, :f]) * a[:, f:]
            y = jnp.dot(act.astype(BF16), w2_s[...], preferred_element_type=F32)
            y_ref[r0:r0 + sub, :] = lax.bitcast_convert_type(_pack_halves(y), I32)


def _experts(block_e, n_used, xs, w1, w3, w2, bm, after):
    n_rows, half = xs.shape
    d, f = w1.shape[1], w1.shape[2]
    rows = lambda i, be, nu: (jnp.minimum(i, nu[0] - 1), 0)
    wblk = lambda i, be, nu: (be[i], 0, 0)
    grid_spec = pltpu.PrefetchScalarGridSpec(
        num_scalar_prefetch=2,
        grid=(n_rows // bm,),
        in_specs=[pl.BlockSpec((bm, half), rows),
                  pl.BlockSpec((1, d, f), wblk),
                  pl.BlockSpec((1, d, f), wblk),
                  pl.BlockSpec((1, f, d), wblk),
                  pl.BlockSpec(memory_space=pl.ANY)],
        out_specs=pl.BlockSpec((bm, half), rows),
        scratch_shapes=[pltpu.VMEM((d, 2 * f), BF16), pltpu.VMEM((f, d), BF16)],
    )
    return pl.pallas_call(
        _experts_kernel,
        grid_spec=grid_spec,
        out_shape=jax.ShapeDtypeStruct((n_rows, half), I32),
        compiler_params=_params("arbitrary"),
        name="experts",
    )(block_e, n_used, xs, w1, w3, w2, after)


def _combine_kernel(r_ref, x1_ref, gate2_ref, gfin_ref, *rest, weighted):
    o_ref = rest[-1]
    if weighted:
        w8 = rest[0][...]
        lo = hi = None
        for k in range(TOP_K):
            rl, rh = _unpack_halves(lax.bitcast_convert_type(r_ref[k], U32))
            wk = w8[:, k:k + 1]
            lo = wk * rl if lo is None else lo + wk * rl
            hi = wk * rh if hi is None else hi + wk * rh
    else:
        lo, hi = _unpack_halves(lax.bitcast_convert_type(r_ref[...], U32))
    routed = jnp.concatenate([lo, hi], axis=1)
    o_ref[...] = _rms(x1_ref[...].astype(F32) + gate2_ref[0] * routed, gfin_ref[...])


def _combine(routed, w8, x1, gate2, g_final, tt, tiles_per_seq, b_src, b0, n_tok_all, out_prev):
    d = x1.shape[1]
    n_tok, half = routed.shape[-2:]
    src0 = b_src * tiles_per_seq
    tile0 = b0 * tiles_per_seq
    weighted = w8 is not None
    r_spec = (pl.BlockSpec((TOP_K, tt, half), lambda i: (0, i, 0)) if weighted
              else pl.BlockSpec((tt, half), lambda i: (i, 0)))
    in_specs = [r_spec,
                pl.BlockSpec((tt, d), lambda i: (i + src0, 0)),
                pl.BlockSpec((1, 1, d), lambda i: (i // tiles_per_seq + b0, 0, 0)),
                pl.BlockSpec((1, d), lambda i: (0, 0))]
    args = [routed, x1, gate2, g_final.reshape(1, d)]
    if weighted:
        in_specs.append(pl.BlockSpec((tt, 8), lambda i: (i + src0, 0)))
        args.append(w8)
    aliases = {}
    if out_prev is not None:
        in_specs.append(pl.BlockSpec(memory_space=pl.ANY))
        args.append(out_prev)
        aliases = {len(args) - 1: 0}
    return pl.pallas_call(
        functools.partial(_combine_kernel, weighted=weighted),
        grid=(n_tok // tt,),
        in_specs=in_specs,
        out_specs=pl.BlockSpec((tt, d), lambda i: (i + tile0, 0)),
        out_shape=jax.ShapeDtypeStruct((n_tok_all, d), F32),
        input_output_aliases=aliases,
        compiler_params=_params("arbitrary"),
        name="combine",
    )(*args)


def _block_experts(counts, bm, n_blocks):
    padded = (counts.astype(I32) + (bm - 1)) // bm * bm
    pend = jnp.cumsum(padded)
    starts = jnp.arange(n_blocks, dtype=I32) * bm
    block_e = jnp.minimum(jnp.sum((pend[None, :] <= starts[:, None]).astype(I32), axis=1), N_EXPERTS - 1)
    return block_e, (pend[-1:] // bm).astype(I32)


def kernel(x, c, w_ada, b_ada, g_mix, w_in, conv_w, conv_b, g_conv, lam_re, lam_im, log_dt, b_re, b_im,
           c_re, c_im, d_skip, w_glu, b_glu, g_ssm, w_out, g_ffn, w_router, router_bias, w1, w3, w2,
           ws1, ws3, ws2, g_final):
    bsz, seq, d = x.shape
    n_tok = bsz * seq
    ts = min(MIX_BACK_TOKENS, seq)
    L = min(S5_CHUNK, seq)
    n_chunks = seq // L
    G, H = SSM_GROUPS, SSM_GROUP_CH

    mod = _adaln(c, w_ada, b_ada).reshape(bsz, 6, 1, d)
    shift1, scale1, gate1, shift2, scale2, gate2 = (mod[:, i] for i in range(6))

    kern, e_mat, f_mat, a_pow = _s5_tables(lam_re, lam_im, log_dt, b_re, b_im, c_re, c_im, d_skip,
                                           L, n_chunks)
    w_in_bf = w_in.astype(BF16)

    NG, NJ = N_EXPERT_GROUPS, EXPERTS_PER_GROUP
    w_r = w_router.astype(F32).reshape(d, NG, NJ).transpose(0, 2, 1).reshape(d, N_EXPERTS)
    r_bias = router_bias.reshape(NG, NJ).T.reshape(N_EXPERTS)
    w_r_hi = w_r.astype(BF16)
    wr_hi = jnp.concatenate([w_r_hi, (w_r - w_r_hi.astype(F32)).astype(BF16)], axis=1)
    wr_lo = jnp.concatenate([w_r_hi, jnp.zeros_like(w_r_hi)], axis=1)
    w_out_bf = w_out.astype(BF16)
    w_glu_bf = w_glu.astype(BF16)
    ws13 = jnp.concatenate([ws1, ws3], axis=1).astype(BF16)
    ws2_bf = ws2.astype(BF16)
    bm = EXPERT_ROWS
    tt = min(COMBINE_TOKENS, seq)

    n_parts = LAYER_PARTS if bsz % LAYER_PARTS == 0 else 1
    pb = bsz // n_parts
    pt = pb * seq
    n_blocks = -(-(pt * TOP_K + N_EXPERTS * (bm - 1)) // bm)
    none = jnp.zeros((8, 128), F32)

    def run_experts(part, after):
        return _experts(*part["block_e"], part["xs"], w1, w3, w2, bm, after)

    def finish(part, ys, out, n_split, sc_sums):
        nb = pb // n_split
        for h in range(n_split):
            lo, hi = h * nb * seq, (h + 1) * nb * seq
            pos = part["pos8"][:TOP_K, lo:hi].reshape(TOP_K * nb * seq)
            if sc_sums:
                routed, w8 = _sc_combine(ys, pos, part["w_lanes"][lo:hi]), None
            else:
                routed, w8 = _sc_gather(ys, pos).reshape(TOP_K, nb * seq, d // 2), part["w8"].reshape(pt, 8)
            out = _combine(routed, w8, part["x1"].reshape(pt, d), gate2, g_final, tt,
                           seq // tt, h * nb, part["b0"] + h * nb, n_tok, out)
        return out

    out, prev, after_front = None, None, none
    for p in range(n_parts):
        b0 = p * pb
        yconv, ut = _mix_front(x, shift1, scale1, g_mix, w_in_bf, conv_w, conv_b, g_conv,
                               min(MIX_FRONT_TOKENS, seq), b0, pb, after_front)
        ys_prev = run_experts(prev, ut) if prev is not None else None
        yt = _s5_core(ut.reshape(pb, G, H, n_chunks, L), kern, e_mat, f_mat, a_pow,
                      none if ys_prev is None else ys_prev).reshape(pb, G * H, seq)
        x1, h2p, e8, w8, r8, counts = _mix_back(
            x, yconv, yt, gate1, shift2, scale2, gate2, w_glu_bf, b_glu, g_ssm,
            w_out_bf[:D_CONV], w_out_bf[D_CONV:], g_ffn, wr_hi, wr_lo, r_bias, ws13, ws2_bf,
            ts, b0, pb)
        counts = counts.reshape(NJ, NG, -1).transpose(1, 0, 2).reshape(N_EXPERTS, -1)
        pos8 = _plan(counts, e8, r8, bm)
        xs = _sc_dispatch(h2p.reshape(pt, d // 2), pos8[:TOP_K].reshape(TOP_K * pt), n_blocks * bm)
        if prev is not None:
            out = finish(prev, ys_prev, out, 1, True)
        w_bits = lax.bitcast_convert_type(w8.reshape(pt, 8).astype(BF16), jnp.uint16).astype(U32)
        w_lanes = jnp.repeat(lax.bitcast_convert_type((w_bits << 16) | w_bits, I32), V7X_SC_LANES, axis=1)
        prev = dict(b0=b0, x1=x1, w8=w8, w_lanes=w_lanes, pos8=pos8, xs=xs,
                    block_e=_block_experts(counts[:, 0], bm, n_blocks))
        after_front = pos8
    ys_last = run_experts(prev, none if out is None else out)
    out = finish(prev, ys_last, out, LAST_PART_SPLIT if pb % LAST_PART_SPLIT == 0 else 1, False)
    return out.reshape(bsz, seq, d)
```

```python
import functools
import math

import jax
import jax.numpy as jnp
import numpy as np
from jax import lax
from jax.experimental import pallas as pl
from jax.experimental.pallas import tpu as pltpu
from jax.experimental.pallas import tpu_sc as plsc

F32 = jnp.float32
BF16 = jnp.bfloat16
U32 = jnp.uint32
I32 = jnp.int32

D_CONV = 768
D_SSM = 256
SSM_GROUPS = 16
SSM_GROUP_CH = 16
SSM_STATE = 64
N_EXPERTS = 64
TOP_K = 6
N_EXPERT_GROUPS = 8
TOPK_GROUPS = 4
EXPERTS_PER_GROUP = 8
ROUTED_SCALE = 2.5
RMS_EPS = 1e-6

S5_CHUNK = 128
EXPERT_ROWS = 1024
EXPERT_SUB = 512
MIX_FRONT_TOKENS = 1024
MIX_FRONT_SUB = 1024
MIX_SUB = 256
MIX_BACK_TOKENS = 1024
MIX_PHASE_LAG = 3
COMBINE_TOKENS = 512
LAYER_PARTS = 2
LAST_PART_SPLIT = 4
RESIDUAL_DTYPE = BF16
V7X_SC_CORES = 2
V7X_SC_SUBCORES = 16
V7X_SC_LANES = 16
SC_WORKERS = V7X_SC_CORES * V7X_SC_SUBCORES
SC_COMBINE_TOKENS = 16
SC_COMBINE_UNROLL = 2
SC_WINDOW = 64
V7X_VMEM_LIMIT = 56 * 1024 * 1024
NEG_INF = float("-inf")
HIGH_HALF = np.uint32(0xFFFF0000)


def _rms(x, g):
    return x * lax.rsqrt(jnp.mean(x * x, axis=-1, keepdims=True) + RMS_EPS) * g


def _gelu_tanh(x):
    return 0.5 * x * (1.0 + jnp.tanh(math.sqrt(2.0 / math.pi) * (x + 0.044715 * (x * x * x))))


def _silu(x):
    return x * jax.nn.sigmoid(x)


def _params(*sem):
    return pltpu.CompilerParams(dimension_semantics=sem, vmem_limit_bytes=V7X_VMEM_LIMIT)


def _pack_halves(y):
    return _pack_rounded(y.astype(BF16).astype(F32))


def _pack_rounded(y):
    m = y.shape[1] // 2
    bits = lax.bitcast_convert_type(y, U32)
    return (bits[:, m:] & HIGH_HALF) | (bits[:, :m] >> 16)


def _unpack_halves(w):
    lo = lax.bitcast_convert_type(w << 16, F32)
    hi = lax.bitcast_convert_type(w & HIGH_HALF, F32)
    return lo, hi


def _adaln_kernel(c_ref, w_ref, b_ref, o_ref):
    o_ref[...] = jnp.dot(_silu(c_ref[...]), w_ref[...], preferred_element_type=F32,
                         precision=lax.Precision.HIGHEST) + b_ref[...]


def _adaln(c, w_ada, b_ada):
    bsz, d = c.shape
    n = w_ada.shape[1]
    bn = 1024
    return pl.pallas_call(
        _adaln_kernel,
        grid=(n // bn,),
        in_specs=[pl.BlockSpec((bsz, d), lambda j: (0, 0)),
                  pl.BlockSpec((d, bn), lambda j: (0, j)),
                  pl.BlockSpec((1, bn), lambda j: (0, j))],
        out_specs=pl.BlockSpec((bsz, bn), lambda j: (0, j)),
        out_shape=jax.ShapeDtypeStruct((bsz, n), F32),
        compiler_params=_params("arbitrary"),
        name="adaln",
    )(c, w_ada, b_ada.reshape(1, n))


def _mix_front_kernel(x_ref, shift_ref, scale_ref, gmix_ref, win_ref, cw_ref, cb_ref, gconv_ref, after_ref,
                      yconv_ref, ut_ref, zprev_ref):
    del after_ref
    s = pl.program_id(1)

    @pl.when(s == 0)
    def _():
        zprev_ref[...] = jnp.zeros_like(zprev_ref)

    cw = cw_ref[...]
    sub = min(MIX_FRONT_SUB, x_ref.shape[1])
    carry = {"prev": zprev_ref[...]}

    def norm(st):
        x = x_ref[0, st["r0"]:st["r0"] + sub, :]
        st["h"] = (_rms(x, gmix_ref[...]) * (1.0 + scale_ref[0]) + shift_ref[0]).astype(BF16)

    def in_proj(st):
        st["proj"] = jnp.dot(st.pop("h"), win_ref[...], preferred_element_type=F32)

    def mixers(st):
        r0 = st["r0"]
        proj = st.pop("proj")
        b_gate = proj[:, :D_CONV]
        c_gate = proj[:, D_CONV:2 * D_CONV]
        v = proj[:, 2 * D_CONV:3 * D_CONV]
        u = proj[:, 3 * D_CONV:]
        z = c_gate * v
        prev = carry["prev"]
        rid = lax.broadcasted_iota(I32, z.shape, 0)
        z1 = jnp.where(rid == 0, prev[7:8, :], pltpu.roll(z, 1, axis=0))
        z2 = jnp.where(rid == 0, prev[6:7, :], jnp.where(rid == 1, prev[7:8, :], pltpu.roll(z, 2, axis=0)))
        carry["prev"] = z[sub - 8:, :]
        conv = cw[0:1, :] * z2 + cw[1:2, :] * z1 + cw[2:3, :] * z + cb_ref[...]
        yconv_ref[0, r0:r0 + sub, :] = _rms(b_gate * conv, gconv_ref[...]).astype(BF16)
        ut_ref[0, :, r0:r0 + sub] = u.T.astype(BF16)

    phases = (norm, in_proj, mixers)
    chains = [{"r0": r0} for r0 in range(0, x_ref.shape[1], sub)]
    for step in range(len(phases) + len(chains) - 1):
        for ci, st in enumerate(chains):
            if 0 <= step - ci < len(phases):
                phases[step - ci](st)
    zprev_ref[...] = carry["prev"]


def _mix_front(x, shift1, scale1, g_mix, w_in_bf, conv_w, conv_b, g_conv, ts, b0, bsz, after):
    _, seq, d = x.shape
    d_in = w_in_bf.shape[1]
    row = lambda b, s: (b + b0, 0, 0)
    const2 = lambda b, s: (0, 0)
    return pl.pallas_call(
        _mix_front_kernel,
        grid=(bsz, seq // ts),
        in_specs=[pl.BlockSpec((1, ts, d), lambda b, s: (b + b0, s, 0)),
                  pl.BlockSpec((1, 1, d), row),
                  pl.BlockSpec((1, 1, d), row),
                  pl.BlockSpec((1, d), const2),
                  pl.BlockSpec((d, d_in), const2),
                  pl.BlockSpec((8, D_CONV), const2),
                  pl.BlockSpec((1, D_CONV), const2),
                  pl.BlockSpec((1, D_CONV), const2),
                  pl.BlockSpec(memory_space=pl.ANY)],
        out_specs=[pl.BlockSpec((1, ts, D_CONV), lambda b, s: (b, s, 0)),
                   pl.BlockSpec((1, D_SSM, ts), lambda b, s: (b, 0, s))],
        out_shape=[jax.ShapeDtypeStruct((bsz, seq, D_CONV), BF16),
                   jax.ShapeDtypeStruct((bsz, D_SSM, seq), BF16)],
        scratch_shapes=[pltpu.VMEM((8, D_CONV), F32)],
        compiler_params=_params("arbitrary", "arbitrary"),
        name="mix_front",
    )(x, shift1, scale1, g_mix.reshape(1, d), w_in_bf,
      jnp.pad(conv_w, ((0, 8 - conv_w.shape[0]), (0, 0))), conv_b.reshape(1, D_CONV),
      g_conv.reshape(1, D_CONV), after)


def _s5_tables(lam_re, lam_im, log_dt, b_re, b_im, c_re, c_im, d_skip, chunk, n_chunks):
    hp = lax.Precision.HIGHEST
    G, H, P, L = SSM_GROUPS, SSM_GROUP_CH, SSM_STATE, chunk
    lr = lam_re.astype(F32)
    li = lam_im.astype(F32)
    dt = jnp.exp(log_dt.astype(F32))[:, None]
    mag = jnp.exp(lr * dt)
    ang = li * dt
    ab_re = mag * jnp.cos(ang)
    ab_im = mag * jnp.sin(ang)
    den = lr * lr + li * li
    nr = ab_re - 1.0
    ni = ab_im
    q_re = (nr * lr + ni * li) / den
    q_im = (ni * lr - nr * li) / den
    br = b_re.astype(F32)
    bi = b_im.astype(F32)
    bb_re = q_re[..., None] * br - q_im[..., None] * bi
    bb_im = q_re[..., None] * bi + q_im[..., None] * br

    def a_power(tau):
        t = tau.astype(F32)[None, :, None]
        m = jnp.exp(t * (lr * dt)[:, None, :])
        th = t * ang[:, None, :]
        return m * jnp.cos(th), m * jnp.sin(th)

    pw_re, pw_im = a_power(jnp.arange(L + 1))
    cr = c_re.astype(F32)[:, None]
    ci = c_im.astype(F32)[:, None]
    cp_re = cr * pw_re[:, :, None, :] - ci * pw_im[:, :, None, :]
    cp_im = cr * pw_im[:, :, None, :] + ci * pw_re[:, :, None, :]
    kern = (jnp.einsum('gthp,gpk->gthk', cp_re[:, :L], bb_re, precision=hp)
            - jnp.einsum('gthp,gpk->gthk', cp_im[:, :L], bb_im, precision=hp))
    kern = kern.at[:, 0].add(d_skip.astype(F32).reshape(G, H)[:, :, None] * jnp.eye(H, dtype=F32))
    kern = kern.transpose(0, 3, 2, 1).reshape(G, H * H, L)

    rev_re = pw_re[:, L - 1::-1]
    rev_im = pw_im[:, L - 1::-1]
    bt_re = bb_re.transpose(0, 2, 1)[:, :, None, :]
    bt_im = bb_im.transpose(0, 2, 1)[:, :, None, :]
    e_re = rev_re[:, None] * bt_re - rev_im[:, None] * bt_im
    e_im = rev_re[:, None] * bt_im + rev_im[:, None] * bt_re
    e_mat = jnp.concatenate([e_re, e_im], axis=-1).reshape(G, H * L, 2 * P)

    f_re = cp_re[:, 1:].transpose(0, 3, 2, 1)
    f_im = -cp_im[:, 1:].transpose(0, 3, 2, 1)
    f_mat = jnp.concatenate([f_re, f_im], axis=1).reshape(G, 2 * P, H * L)

    n_steps = max(1, (n_chunks - 1).bit_length())
    sr, si = a_power(L * (2 ** jnp.arange(n_steps)))
    a_pow = jnp.stack([jnp.concatenate([sr, sr], axis=-1),
                       jnp.concatenate([-si, si], axis=-1)], axis=2)
    return kern, e_mat.astype(BF16), f_mat.astype(BF16), a_pow


def _s5_kernel(u_ref, k_ref, e_ref, f_ref, a_ref, after_ref, y_ref, toep_ref, *, n_chunks, n_steps, chunk):
    del after_ref
    L, H = chunk, SSM_GROUP_CH

    causal = lax.broadcasted_iota(I32, (L, L), 1) >= lax.broadcasted_iota(I32, (L, L), 0)

    def build(hin, carry):
        r0 = pl.multiple_of(hin * L, L)
        for hout in range(H):
            krow = k_ref[0, pl.ds(hin * H + hout, 1), :]
            blk = pltpu.roll(jnp.broadcast_to(krow, (L, L)), 0, axis=1, stride=1, stride_axis=0)
            toep_ref[pl.ds(r0, L), hout * L:(hout + 1) * L] = jnp.where(causal, blk, 0.0).astype(BF16)
        return carry

    lax.fori_loop(0, H, build, 0)

    bsz = u_ref.shape[0]
    u = jnp.concatenate([u_ref[:, h].reshape(bsz * n_chunks, L) for h in range(H)], axis=1)
    st = jnp.dot(u, e_ref[0], preferred_element_type=F32)
    cidx = lax.broadcasted_iota(I32, st.shape, 0) % n_chunks
    p2 = st.shape[1]

    def shifted(xv, d):
        return jnp.where(cidx >= d, pltpu.roll(xv, d, axis=0), 0.0)

    for k in range(n_steps):
        d = 1 << k
        if d >= n_chunks:
            break
        z = shifted(st, d)
        st = st + a_ref[0, k, 0:1, :] * z + a_ref[0, k, 1:2, :] * pltpu.roll(z, p2 // 2, axis=1)
    s_in = shifted(st, 1)
    y = jnp.dot(u, toep_ref[...], preferred_element_type=F32)
    y = (y + jnp.dot(s_in.astype(BF16), f_ref[0], preferred_element_type=F32)).astype(BF16)
    for h in range(H):
        y_ref[:, h] = y[:, h * L:(h + 1) * L].reshape(bsz, n_chunks, L)


def _s5_core(ut5, kern, e_mat, f_mat, a_pow, after):
    bsz, g, h, n_chunks, chunk = ut5.shape
    hl = h * chunk
    n_steps = a_pow.shape[1]
    p2 = e_mat.shape[2]
    blk = lambda i: (i, 0, 0)
    seq_blk = pl.BlockSpec((bsz, None, h, n_chunks, chunk), lambda i: (0, i, 0, 0, 0))
    return pl.pallas_call(
        functools.partial(_s5_kernel, n_chunks=n_chunks, n_steps=n_steps, chunk=chunk),
        grid=(g,),
        in_specs=[seq_blk,
                  pl.BlockSpec((1,) + kern.shape[1:], blk),
                  pl.BlockSpec((1, hl, p2), blk),
                  pl.BlockSpec((1, p2, hl), blk),
                  pl.BlockSpec((1, n_steps, 2, p2), lambda i: (i, 0, 0, 0)),
                  pl.BlockSpec(memory_space=pl.ANY)],
        out_specs=seq_blk,
        out_shape=jax.ShapeDtypeStruct(ut5.shape, BF16),
        scratch_shapes=[pltpu.VMEM((hl, hl), BF16)],
        compiler_params=_params("arbitrary"),
        name="s5_core",
    )(ut5, kern, e_mat, f_mat, a_pow, after)


def _route(scores, biased, tri, base):
    n_tok = scores.shape[1]
    shape3 = (EXPERTS_PER_GROUP, N_EXPERT_GROUPS, n_tok)
    sc3 = scores.reshape(shape3)
    b3 = biased.reshape(shape3)
    j_iota = lax.broadcasted_iota(I32, shape3, 0)
    e_iota = lax.broadcasted_iota(I32, shape3, 1) * EXPERTS_PER_GROUP + j_iota

    def red(fn, x):
        return fn(fn(x, axis=0, keepdims=True), axis=1, keepdims=True)

    m1 = jnp.max(b3, axis=0, keepdims=True)
    i1 = jnp.min(jnp.where(b3 == m1, j_iota, EXPERTS_PER_GROUP), axis=0, keepdims=True)
    m2 = jnp.max(jnp.where(j_iota == i1, NEG_INF, b3), axis=0, keepdims=True)
    gs = m1 + m2

    g_iota = lax.broadcasted_iota(I32, gs.shape, 1)
    gsel = jnp.zeros(gs.shape, F32)
    cur = gs
    for _ in range(TOPK_GROUPS):
        m = jnp.max(cur, axis=1, keepdims=True)
        ig = jnp.min(jnp.where(cur == m, g_iota, N_EXPERT_GROUPS), axis=1, keepdims=True)
        pick = g_iota == ig
        gsel = jnp.where(pick, 1.0, gsel)
        cur = jnp.where(pick, NEG_INF, cur)

    cur = jnp.where(gsel > 0.0, b3, NEG_INF)
    sel = jnp.zeros(shape3, F32)
    ids, vals = [], []
    for _ in range(TOP_K):
        m = red(jnp.max, cur)
        ie = red(jnp.min, jnp.where(cur == m, e_iota, N_EXPERTS))
        pick = e_iota == ie
        ids.append(ie)
        vals.append(red(jnp.sum, jnp.where(pick, sc3, 0.0)))
        sel = jnp.where(pick, 1.0, sel)
        cur = jnp.where(pick, NEG_INF, cur)
    tot = functools.reduce(lambda a, b: a + b, vals)

    sel2 = sel.reshape(N_EXPERTS, n_tok)
    before = jnp.dot(sel2.astype(BF16), tri, preferred_element_type=F32) + base
    before3 = before.reshape(shape3)
    ranks = [red(jnp.sum, jnp.where(e_iota == ie, before3, 0.0)) for ie in ids]

    def rows(parts, dtype):
        parts = [p.reshape(1, n_tok).astype(dtype) for p in parts]
        return jnp.concatenate(parts + [jnp.zeros((8 - len(parts), n_tok), dtype)], axis=0)

    e8 = rows(ids, I32)
    w8 = rows([v / tot * ROUTED_SCALE for v in vals], F32)
    r8 = rows(ranks, I32)
    return e8, w8, r8, jnp.sum(sel2, axis=1, keepdims=True)


def _mix_back_kernel(x_ref, yconv_ref, yt_ref, gate1_ref, shift2_ref, scale2_ref, gate2_ref,
                     wglu_ref, bglu_ref, gssm_ref, woc_ref, wos_ref, gffn_ref,
                     wrh_ref, wrl_ref, rbias_ref, ws13_ref, ws2_ref, tri_ref,
                     x1_ref, h2p_ref, e8_ref, w8_ref, r8_ref, cnt_ref):
    first = jnp.logical_and(pl.program_id(0) == 0, pl.program_id(1) == 0)

    @pl.when(first)
    def _():
        cnt_ref[...] = jnp.zeros_like(cnt_ref)

    sub = tri_ref.shape[0]

    def ssm_post(st):
        r0 = st["r0"]
        y = yt_ref[0, :, r0:r0 + sub].astype(F32).T
        y = _gelu_tanh(y)
        y = y * jax.nn.sigmoid(jnp.dot(y.astype(BF16), wglu_ref[...], preferred_element_type=F32)
                               + bglu_ref[...])
        st["y_ssm"] = _rms(y, gssm_ref[...]).astype(BF16)

    def out_proj(st):
        r0 = st["r0"]
        mix = jnp.dot(yconv_ref[0, r0:r0 + sub, :], woc_ref[...], preferred_element_type=F32)
        mix = mix + jnp.dot(st.pop("y_ssm"), wos_ref[...], preferred_element_type=F32)
        st["x1"] = x_ref[0, r0:r0 + sub, :] + gate1_ref[0] * mix

    def ffn_in(st):
        r0 = st["r0"]
        h2 = _rms(st["x1"], gffn_ref[...] * (1.0 + scale2_ref[0])) + shift2_ref[0]
        h2b = h2.astype(BF16)
        h2r = h2b.astype(F32)
        h2p_ref[0, r0:r0 + sub, :] = lax.bitcast_convert_type(_pack_rounded(h2r), I32)
        st["h2b"] = h2b
        st["h2l"] = (h2 - h2r).astype(BF16)

    def shared_expert(st):
        r0 = st["r0"]
        a = jnp.dot(st["h2b"], ws13_ref[...], preferred_element_type=F32)
        f = a.shape[1] // 2
        act = _silu(a[:, :f]) * a[:, f:]
        shared = jnp.dot(act.astype(BF16), ws2_ref[...], preferred_element_type=F32)
        x1_ref[0, r0:r0 + sub, :] = (st.pop("x1") + gate2_ref[0] * shared).astype(x1_ref.dtype)

    def router(st):
        lt = (jnp.dot(st.pop("h2b"), wrh_ref[...], preferred_element_type=F32)
              + jnp.dot(st.pop("h2l"), wrl_ref[...], preferred_element_type=F32)).T
        st["scores"] = jax.nn.sigmoid(lt[:N_EXPERTS] + lt[N_EXPERTS:])

    def route(st):
        r0 = st["r0"]
        scores = st.pop("scores")
        e8, w8, r8, cnt = _route(scores, scores + rbias_ref[...], tri_ref[...], carry["base"])
        e8_ref[:, r0:r0 + sub] = e8
        w8_ref[0, r0:r0 + sub, :] = w8.T
        r8_ref[:, r0:r0 + sub] = r8
        carry["base"] = carry["base"] + cnt

    carry = {"base": cnt_ref[:, 0:1]}
    phases = (ssm_post, out_proj, ffn_in, shared_expert, router, route)
    chains = [{"r0": r0} for r0 in range(0, x_ref.shape[1], sub)]
    lag = MIX_PHASE_LAG
    for step in range(len(phases) + lag * (len(chains) - 1)):
        for ci, st in enumerate(chains):
            ph = step - lag * ci
            if 0 <= ph < len(phases):
                phases[ph](st)
    cnt_ref[...] = jnp.broadcast_to(carry["base"], cnt_ref.shape)


def _mix_back(x, yconv, yt, gate1, shift2, scale2, gate2, w_glu_bf, b_glu, g_ssm, wo_conv, wo_ssm,
              g_ffn, wr_hi, wr_lo, router_bias, ws13, ws2_bf, ts, b0, bsz):
    _, seq, d = x.shape
    n_tok = bsz * seq
    tiles = seq // ts
    row = lambda b, s: (b + b0, 0, 0)
    const2 = lambda b, s: (0, 0)
    tile_in = lambda b, s: (b + b0, s, 0)
    tile = lambda b, s: (b, s, 0)
    flat = lambda b, s: (0, b * tiles + s)
    full = lambda a: pl.BlockSpec(a.shape, const2)
    sub = min(MIX_SUB, ts)
    tri = jnp.triu(jnp.ones((sub, sub), BF16), k=1)
    args = (w_glu_bf, b_glu.reshape(1, D_SSM), g_ssm.reshape(1, D_SSM), wo_conv, wo_ssm,
            g_ffn.reshape(1, d), wr_hi, wr_lo, router_bias.reshape(N_EXPERTS, 1), ws13, ws2_bf, tri)
    return pl.pallas_call(
        _mix_back_kernel,
        grid=(bsz, tiles),
        in_specs=[pl.BlockSpec((1, ts, d), tile_in),
                  pl.BlockSpec((1, ts, D_CONV), tile),
                  pl.BlockSpec((1, D_SSM, ts), lambda b, s: (b, 0, s)),
                  pl.BlockSpec((1, 1, d), row), pl.BlockSpec((1, 1, d), row),
                  pl.BlockSpec((1, 1, d), row), pl.BlockSpec((1, 1, d), row)]
                 + [full(a) for a in args],
        out_specs=[pl.BlockSpec((1, ts, d), tile),
                   pl.BlockSpec((1, ts, d // 2), tile),
                   pl.BlockSpec((8, ts), flat),
                   pl.BlockSpec((1, ts, 8), tile),
                   pl.BlockSpec((8, ts), flat),
                   pl.BlockSpec((N_EXPERTS, 128), const2)],
        out_shape=[jax.ShapeDtypeStruct((bsz, seq, d), RESIDUAL_DTYPE),
                   jax.ShapeDtypeStruct((bsz, seq, d // 2), I32),
                   jax.ShapeDtypeStruct((8, n_tok), I32),
                   jax.ShapeDtypeStruct((bsz, seq, 8), F32),
                   jax.ShapeDtypeStruct((8, n_tok), I32),
                   jax.ShapeDtypeStruct((N_EXPERTS, 128), F32)],
        compiler_params=_params("arbitrary", "arbitrary"),
        name="mix_back",
    )(x, yconv, yt, gate1, shift2, scale2, gate2, *args)


def _plan_kernel(cnt_ref, e8_ref, r8_ref, pos_ref, *, bm):
    cnt = cnt_ref[...].astype(I32)
    padded = (cnt + (bm - 1)) // bm * bm
    rid = lax.broadcasted_iota(I32, padded.shape, 0)
    incl = padded
    d = 1
    while d < N_EXPERTS:
        incl = incl + jnp.where(rid >= d, pltpu.roll(incl, d, axis=0), 0)
        d *= 2
    pstart = incl - padded
    lanes = pstart.shape[1]

    def chunk(ci, carry):
        c0 = pl.multiple_of(ci * lanes, lanes)
        e = e8_ref[:, pl.ds(c0, lanes)]
        acc = r8_ref[:, pl.ds(c0, lanes)]
        for ex in range(N_EXPERTS):
            acc = acc + jnp.where(e == ex, pstart[ex:ex + 1, :], 0)
        pos_ref[:, pl.ds(c0, lanes)] = acc
        return carry

    lax.fori_loop(0, e8_ref.shape[1] // lanes, chunk, 0)


def _plan(counts, e8, r8, bm):
    n_tok = e8.shape[1]
    tb = min(8192, n_tok)
    return pl.pallas_call(
        functools.partial(_plan_kernel, bm=bm),
        grid=(n_tok // tb,),
        in_specs=[pl.BlockSpec(counts.shape, lambda i: (0, 0)),
                  pl.BlockSpec((8, tb), lambda i: (0, i)),
                  pl.BlockSpec((8, tb), lambda i: (0, i))],
        out_specs=pl.BlockSpec((8, tb), lambda i: (0, i)),
        out_shape=jax.ShapeDtypeStruct((8, n_tok), I32),
        compiler_params=_params("arbitrary"),
        name="plan",
    )(counts, e8, r8)


def _sc_mesh():
    return plsc.VectorSubcoreMesh(core_axis_name="c", subcore_axis_name="s",
                                  num_cores=V7X_SC_CORES, num_subcores=V7X_SC_SUBCORES)


def _sc_worker_base(per_worker):
    return (lax.axis_index("s") * V7X_SC_CORES + lax.axis_index("c")) * per_worker


def _sc_dispatch(h2p, posflat, n_rows):
    n_tok, half = h2p.shape
    win = SC_WINDOW
    per_worker = n_tok // SC_WORKERS
    n_win = per_worker // win

    wins_per_k = n_tok // win

    @functools.partial(
        pl.kernel, mesh=_sc_mesh(),
        out_type=jax.ShapeDtypeStruct((n_rows, half), I32),
        scratch_types=[pltpu.VMEM((TOP_K, n_win, win), I32), pltpu.VMEM((2, win, half), I32),
                       pltpu.SemaphoreType.DMA((2,)), pltpu.SemaphoreType.DMA((2,))],
        name="sc_dispatch")
    def run(h2p_hbm, pos_hbm, xs_hbm, idx_v, rows_v, sem_in, sem_out):
        wid = _sc_worker_base(1)
        for k in range(TOP_K):
            pltpu.sync_copy(pos_hbm.at[pl.ds(k * wins_per_k + wid * n_win, n_win)], idx_v.at[k])

        @pl.loop(0, n_win, step=2)
        def _(j):
            loads = []
            for b in range(2):
                t0 = pl.multiple_of((wid * n_win + j + b) * win, win)
                loads.append(pltpu.async_copy(h2p_hbm.at[pl.ds(t0, win)], rows_v.at[b], sem_in.at[b]))
            stores = []
            for b in range(2):
                loads[b].wait()
                for k in range(TOP_K):
                    stores.append(pltpu.async_copy(rows_v.at[b], xs_hbm.at[idx_v.at[k, j + b]], sem_out.at[b]))
            for cp in stores:
                cp.wait()

    return run(h2p, posflat.reshape(TOP_K * wins_per_k, win))


def _sc_combine(ys, posflat, w_lanes):
    n_tok = w_lanes.shape[0]
    half = ys.shape[1]
    win, step, lanes = SC_WINDOW, SC_COMBINE_TOKENS, V7X_SC_LANES
    per_worker = n_tok // SC_WORKERS
    n_win = per_worker // win
    wins_per_k = n_tok // win

    @functools.partial(
        pl.kernel, mesh=_sc_mesh(),
        out_type=jax.ShapeDtypeStruct((n_tok, half), I32),
        scratch_types=[pltpu.VMEM((TOP_K, n_win, win), I32), pltpu.VMEM((2, TOP_K, step, half), I32),
                       pltpu.VMEM((2, step, w_lanes.shape[1]), I32), pltpu.VMEM((step, half), I32),
                       pltpu.SemaphoreType.DMA((2,))],
        compiler_params=pltpu.CompilerParams(needs_layout_passes=False),
        name="sc_combine")
    def run(ys_hbm, pos_hbm, w_hbm, out_hbm, idx_v, rows_v, w_v, out_v, sem):
        wid = _sc_worker_base(1)
        n_steps = per_worker // step
        for k in range(TOP_K):
            pltpu.sync_copy(pos_hbm.at[pl.ds(k * wins_per_k + wid * n_win, n_win)], idx_v.at[k])

        def copies(s, slot):
            wdw = s // (win // step)
            q0 = pl.multiple_of((s % (win // step)) * step, step)
            t0 = pl.multiple_of(wid * per_worker + s * step, step)
            cps = [pltpu.make_async_copy(ys_hbm.at[idx_v[k, wdw, pl.ds(q0, step)]], rows_v.at[slot, k],
                                         sem.at[slot]) for k in range(TOP_K)]
            return cps + [pltpu.make_async_copy(w_hbm.at[pl.ds(t0, step)], w_v.at[slot], sem.at[slot])]

        def compute(s, slot):
            @pl.loop(0, step, step=SC_COMBINE_UNROLL)
            def _(i0):
                for i in [i0 + di for di in range(SC_COMBINE_UNROLL)]:
                    wb = [plsc.bitcast(w_v[slot, i, pl.ds(k * lanes, lanes)], BF16) for k in range(TOP_K)]
                    for c in range(0, half, lanes):
                        t = [wb[k] * plsc.bitcast(rows_v[slot, k, i, pl.ds(c, lanes)], BF16)
                             for k in range(TOP_K)]
                        while len(t) > 1:
                            t = [t[a] + t[a + 1] for a in range(0, len(t) - 1, 2)] + t[len(t) & ~1:]
                        out_v[i, pl.ds(c, lanes)] = plsc.bitcast(t[0], I32)

            t0 = pl.multiple_of(wid * per_worker + s * step, step)
            pltpu.sync_copy(out_v, out_hbm.at[pl.ds(t0, step)])

        for cp in copies(0, 0):
            cp.start()

        @pl.loop(0, n_steps, step=2)
        def _(s):
            for cp in copies(s + 1, 1):
                cp.start()
            for cp in copies(s, 0):
                cp.wait()
            compute(s, 0)

            @pl.when(s + 2 < n_steps)
            def _():
                for cp in copies(s + 2, 0):
                    cp.start()

            for cp in copies(s + 1, 1):
                cp.wait()
            compute(s + 1, 1)

    return run(ys, posflat.reshape(TOP_K * wins_per_k, win), w_lanes)


def _experts_kernel(be_ref, nu_ref, x_ref, w1_ref, w3_ref, w2_ref, after_ref, y_ref, w13_s, w2_s):
    del after_ref
    i = pl.program_id(0)
    changed = jnp.logical_or(i == 0, be_ref[i] != be_ref[jnp.maximum(i - 1, 0)])

    @pl.when(changed)
    def _():
        f = w1_ref.shape[2]
        w13_s[:, :f] = w1_ref[0].astype(BF16)
        w13_s[:, f:] = w3_ref[0].astype(BF16)
        w2_s[...] = w2_ref[0].astype(BF16)

    @pl.when(i < nu_ref[0])
    def _():
        half = x_ref.shape[1]
        sub = min(EXPERT_SUB, x_ref.shape[0])
        for r0 in range(0, x_ref.shape[0], sub):
            lo, hi = _unpack_halves(lax.bitcast_convert_type(x_ref[r0:r0 + sub, :], U32))
            a = jnp.dot(lo.astype(BF16), w13_s[:half, :], preferred_element_type=F32)
            a = a + jnp.dot(hi.astype(BF16), w13_s[half:, :], preferred_element_type=F32)
            f = a.shape[1] // 2
            act = _silu(a[:, :f]) * a[:, f:]
            y = jnp.dot(act.astype(BF16), w2_s[...], preferred_element_type=F32)
            y_ref[r0:r0 + sub, :] = lax.bitcast_convert_type(_pack_halves(y), I32)


def _experts(block_e, n_used, xs, w1, w3, w2, bm, after):
    n_rows, half = xs.shape
    d, f = w1.shape[1], w1.shape[2]
    rows = lambda i, be, nu: (jnp.minimum(i, nu[0] - 1), 0)
    wblk = lambda i, be, nu: (be[i], 0, 0)
    grid_spec = pltpu.PrefetchScalarGridSpec(
        num_scalar_prefetch=2,
        grid=(n_rows // bm,),
        in_specs=[pl.BlockSpec((bm, half), rows),
                  pl.BlockSpec((1, d, f), wblk),
                  pl.BlockSpec((1, d, f), wblk),
                  pl.BlockSpec((1, f, d), wblk),
                  pl.BlockSpec(memory_space=pl.ANY)],
        out_specs=pl.BlockSpec((bm, half), rows),
        scratch_shapes=[pltpu.VMEM((d, 2 * f), BF16), pltpu.VMEM((f, d), BF16)],
    )
    return pl.pallas_call(
        _experts_kernel,
        grid_spec=grid_spec,
        out_shape=jax.ShapeDtypeStruct((n_rows, half), I32),
        compiler_params=_params("arbitrary"),
        name="experts",
    )(block_e, n_used, xs, w1, w3, w2, after)


def _combine_kernel(r_ref, x1_ref, gate2_ref, gfin_ref, *rest):
    o_ref = rest[-1]
    routed = jnp.concatenate(_unpack_halves(lax.bitcast_convert_type(r_ref[...], U32)), axis=1)
    o_ref[...] = _rms(x1_ref[...].astype(F32) + gate2_ref[0] * routed, gfin_ref[...])


def _combine(routed, x1, gate2, g_final, tt, tiles_per_seq, b_src, b0, n_tok_all, out_prev):
    d = x1.shape[1]
    n_tok, half = routed.shape
    src0 = b_src * tiles_per_seq
    tile0 = b0 * tiles_per_seq
    in_specs = [pl.BlockSpec((tt, half), lambda i: (i, 0)),
                pl.BlockSpec((tt, d), lambda i: (i + src0, 0)),
                pl.BlockSpec((1, 1, d), lambda i: (i // tiles_per_seq + b0, 0, 0)),
                pl.BlockSpec((1, d), lambda i: (0, 0))]
    args = [routed, x1, gate2, g_final.reshape(1, d)]
    aliases = {}
    if out_prev is not None:
        in_specs.append(pl.BlockSpec(memory_space=pl.ANY))
        args.append(out_prev)
        aliases = {len(args) - 1: 0}
    return pl.pallas_call(
        _combine_kernel,
        grid=(n_tok // tt,),
        in_specs=in_specs,
        out_specs=pl.BlockSpec((tt, d), lambda i: (i + tile0, 0)),
        out_shape=jax.ShapeDtypeStruct((n_tok_all, d), F32),
        input_output_aliases=aliases,
        compiler_params=_params("arbitrary"),
        name="combine",
    )(*args)


def _block_experts(counts, bm, n_blocks):
    padded = (counts.astype(I32) + (bm - 1)) // bm * bm
    pend = jnp.cumsum(padded)
    starts = jnp.arange(n_blocks, dtype=I32) * bm
    block_e = jnp.minimum(jnp.sum((pend[None, :] <= starts[:, None]).astype(I32), axis=1), N_EXPERTS - 1)
    return block_e, (pend[-1:] // bm).astype(I32)


def kernel(x, c, w_ada, b_ada, g_mix, w_in, conv_w, conv_b, g_conv, lam_re, lam_im, log_dt, b_re, b_im,
           c_re, c_im, d_skip, w_glu, b_glu, g_ssm, w_out, g_ffn, w_router, router_bias, w1, w3, w2,
           ws1, ws3, ws2, g_final):
    bsz, seq, d = x.shape
    n_tok = bsz * seq
    ts = min(MIX_BACK_TOKENS, seq)
    L = min(S5_CHUNK, seq)
    n_chunks = seq // L
    G, H = SSM_GROUPS, SSM_GROUP_CH

    mod = _adaln(c, w_ada, b_ada).reshape(bsz, 6, 1, d)
    shift1, scale1, gate1, shift2, scale2, gate2 = (mod[:, i] for i in range(6))

    kern, e_mat, f_mat, a_pow = _s5_tables(lam_re, lam_im, log_dt, b_re, b_im, c_re, c_im, d_skip,
                                           L, n_chunks)
    w_in_bf = w_in.astype(BF16)

    NG, NJ = N_EXPERT_GROUPS, EXPERTS_PER_GROUP
    w_r = w_router.astype(F32).reshape(d, NG, NJ).transpose(0, 2, 1).reshape(d, N_EXPERTS)
    r_bias = router_bias.reshape(NG, NJ).T.reshape(N_EXPERTS)
    w_r_hi = w_r.astype(BF16)
    wr_hi = jnp.concatenate([w_r_hi, (w_r - w_r_hi.astype(F32)).astype(BF16)], axis=1)
    wr_lo = jnp.concatenate([w_r_hi, jnp.zeros_like(w_r_hi)], axis=1)
    w_out_bf = w_out.astype(BF16)
    w_glu_bf = w_glu.astype(BF16)
    ws13 = jnp.concatenate([ws1, ws3], axis=1).astype(BF16)
    ws2_bf = ws2.astype(BF16)
    bm = EXPERT_ROWS
    tt = min(COMBINE_TOKENS, seq)

    n_parts = LAYER_PARTS if bsz % LAYER_PARTS == 0 else 1
    pb = bsz // n_parts
    pt = pb * seq
    n_blocks = -(-(pt * TOP_K + N_EXPERTS * (bm - 1)) // bm)
    none = jnp.zeros((8, 128), F32)

    def run_experts(part, after):
        return _experts(*part["block_e"], part["xs"], w1, w3, w2, bm, after)

    def finish(part, ys, out, n_split):
        nb = pb // n_split
        for h in range(n_split):
            lo, hi = h * nb * seq, (h + 1) * nb * seq
            pos = part["pos8"][:TOP_K, lo:hi].reshape(TOP_K * nb * seq)
            routed = _sc_combine(ys, pos, part["w_lanes"][lo:hi])
            out = _combine(routed, part["x1"].reshape(pt, d), gate2, g_final, tt,
                           seq // tt, h * nb, part["b0"] + h * nb, n_tok, out)
        return out

    out, prev, after_front = None, None, none
    for p in range(n_parts):
        b0 = p * pb
        yconv, ut = _mix_front(x, shift1, scale1, g_mix, w_in_bf, conv_w, conv_b, g_conv,
                               min(MIX_FRONT_TOKENS, seq), b0, pb, after_front)
        ys_prev = run_experts(prev, ut) if prev is not None else None
        yt = _s5_core(ut.reshape(pb, G, H, n_chunks, L), kern, e_mat, f_mat, a_pow,
                      none if ys_prev is None else ys_prev).reshape(pb, G * H, seq)
        x1, h2p, e8, w8, r8, counts = _mix_back(
            x, yconv, yt, gate1, shift2, scale2, gate2, w_glu_bf, b_glu, g_ssm,
            w_out_bf[:D_CONV], w_out_bf[D_CONV:], g_ffn, wr_hi, wr_lo, r_bias, ws13, ws2_bf,
            ts, b0, pb)
        counts = counts.reshape(NJ, NG, -1).transpose(1, 0, 2).reshape(N_EXPERTS, -1)
        pos8 = _plan(counts, e8, r8, bm)
        xs = _sc_dispatch(h2p.reshape(pt, d // 2), pos8[:TOP_K].reshape(TOP_K * pt), n_blocks * bm)
        if prev is not None:
            out = finish(prev, ys_prev, out, 1)
        w_bits = lax.bitcast_convert_type(w8.reshape(pt, 8).astype(BF16), jnp.uint16).astype(U32)
        w_lanes = jnp.repeat(lax.bitcast_convert_type((w_bits << 16) | w_bits, I32), V7X_SC_LANES, axis=1)
        prev = dict(b0=b0, x1=x1, w_lanes=w_lanes, pos8=pos8, xs=xs,
                    block_e=_block_experts(counts[:, 0], bm, n_blocks))
        after_front = pos8
    ys_last = run_experts(prev, none if out is None else out)
    out = finish(prev, ys_last, out, LAST_PART_SPLIT if pb % LAST_PART_SPLIT == 0 else 1)
    return out.reshape(bsz, seq, d)
```

```python
import functools
import math

import jax
import jax.numpy as jnp
import numpy as np
from jax import lax
from jax.experimental import pallas as pl
from jax.experimental.pallas import tpu as pltpu
from jax.experimental.pallas import tpu_sc as plsc

F32 = jnp.float32
BF16 = jnp.bfloat16
U32 = jnp.uint32
I32 = jnp.int32

D_CONV = 768
D_SSM = 256
SSM_GROUPS = 16
SSM_GROUP_CH = 16
SSM_STATE = 64
N_EXPERTS = 64
TOP_K = 6
N_EXPERT_GROUPS = 8
TOPK_GROUPS = 4
EXPERTS_PER_GROUP = 8
ROUTED_SCALE = 2.5
RMS_EPS = 1e-6

S5_CHUNK = 128
EXPERT_ROWS = 1024
EXPERT_SUB = 512
EXPERT_ROW_BUFFERS = 3
MIX_FRONT_TOKENS = 1024
MIX_FRONT_SUB = 1024
MIX_SUB = 256
MIX_BACK_TOKENS = 1024
MIX_PHASE_LAG = 3
COMBINE_TOKENS = 512
LAYER_PARTS = 2
LAST_PART_SPLIT = 4
RESIDUAL_DTYPE = BF16
V7X_SC_CORES = 2
V7X_SC_SUBCORES = 16
V7X_SC_LANES = 16
SC_WORKERS = V7X_SC_CORES * V7X_SC_SUBCORES
SC_COMBINE_TOKENS = 16
SC_COMBINE_UNROLL = 2
SC_WINDOW = 64
V7X_VMEM_LIMIT = 56 * 1024 * 1024
NEG_INF = float("-inf")
HIGH_HALF = np.uint32(0xFFFF0000)


def _rms(x, g):
    return x * lax.rsqrt(jnp.mean(x * x, axis=-1, keepdims=True) + RMS_EPS) * g


def _gelu_tanh(x):
    return 0.5 * x * (1.0 + jnp.tanh(math.sqrt(2.0 / math.pi) * (x + 0.044715 * (x * x * x))))


def _silu(x):
    return x * jax.nn.sigmoid(x)


def _params(*sem):
    return pltpu.CompilerParams(dimension_semantics=sem, vmem_limit_bytes=V7X_VMEM_LIMIT)


def _pack_halves(y):
    return _pack_rounded(y.astype(BF16).astype(F32))


def _pack_rounded(y):
    m = y.shape[1] // 2
    bits = lax.bitcast_convert_type(y, U32)
    return (bits[:, m:] & HIGH_HALF) | (bits[:, :m] >> 16)


def _unpack_halves(w):
    lo = lax.bitcast_convert_type(w << 16, F32)
    hi = lax.bitcast_convert_type(w & HIGH_HALF, F32)
    return lo, hi


def _adaln_kernel(c_ref, w_ref, b_ref, o_ref):
    o_ref[...] = jnp.dot(_silu(c_ref[...]), w_ref[...], preferred_element_type=F32,
                         precision=lax.Precision.HIGHEST) + b_ref[...]


def _adaln(c, w_ada, b_ada):
    bsz, d = c.shape
    n = w_ada.shape[1]
    bn = 1024
    return pl.pallas_call(
        _adaln_kernel,
        grid=(n // bn,),
        in_specs=[pl.BlockSpec((bsz, d), lambda j: (0, 0)),
                  pl.BlockSpec((d, bn), lambda j: (0, j)),
                  pl.BlockSpec((1, bn), lambda j: (0, j))],
        out_specs=pl.BlockSpec((bsz, bn), lambda j: (0, j)),
        out_shape=jax.ShapeDtypeStruct((bsz, n), F32),
        compiler_params=_params("arbitrary"),
        name="adaln",
    )(c, w_ada, b_ada.reshape(1, n))


def _mix_front_kernel(x_ref, shift_ref, scale_ref, gmix_ref, win_ref, cw_ref, cb_ref, gconv_ref, after_ref,
                      yconv_ref, ut_ref, zprev_ref):
    del after_ref
    s = pl.program_id(1)

    @pl.when(s == 0)
    def _():
        zprev_ref[...] = jnp.zeros_like(zprev_ref)

    cw = cw_ref[...]
    sub = min(MIX_FRONT_SUB, x_ref.shape[1])
    carry = {"prev": zprev_ref[...]}

    def norm(st):
        x = x_ref[0, st["r0"]:st["r0"] + sub, :]
        st["h"] = (_rms(x, gmix_ref[...]) * (1.0 + scale_ref[0]) + shift_ref[0]).astype(BF16)

    def in_proj(st):
        st["proj"] = jnp.dot(st.pop("h"), win_ref[...], preferred_element_type=F32)

    def mixers(st):
        r0 = st["r0"]
        proj = st.pop("proj")
        b_gate = proj[:, :D_CONV]
        c_gate = proj[:, D_CONV:2 * D_CONV]
        v = proj[:, 2 * D_CONV:3 * D_CONV]
        u = proj[:, 3 * D_CONV:]
        z = c_gate * v
        prev = carry["prev"]
        rid = lax.broadcasted_iota(I32, z.shape, 0)
        z1 = jnp.where(rid == 0, prev[7:8, :], pltpu.roll(z, 1, axis=0))
        z2 = jnp.where(rid == 0, prev[6:7, :], jnp.where(rid == 1, prev[7:8, :], pltpu.roll(z, 2, axis=0)))
        carry["prev"] = z[sub - 8:, :]
        conv = cw[0:1, :] * z2 + cw[1:2, :] * z1 + cw[2:3, :] * z + cb_ref[...]
        yconv_ref[0, r0:r0 + sub, :] = _rms(b_gate * conv, gconv_ref[...]).astype(BF16)
        ut_ref[0, :, r0:r0 + sub] = u.T.astype(BF16)

    phases = (norm, in_proj, mixers)
    chains = [{"r0": r0} for r0 in range(0, x_ref.shape[1], sub)]
    for step in range(len(phases) + len(chains) - 1):
        for ci, st in enumerate(chains):
            if 0 <= step - ci < len(phases):
                phases[step - ci](st)
    zprev_ref[...] = carry["prev"]


def _mix_front(x, shift1, scale1, g_mix, w_in_bf, conv_w, conv_b, g_conv, ts, b0, bsz, after):
    _, seq, d = x.shape
    d_in = w_in_bf.shape[1]
    row = lambda b, s: (b + b0, 0, 0)
    const2 = lambda b, s: (0, 0)
    return pl.pallas_call(
        _mix_front_kernel,
        grid=(bsz, seq // ts),
        in_specs=[pl.BlockSpec((1, ts, d), lambda b, s: (b + b0, s, 0)),
                  pl.BlockSpec((1, 1, d), row),
                  pl.BlockSpec((1, 1, d), row),
                  pl.BlockSpec((1, d), const2),
                  pl.BlockSpec((d, d_in), const2),
                  pl.BlockSpec((8, D_CONV), const2),
                  pl.BlockSpec((1, D_CONV), const2),
                  pl.BlockSpec((1, D_CONV), const2),
                  pl.BlockSpec(memory_space=pl.ANY)],
        out_specs=[pl.BlockSpec((1, ts, D_CONV), lambda b, s: (b, s, 0)),
                   pl.BlockSpec((1, D_SSM, ts), lambda b, s: (b, 0, s))],
        out_shape=[jax.ShapeDtypeStruct((bsz, seq, D_CONV), BF16),
                   jax.ShapeDtypeStruct((bsz, D_SSM, seq), BF16)],
        scratch_shapes=[pltpu.VMEM((8, D_CONV), F32)],
        compiler_params=_params("arbitrary", "arbitrary"),
        name="mix_front",
    )(x, shift1, scale1, g_mix.reshape(1, d), w_in_bf,
      jnp.pad(conv_w, ((0, 8 - conv_w.shape[0]), (0, 0))), conv_b.reshape(1, D_CONV),
      g_conv.reshape(1, D_CONV), after)


def _s5_tables(lam_re, lam_im, log_dt, b_re, b_im, c_re, c_im, d_skip, chunk, n_chunks):
    hp = lax.Precision.HIGHEST
    G, H, P, L = SSM_GROUPS, SSM_GROUP_CH, SSM_STATE, chunk
    lr = lam_re.astype(F32)
    li = lam_im.astype(F32)
    dt = jnp.exp(log_dt.astype(F32))[:, None]
    mag = jnp.exp(lr * dt)
    ang = li * dt
    ab_re = mag * jnp.cos(ang)
    ab_im = mag * jnp.sin(ang)
    den = lr * lr + li * li
    nr = ab_re - 1.0
    ni = ab_im
    q_re = (nr * lr + ni * li) / den
    q_im = (ni * lr - nr * li) / den
    br = b_re.astype(F32)
    bi = b_im.astype(F32)
    bb_re = q_re[..., None] * br - q_im[..., None] * bi
    bb_im = q_re[..., None] * bi + q_im[..., None] * br

    def a_power(tau):
        t = tau.astype(F32)[None, :, None]
        m = jnp.exp(t * (lr * dt)[:, None, :])
        th = t * ang[:, None, :]
        return m * jnp.cos(th), m * jnp.sin(th)

    pw_re, pw_im = a_power(jnp.arange(L + 1))
    cr = c_re.astype(F32)[:, None]
    ci = c_im.astype(F32)[:, None]
    cp_re = cr * pw_re[:, :, None, :] - ci * pw_im[:, :, None, :]
    cp_im = cr * pw_im[:, :, None, :] + ci * pw_re[:, :, None, :]
    kern = (jnp.einsum('gthp,gpk->gthk', cp_re[:, :L], bb_re, precision=hp)
            - jnp.einsum('gthp,gpk->gthk', cp_im[:, :L], bb_im, precision=hp))
    kern = kern.at[:, 0].add(d_skip.astype(F32).reshape(G, H)[:, :, None] * jnp.eye(H, dtype=F32))
    kern = kern.transpose(0, 3, 2, 1).reshape(G, H * H, L)

    rev_re = pw_re[:, L - 1::-1]
    rev_im = pw_im[:, L - 1::-1]
    bt_re = bb_re.transpose(0, 2, 1)[:, :, None, :]
    bt_im = bb_im.transpose(0, 2, 1)[:, :, None, :]
    e_re = rev_re[:, None] * bt_re - rev_im[:, None] * bt_im
    e_im = rev_re[:, None] * bt_im + rev_im[:, None] * bt_re
    e_mat = jnp.concatenate([e_re, e_im], axis=-1).reshape(G, H * L, 2 * P)

    f_re = cp_re[:, 1:].transpose(0, 3, 2, 1)
    f_im = -cp_im[:, 1:].transpose(0, 3, 2, 1)
    f_mat = jnp.concatenate([f_re, f_im], axis=1).reshape(G, 2 * P, H * L)

    n_steps = max(1, (n_chunks - 1).bit_length())
    sr, si = a_power(L * (2 ** jnp.arange(n_steps)))
    a_pow = jnp.stack([jnp.concatenate([sr, sr], axis=-1),
                       jnp.concatenate([-si, si], axis=-1)], axis=2)
    return kern, e_mat.astype(BF16), f_mat.astype(BF16), a_pow


def _s5_kernel(u_ref, k_ref, e_ref, f_ref, a_ref, after_ref, y_ref, toep_ref, *, n_chunks, n_steps, chunk):
    del after_ref
    L, H = chunk, SSM_GROUP_CH

    causal = lax.broadcasted_iota(I32, (L, L), 1) >= lax.broadcasted_iota(I32, (L, L), 0)

    def build(hin, carry):
        r0 = pl.multiple_of(hin * L, L)
        for hout in range(H):
            krow = k_ref[0, pl.ds(hin * H + hout, 1), :]
            blk = pltpu.roll(jnp.broadcast_to(krow, (L, L)), 0, axis=1, stride=1, stride_axis=0)
            toep_ref[pl.ds(r0, L), hout * L:(hout + 1) * L] = jnp.where(causal, blk, 0.0).astype(BF16)
        return carry

    lax.fori_loop(0, H, build, 0)

    bsz = u_ref.shape[0]
    u = jnp.concatenate([u_ref[:, h].reshape(bsz * n_chunks, L) for h in range(H)], axis=1)
    st = jnp.dot(u, e_ref[0], preferred_element_type=F32)
    cidx = lax.broadcasted_iota(I32, st.shape, 0) % n_chunks
    p2 = st.shape[1]

    def shifted(xv, d):
        return jnp.where(cidx >= d, pltpu.roll(xv, d, axis=0), 0.0)

    for k in range(n_steps):
        d = 1 << k
        if d >= n_chunks:
            break
        z = shifted(st, d)
        st = st + a_ref[0, k, 0:1, :] * z + a_ref[0, k, 1:2, :] * pltpu.roll(z, p2 // 2, axis=1)
    s_in = shifted(st, 1)
    y = jnp.dot(u, toep_ref[...], preferred_element_type=F32)
    y = (y + jnp.dot(s_in.astype(BF16), f_ref[0], preferred_element_type=F32)).astype(BF16)
    for h in range(H):
        y_ref[:, h] = y[:, h * L:(h + 1) * L].reshape(bsz, n_chunks, L)


def _s5_core(ut5, kern, e_mat, f_mat, a_pow, after):
    bsz, g, h, n_chunks, chunk = ut5.shape
    hl = h * chunk
    n_steps = a_pow.shape[1]
    p2 = e_mat.shape[2]
    blk = lambda i: (i, 0, 0)
    seq_blk = pl.BlockSpec((bsz, None, h, n_chunks, chunk), lambda i: (0, i, 0, 0, 0))
    return pl.pallas_call(
        functools.partial(_s5_kernel, n_chunks=n_chunks, n_steps=n_steps, chunk=chunk),
        grid=(g,),
        in_specs=[seq_blk,
                  pl.BlockSpec((1,) + kern.shape[1:], blk),
                  pl.BlockSpec((1, hl, p2), blk),
                  pl.BlockSpec((1, p2, hl), blk),
                  pl.BlockSpec((1, n_steps, 2, p2), lambda i: (i, 0, 0, 0)),
                  pl.BlockSpec(memory_space=pl.ANY)],
        out_specs=seq_blk,
        out_shape=jax.ShapeDtypeStruct(ut5.shape, BF16),
        scratch_shapes=[pltpu.VMEM((hl, hl), BF16)],
        compiler_params=_params("arbitrary"),
        name="s5_core",
    )(ut5, kern, e_mat, f_mat, a_pow, after)


def _route(scores, biased, tri, base):
    n_tok = scores.shape[1]
    shape3 = (EXPERTS_PER_GROUP, N_EXPERT_GROUPS, n_tok)
    sc3 = scores.reshape(shape3)
    b3 = biased.reshape(shape3)
    j_iota = lax.broadcasted_iota(I32, shape3, 0)
    e_iota = lax.broadcasted_iota(I32, shape3, 1) * EXPERTS_PER_GROUP + j_iota

    def red(fn, x):
        return fn(fn(x, axis=0, keepdims=True), axis=1, keepdims=True)

    m1 = jnp.max(b3, axis=0, keepdims=True)
    i1 = jnp.min(jnp.where(b3 == m1, j_iota, EXPERTS_PER_GROUP), axis=0, keepdims=True)
    m2 = jnp.max(jnp.where(j_iota == i1, NEG_INF, b3), axis=0, keepdims=True)
    gs = m1 + m2

    g_iota = lax.broadcasted_iota(I32, gs.shape, 1)
    gsel = jnp.zeros(gs.shape, F32)
    cur = gs
    for _ in range(TOPK_GROUPS):
        m = jnp.max(cur, axis=1, keepdims=True)
        ig = jnp.min(jnp.where(cur == m, g_iota, N_EXPERT_GROUPS), axis=1, keepdims=True)
        pick = g_iota == ig
        gsel = jnp.where(pick, 1.0, gsel)
        cur = jnp.where(pick, NEG_INF, cur)

    cur = jnp.where(gsel > 0.0, b3, NEG_INF)
    sel = jnp.zeros(shape3, F32)
    ids, vals = [], []
    for _ in range(TOP_K):
        m = red(jnp.max, cur)
        ie = red(jnp.min, jnp.where(cur == m, e_iota, N_EXPERTS))
        pick = e_iota == ie
        ids.append(ie)
        vals.append(red(jnp.sum, jnp.where(pick, sc3, 0.0)))
        sel = jnp.where(pick, 1.0, sel)
        cur = jnp.where(pick, NEG_INF, cur)
    tot = functools.reduce(lambda a, b: a + b, vals)

    sel2 = sel.reshape(N_EXPERTS, n_tok)
    before = jnp.dot(sel2.astype(BF16), tri, preferred_element_type=F32) + base
    before3 = before.reshape(shape3)
    ranks = [red(jnp.sum, jnp.where(e_iota == ie, before3, 0.0)) for ie in ids]

    def rows(parts, dtype):
        parts = [p.reshape(1, n_tok).astype(dtype) for p in parts]
        return jnp.concatenate(parts + [jnp.zeros((8 - len(parts), n_tok), dtype)], axis=0)

    e8 = rows(ids, I32)
    w8 = rows([v / tot * ROUTED_SCALE for v in vals], F32)
    r8 = rows(ranks, I32)
    return e8, w8, r8, jnp.sum(sel2, axis=1, keepdims=True)


def _mix_back_kernel(x_ref, yconv_ref, yt_ref, gate1_ref, shift2_ref, scale2_ref, gate2_ref,
                     wglu_ref, bglu_ref, gssm_ref, woc_ref, wos_ref, gffn_ref,
                     wrh_ref, wrl_ref, rbias_ref, ws13_ref, ws2_ref, tri_ref,
                     x1_ref, h2p_ref, e8_ref, w8_ref, r8_ref, cnt_ref):
    first = jnp.logical_and(pl.program_id(0) == 0, pl.program_id(1) == 0)

    @pl.when(first)
    def _():
        cnt_ref[...] = jnp.zeros_like(cnt_ref)

    sub = tri_ref.shape[0]

    def ssm_post(st):
        r0 = st["r0"]
        y = yt_ref[0, :, r0:r0 + sub].astype(F32).T
        y = _gelu_tanh(y)
        y = y * jax.nn.sigmoid(jnp.dot(y.astype(BF16), wglu_ref[...], preferred_element_type=F32)
                               + bglu_ref[...])
        st["y_ssm"] = _rms(y, gssm_ref[...]).astype(BF16)

    def out_proj(st):
        r0 = st["r0"]
        mix = jnp.dot(yconv_ref[0, r0:r0 + sub, :], woc_ref[...], preferred_element_type=F32)
        mix = mix + jnp.dot(st.pop("y_ssm"), wos_ref[...], preferred_element_type=F32)
        st["x1"] = x_ref[0, r0:r0 + sub, :] + gate1_ref[0] * mix

    def ffn_in(st):
        r0 = st["r0"]
        h2 = _rms(st["x1"], gffn_ref[...] * (1.0 + scale2_ref[0])) + shift2_ref[0]
        h2b = h2.astype(BF16)
        h2r = h2b.astype(F32)
        h2p_ref[0, r0:r0 + sub, :] = lax.bitcast_convert_type(_pack_rounded(h2r), I32)
        st["h2b"] = h2b
        st["h2l"] = (h2 - h2r).astype(BF16)

    def shared_expert(st):
        r0 = st["r0"]
        a = jnp.dot(st["h2b"], ws13_ref[...], preferred_element_type=F32)
        f = a.shape[1] // 2
        act = _silu(a[:, :f]) * a[:, f:]
        shared = jnp.dot(act.astype(BF16), ws2_ref[...], preferred_element_type=F32)
        x1_ref[0, r0:r0 + sub, :] = (st.pop("x1") + gate2_ref[0] * shared).astype(x1_ref.dtype)

    def router(st):
        lt = (jnp.dot(st.pop("h2b"), wrh_ref[...], preferred_element_type=F32)
              + jnp.dot(st.pop("h2l"), wrl_ref[...], preferred_element_type=F32)).T
        st["scores"] = jax.nn.sigmoid(lt[:N_EXPERTS] + lt[N_EXPERTS:])

    def route(st):
        r0 = st["r0"]
        scores = st.pop("scores")
        e8, w8, r8, cnt = _route(scores, scores + rbias_ref[...], tri_ref[...], carry["base"])
        e8_ref[:, r0:r0 + sub] = e8
        w8_ref[0, r0:r0 + sub, :] = w8.T
        r8_ref[:, r0:r0 + sub] = r8
        carry["base"] = carry["base"] + cnt

    carry = {"base": cnt_ref[:, 0:1]}
    phases = (ssm_post, out_proj, ffn_in, shared_expert, router, route)
    chains = [{"r0": r0} for r0 in range(0, x_ref.shape[1], sub)]
    lag = MIX_PHASE_LAG
    for step in range(len(phases) + lag * (len(chains) - 1)):
        for ci, st in enumerate(chains):
            ph = step - lag * ci
            if 0 <= ph < len(phases):
                phases[ph](st)
    cnt_ref[...] = jnp.broadcast_to(carry["base"], cnt_ref.shape)


def _mix_back(x, yconv, yt, gate1, shift2, scale2, gate2, w_glu_bf, b_glu, g_ssm, wo_conv, wo_ssm,
              g_ffn, wr_hi, wr_lo, router_bias, ws13, ws2_bf, ts, b0, bsz):
    _, seq, d = x.shape
    n_tok = bsz * seq
    tiles = seq // ts
    row = lambda b, s: (b + b0, 0, 0)
    const2 = lambda b, s: (0, 0)
    tile_in = lambda b, s: (b + b0, s, 0)
    tile = lambda b, s: (b, s, 0)
    flat = lambda b, s: (0, b * tiles + s)
    full = lambda a: pl.BlockSpec(a.shape, const2)
    sub = min(MIX_SUB, ts)
    tri = jnp.triu(jnp.ones((sub, sub), BF16), k=1)
    args = (w_glu_bf, b_glu.reshape(1, D_SSM), g_ssm.reshape(1, D_SSM), wo_conv, wo_ssm,
            g_ffn.reshape(1, d), wr_hi, wr_lo, router_bias.reshape(N_EXPERTS, 1), ws13, ws2_bf, tri)
    return pl.pallas_call(
        _mix_back_kernel,
        grid=(bsz, tiles),
        in_specs=[pl.BlockSpec((1, ts, d), tile_in),
                  pl.BlockSpec((1, ts, D_CONV), tile),
                  pl.BlockSpec((1, D_SSM, ts), lambda b, s: (b, 0, s)),
                  pl.BlockSpec((1, 1, d), row), pl.BlockSpec((1, 1, d), row),
                  pl.BlockSpec((1, 1, d), row), pl.BlockSpec((1, 1, d), row)]
                 + [full(a) for a in args],
        out_specs=[pl.BlockSpec((1, ts, d), tile),
                   pl.BlockSpec((1, ts, d // 2), tile),
                   pl.BlockSpec((8, ts), flat),
                   pl.BlockSpec((1, ts, 8), tile),
                   pl.BlockSpec((8, ts), flat),
                   pl.BlockSpec((N_EXPERTS, 128), const2)],
        out_shape=[jax.ShapeDtypeStruct((bsz, seq, d), RESIDUAL_DTYPE),
                   jax.ShapeDtypeStruct((bsz, seq, d // 2), I32),
                   jax.ShapeDtypeStruct((8, n_tok), I32),
                   jax.ShapeDtypeStruct((bsz, seq, 8), F32),
                   jax.ShapeDtypeStruct((8, n_tok), I32),
                   jax.ShapeDtypeStruct((N_EXPERTS, 128), F32)],
        compiler_params=_params("arbitrary", "arbitrary"),
        name="mix_back",
    )(x, yconv, yt, gate1, shift2, scale2, gate2, *args)


def _plan_kernel(cnt_ref, e8_ref, r8_ref, pos_ref, *, bm):
    cnt = cnt_ref[...].astype(I32)
    padded = (cnt + (bm - 1)) // bm * bm
    rid = lax.broadcasted_iota(I32, padded.shape, 0)
    incl = padded
    d = 1
    while d < N_EXPERTS:
        incl = incl + jnp.where(rid >= d, pltpu.roll(incl, d, axis=0), 0)
        d *= 2
    pstart = incl - padded
    lanes = pstart.shape[1]

    def chunk(ci, carry):
        c0 = pl.multiple_of(ci * lanes, lanes)
        e = e8_ref[:, pl.ds(c0, lanes)]
        acc = r8_ref[:, pl.ds(c0, lanes)]
        for ex in range(N_EXPERTS):
            acc = acc + jnp.where(e == ex, pstart[ex:ex + 1, :], 0)
        pos_ref[:, pl.ds(c0, lanes)] = acc
        return carry

    lax.fori_loop(0, e8_ref.shape[1] // lanes, chunk, 0)


def _plan(counts, e8, r8, bm):
    n_tok = e8.shape[1]
    tb = min(8192, n_tok)
    return pl.pallas_call(
        functools.partial(_plan_kernel, bm=bm),
        grid=(n_tok // tb,),
        in_specs=[pl.BlockSpec(counts.shape, lambda i: (0, 0)),
                  pl.BlockSpec((8, tb), lambda i: (0, i)),
                  pl.BlockSpec((8, tb), lambda i: (0, i))],
        out_specs=pl.BlockSpec((8, tb), lambda i: (0, i)),
        out_shape=jax.ShapeDtypeStruct((8, n_tok), I32),
        compiler_params=_params("arbitrary"),
        name="plan",
    )(counts, e8, r8)


def _sc_mesh():
    return plsc.VectorSubcoreMesh(core_axis_name="c", subcore_axis_name="s",
                                  num_cores=V7X_SC_CORES, num_subcores=V7X_SC_SUBCORES)


def _sc_worker_base(per_worker):
    return (lax.axis_index("s") * V7X_SC_CORES + lax.axis_index("c")) * per_worker


def _sc_dispatch(h2p, posflat, n_rows):
    n_tok, half = h2p.shape
    win = SC_WINDOW
    per_worker = n_tok // SC_WORKERS
    n_win = per_worker // win

    wins_per_k = n_tok // win

    @functools.partial(
        pl.kernel, mesh=_sc_mesh(),
        out_type=jax.ShapeDtypeStruct((n_rows, half), I32),
        scratch_types=[pltpu.VMEM((TOP_K, n_win, win), I32), pltpu.VMEM((2, win, half), I32),
                       pltpu.SemaphoreType.DMA((2,)), pltpu.SemaphoreType.DMA((2,))],
        name="sc_dispatch")
    def run(h2p_hbm, pos_hbm, xs_hbm, idx_v, rows_v, sem_in, sem_out):
        wid = _sc_worker_base(1)
        for k in range(TOP_K):
            pltpu.sync_copy(pos_hbm.at[pl.ds(k * wins_per_k + wid * n_win, n_win)], idx_v.at[k])

        @pl.loop(0, n_win, step=2)
        def _(j):
            loads = []
            for b in range(2):
                t0 = pl.multiple_of((wid * n_win + j + b) * win, win)
                loads.append(pltpu.async_copy(h2p_hbm.at[pl.ds(t0, win)], rows_v.at[b], sem_in.at[b]))
            stores = []
            for b in range(2):
                loads[b].wait()
                for k in range(TOP_K):
                    stores.append(pltpu.async_copy(rows_v.at[b], xs_hbm.at[idx_v.at[k, j + b]], sem_out.at[b]))
            for cp in stores:
                cp.wait()

    return run(h2p, posflat.reshape(TOP_K * wins_per_k, win))


def _sc_combine(ys, posflat, w_lanes):
    n_tok = w_lanes.shape[0]
    half = ys.shape[1]
    win, step, lanes = SC_WINDOW, SC_COMBINE_TOKENS, V7X_SC_LANES
    per_worker = n_tok // SC_WORKERS
    n_win = per_worker // win
    wins_per_k = n_tok // win

    @functools.partial(
        pl.kernel, mesh=_sc_mesh(),
        out_type=jax.ShapeDtypeStruct((n_tok, half), I32),
        scratch_types=[pltpu.VMEM((TOP_K, n_win, win), I32), pltpu.VMEM((2, TOP_K, step, half), I32),
                       pltpu.VMEM((2, step, w_lanes.shape[1]), I32), pltpu.VMEM((step, half), I32),
                       pltpu.SemaphoreType.DMA((2,))],
        compiler_params=pltpu.CompilerParams(needs_layout_passes=False),
        name="sc_combine")
    def run(ys_hbm, pos_hbm, w_hbm, out_hbm, idx_v, rows_v, w_v, out_v, sem):
        wid = _sc_worker_base(1)
        n_steps = per_worker // step
        for k in range(TOP_K):
            pltpu.sync_copy(pos_hbm.at[pl.ds(k * wins_per_k + wid * n_win, n_win)], idx_v.at[k])

        def copies(s, slot):
            wdw = s // (win // step)
            q0 = pl.multiple_of((s % (win // step)) * step, step)
            t0 = pl.multiple_of(wid * per_worker + s * step, step)
            cps = [pltpu.make_async_copy(ys_hbm.at[idx_v[k, wdw, pl.ds(q0, step)]], rows_v.at[slot, k],
                                         sem.at[slot]) for k in range(TOP_K)]
            return cps + [pltpu.make_async_copy(w_hbm.at[pl.ds(t0, step)], w_v.at[slot], sem.at[slot])]

        def compute(s, slot):
            @pl.loop(0, step, step=SC_COMBINE_UNROLL)
            def _(i0):
                for i in [i0 + di for di in range(SC_COMBINE_UNROLL)]:
                    wb = [plsc.bitcast(w_v[slot, i, pl.ds(k * lanes, lanes)], BF16) for k in range(TOP_K)]
                    for c in range(0, half, lanes):
                        t = [wb[k] * plsc.bitcast(rows_v[slot, k, i, pl.ds(c, lanes)], BF16)
                             for k in range(TOP_K)]
                        while len(t) > 1:
                            t = [t[a] + t[a + 1] for a in range(0, len(t) - 1, 2)] + t[len(t) & ~1:]
                        out_v[i, pl.ds(c, lanes)] = plsc.bitcast(t[0], I32)

            t0 = pl.multiple_of(wid * per_worker + s * step, step)
            pltpu.sync_copy(out_v, out_hbm.at[pl.ds(t0, step)])

        for cp in copies(0, 0):
            cp.start()

        @pl.loop(0, n_steps, step=2)
        def _(s):
            for cp in copies(s + 1, 1):
                cp.start()
            for cp in copies(s, 0):
                cp.wait()
            compute(s, 0)

            @pl.when(s + 2 < n_steps)
            def _():
                for cp in copies(s + 2, 0):
                    cp.start()

            for cp in copies(s + 1, 1):
                cp.wait()
            compute(s + 1, 1)

    return run(ys, posflat.reshape(TOP_K * wins_per_k, win), w_lanes)


def _experts_kernel(be_ref, nu_ref, xs_hbm, w1_ref, w3_ref, w2_ref, after_ref, y_ref, w13_s, w2_s, xbuf, sem):
    del after_ref
    i = pl.program_id(0)
    n_used = nu_ref[0]
    bm = xbuf.shape[1]
    nbuf = xbuf.shape[0]

    def row_copy(blk):
        slot = blk % nbuf
        r0 = pl.multiple_of(blk * bm, bm)
        return pltpu.make_async_copy(xs_hbm.at[pl.ds(r0, bm)], xbuf.at[slot], sem.at[slot])

    @pl.when(i == 0)
    def _():
        for b in range(nbuf - 1):
            @pl.when(b < n_used)
            def _():
                row_copy(b).start()

    @pl.when(i + nbuf - 1 < n_used)
    def _():
        row_copy(i + nbuf - 1).start()

    changed = jnp.logical_or(i == 0, be_ref[i] != be_ref[jnp.maximum(i - 1, 0)])

    @pl.when(changed)
    def _():
        f = w1_ref.shape[2]
        w13_s[:, :f] = w1_ref[0].astype(BF16)
        w13_s[:, f:] = w3_ref[0].astype(BF16)
        w2_s[...] = w2_ref[0].astype(BF16)

    @pl.when(i < n_used)
    def _():
        row_copy(i).wait()
        x_ref = xbuf.at[i % nbuf]
        half = x_ref.shape[1]
        sub = min(EXPERT_SUB, x_ref.shape[0])
        for r0 in range(0, x_ref.shape[0], sub):
            lo, hi = _unpack_halves(lax.bitcast_convert_type(x_ref[r0:r0 + sub, :], U32))
            a = jnp.dot(lo.astype(BF16), w13_s[:half, :], preferred_element_type=F32)
            a = a + jnp.dot(hi.astype(BF16), w13_s[half:, :], preferred_element_type=F32)
            f = a.shape[1] // 2
            act = _silu(a[:, :f]) * a[:, f:]
            y = jnp.dot(act.astype(BF16), w2_s[...], preferred_element_type=F32)
            y_ref[r0:r0 + sub, :] = lax.bitcast_convert_type(_pack_halves(y), I32)


def _experts(block_e, n_used, xs, w1, w3, w2, bm, after):
    n_rows, half = xs.shape
    d, f = w1.shape[1], w1.shape[2]
    rows = lambda i, be, nu: (jnp.minimum(i, nu[0] - 1), 0)
    wblk = lambda i, be, nu: (be[i], 0, 0)
    grid_spec = pltpu.PrefetchScalarGridSpec(
        num_scalar_prefetch=2,
        grid=(n_rows // bm,),
        in_specs=[pl.BlockSpec(memory_space=pl.ANY),
                  pl.BlockSpec((1, d, f), wblk),
                  pl.BlockSpec((1, d, f), wblk),
                  pl.BlockSpec((1, f, d), wblk),
                  pl.BlockSpec(memory_space=pl.ANY)],
        out_specs=pl.BlockSpec((bm, half), rows),
        scratch_shapes=[pltpu.VMEM((d, 2 * f), BF16), pltpu.VMEM((f, d), BF16),
                        pltpu.VMEM((EXPERT_ROW_BUFFERS, bm, half), I32),
                        pltpu.SemaphoreType.DMA((EXPERT_ROW_BUFFERS,))],
    )
    return pl.pallas_call(
        _experts_kernel,
        grid_spec=grid_spec,
        out_shape=jax.ShapeDtypeStruct((n_rows, half), I32),
        compiler_params=_params("arbitrary"),
        name="experts",
    )(block_e, n_used, xs, w1, w3, w2, after)


def _combine_kernel(r_ref, x1_ref, gate2_ref, gfin_ref, *rest):
    o_ref = rest[-1]
    routed = jnp.concatenate(_unpack_halves(lax.bitcast_convert_type(r_ref[...], U32)), axis=1)
    o_ref[...] = _rms(x1_ref[...].astype(F32) + gate2_ref[0] * routed, gfin_ref[...])


def _combine(routed, x1, gate2, g_final, tt, tiles_per_seq, b_src, b0, n_tok_all, out_prev):
    d = x1.shape[1]
    n_tok, half = routed.shape
    src0 = b_src * tiles_per_seq
    tile0 = b0 * tiles_per_seq
    in_specs = [pl.BlockSpec((tt, half), lambda i: (i, 0)),
                pl.BlockSpec((tt, d), lambda i: (i + src0, 0)),
                pl.BlockSpec((1, 1, d), lambda i: (i // tiles_per_seq + b0, 0, 0)),
                pl.BlockSpec((1, d), lambda i: (0, 0))]
    args = [routed, x1, gate2, g_final.reshape(1, d)]
    aliases = {}
    if out_prev is not None:
        in_specs.append(pl.BlockSpec(memory_space=pl.ANY))
        args.append(out_prev)
        aliases = {len(args) - 1: 0}
    return pl.pallas_call(
        _combine_kernel,
        grid=(n_tok // tt,),
        in_specs=in_specs,
        out_specs=pl.BlockSpec((tt, d), lambda i: (i + tile0, 0)),
        out_shape=jax.ShapeDtypeStruct((n_tok_all, d), F32),
        input_output_aliases=aliases,
        compiler_params=_params("arbitrary"),
        name="combine",
    )(*args)


def _block_experts(counts, bm, n_blocks):
    padded = (counts.astype(I32) + (bm - 1)) // bm * bm
    pend = jnp.cumsum(padded)
    starts = jnp.arange(n_blocks, dtype=I32) * bm
    block_e = jnp.minimum(jnp.sum((pend[None, :] <= starts[:, None]).astype(I32), axis=1), N_EXPERTS - 1)
    return block_e, (pend[-1:] // bm).astype(I32)


def kernel(x, c, w_ada, b_ada, g_mix, w_in, conv_w, conv_b, g_conv, lam_re, lam_im, log_dt, b_re, b_im,
           c_re, c_im, d_skip, w_glu, b_glu, g_ssm, w_out, g_ffn, w_router, router_bias, w1, w3, w2,
           ws1, ws3, ws2, g_final):
    bsz, seq, d = x.shape
    n_tok = bsz * seq
    ts = min(MIX_BACK_TOKENS, seq)
    L = min(S5_CHUNK, seq)
    n_chunks = seq // L
    G, H = SSM_GROUPS, SSM_GROUP_CH

    mod = _adaln(c, w_ada, b_ada).reshape(bsz, 6, 1, d)
    shift1, scale1, gate1, shift2, scale2, gate2 = (mod[:, i] for i in range(6))

    kern, e_mat, f_mat, a_pow = _s5_tables(lam_re, lam_im, log_dt, b_re, b_im, c_re, c_im, d_skip,
                                           L, n_chunks)
    w_in_bf = w_in.astype(BF16)

    NG, NJ = N_EXPERT_GROUPS, EXPERTS_PER_GROUP
    w_r = w_router.astype(F32).reshape(d, NG, NJ).transpose(0, 2, 1).reshape(d, N_EXPERTS)
    r_bias = router_bias.reshape(NG, NJ).T.reshape(N_EXPERTS)
    w_r_hi = w_r.astype(BF16)
    wr_hi = jnp.concatenate([w_r_hi, (w_r - w_r_hi.astype(F32)).astype(BF16)], axis=1)
    wr_lo = jnp.concatenate([w_r_hi, jnp.zeros_like(w_r_hi)], axis=1)
    w_out_bf = w_out.astype(BF16)
    w_glu_bf = w_glu.astype(BF16)
    ws13 = jnp.concatenate([ws1, ws3], axis=1).astype(BF16)
    ws2_bf = ws2.astype(BF16)
    bm = EXPERT_ROWS
    tt = min(COMBINE_TOKENS, seq)

    n_parts = LAYER_PARTS if bsz % LAYER_PARTS == 0 else 1
    pb = bsz // n_parts
    pt = pb * seq
    n_blocks = -(-(pt * TOP_K + N_EXPERTS * (bm - 1)) // bm)
    none = jnp.zeros((8, 128), F32)

    def run_experts(part, after):
        return _experts(*part["block_e"], part["xs"], w1, w3, w2, bm, after)

    def finish(part, ys, out, n_split):
        nb = pb // n_split
        for h in range(n_split):
            lo, hi = h * nb * seq, (h + 1) * nb * seq
            pos = part["pos8"][:TOP_K, lo:hi].reshape(TOP_K * nb * seq)
            routed = _sc_combine(ys, pos, part["w_lanes"][lo:hi])
            out = _combine(routed, part["x1"].reshape(pt, d), gate2, g_final, tt,
                           seq // tt, h * nb, part["b0"] + h * nb, n_tok, out)
        return out

    out, prev, after_front = None, None, none
    for p in range(n_parts):
        b0 = p * pb
        yconv, ut = _mix_front(x, shift1, scale1, g_mix, w_in_bf, conv_w, conv_b, g_conv,
                               min(MIX_FRONT_TOKENS, seq), b0, pb, after_front)
        ys_prev = run_experts(prev, ut) if prev is not None else None
        yt = _s5_core(ut.reshape(pb, G, H, n_chunks, L), kern, e_mat, f_mat, a_pow,
                      none if ys_prev is None else ys_prev).reshape(pb, G * H, seq)
        x1, h2p, e8, w8, r8, counts = _mix_back(
            x, yconv, yt, gate1, shift2, scale2, gate2, w_glu_bf, b_glu, g_ssm,
            w_out_bf[:D_CONV], w_out_bf[D_CONV:], g_ffn, wr_hi, wr_lo, r_bias, ws13, ws2_bf,
            ts, b0, pb)
        counts = counts.reshape(NJ, NG, -1).transpose(1, 0, 2).reshape(N_EXPERTS, -1)
        pos8 = _plan(counts, e8, r8, bm)
        xs = _sc_dispatch(h2p.reshape(pt, d // 2), pos8[:TOP_K].reshape(TOP_K * pt), n_blocks * bm)
        if prev is not None:
            out = finish(prev, ys_prev, out, 1)
        w_bits = lax.bitcast_convert_type(w8.reshape(pt, 8).astype(BF16), jnp.uint16).astype(U32)
        w_lanes = jnp.repeat(lax.bitcast_convert_type((w_bits << 16) | w_bits, I32), V7X_SC_LANES, axis=1)
        prev = dict(b0=b0, x1=x1, w_lanes=w_lanes, pos8=pos8, xs=xs,
                    block_e=_block_experts(counts[:, 0], bm, n_blocks))
        after_front = pos8
    ys_last = run_experts(prev, none if out is None else out)
    out = finish(prev, ys_last, out, LAST_PART_SPLIT if pb % LAST_PART_SPLIT == 0 else 1)
    return out.reshape(bsz, seq, d)
```

```python
import functools
import math

import jax
import jax.numpy as jnp
import numpy as np
from jax import lax
from jax.experimental import pallas as pl
from jax.experimental.pallas import tpu as pltpu
from jax.experimental.pallas import tpu_sc as plsc

F32 = jnp.float32
BF16 = jnp.bfloat16
U32 = jnp.uint32
I32 = jnp.int32

D_CONV = 768
D_SSM = 256
SSM_GROUPS = 16
SSM_GROUP_CH = 16
SSM_STATE = 64
N_EXPERTS = 64
TOP_K = 6
N_EXPERT_GROUPS = 8
TOPK_GROUPS = 4
EXPERTS_PER_GROUP = 8
ROUTED_SCALE = 2.5
RMS_EPS = 1e-6

S5_CHUNK = 128
EXPERT_ROWS = 1024
EXPERT_SUB = 512
EXPERT_ROW_BUFFERS = 4
MIX_FRONT_TOKENS = 1024
MIX_FRONT_SUB = 1024
MIX_SUB = 256
MIX_BACK_TOKENS = 1024
MIX_PHASE_LAG = 3
COMBINE_TOKENS = 512
LAYER_PARTS = 2
LAST_PART_SPLIT = 4
RESIDUAL_DTYPE = BF16
V7X_SC_CORES = 2
V7X_SC_SUBCORES = 16
V7X_SC_LANES = 16
SC_WORKERS = V7X_SC_CORES * V7X_SC_SUBCORES
SC_COMBINE_TOKENS = 16
SC_COMBINE_UNROLL = 2
SC_WINDOW = 64
V7X_VMEM_LIMIT = 56 * 1024 * 1024
NEG_INF = float("-inf")
HIGH_HALF = np.uint32(0xFFFF0000)


def _rms(x, g):
    return x * lax.rsqrt(jnp.mean(x * x, axis=-1, keepdims=True) + RMS_EPS) * g


def _gelu_tanh(x):
    return 0.5 * x * (1.0 + jnp.tanh(math.sqrt(2.0 / math.pi) * (x + 0.044715 * (x * x * x))))


def _silu(x):
    return x * jax.nn.sigmoid(x)


def _params(*sem):
    return pltpu.CompilerParams(dimension_semantics=sem, vmem_limit_bytes=V7X_VMEM_LIMIT)


def _pack_halves(y):
    return _pack_rounded(y.astype(BF16).astype(F32))


def _pack_rounded(y):
    m = y.shape[1] // 2
    bits = lax.bitcast_convert_type(y, U32)
    return (bits[:, m:] & HIGH_HALF) | (bits[:, :m] >> 16)


def _unpack_halves(w):
    lo = lax.bitcast_convert_type(w << 16, F32)
    hi = lax.bitcast_convert_type(w & HIGH_HALF, F32)
    return lo, hi


def _adaln_kernel(c_ref, w_ref, b_ref, o_ref):
    o_ref[...] = jnp.dot(_silu(c_ref[...]), w_ref[...], preferred_element_type=F32,
                         precision=lax.Precision.HIGHEST) + b_ref[...]


def _adaln(c, w_ada, b_ada):
    bsz, d = c.shape
    n = w_ada.shape[1]
    bn = 1024
    return pl.pallas_call(
        _adaln_kernel,
        grid=(n // bn,),
        in_specs=[pl.BlockSpec((bsz, d), lambda j: (0, 0)),
                  pl.BlockSpec((d, bn), lambda j: (0, j)),
                  pl.BlockSpec((1, bn), lambda j: (0, j))],
        out_specs=pl.BlockSpec((bsz, bn), lambda j: (0, j)),
        out_shape=jax.ShapeDtypeStruct((bsz, n), F32),
        compiler_params=_params("arbitrary"),
        name="adaln",
    )(c, w_ada, b_ada.reshape(1, n))


def _mix_front_kernel(x_ref, shift_ref, scale_ref, gmix_ref, win_ref, cw_ref, cb_ref, gconv_ref, after_ref,
                      yconv_ref, ut_ref, zprev_ref):
    del after_ref
    s = pl.program_id(1)

    @pl.when(s == 0)
    def _():
        zprev_ref[...] = jnp.zeros_like(zprev_ref)

    cw = cw_ref[...]
    sub = min(MIX_FRONT_SUB, x_ref.shape[1])
    carry = {"prev": zprev_ref[...]}

    def norm(st):
        x = x_ref[0, st["r0"]:st["r0"] + sub, :]
        st["h"] = (_rms(x, gmix_ref[...]) * (1.0 + scale_ref[0]) + shift_ref[0]).astype(BF16)

    def in_proj(st):
        st["proj"] = jnp.dot(st.pop("h"), win_ref[...], preferred_element_type=F32)

    def mixers(st):
        r0 = st["r0"]
        proj = st.pop("proj")
        b_gate = proj[:, :D_CONV]
        c_gate = proj[:, D_CONV:2 * D_CONV]
        v = proj[:, 2 * D_CONV:3 * D_CONV]
        u = proj[:, 3 * D_CONV:]
        z = c_gate * v
        prev = carry["prev"]
        rid = lax.broadcasted_iota(I32, z.shape, 0)
        z1 = jnp.where(rid == 0, prev[7:8, :], pltpu.roll(z, 1, axis=0))
        z2 = jnp.where(rid == 0, prev[6:7, :], jnp.where(rid == 1, prev[7:8, :], pltpu.roll(z, 2, axis=0)))
        carry["prev"] = z[sub - 8:, :]
        conv = cw[0:1, :] * z2 + cw[1:2, :] * z1 + cw[2:3, :] * z + cb_ref[...]
        yconv_ref[0, r0:r0 + sub, :] = _rms(b_gate * conv, gconv_ref[...]).astype(BF16)
        ut_ref[0, :, r0:r0 + sub] = u.T.astype(BF16)

    phases = (norm, in_proj, mixers)
    chains = [{"r0": r0} for r0 in range(0, x_ref.shape[1], sub)]
    for step in range(len(phases) + len(chains) - 1):
        for ci, st in enumerate(chains):
            if 0 <= step - ci < len(phases):
                phases[step - ci](st)
    zprev_ref[...] = carry["prev"]


def _mix_front(x, shift1, scale1, g_mix, w_in_bf, conv_w, conv_b, g_conv, ts, b0, bsz, after):
    _, seq, d = x.shape
    d_in = w_in_bf.shape[1]
    row = lambda b, s: (b + b0, 0, 0)
    const2 = lambda b, s: (0, 0)
    return pl.pallas_call(
        _mix_front_kernel,
        grid=(bsz, seq // ts),
        in_specs=[pl.BlockSpec((1, ts, d), lambda b, s: (b + b0, s, 0)),
                  pl.BlockSpec((1, 1, d), row),
                  pl.BlockSpec((1, 1, d), row),
                  pl.BlockSpec((1, d), const2),
                  pl.BlockSpec((d, d_in), const2),
                  pl.BlockSpec((8, D_CONV), const2),
                  pl.BlockSpec((1, D_CONV), const2),
                  pl.BlockSpec((1, D_CONV), const2),
                  pl.BlockSpec(memory_space=pl.ANY)],
        out_specs=[pl.BlockSpec((1, ts, D_CONV), lambda b, s: (b, s, 0)),
                   pl.BlockSpec((1, D_SSM, ts), lambda b, s: (b, 0, s))],
        out_shape=[jax.ShapeDtypeStruct((bsz, seq, D_CONV), BF16),
                   jax.ShapeDtypeStruct((bsz, D_SSM, seq), BF16)],
        scratch_shapes=[pltpu.VMEM((8, D_CONV), F32)],
        compiler_params=_params("arbitrary", "arbitrary"),
        name="mix_front",
    )(x, shift1, scale1, g_mix.reshape(1, d), w_in_bf,
      jnp.pad(conv_w, ((0, 8 - conv_w.shape[0]), (0, 0))), conv_b.reshape(1, D_CONV),
      g_conv.reshape(1, D_CONV), after)


def _s5_tables(lam_re, lam_im, log_dt, b_re, b_im, c_re, c_im, d_skip, chunk, n_chunks):
    hp = lax.Precision.HIGHEST
    G, H, P, L = SSM_GROUPS, SSM_GROUP_CH, SSM_STATE, chunk
    lr = lam_re.astype(F32)
    li = lam_im.astype(F32)
    dt = jnp.exp(log_dt.astype(F32))[:, None]
    mag = jnp.exp(lr * dt)
    ang = li * dt
    ab_re = mag * jnp.cos(ang)
    ab_im = mag * jnp.sin(ang)
    den = lr * lr + li * li
    nr = ab_re - 1.0
    ni = ab_im
    q_re = (nr * lr + ni * li) / den
    q_im = (ni * lr - nr * li) / den
    br = b_re.astype(F32)
    bi = b_im.astype(F32)
    bb_re = q_re[..., None] * br - q_im[..., None] * bi
    bb_im = q_re[..., None] * bi + q_im[..., None] * br

    def a_power(tau):
        t = tau.astype(F32)[None, :, None]
        m = jnp.exp(t * (lr * dt)[:, None, :])
        th = t * ang[:, None, :]
        return m * jnp.cos(th), m * jnp.sin(th)

    pw_re, pw_im = a_power(jnp.arange(L + 1))
    cr = c_re.astype(F32)[:, None]
    ci = c_im.astype(F32)[:, None]
    cp_re = cr * pw_re[:, :, None, :] - ci * pw_im[:, :, None, :]
    cp_im = cr * pw_im[:, :, None, :] + ci * pw_re[:, :, None, :]
    kern = (jnp.einsum('gthp,gpk->gthk', cp_re[:, :L], bb_re, precision=hp)
            - jnp.einsum('gthp,gpk->gthk', cp_im[:, :L], bb_im, precision=hp))
    kern = kern.at[:, 0].add(d_skip.astype(F32).reshape(G, H)[:, :, None] * jnp.eye(H, dtype=F32))
    kern = kern.transpose(0, 3, 2, 1).reshape(G, H * H, L)

    rev_re = pw_re[:, L - 1::-1]
    rev_im = pw_im[:, L - 1::-1]
    bt_re = bb_re.transpose(0, 2, 1)[:, :, None, :]
    bt_im = bb_im.transpose(0, 2, 1)[:, :, None, :]
    e_re = rev_re[:, None] * bt_re - rev_im[:, None] * bt_im
    e_im = rev_re[:, None] * bt_im + rev_im[:, None] * bt_re
    e_mat = jnp.concatenate([e_re, e_im], axis=-1).reshape(G, H * L, 2 * P)

    f_re = cp_re[:, 1:].transpose(0, 3, 2, 1)
    f_im = -cp_im[:, 1:].transpose(0, 3, 2, 1)
    f_mat = jnp.concatenate([f_re, f_im], axis=1).reshape(G, 2 * P, H * L)

    n_steps = max(1, (n_chunks - 1).bit_length())
    sr, si = a_power(L * (2 ** jnp.arange(n_steps)))
    a_pow = jnp.stack([jnp.concatenate([sr, sr], axis=-1),
                       jnp.concatenate([-si, si], axis=-1)], axis=2)
    return kern, e_mat.astype(BF16), f_mat.astype(BF16), a_pow


def _s5_kernel(u_ref, k_ref, e_ref, f_ref, a_ref, after_ref, y_ref, toep_ref, *, n_chunks, n_steps, chunk):
    del after_ref
    L, H = chunk, SSM_GROUP_CH

    causal = lax.broadcasted_iota(I32, (L, L), 1) >= lax.broadcasted_iota(I32, (L, L), 0)

    def build(hin, carry):
        r0 = pl.multiple_of(hin * L, L)
        for hout in range(H):
            krow = k_ref[0, pl.ds(hin * H + hout, 1), :]
            blk = pltpu.roll(jnp.broadcast_to(krow, (L, L)), 0, axis=1, stride=1, stride_axis=0)
            toep_ref[pl.ds(r0, L), hout * L:(hout + 1) * L] = jnp.where(causal, blk, 0.0).astype(BF16)
        return carry

    lax.fori_loop(0, H, build, 0)

    bsz = u_ref.shape[0]
    u = jnp.concatenate([u_ref[:, h].reshape(bsz * n_chunks, L) for h in range(H)], axis=1)
    st = jnp.dot(u, e_ref[0], preferred_element_type=F32)
    cidx = lax.broadcasted_iota(I32, st.shape, 0) % n_chunks
    p2 = st.shape[1]

    def shifted(xv, d):
        return jnp.where(cidx >= d, pltpu.roll(xv, d, axis=0), 0.0)

    for k in range(n_steps):
        d = 1 << k
        if d >= n_chunks:
            break
        z = shifted(st, d)
        st = st + a_ref[0, k, 0:1, :] * z + a_ref[0, k, 1:2, :] * pltpu.roll(z, p2 // 2, axis=1)
    s_in = shifted(st, 1)
    y = jnp.dot(u, toep_ref[...], preferred_element_type=F32)
    y = (y + jnp.dot(s_in.astype(BF16), f_ref[0], preferred_element_type=F32)).astype(BF16)
    for h in range(H):
        y_ref[:, h] = y[:, h * L:(h + 1) * L].reshape(bsz, n_chunks, L)


def _s5_core(ut5, kern, e_mat, f_mat, a_pow, after):
    bsz, g, h, n_chunks, chunk = ut5.shape
    hl = h * chunk
    n_steps = a_pow.shape[1]
    p2 = e_mat.shape[2]
    blk = lambda i: (i, 0, 0)
    seq_blk = pl.BlockSpec((bsz, None, h, n_chunks, chunk), lambda i: (0, i, 0, 0, 0))
    return pl.pallas_call(
        functools.partial(_s5_kernel, n_chunks=n_chunks, n_steps=n_steps, chunk=chunk),
        grid=(g,),
        in_specs=[seq_blk,
                  pl.BlockSpec((1,) + kern.shape[1:], blk),
                  pl.BlockSpec((1, hl, p2), blk),
                  pl.BlockSpec((1, p2, hl), blk),
                  pl.BlockSpec((1, n_steps, 2, p2), lambda i: (i, 0, 0, 0)),
                  pl.BlockSpec(memory_space=pl.ANY)],
        out_specs=seq_blk,
        out_shape=jax.ShapeDtypeStruct(ut5.shape, BF16),
        scratch_shapes=[pltpu.VMEM((hl, hl), BF16)],
        compiler_params=_params("arbitrary"),
        name="s5_core",
    )(ut5, kern, e_mat, f_mat, a_pow, after)


def _route(scores, biased, tri, base):
    n_tok = scores.shape[1]
    shape3 = (EXPERTS_PER_GROUP, N_EXPERT_GROUPS, n_tok)
    sc3 = scores.reshape(shape3)
    b3 = biased.reshape(shape3)
    j_iota = lax.broadcasted_iota(I32, shape3, 0)
    e_iota = lax.broadcasted_iota(I32, shape3, 1) * EXPERTS_PER_GROUP + j_iota

    def red(fn, x):
        return fn(fn(x, axis=0, keepdims=True), axis=1, keepdims=True)

    m1 = jnp.max(b3, axis=0, keepdims=True)
    i1 = jnp.min(jnp.where(b3 == m1, j_iota, EXPERTS_PER_GROUP), axis=0, keepdims=True)
    m2 = jnp.max(jnp.where(j_iota == i1, NEG_INF, b3), axis=0, keepdims=True)
    gs = m1 + m2

    g_iota = lax.broadcasted_iota(I32, gs.shape, 1)
    gsel = jnp.zeros(gs.shape, F32)
    cur = gs
    for _ in range(TOPK_GROUPS):
        m = jnp.max(cur, axis=1, keepdims=True)
        ig = jnp.min(jnp.where(cur == m, g_iota, N_EXPERT_GROUPS), axis=1, keepdims=True)
        pick = g_iota == ig
        gsel = jnp.where(pick, 1.0, gsel)
        cur = jnp.where(pick, NEG_INF, cur)

    cur = jnp.where(gsel > 0.0, b3, NEG_INF)
    sel = jnp.zeros(shape3, F32)
    ids, vals = [], []
    for _ in range(TOP_K):
        m = red(jnp.max, cur)
        ie = red(jnp.min, jnp.where(cur == m, e_iota, N_EXPERTS))
        pick = e_iota == ie
        ids.append(ie)
        vals.append(red(jnp.sum, jnp.where(pick, sc3, 0.0)))
        sel = jnp.where(pick, 1.0, sel)
        cur = jnp.where(pick, NEG_INF, cur)
    tot = functools.reduce(lambda a, b: a + b, vals)

    sel2 = sel.reshape(N_EXPERTS, n_tok)
    before = jnp.dot(sel2.astype(BF16), tri, preferred_element_type=F32) + base
    before3 = before.reshape(shape3)
    ranks = [red(jnp.sum, jnp.where(e_iota == ie, before3, 0.0)) for ie in ids]

    def rows(parts, dtype):
        parts = [p.reshape(1, n_tok).astype(dtype) for p in parts]
        return jnp.concatenate(parts + [jnp.zeros((8 - len(parts), n_tok), dtype)], axis=0)

    e8 = rows(ids, I32)
    w8 = rows([v / tot * ROUTED_SCALE for v in vals], F32)
    r8 = rows(ranks, I32)
    return e8, w8, r8, jnp.sum(sel2, axis=1, keepdims=True)


def _mix_back_kernel(x_ref, yconv_ref, yt_ref, gate1_ref, shift2_ref, scale2_ref, gate2_ref,
                     wglu_ref, bglu_ref, gssm_ref, woc_ref, wos_ref, gffn_ref,
                     wrh_ref, wrl_ref, rbias_ref, ws13_ref, ws2_ref, tri_ref,
                     x1_ref, h2p_ref, e8_ref, w8_ref, r8_ref, cnt_ref):
    first = jnp.logical_and(pl.program_id(0) == 0, pl.program_id(1) == 0)

    @pl.when(first)
    def _():
        cnt_ref[...] = jnp.zeros_like(cnt_ref)

    sub = tri_ref.shape[0]

    def ssm_post(st):
        r0 = st["r0"]
        y = yt_ref[0, :, r0:r0 + sub].astype(F32).T
        y = _gelu_tanh(y)
        y = y * jax.nn.sigmoid(jnp.dot(y.astype(BF16), wglu_ref[...], preferred_element_type=F32)
                               + bglu_ref[...])
        st["y_ssm"] = _rms(y, gssm_ref[...]).astype(BF16)

    def out_proj(st):
        r0 = st["r0"]
        mix = jnp.dot(yconv_ref[0, r0:r0 + sub, :], woc_ref[...], preferred_element_type=F32)
        mix = mix + jnp.dot(st.pop("y_ssm"), wos_ref[...], preferred_element_type=F32)
        st["x1"] = x_ref[0, r0:r0 + sub, :] + gate1_ref[0] * mix

    def ffn_in(st):
        r0 = st["r0"]
        h2 = _rms(st["x1"], gffn_ref[...] * (1.0 + scale2_ref[0])) + shift2_ref[0]
        h2b = h2.astype(BF16)
        h2r = h2b.astype(F32)
        h2p_ref[0, r0:r0 + sub, :] = lax.bitcast_convert_type(_pack_rounded(h2r), I32)
        st["h2b"] = h2b
        st["h2l"] = (h2 - h2r).astype(BF16)

    def shared_expert(st):
        r0 = st["r0"]
        a = jnp.dot(st["h2b"], ws13_ref[...], preferred_element_type=F32)
        f = a.shape[1] // 2
        act = _silu(a[:, :f]) * a[:, f:]
        shared = jnp.dot(act.astype(BF16), ws2_ref[...], preferred_element_type=F32)
        x1_ref[0, r0:r0 + sub, :] = (st.pop("x1") + gate2_ref[0] * shared).astype(x1_ref.dtype)

    def router(st):
        lt = (jnp.dot(st.pop("h2b"), wrh_ref[...], preferred_element_type=F32)
              + jnp.dot(st.pop("h2l"), wrl_ref[...], preferred_element_type=F32)).T
        st["scores"] = jax.nn.sigmoid(lt[:N_EXPERTS] + lt[N_EXPERTS:])

    def route(st):
        r0 = st["r0"]
        scores = st.pop("scores")
        e8, w8, r8, cnt = _route(scores, scores + rbias_ref[...], tri_ref[...], carry["base"])
        e8_ref[:, r0:r0 + sub] = e8
        w8_ref[0, r0:r0 + sub, :] = w8.T
        r8_ref[:, r0:r0 + sub] = r8
        carry["base"] = carry["base"] + cnt

    carry = {"base": cnt_ref[:, 0:1]}
    phases = (ssm_post, out_proj, ffn_in, shared_expert, router, route)
    chains = [{"r0": r0} for r0 in range(0, x_ref.shape[1], sub)]
    lag = MIX_PHASE_LAG
    for step in range(len(phases) + lag * (len(chains) - 1)):
        for ci, st in enumerate(chains):
            ph = step - lag * ci
            if 0 <= ph < len(phases):
                phases[ph](st)
    cnt_ref[...] = jnp.broadcast_to(carry["base"], cnt_ref.shape)


def _mix_back(x, yconv, yt, gate1, shift2, scale2, gate2, w_glu_bf, b_glu, g_ssm, wo_conv, wo_ssm,
              g_ffn, wr_hi, wr_lo, router_bias, ws13, ws2_bf, ts, b0, bsz):
    _, seq, d = x.shape
    n_tok = bsz * seq
    tiles = seq // ts
    row = lambda b, s: (b + b0, 0, 0)
    const2 = lambda b, s: (0, 0)
    tile_in = lambda b, s: (b + b0, s, 0)
    tile = lambda b, s: (b, s, 0)
    flat = lambda b, s: (0, b * tiles + s)
    full = lambda a: pl.BlockSpec(a.shape, const2)
    sub = min(MIX_SUB, ts)
    tri = jnp.triu(jnp.ones((sub, sub), BF16), k=1)
    args = (w_glu_bf, b_glu.reshape(1, D_SSM), g_ssm.reshape(1, D_SSM), wo_conv, wo_ssm,
            g_ffn.reshape(1, d), wr_hi, wr_lo, router_bias.reshape(N_EXPERTS, 1), ws13, ws2_bf, tri)
    return pl.pallas_call(
        _mix_back_kernel,
        grid=(bsz, tiles),
        in_specs=[pl.BlockSpec((1, ts, d), tile_in),
                  pl.BlockSpec((1, ts, D_CONV), tile),
                  pl.BlockSpec((1, D_SSM, ts), lambda b, s: (b, 0, s)),
                  pl.BlockSpec((1, 1, d), row), pl.BlockSpec((1, 1, d), row),
                  pl.BlockSpec((1, 1, d), row), pl.BlockSpec((1, 1, d), row)]
                 + [full(a) for a in args],
        out_specs=[pl.BlockSpec((1, ts, d), tile),
                   pl.BlockSpec((1, ts, d // 2), tile),
                   pl.BlockSpec((8, ts), flat),
                   pl.BlockSpec((1, ts, 8), tile),
                   pl.BlockSpec((8, ts), flat),
                   pl.BlockSpec((N_EXPERTS, 128), const2)],
        out_shape=[jax.ShapeDtypeStruct((bsz, seq, d), RESIDUAL_DTYPE),
                   jax.ShapeDtypeStruct((bsz, seq, d // 2), I32),
                   jax.ShapeDtypeStruct((8, n_tok), I32),
                   jax.ShapeDtypeStruct((bsz, seq, 8), F32),
                   jax.ShapeDtypeStruct((8, n_tok), I32),
                   jax.ShapeDtypeStruct((N_EXPERTS, 128), F32)],
        compiler_params=_params("arbitrary", "arbitrary"),
        name="mix_back",
    )(x, yconv, yt, gate1, shift2, scale2, gate2, *args)


def _plan_kernel(cnt_ref, e8_ref, r8_ref, pos_ref, *, bm):
    cnt = cnt_ref[...].astype(I32)
    padded = (cnt + (bm - 1)) // bm * bm
    rid = lax.broadcasted_iota(I32, padded.shape, 0)
    incl = padded
    d = 1
    while d < N_EXPERTS:
        incl = incl + jnp.where(rid >= d, pltpu.roll(incl, d, axis=0), 0)
        d *= 2
    pstart = incl - padded
    lanes = pstart.shape[1]

    def chunk(ci, carry):
        c0 = pl.multiple_of(ci * lanes, lanes)
        e = e8_ref[:, pl.ds(c0, lanes)]
        acc = r8_ref[:, pl.ds(c0, lanes)]
        for ex in range(N_EXPERTS):
            acc = acc + jnp.where(e == ex, pstart[ex:ex + 1, :], 0)
        pos_ref[:, pl.ds(c0, lanes)] = acc
        return carry

    lax.fori_loop(0, e8_ref.shape[1] // lanes, chunk, 0)


def _plan(counts, e8, r8, bm):
    n_tok = e8.shape[1]
    tb = min(8192, n_tok)
    return pl.pallas_call(
        functools.partial(_plan_kernel, bm=bm),
        grid=(n_tok // tb,),
        in_specs=[pl.BlockSpec(counts.shape, lambda i: (0, 0)),
                  pl.BlockSpec((8, tb), lambda i: (0, i)),
                  pl.BlockSpec((8, tb), lambda i: (0, i))],
        out_specs=pl.BlockSpec((8, tb), lambda i: (0, i)),
        out_shape=jax.ShapeDtypeStruct((8, n_tok), I32),
        compiler_params=_params("arbitrary"),
        name="plan",
    )(counts, e8, r8)


def _sc_mesh():
    return plsc.VectorSubcoreMesh(core_axis_name="c", subcore_axis_name="s",
                                  num_cores=V7X_SC_CORES, num_subcores=V7X_SC_SUBCORES)


def _sc_worker_base(per_worker):
    return (lax.axis_index("s") * V7X_SC_CORES + lax.axis_index("c")) * per_worker


def _sc_dispatch(h2p, posflat, n_rows):
    n_tok, half = h2p.shape
    win = SC_WINDOW
    per_worker = n_tok // SC_WORKERS
    n_win = per_worker // win

    wins_per_k = n_tok // win

    @functools.partial(
        pl.kernel, mesh=_sc_mesh(),
        out_type=jax.ShapeDtypeStruct((n_rows, half), I32),
        scratch_types=[pltpu.VMEM((TOP_K, n_win, win), I32), pltpu.VMEM((2, win, half), I32),
                       pltpu.SemaphoreType.DMA((2,)), pltpu.SemaphoreType.DMA((2,))],
        name="sc_dispatch")
    def run(h2p_hbm, pos_hbm, xs_hbm, idx_v, rows_v, sem_in, sem_out):
        wid = _sc_worker_base(1)
        for k in range(TOP_K):
            pltpu.sync_copy(pos_hbm.at[pl.ds(k * wins_per_k + wid * n_win, n_win)], idx_v.at[k])

        @pl.loop(0, n_win, step=2)
        def _(j):
            loads = []
            for b in range(2):
                t0 = pl.multiple_of((wid * n_win + j + b) * win, win)
                loads.append(pltpu.async_copy(h2p_hbm.at[pl.ds(t0, win)], rows_v.at[b], sem_in.at[b]))
            stores = []
            for b in range(2):
                loads[b].wait()
                for k in range(TOP_K):
                    stores.append(pltpu.async_copy(rows_v.at[b], xs_hbm.at[idx_v.at[k, j + b]], sem_out.at[b]))
            for cp in stores:
                cp.wait()

    return run(h2p, posflat.reshape(TOP_K * wins_per_k, win))


def _sc_combine(ys, posflat, w_lanes):
    n_tok = w_lanes.shape[0]
    half = ys.shape[1]
    win, step, lanes = SC_WINDOW, SC_COMBINE_TOKENS, V7X_SC_LANES
    per_worker = n_tok // SC_WORKERS
    n_win = per_worker // win
    wins_per_k = n_tok // win

    @functools.partial(
        pl.kernel, mesh=_sc_mesh(),
        out_type=jax.ShapeDtypeStruct((n_tok, half), I32),
        scratch_types=[pltpu.VMEM((TOP_K, n_win, win), I32), pltpu.VMEM((2, TOP_K, step, half), I32),
                       pltpu.VMEM((2, step, w_lanes.shape[1]), I32), pltpu.VMEM((step, half), I32),
                       pltpu.SemaphoreType.DMA((2,))],
        compiler_params=pltpu.CompilerParams(needs_layout_passes=False),
        name="sc_combine")
    def run(ys_hbm, pos_hbm, w_hbm, out_hbm, idx_v, rows_v, w_v, out_v, sem):
        wid = _sc_worker_base(1)
        n_steps = per_worker // step
        for k in range(TOP_K):
            pltpu.sync_copy(pos_hbm.at[pl.ds(k * wins_per_k + wid * n_win, n_win)], idx_v.at[k])

        def copies(s, slot):
            wdw = s // (win // step)
            q0 = pl.multiple_of((s % (win // step)) * step, step)
            t0 = pl.multiple_of(wid * per_worker + s * step, step)
            cps = [pltpu.make_async_copy(ys_hbm.at[idx_v[k, wdw, pl.ds(q0, step)]], rows_v.at[slot, k],
                                         sem.at[slot]) for k in range(TOP_K)]
            return cps + [pltpu.make_async_copy(w_hbm.at[pl.ds(t0, step)], w_v.at[slot], sem.at[slot])]

        def compute(s, slot):
            @pl.loop(0, step, step=SC_COMBINE_UNROLL)
            def _(i0):
                for i in [i0 + di for di in range(SC_COMBINE_UNROLL)]:
                    wb = [plsc.bitcast(w_v[slot, i, pl.ds(k * lanes, lanes)], BF16) for k in range(TOP_K)]
                    for c in range(0, half, lanes):
                        t = [wb[k] * plsc.bitcast(rows_v[slot, k, i, pl.ds(c, lanes)], BF16)
                             for k in range(TOP_K)]
                        while len(t) > 1:
                            t = [t[a] + t[a + 1] for a in range(0, len(t) - 1, 2)] + t[len(t) & ~1:]
                        out_v[i, pl.ds(c, lanes)] = plsc.bitcast(t[0], I32)

            t0 = pl.multiple_of(wid * per_worker + s * step, step)
            pltpu.sync_copy(out_v, out_hbm.at[pl.ds(t0, step)])

        for cp in copies(0, 0):
            cp.start()

        @pl.loop(0, n_steps, step=2)
        def _(s):
            for cp in copies(s + 1, 1):
                cp.start()
            for cp in copies(s, 0):
                cp.wait()
            compute(s, 0)

            @pl.when(s + 2 < n_steps)
            def _():
                for cp in copies(s + 2, 0):
                    cp.start()

            for cp in copies(s + 1, 1):
                cp.wait()
            compute(s + 1, 1)

    return run(ys, posflat.reshape(TOP_K * wins_per_k, win), w_lanes)


def _experts_kernel(be_ref, nu_ref, xs_hbm, w1_ref, w3_ref, w2_ref, after_ref, y_ref, w13_s, w2_s, xbuf, sem):
    del after_ref
    i = pl.program_id(0)
    n_used = nu_ref[0]
    bm = xbuf.shape[1]
    nbuf = xbuf.shape[0]

    def row_copy(blk):
        slot = blk % nbuf
        r0 = pl.multiple_of(blk * bm, bm)
        return pltpu.make_async_copy(xs_hbm.at[pl.ds(r0, bm)], xbuf.at[slot], sem.at[slot])

    @pl.when(i == 0)
    def _():
        for b in range(nbuf - 1):
            @pl.when(b < n_used)
            def _():
                row_copy(b).start()

    @pl.when(i + nbuf - 1 < n_used)
    def _():
        row_copy(i + nbuf - 1).start()

    changed = jnp.logical_or(i == 0, be_ref[i] != be_ref[jnp.maximum(i - 1, 0)])

    @pl.when(changed)
    def _():
        f = w1_ref.shape[2]
        w13_s[:, :f] = w1_ref[0].astype(BF16)
        w13_s[:, f:] = w3_ref[0].astype(BF16)
        w2_s[...] = w2_ref[0].astype(BF16)

    @pl.when(i < n_used)
    def _():
        row_copy(i).wait()
        x_ref = xbuf.at[i % nbuf]
        half = x_ref.shape[1]
        sub = min(EXPERT_SUB, x_ref.shape[0])
        for r0 in range(0, x_ref.shape[0], sub):
            lo, hi = _unpack_halves(lax.bitcast_convert_type(x_ref[r0:r0 + sub, :], U32))
            a = jnp.dot(lo.astype(BF16), w13_s[:half, :], preferred_element_type=F32)
            a = a + jnp.dot(hi.astype(BF16), w13_s[half:, :], preferred_element_type=F32)
            f = a.shape[1] // 2
            act = _silu(a[:, :f]) * a[:, f:]
            y = jnp.dot(act.astype(BF16), w2_s[...], preferred_element_type=F32)
            y_ref[r0:r0 + sub, :] = lax.bitcast_convert_type(_pack_halves(y), I32)


def _experts(block_e, n_used, xs, w1, w3, w2, bm, after):
    n_rows, half = xs.shape
    d, f = w1.shape[1], w1.shape[2]
    rows = lambda i, be, nu: (jnp.minimum(i, nu[0] - 1), 0)
    wblk = lambda i, be, nu: (be[i], 0, 0)
    grid_spec = pltpu.PrefetchScalarGridSpec(
        num_scalar_prefetch=2,
        grid=(n_rows // bm,),
        in_specs=[pl.BlockSpec(memory_space=pl.ANY),
                  pl.BlockSpec((1, d, f), wblk),
                  pl.BlockSpec((1, d, f), wblk),
                  pl.BlockSpec((1, f, d), wblk),
                  pl.BlockSpec(memory_space=pl.ANY)],
        out_specs=pl.BlockSpec((bm, half), rows),
        scratch_shapes=[pltpu.VMEM((d, 2 * f), BF16), pltpu.VMEM((f, d), BF16),
                        pltpu.VMEM((EXPERT_ROW_BUFFERS, bm, half), I32),
                        pltpu.SemaphoreType.DMA((EXPERT_ROW_BUFFERS,))],
    )
    return pl.pallas_call(
        _experts_kernel,
        grid_spec=grid_spec,
        out_shape=jax.ShapeDtypeStruct((n_rows, half), I32),
        compiler_params=_params("arbitrary"),
        name="experts",
    )(block_e, n_used, xs, w1, w3, w2, after)


def _combine_kernel(r_ref, x1_ref, gate2_ref, gfin_ref, *rest):
    o_ref = rest[-1]
    routed = jnp.concatenate(_unpack_halves(lax.bitcast_convert_type(r_ref[...], U32)), axis=1)
    o_ref[...] = _rms(x1_ref[...].astype(F32) + gate2_ref[0] * routed, gfin_ref[...])


def _combine(routed, x1, gate2, g_final, tt, tiles_per_seq, b_src, b0, n_tok_all, out_prev):
    d = x1.shape[1]
    n_tok, half = routed.shape
    src0 = b_src * tiles_per_seq
    tile0 = b0 * tiles_per_seq
    in_specs = [pl.BlockSpec((tt, half), lambda i: (i, 0)),
                pl.BlockSpec((tt, d), lambda i: (i + src0, 0)),
                pl.BlockSpec((1, 1, d), lambda i: (i // tiles_per_seq + b0, 0, 0)),
                pl.BlockSpec((1, d), lambda i: (0, 0))]
    args = [routed, x1, gate2, g_final.reshape(1, d)]
    aliases = {}
    if out_prev is not None:
        in_specs.append(pl.BlockSpec(memory_space=pl.ANY))
        args.append(out_prev)
        aliases = {len(args) - 1: 0}
    return pl.pallas_call(
        _combine_kernel,
        grid=(n_tok // tt,),
        in_specs=in_specs,
        out_specs=pl.BlockSpec((tt, d), lambda i: (i + tile0, 0)),
        out_shape=jax.ShapeDtypeStruct((n_tok_all, d), F32),
        input_output_aliases=aliases,
        compiler_params=_params("arbitrary"),
        name="combine",
    )(*args)


def _block_experts(counts, bm, n_blocks):
    padded = (counts.astype(I32) + (bm - 1)) // bm * bm
    pend = jnp.cumsum(padded)
    starts = jnp.arange(n_blocks, dtype=I32) * bm
    block_e = jnp.minimum(jnp.sum((pend[None, :] <= starts[:, None]).astype(I32), axis=1), N_EXPERTS - 1)
    return block_e, (pend[-1:] // bm).astype(I32)


def kernel(x, c, w_ada, b_ada, g_mix, w_in, conv_w, conv_b, g_conv, lam_re, lam_im, log_dt, b_re, b_im,
           c_re, c_im, d_skip, w_glu, b_glu, g_ssm, w_out, g_ffn, w_router, router_bias, w1, w3, w2,
           ws1, ws3, ws2, g_final):
    bsz, seq, d = x.shape
    n_tok = bsz * seq
    ts = min(MIX_BACK_TOKENS, seq)
    L = min(S5_CHUNK, seq)
    n_chunks = seq // L
    G, H = SSM_GROUPS, SSM_GROUP_CH

    mod = _adaln(c, w_ada, b_ada).reshape(bsz, 6, 1, d)
    shift1, scale1, gate1, shift2, scale2, gate2 = (mod[:, i] for i in range(6))

    kern, e_mat, f_mat, a_pow = _s5_tables(lam_re, lam_im, log_dt, b_re, b_im, c_re, c_im, d_skip,
                                           L, n_chunks)
    w_in_bf = w_in.astype(BF16)

    NG, NJ = N_EXPERT_GROUPS, EXPERTS_PER_GROUP
    w_r = w_router.astype(F32).reshape(d, NG, NJ).transpose(0, 2, 1).reshape(d, N_EXPERTS)
    r_bias = router_bias.reshape(NG, NJ).T.reshape(N_EXPERTS)
    w_r_hi = w_r.astype(BF16)
    wr_hi = jnp.concatenate([w_r_hi, (w_r - w_r_hi.astype(F32)).astype(BF16)], axis=1)
    wr_lo = jnp.concatenate([w_r_hi, jnp.zeros_like(w_r_hi)], axis=1)
    w_out_bf = w_out.astype(BF16)
    w_glu_bf = w_glu.astype(BF16)
    ws13 = jnp.concatenate([ws1, ws3], axis=1).astype(BF16)
    ws2_bf = ws2.astype(BF16)
    bm = EXPERT_ROWS
    tt = min(COMBINE_TOKENS, seq)

    n_parts = LAYER_PARTS if bsz % LAYER_PARTS == 0 else 1
    pb = bsz // n_parts
    pt = pb * seq
    n_blocks = -(-(pt * TOP_K + N_EXPERTS * (bm - 1)) // bm)
    none = jnp.zeros((8, 128), F32)

    def run_experts(part, after):
        return _experts(*part["block_e"], part["xs"], w1, w3, w2, bm, after)

    def finish(part, ys, out, n_split):
        nb = pb // n_split
        for h in range(n_split):
            lo, hi = h * nb * seq, (h + 1) * nb * seq
            pos = part["pos8"][:TOP_K, lo:hi].reshape(TOP_K * nb * seq)
            routed = _sc_combine(ys, pos, part["w_lanes"][lo:hi])
            out = _combine(routed, part["x1"].reshape(pt, d), gate2, g_final, tt,
                           seq // tt, h * nb, part["b0"] + h * nb, n_tok, out)
        return out

    out, prev, after_front = None, None, none
    for p in range(n_parts):
        b0 = p * pb
        yconv, ut = _mix_front(x, shift1, scale1, g_mix, w_in_bf, conv_w, conv_b, g_conv,
                               min(MIX_FRONT_TOKENS, seq), b0, pb, after_front)
        ys_prev = run_experts(prev, ut) if prev is not None else None
        yt = _s5_core(ut.reshape(pb, G, H, n_chunks, L), kern, e_mat, f_mat, a_pow,
                      none if ys_prev is None else ys_prev).reshape(pb, G * H, seq)
        x1, h2p, e8, w8, r8, counts = _mix_back(
            x, yconv, yt, gate1, shift2, scale2, gate2, w_glu_bf, b_glu, g_ssm,
            w_out_bf[:D_CONV], w_out_bf[D_CONV:], g_ffn, wr_hi, wr_lo, r_bias, ws13, ws2_bf,
            ts, b0, pb)
        counts = counts.reshape(NJ, NG, -1).transpose(1, 0, 2).reshape(N_EXPERTS, -1)
        pos8 = _plan(counts, e8, r8, bm)
        xs = _sc_dispatch(h2p.reshape(pt, d // 2), pos8[:TOP_K].reshape(TOP_K * pt), n_blocks * bm)
        if prev is not None:
            out = finish(prev, ys_prev, out, 1)
        w_bits = lax.bitcast_convert_type(w8.reshape(pt, 8).astype(BF16), jnp.uint16).astype(U32)
        w_lanes = jnp.repeat(lax.bitcast_convert_type((w_bits << 16) | w_bits, I32), V7X_SC_LANES, axis=1)
        prev = dict(b0=b0, x1=x1, w_lanes=w_lanes, pos8=pos8, xs=xs,
                    block_e=_block_experts(counts[:, 0], bm, n_blocks))
        after_front = pos8
    ys_last = run_experts(prev, none if out is None else out)
    out = finish(prev, ys_last, out, LAST_PART_SPLIT if pb % LAST_PART_SPLIT == 0 else 1)
    return out.reshape(bsz, seq, d)
```
